```python
import jax, jax.numpy as jnp
from jax import lax
import numpy as np

D_MODEL = 2048
BATCH = 1
SEQ = 16384
DEPTH = 1

N_META = 16
EPS = 1e-6
GDN_QK_HEADS = 16
GDN_V_HEADS = 32
GDN_HEAD_DIM = 128
GDN_CONV = 4
GDN_CHUNK = 64
ATT_HEADS = 16
ATT_KV_HEADS = 2
ATT_HEAD_DIM = 128
IDX_HEADS = 16
IDX_HEAD_DIM = 128
TOPK_MAX = 256
Q_BLOCK = 128
NEG = -1e30
ROPE_THETA = 500000.0
ROPE_FRACTION = 4
D_FF = 3 * D_MODEL
FFN_CONV = 3

GDN_QK_W = GDN_QK_HEADS * GDN_HEAD_DIM
GDN_V_W = GDN_V_HEADS * GDN_HEAD_DIM
GDN_CONV_W = 2 * GDN_QK_W + GDN_V_W
ATT_Q_W = ATT_HEADS * ATT_HEAD_DIM
ATT_KV_W = ATT_KV_HEADS * ATT_HEAD_DIM
IDX_Q_W = IDX_HEADS * IDX_HEAD_DIM
IN_SPLITS = (GDN_QK_W, GDN_QK_W, GDN_V_W,
             GDN_V_W, GDN_V_HEADS, GDN_V_HEADS,
             ATT_Q_W, ATT_KV_W, ATT_KV_W,
             IDX_Q_W, IDX_HEAD_DIM, IDX_HEADS,
             D_MODEL, D_MODEL)
D_IN = sum(IN_SPLITS)

kernel_name = 'hybrid_gdn_dsa_gated_block'


def rmsnorm(x, g):
    xf = x.astype(jnp.float32)
    y = xf * lax.rsqrt(jnp.mean(xf * xf, axis=-1, keepdims=True) + EPS)
    return (y * g.astype(jnp.float32)).astype(x.dtype)


def l2norm(x):
    return x * lax.rsqrt(jnp.sum(x * x, axis=-1, keepdims=True) + EPS)


def causal_dwconv(x, w):
    K = w.shape[0]
    L = x.shape[1]
    xp = jnp.pad(x, ((0, 0), (K - 1, 0), (0, 0)))
    y = xp[:, 0:L] * w[0]
    for j in range(1, K):
        y = y + xp[:, j:j + L] * w[j]
    return y


def partial_rope(x, pos):
    r = x.shape[-1] // ROPE_FRACTION
    half = r // 2
    inv = ROPE_THETA ** (-jnp.arange(half, dtype=jnp.float32) / half)
    ang = pos.astype(jnp.float32)[:, None] * inv[None, :]
    cos = jnp.cos(ang)[None, :, None, :]
    sin = jnp.sin(ang)[None, :, None, :]
    xr = x[..., :r].astype(jnp.float32)
    x1, x2 = xr[..., :half], xr[..., half:]
    rot = jnp.concatenate([x1 * cos - x2 * sin, x2 * cos + x1 * sin], axis=-1)
    return jnp.concatenate([rot.astype(x.dtype), x[..., r:]], axis=-1)


def gated_deltanet(q, k, v, z, b, a, conv_w, a_log, dt_bias, norm_g):
    f32 = jnp.float32
    B, L, _ = q.shape
    H, Dh, C = GDN_V_HEADS, GDN_HEAD_DIM, GDN_CHUNK
    qkv = jax.nn.silu(causal_dwconv(jnp.concatenate([q, k, v], axis=-1), conv_w)).astype(f32)
    q, k, v = jnp.split(qkv, [GDN_QK_W, 2 * GDN_QK_W], axis=-1)
    rep = GDN_V_HEADS // GDN_QK_HEADS
    q = jnp.repeat(l2norm(q.reshape(B, L, GDN_QK_HEADS, Dh)), rep, axis=2) * (Dh ** -0.5)
    k = jnp.repeat(l2norm(k.reshape(B, L, GDN_QK_HEADS, Dh)), rep, axis=2)
    v = v.reshape(B, L, H, Dh)
    beta = jax.nn.sigmoid(b.astype(f32))
    g = -jnp.exp(a_log.astype(f32)) * jax.nn.softplus(a.astype(f32) + dt_bias.astype(f32))
    lead = (-N_META) % C
    tail = (-(lead + L)) % C
    n = (lead + L + tail) // C

    def to_chunks(t):
        t = jnp.pad(t, ((0, 0), (lead, tail)) + ((0, 0),) * (t.ndim - 2))
        t = t.reshape((B, n, C) + t.shape[2:])
        return jnp.moveaxis(t, (1, 3), (0, 2))

    incl = jnp.tril(jnp.ones((C, C), dtype=bool))
    strict = jnp.tril(jnp.ones((C, C), dtype=bool), -1)
    eye = jnp.eye(C, dtype=f32)

    def step(S, inp):
        qc, kc, vc, bc, gc = inp
        G = jnp.cumsum(gc, axis=-1)
        diff = G[..., :, None] - G[..., None, :]
        decay = jnp.where(incl, jnp.exp(jnp.where(incl, diff, 0.0)), 0.0)
        kk = jnp.einsum('bhid,bhjd->bhij', kc, kc)
        lower = jnp.where(strict, bc[..., :, None] * kk * decay, 0.0)
        rhs = jnp.concatenate([vc * bc[..., None], kc * (bc * jnp.exp(G))[..., None]], axis=-1)
        sol = lax.linalg.triangular_solve(eye + lower, rhs, left_side=True, lower=True)
        u, w = sol[..., :Dh], sol[..., Dh:]
        v_new = u - jnp.einsum('bhck,bhkv->bhcv', w, S)
        attn = jnp.where(incl, jnp.einsum('bhid,bhjd->bhij', qc, kc) * decay, 0.0)
        o = (jnp.einsum('bhck,bhkv->bhcv', qc * jnp.exp(G)[..., None], S)
             + jnp.einsum('bhij,bhjv->bhiv', attn, v_new))
        G_last = G[..., -1:]
        S = (S * jnp.exp(G_last)[..., None]
             + jnp.einsum('bhck,bhcv->bhkv', kc * jnp.exp(G_last - G)[..., None], v_new))
        return S, o

    S0 = jnp.zeros((B, H, Dh, Dh), f32)
    _, o = lax.scan(step, S0, (to_chunks(q), to_chunks(k), to_chunks(v),
                               to_chunks(beta), to_chunks(g)))
    o = jnp.moveaxis(o, (0, 2), (1, 3)).reshape(B, n * C, H, Dh)[:, lead:lead + L]
    o = rmsnorm(o, norm_g) * jax.nn.silu(z.astype(f32).reshape(B, L, H, Dh))
    return o.reshape(B, L, GDN_V_W).astype(z.dtype)


def dsa_attention(q, k, v, iq, ik, iw, pos):
    f32 = jnp.float32
    B, L, _ = q.shape
    topk = min(TOPK_MAX, L // 4)
    q = partial_rope(q.reshape(B, L, ATT_HEADS, ATT_HEAD_DIM), pos)
    k = partial_rope(k.reshape(B, L, ATT_KV_HEADS, ATT_HEAD_DIM), pos)
    v = v.reshape(B, L, ATT_KV_HEADS, ATT_HEAD_DIM)
    iq = partial_rope(iq.reshape(B, L, IDX_HEADS, IDX_HEAD_DIM), pos)
    ik = partial_rope(ik.reshape(B, L, 1, IDX_HEAD_DIM), pos)[:, :, 0]
    iw = iw.astype(f32) * ((IDX_HEADS ** -0.5) * (IDX_HEAD_DIM ** -0.5))
    nb = -(-L // Q_BLOCK)
    Lp = nb * Q_BLOCK
    group = ATT_HEADS // ATT_KV_HEADS
    key_pos = jnp.arange(L, dtype=jnp.int32)

    def to_blocks(t):
        t = jnp.pad(t, ((0, 0), (0, Lp - L)) + ((0, 0),) * (t.ndim - 2))
        t = t.reshape((B, nb, Q_BLOCK) + t.shape[2:])
        return jnp.moveaxis(t, 1, 0)

    def attend(inp):
        qb, iqb, iwb, start = inp
        t = start + jnp.arange(Q_BLOCK, dtype=jnp.int32)
        visible = key_pos[None, :] <= t[:, None]
        logits = jnp.einsum('bqhd,bsd->bqhs', iqb, ik, preferred_element_type=f32)
        score = jnp.einsum('bqhs,bqh->bqs', jax.nn.relu(logits), iwb)
        score = jnp.where(visible[None], score, NEG)
        _, idx = lax.top_k(score, topk)
        valid = idx <= t[None, :, None]
        ksel = jax.vmap(lambda kk, ii: kk[ii])(k, idx)
        vsel = jax.vmap(lambda vv, ii: vv[ii])(v, idx)
        qg = qb.reshape(B, Q_BLOCK, ATT_KV_HEADS, group, ATT_HEAD_DIM)
        s = jnp.einsum('bqgrd,bqkgd->bqgrk', qg, ksel, preferred_element_type=f32) * (ATT_HEAD_DIM ** -0.5)
        s = jnp.where(valid[:, :, None, None, :], s, NEG)
        p = jax.nn.softmax(s, axis=-1)
        o = jnp.einsum('bqgrk,bqkgd->bqgrd', p.astype(vsel.dtype), vsel)
        return o.reshape(B, Q_BLOCK, ATT_Q_W)

    starts = jnp.arange(nb, dtype=jnp.int32) * Q_BLOCK
    out = lax.map(attend, (to_blocks(q), to_blocks(iq), to_blocks(iw), starts))
    return jnp.moveaxis(out, 0, 1).reshape(B, Lp, ATT_Q_W)[:, :L]


def setup_inputs(seed: int = 0) -> dict:
    key = jax.random.key(seed)
    ks = jax.random.split(key, 18)
    f32 = jnp.float32

    def nrm(k, shape, scale):
        return jax.random.normal(k, shape, f32) * scale

    def gain(k, shape):
        return 1.0 + 0.05 * jax.random.normal(k, shape, f32)

    last_tap = (jnp.arange(FFN_CONV) == FFN_CONV - 1).astype(f32)[:, None]
    return {
        'x': nrm(ks[0], (BATCH, SEQ, D_MODEL), 1.0),
        'meta_tokens': nrm(ks[1], (N_META, D_MODEL), 1.0),
        'mix_pre_g': gain(ks[2], (DEPTH, D_MODEL)),
        'w_in': nrm(ks[3], (DEPTH, D_MODEL, D_IN), D_MODEL ** -0.5),
        'gdn_conv_w': nrm(ks[4], (DEPTH, GDN_CONV, GDN_CONV_W), GDN_CONV ** -0.5),
        'gdn_a_log': jnp.log(jax.random.uniform(ks[5], (DEPTH, GDN_V_HEADS), f32, 1.0, 16.0)),
        'gdn_dt_bias': nrm(ks[6], (DEPTH, GDN_V_HEADS), 0.1),
        'gdn_norm_g': gain(ks[7], (DEPTH, GDN_HEAD_DIM)),
        'w_branch_gdn': nrm(ks[8], (DEPTH, GDN_V_W, D_MODEL), GDN_V_W ** -0.5),
        'w_branch_att': nrm(ks[9], (DEPTH, ATT_Q_W, D_MODEL), ATT_Q_W ** -0.5),
        'w_out': nrm(ks[10], (DEPTH, D_MODEL, D_MODEL), D_MODEL ** -0.5),
        'mix_post_g': gain(ks[11], (DEPTH, D_MODEL)),
        'ffn_pre_g': gain(ks[12], (DEPTH, D_MODEL)),
        'w_up': nrm(ks[13], (DEPTH, D_MODEL, 2 * D_FF), D_MODEL ** -0.5),
        'ffn_conv_w': nrm(ks[14], (DEPTH, FFN_CONV, 2 * D_FF), 0.3) + last_tap,
        'ffn_conv_b': nrm(ks[15], (DEPTH, 2 * D_FF), 0.02),
        'w_down': nrm(ks[16], (DEPTH, D_FF, D_MODEL), D_FF ** -0.5),
        'ffn_post_g': gain(ks[17], (DEPTH, D_MODEL)),
    }


def reference(x, meta_tokens, mix_pre_g, w_in, gdn_conv_w, gdn_a_log, gdn_dt_bias, gdn_norm_g,
              w_branch_gdn, w_branch_att, w_out, mix_post_g, ffn_pre_g, w_up, ffn_conv_w,
              ffn_conv_b, w_down, ffn_post_g):
    B = x.shape[0]
    meta = jnp.broadcast_to(meta_tokens.astype(x.dtype)[None], (B, N_META, D_MODEL))
    h = jnp.concatenate([meta, x], axis=1)
    L = h.shape[1]
    pos = jnp.arange(L, dtype=jnp.int32)
    split_at = np.cumsum(IN_SPLITS)[:-1].tolist()
    for i in range(DEPTH):
        u = rmsnorm(h, mix_pre_g[i])
        (gq, gk, gv, gz, gb, ga, aq, ak, av, iq, ik, iw, gate_gdn, gate_att) = jnp.split(
            u @ w_in[i], split_at, axis=-1)
        y_gdn = gated_deltanet(gq, gk, gv, gz, gb, ga, gdn_conv_w[i], gdn_a_log[i],
                               gdn_dt_bias[i], gdn_norm_g[i]) @ w_branch_gdn[i]
        y_att = dsa_attention(aq, ak, av, iq, ik, iw, pos) @ w_branch_att[i]
        merged = jax.nn.sigmoid(gate_gdn) * y_gdn + jax.nn.sigmoid(gate_att) * y_att
        h = h + rmsnorm(merged @ w_out[i], mix_post_g[i])
        u = rmsnorm(h, ffn_pre_g[i])
        up = causal_dwconv(u @ w_up[i], ffn_conv_w[i]) + ffn_conv_b[i]
        gate, val = jnp.split(up, 2, axis=-1)
        h = h + rmsnorm((jax.nn.silu(gate) * val) @ w_down[i], ffn_post_g[i])
    return h[:, N_META:]
```

```python
import functools
import struct

import jax
import jax.numpy as jnp
from jax import lax
from jax.experimental import pallas as pl
from jax.experimental.pallas import tpu as pltpu

f32 = jnp.float32
bf16 = jnp.bfloat16
i32 = jnp.int32

D_MODEL = 2048
N_META = 16
EPS = 1e-6
GDN_QK_HEADS = 16
GDN_V_HEADS = 32
HEAD_DIM = 128
GDN_CONV = 4
ATT_HEADS = 16
ATT_KV_HEADS = 2
IDX_HEADS = 16
TOPK_MAX = 256
NEG = -1e30
ROPE_THETA = 500000.0
ROPE_DIMS = HEAD_DIM // 4
D_FF = 3 * D_MODEL
FFN_CONV = 3
GDN_QK_W = GDN_QK_HEADS * HEAD_DIM
GDN_V_W = GDN_V_HEADS * HEAD_DIM
ATT_Q_W = ATT_HEADS * HEAD_DIM
ATT_KV_W = ATT_KV_HEADS * HEAD_DIM
IDX_Q_W = IDX_HEADS * HEAD_DIM

FRAME_X0 = 256
FRAME_OFF = FRAME_X0 - N_META
CHUNK = 128

LANES = 128
VMEM_LIMIT = 56 * 1024 * 1024

COL_GQKV = 0
COL_GZ = COL_GQKV + 2 * GDN_QK_W + GDN_V_W
COL_AQ = COL_GZ + GDN_V_W
COL_IQ = COL_AQ + ATT_Q_W
COL_GATE_GDN = COL_IQ + IDX_Q_W
COL_GATE_ATT = COL_GATE_GDN + D_MODEL
COL_AK = COL_GATE_ATT + D_MODEL
COL_AV = COL_AK + ATT_KV_W
COL_IK = COL_AV + ATT_KV_W
PROJ_W = COL_IK + HEAD_DIM


def _cparams(sem):
    return pltpu.CompilerParams(dimension_semantics=sem, vmem_limit_bytes=VMEM_LIMIT)


def _pick(n, cands):
    for c in cands:
        if n % c == 0:
            return c
    raise ValueError(f"no tile for {n} in {cands}")


def _sigmoid(x):
    return 1.0 / (1.0 + jnp.exp(-x))


def _dot(a, b):
    return jnp.dot(a, b, preferred_element_type=f32)


def _dot_nt(a, b):
    return lax.dot_general(a, b, (((1,), (1,)), ((), ())), preferred_element_type=f32)


def _rms_rows_kernel(h_ref, g_ref, o_ref):
    h = h_ref[...]
    y = h * lax.rsqrt(jnp.mean(h * h, axis=-1, keepdims=True) + EPS)
    o_ref[...] = (y * g_ref[...]).astype(o_ref.dtype)


def rms_rows(h, g):
    lp, d = h.shape
    tm = _pick(lp, (640, 256, 128))
    return pl.pallas_call(
        _rms_rows_kernel,
        grid=(lp // tm,),
        in_specs=[pl.BlockSpec((tm, d), lambda i: (i, 0)),
                  pl.BlockSpec((1, d), lambda i: (0, 0))],
        out_specs=pl.BlockSpec((tm, d), lambda i: (i, 0)),
        out_shape=jax.ShapeDtypeStruct((lp, d), bf16),
        compiler_params=_cparams(("parallel",)),
        name="rms_rows",
    )(h, g.reshape(1, d))


def _mm_kernel(*refs, nk, n_extra, n_out, epilogue):
    a_ref, w_ref = refs[0], refs[1]
    extra = refs[2:2 + n_extra]
    outs = refs[2 + n_extra:2 + n_extra + n_out]
    if nk == 1:
        epilogue(_dot(a_ref[...], w_ref[...]), extra, outs)
        return
    acc_ref = refs[-1]
    k = pl.program_id(2)

    @pl.when(k == 0)
    def _():
        acc_ref[...] = jnp.zeros_like(acc_ref)

    acc_ref[...] += _dot(a_ref[...], w_ref[...])

    @pl.when(k == nk - 1)
    def _():
        epilogue(acc_ref[...], extra, outs)


def matmul(a, w, *, tm, tn, tk, epilogue, extra=(), extra_specs=(), out_shapes, a_col0=0, name):
    m = a.shape[0]
    kdim, n = w.shape
    nk = kdim // tk
    assert m % tm == 0 and n % tn == 0 and kdim % tk == 0
    in_specs = [pl.BlockSpec((tm, tk), lambda i, j, k: (i, a_col0 + k)),
                pl.BlockSpec((tk, tn), lambda i, j, k: (k, j))]
    in_specs += [pl.BlockSpec(bs, (lambda i, j, k, f=f: f(i, j))) for bs, f in extra_specs]
    out_specs = [pl.BlockSpec((tm, tn), lambda i, j, k: (i, j)) for _ in out_shapes]
    scratch = [] if nk == 1 else [pltpu.VMEM((tm, tn), f32)]
    kern = functools.partial(_mm_kernel, nk=nk, n_extra=len(extra), n_out=len(out_shapes),
                             epilogue=epilogue)
    res = pl.pallas_call(
        kern,
        grid=(m // tm, n // tn, nk),
        in_specs=in_specs,
        out_specs=out_specs,
        out_shape=out_shapes,
        scratch_shapes=scratch,
        compiler_params=_cparams(("parallel", "parallel", "arbitrary")),
        name=name,
    )(a, w, *extra)
    return res


def _ep_cast(acc, extra, outs):
    outs[0][...] = acc.astype(outs[0].dtype)


def _ep_gate(acc, extra, outs):
    g = extra[0][...].astype(f32)
    outs[0][...] = (_sigmoid(g) * acc).astype(outs[0].dtype)


def _ep_gate_add(acc, extra, outs):
    g = extra[0][...].astype(f32)
    outs[0][...] = (extra[1][...].astype(f32) + _sigmoid(g) * acc).astype(outs[0].dtype)


def _rms(t, g):
    return t * lax.rsqrt(jnp.mean(t * t, axis=-1, keepdims=True) + EPS) * g


def _ep_res_norm2(acc, extra, outs):
    h_ref, g_ref, g2_ref = extra
    h1 = h_ref[...] + _rms(acc, g_ref[...])
    outs[0][...] = h1
    outs[1][...] = _rms(h1, g2_ref[...]).astype(outs[1].dtype)


def _ep_res_norm(acc, extra, outs):
    h_ref, g_ref = extra
    outs[0][...] = h_ref[...] + _rms(acc, g_ref[...])


HALO = 16


def _gdn_prep_kernel(x_ref, halo_ref, w_ref, o_ref, ext_ref, *, tm, tc):
    i = pl.program_id(0)
    c = pl.program_id(1)
    halo = halo_ref[...].astype(f32)
    ext_ref[0:HALO, :] = jnp.where(i > 0, halo, 0.0)
    ext_ref[HALO:HALO + tm, :] = x_ref[...].astype(f32)
    w = w_ref[...]
    y = ext_ref[HALO:HALO + tm, :] * w[GDN_CONV - 1:GDN_CONV, :]
    for j in range(GDN_CONV - 1):
        s0 = HALO - (GDN_CONV - 1) + j
        y = y + ext_ref[s0:s0 + tm, :] * w[j:j + 1, :]
    s = y * _sigmoid(y)
    is_q = c < (GDN_QK_W // tc)
    is_qk = c < (2 * GDN_QK_W // tc)
    qscale = jnp.where(is_q, HEAD_DIM ** -0.5, 1.0).astype(f32)
    for hh in range(tc // HEAD_DIM):
        seg = s[:, hh * HEAD_DIM:(hh + 1) * HEAD_DIM]
        r = lax.rsqrt(jnp.sum(seg * seg, axis=-1, keepdims=True) + EPS) * qscale
        fac = jnp.where(is_qk, r, 1.0)
        o_ref[:, hh * HEAD_DIM:(hh + 1) * HEAD_DIM] = (seg * fac).astype(o_ref.dtype)


def gdn_prep(proj, conv_w):
    lp = proj.shape[0]
    width = 2 * GDN_QK_W + GDN_V_W
    tm = _pick(lp, (640, 256, 128))
    tc = 512
    kern = functools.partial(_gdn_prep_kernel, tm=tm, tc=tc)
    return pl.pallas_call(
        kern,
        grid=(lp // tm, width // tc),
        in_specs=[pl.BlockSpec((tm, tc), lambda i, c: (i, c)),
                  pl.BlockSpec((HALO, tc), lambda i, c: (jnp.maximum(i * (tm // HALO) - 1, 0), c)),
                  pl.BlockSpec((GDN_CONV, tc), lambda i, c: (0, c))],
        out_specs=pl.BlockSpec((tm, tc), lambda i, c: (i, c)),
        out_shape=jax.ShapeDtypeStruct((lp, width), bf16),
        scratch_shapes=[pltpu.VMEM((HALO + tm, tc), f32)],
        compiler_params=_cparams(("parallel", "parallel")),
        name="gdn_prep",
    )(proj, proj, conv_w)


def _gdn_gates_kernel(b_ref, a_ref, alog_ref, dt_ref, beta_ref, gcum_ref):
    i = pl.program_id(0)
    rows = i * CHUNK + lax.broadcasted_iota(i32, (CHUNK, 1), 0)
    valid = rows >= FRAME_OFF
    beta_ref[...] = jnp.where(valid, _sigmoid(b_ref[...]), 0.0)
    a = a_ref[...] + dt_ref[...]
    sp = jnp.maximum(a, 0.0) + jnp.log1p(jnp.exp(-jnp.abs(a)))
    g = jnp.where(valid, -jnp.exp(alog_ref[...]) * sp, 0.0)
    tri = (lax.broadcasted_iota(i32, (CHUNK, CHUNK), 0)
           >= lax.broadcasted_iota(i32, (CHUNK, CHUNK), 1)).astype(f32)
    gcum_ref[...] = jnp.dot(tri, g, preferred_element_type=f32, precision=lax.Precision.HIGHEST)


def gdn_gates(gb, ga, a_log, dt_bias):
    lp, nh = gb.shape
    spec = pl.BlockSpec((CHUNK, nh), lambda i: (i, 0))
    vec = pl.BlockSpec((1, nh), lambda i: (0, 0))
    return pl.pallas_call(
        _gdn_gates_kernel,
        grid=(lp // CHUNK,),
        in_specs=[spec, spec, vec, vec],
        out_specs=[spec, spec],
        out_shape=[jax.ShapeDtypeStruct((lp, nh), f32)] * 2,
        compiler_params=_cparams(("parallel",)),
        name="gdn_gates",
    )(gb, ga, a_log.reshape(1, nh), dt_bias.reshape(1, nh))


def _block_mask(size):
    r = lax.broadcasted_iota(i32, (CHUNK, CHUNK), 0) // size
    c = lax.broadcasted_iota(i32, (CHUNK, CHUNK), 1) // size
    return r == c


def _unit_lower_inverse(a):
    row = lax.broadcasted_iota(i32, (CHUNK, CHUNK), 0)
    col = lax.broadcasted_iota(i32, (CHUNK, CHUNK), 1)
    eye = (row == col).astype(f32)
    base = 8
    m_prev = _block_mask(base)
    ad = jnp.where(m_prev, a, 0.0)
    adb = ad.astype(bf16)
    a2 = _dot(adb, adb)
    a2b = a2.astype(bf16)
    a4 = _dot(a2b, a2b)
    x = _dot((eye - ad).astype(bf16), (eye + a2).astype(bf16))
    x = _dot(x.astype(bf16), (eye + a4).astype(bf16))
    size = base * 2
    while size <= CHUNK:
        m_cur = _block_mask(size)
        b = jnp.where(jnp.logical_and(m_cur, jnp.logical_not(m_prev)), a, 0.0)
        xb = x.astype(bf16)
        x = x - _dot(_dot(xb, b.astype(bf16)).astype(bf16), xb)
        m_prev = m_cur
        size *= 2
    return x


def _gdn_kernel(q_ref, k_ref, kt_ref, v_ref, z_ref, ng_ref,
                gc0_ref, gc1_ref, gr0_ref, gr1_ref, bc0_ref, bc1_ref, br0_ref, br1_ref,
                o_ref, s_ref, *, rb):
    r = pl.program_id(1)

    @pl.when(r == 0)
    def _():
        s_ref[...] = jnp.zeros_like(s_ref)

    gcs, grs = (gc0_ref, gc1_ref), (gr0_ref, gr1_ref)
    bcs, brs = (bc0_ref, bc1_ref), (br0_ref, br1_ref)
    row = lax.broadcasted_iota(i32, (CHUNK, CHUNK), 0)
    col = lax.broadcasted_iota(i32, (CHUNK, CHUNK), 1)
    incl = row >= col
    strict = row > col
    ng = ng_ref[...]

    def chunk(c, carry):
        r0 = pl.multiple_of(c * CHUNK, CHUNK)
        q = q_ref[pl.ds(r0, CHUNK), :]
        k = k_ref[pl.ds(r0, CHUNK), :]
        kt = kt_ref[:, pl.ds(r0, CHUNK)]
        kk = _dot(k, kt)
        qk = _dot(q, kt)
        for e in range(2):
            gc = gcs[e][pl.ds(r0, CHUNK), :]
            gr = grs[e][:, pl.ds(r0, CHUNK)]
            bc = bcs[e][pl.ds(r0, CHUNK), :]
            br = brs[e][:, pl.ds(r0, CHUNK)]
            dec = jnp.where(incl, jnp.exp(jnp.where(incl, gc - gr, 0.0)), 0.0)
            a = jnp.where(strict, bc * kk * dec, 0.0)
            t = _unit_lower_inverse(a)
            v = v_ref[pl.ds(r0, CHUNK), e * HEAD_DIM:(e + 1) * HEAD_DIM]
            u = _dot((t * br).astype(bf16), v)
            w = _dot((t * (br * jnp.exp(gr))).astype(bf16), k)
            s = s_ref[e]
            sb = s.astype(bf16)
            v_new = u - _dot(w.astype(bf16), sb)
            vb = v_new.astype(bf16)
            attn = jnp.where(incl, qk * dec, 0.0)
            o = jnp.exp(gc) * _dot(q, sb) + _dot(attn.astype(bf16), vb)
            g_last = gr[:, CHUNK - 1:CHUNK]
            kdt = (kt.astype(f32) * jnp.exp(g_last - gr)).astype(bf16)
            s_ref[e] = s * jnp.exp(g_last) + _dot(kdt, vb)
            on = _rms(o, ng)
            z = z_ref[pl.ds(r0, CHUNK), e * HEAD_DIM:(e + 1) * HEAD_DIM].astype(f32)
            o_ref[pl.ds(r0, CHUNK), e * HEAD_DIM:(e + 1) * HEAD_DIM] = (
                on * (z * _sigmoid(z))).astype(o_ref.dtype)
        return carry

    lax.fori_loop(0, rb // CHUNK, chunk, 0)


def gdn_core(qkv, kt, proj, norm_g, gcol, grow, bcol, brow):
    lp = qkv.shape[0]
    rb = _pick(lp, (640, 256, 128))
    nqk = GDN_QK_HEADS
    kcol0 = GDN_QK_W // HEAD_DIM
    vcol0 = 2 * GDN_QK_W // (2 * HEAD_DIM)
    zcol0 = COL_GZ // (2 * HEAD_DIM)

    def colspec(e):
        return pl.BlockSpec((None, rb, 1), lambda j, r: (2 * j + e, r, 0))

    def rowspec(e):
        return pl.BlockSpec((None, 1, rb), lambda j, r: (2 * j + e, 0, r))

    kern = functools.partial(_gdn_kernel, rb=rb)
    return pl.pallas_call(
        kern,
        grid=(nqk, lp // rb),
        in_specs=[pl.BlockSpec((rb, HEAD_DIM), lambda j, r: (r, j)),
                  pl.BlockSpec((rb, HEAD_DIM), lambda j, r: (r, kcol0 + j)),
                  pl.BlockSpec((HEAD_DIM, rb), lambda j, r: (j, r)),
                  pl.BlockSpec((rb, 2 * HEAD_DIM), lambda j, r: (r, vcol0 + j)),
                  pl.BlockSpec((rb, 2 * HEAD_DIM), lambda j, r: (r, zcol0 + j)),
                  pl.BlockSpec((1, HEAD_DIM), lambda j, r: (0, 0)),
                  colspec(0), colspec(1), rowspec(0), rowspec(1),
                  colspec(0), colspec(1), rowspec(0), rowspec(1)],
        out_specs=pl.BlockSpec((rb, 2 * HEAD_DIM), lambda j, r: (r, j)),
        out_shape=jax.ShapeDtypeStruct((lp, GDN_V_W), bf16),
        scratch_shapes=[pltpu.VMEM((2, HEAD_DIM, HEAD_DIM), f32)],
        compiler_params=_cparams(("parallel", "arbitrary")),
        name="gdn_core",
    )(qkv, qkv, kt, qkv, proj, norm_g.reshape(1, HEAD_DIM),
      gcol, gcol, grow, grow, bcol, bcol, brow, brow)


def _rope_kernel(x_ref, c_ref, s1_ref, s2_ref, o_ref, *, nh, scale, head_major):
    cc = c_ref[...]
    s1 = s1_ref[...]
    s2 = s2_ref[...]
    half = ROPE_DIMS // 2
    for h in range(nh):
        seg = x_ref[:, h * HEAD_DIM:(h + 1) * HEAD_DIM].astype(f32)
        y = (seg * cc + pltpu.roll(seg, half, 1) * s1
             + pltpu.roll(seg, HEAD_DIM - half, 1) * s2)
        if scale != 1.0:
            y = y * scale
        if head_major:
            o_ref[h] = y.astype(o_ref.dtype)
        else:
            o_ref[:, h * HEAD_DIM:(h + 1) * HEAD_DIM] = y.astype(o_ref.dtype)


def rope(proj, col0, nh, tabs, *, scale=1.0, head_major=False):
    lp = proj.shape[0]
    tm = _pick(lp, (640, 256, 128))
    w = nh * HEAD_DIM
    assert col0 % w == 0
    tab = pl.BlockSpec((tm, HEAD_DIM), lambda i: (i, 0))
    if head_major:
        out_spec = pl.BlockSpec((nh, tm, HEAD_DIM), lambda i: (0, i, 0))
        out_shape = jax.ShapeDtypeStruct((nh, lp, HEAD_DIM), bf16)
    else:
        out_spec = pl.BlockSpec((tm, w), lambda i: (i, 0))
        out_shape = jax.ShapeDtypeStruct((lp, w), bf16)
    kern = functools.partial(_rope_kernel, nh=nh, scale=scale, head_major=head_major)
    return pl.pallas_call(
        kern,
        grid=(lp // tm,),
        in_specs=[pl.BlockSpec((tm, w), lambda i: (i, col0 // w)), tab, tab, tab],
        out_specs=out_spec,
        out_shape=out_shape,
        compiler_params=_cparams(("parallel",)),
        name=f"rope_{col0}",
    )(proj, *tabs)


def rope_tables(lp):
    half = ROPE_DIMS // 2
    pos = (jnp.arange(lp, dtype=jnp.int32) - FRAME_OFF).astype(f32)
    inv = ROPE_THETA ** (-jnp.arange(half, dtype=f32) / half)
    ang = pos[:, None] * inv[None, :]
    cos, sin = jnp.cos(ang), jnp.sin(ang)
    zeros = jnp.zeros((lp, HEAD_DIM - ROPE_DIMS), f32)
    z16 = jnp.zeros((lp, half), f32)
    cc = jnp.concatenate([cos, cos, jnp.ones_like(zeros)], axis=1)
    s1 = jnp.concatenate([z16, sin, zeros], axis=1)
    s2 = jnp.concatenate([-sin, z16, zeros], axis=1)
    return cc, s1, s2


INT_MIN = -2 ** 31
INT_MAX = 2 ** 31 - 1


def _sortable(x):
    b = lax.bitcast_convert_type(x, i32)
    return b ^ (lax.shift_right_arithmetic(b, 31) & INT_MAX)


_NEG_BITS = struct.unpack("<i", struct.pack("<f", NEG))[0]
_KEY_NEG = _NEG_BITS ^ ((_NEG_BITS >> 31) & INT_MAX)


def _select_kernel(iq_ref, ik_ref, iw_ref, bias_ref, key_ref, *, tq, ts, lp, topk):
    i = pl.program_id(0)
    nkb = ((i + 1) * tq + ts - 1) // ts
    rem = lp - nkb * ts
    iw = iw_ref[...] * ((IDX_HEADS ** -0.5) * (HEAD_DIM ** -0.5))
    qpos = i * tq + lax.broadcasted_iota(i32, (tq, 1), 0)
    lane = lax.broadcasted_iota(i32, (1, ts), 1)

    def score_blk(kb, carry):
        k0 = pl.multiple_of(kb * ts, ts)
        ikb = ik_ref[pl.ds(k0, ts), :]
        acc = jnp.zeros((tq, ts), f32)
        for h in range(IDX_HEADS):
            lg = _dot_nt(iq_ref[h], ikb)
            acc = acc + jnp.maximum(lg, 0.0) * iw[:, h:h + 1]
        kpos = k0 + lane
        sc = jnp.where(kpos <= qpos, acc, NEG)
        sc = jnp.where(kpos < FRAME_OFF, -jnp.inf, sc)
        key_ref[:, pl.ds(k0, ts)] = _sortable(sc)
        return carry

    lax.fori_loop(0, nkb, score_blk, 0)

    def count(pred_fn):
        def blk(kb, cnt):
            k0 = pl.multiple_of(kb * ts, ts)
            m = pred_fn(key_ref[:, pl.ds(k0, ts)], k0).astype(i32)
            for j in range(ts // LANES):
                cnt = cnt + m[:, j * LANES:(j + 1) * LANES]
            return cnt
        cnt = lax.fori_loop(0, nkb, blk, jnp.zeros((tq, LANES), i32))
        return jnp.sum(cnt, axis=1, keepdims=True)

    def count_ge(cand):
        return count(lambda kv, k0: kv >= cand) + jnp.where(cand <= _KEY_NEG, rem, 0)

    def bis(it, tau):
        bit = lax.shift_left(jnp.int32(1), 31 - it)
        t2 = tau | bit
        return jnp.where(count_ge(t2 ^ INT_MIN) >= topk, t2, tau)

    tau = lax.fori_loop(0, 32, bis, jnp.zeros((tq, 1), i32))
    thr = tau ^ INT_MIN
    n_ge = count_ge(thr)
    n_gt = (count(lambda kv, k0: kv > thr) + jnp.where(thr < _KEY_NEG, rem, 0))
    need = topk - n_gt
    has_tie = jnp.max(jnp.where(n_ge > topk, 1, 0)) > 0

    def tie_cut():
        def bis2(it, cut):
            bit = lax.shift_left(jnp.int32(1), 14 - it)
            c2 = cut | bit
            n = count(lambda kv, k0: jnp.logical_and(kv == thr, (k0 + lane) < c2))
            return jnp.where(n < need, c2, cut)
        return lax.fori_loop(0, 15, bis2, jnp.zeros((tq, 1), i32))

    cut = lax.cond(has_tie, tie_cut, lambda: jnp.full((tq, 1), INT_MAX, i32))

    def bias_blk(kb, carry):
        k0 = pl.multiple_of(kb * ts, ts)
        kv = key_ref[:, pl.ds(k0, ts)]
        kpos = k0 + lane
        sel = jnp.logical_or(kv > thr, jnp.logical_and(kv == thr, kpos <= cut))
        vis = jnp.logical_and(kpos <= qpos, kpos >= FRAME_OFF)
        bias_ref[:, pl.ds(k0, ts)] = jnp.where(jnp.logical_and(sel, vis), 0.0, NEG)
        return carry

    lax.fori_loop(0, nkb, bias_blk, 0)

    def fill_blk(kb, carry):
        k0 = pl.multiple_of(kb * ts, ts)
        bias_ref[:, pl.ds(k0, ts)] = jnp.full((tq, ts), NEG, f32)
        return carry

    lax.fori_loop(nkb, lp // ts, fill_blk, 0)


def dsa_select(iq, ik, iw, topk):
    nh, lp, _ = iq.shape
    tq = 128
    ts = _pick(lp, (640, 256, 128))
    kern = functools.partial(_select_kernel, tq=tq, ts=ts, lp=lp, topk=topk)
    return pl.pallas_call(
        kern,
        grid=(lp // tq,),
        in_specs=[pl.BlockSpec((nh, tq, HEAD_DIM), lambda i: (0, i, 0)),
                  pl.BlockSpec((lp, HEAD_DIM), lambda i: (0, 0)),
                  pl.BlockSpec((tq, nh), lambda i: (i, 0))],
        out_specs=pl.BlockSpec((tq, lp), lambda i: (i, 0)),
        out_shape=jax.ShapeDtypeStruct((lp, lp), f32),
        scratch_shapes=[pltpu.VMEM((tq, lp), i32)],
        compiler_params=_cparams(("parallel",)),
        name="dsa_select",
    )(iq, ik, iw)


def _attn_kernel(q_ref, k_ref, v_ref, b_ref, o_ref, m_ref, l_ref, acc_ref, *, tq, ts):
    i = pl.program_id(0)
    nkb = ((i + 1) * tq + ts - 1) // ts
    group = ATT_HEADS // ATT_KV_HEADS
    qpos = i * tq + lax.broadcasted_iota(i32, (tq, 1), 0)
    for g in range(ATT_KV_HEADS):
        qg = jnp.concatenate(
            [q_ref[:, (g * group + r) * HEAD_DIM:(g * group + r + 1) * HEAD_DIM]
             for r in range(group)], axis=0)
        m_ref[...] = jnp.full(m_ref.shape, -jnp.inf, f32)
        l_ref[...] = jnp.zeros_like(l_ref)
        acc_ref[...] = jnp.zeros_like(acc_ref)

        def blk(kb, carry):
            k0 = pl.multiple_of(kb * ts, ts)
            kb_ = k_ref[pl.ds(k0, ts), g * HEAD_DIM:(g + 1) * HEAD_DIM]
            vb_ = v_ref[pl.ds(k0, ts), g * HEAD_DIM:(g + 1) * HEAD_DIM]
            bias = b_ref[:, pl.ds(k0, ts)]
            s = _dot_nt(qg, kb_) + jnp.concatenate([bias] * group, axis=0)
            m_old = m_ref[...]
            m_new = jnp.maximum(m_old, jnp.max(s, axis=-1, keepdims=True))
            alpha = jnp.exp(m_old - m_new)
            p = jnp.exp(s - m_new)
            l_ref[...] = alpha * l_ref[...] + jnp.sum(p, axis=-1, keepdims=True)
            acc_ref[...] = alpha * acc_ref[...] + _dot(p.astype(bf16), vb_)
            m_ref[...] = m_new
            return carry

        lax.fori_loop(0, nkb, blk, 0)
        o = acc_ref[...] / l_ref[...]
        for r in range(group):
            orr = jnp.where(qpos >= FRAME_OFF, o[r * tq:(r + 1) * tq], 0.0)
            o_ref[:, (g * group + r) * HEAD_DIM:(g * group + r + 1) * HEAD_DIM] = orr.astype(o_ref.dtype)


def dsa_attention(q, k, v, bias):
    lp = q.shape[0]
    tq = 128
    ts = _pick(lp, (640, 256, 128))
    group = ATT_HEADS // ATT_KV_HEADS
    kern = functools.partial(_attn_kernel, tq=tq, ts=ts)
    return pl.pallas_call(
        kern,
        grid=(lp // tq,),
        in_specs=[pl.BlockSpec((tq, ATT_Q_W), lambda i: (i, 0)),
                  pl.BlockSpec((lp, ATT_KV_W), lambda i: (0, 0)),
                  pl.BlockSpec((lp, ATT_KV_W), lambda i: (0, 0)),
                  pl.BlockSpec((tq, lp), lambda i: (i, 0))],
        out_specs=pl.BlockSpec((tq, ATT_Q_W), lambda i: (i, 0)),
        out_shape=jax.ShapeDtypeStruct((lp, ATT_Q_W), bf16),
        scratch_shapes=[pltpu.VMEM((group * tq, 1), f32),
                        pltpu.VMEM((group * tq, 1), f32),
                        pltpu.VMEM((group * tq, HEAD_DIM), f32)],
        compiler_params=_cparams(("parallel",)),
        name="dsa_attention",
    )(q, k, v, bias)


def _ffn_conv_kernel(g_ref, gh_ref, v_ref, vh_ref, wg_ref, wv_ref, bg_ref, bv_ref, o_ref,
                     eg_ref, ev_ref, *, tm):
    i = pl.program_id(0)

    def conv(x_ref, h_ref, w_ref, b_ref, ext_ref):
        ext_ref[0:HALO, :] = jnp.where(i > 0, h_ref[...].astype(f32), 0.0)
        ext_ref[HALO:HALO + tm, :] = x_ref[...].astype(f32)
        w = w_ref[...]
        y = ext_ref[HALO:HALO + tm, :] * w[FFN_CONV - 1:FFN_CONV, :]
        for j in range(FFN_CONV - 1):
            s0 = HALO - (FFN_CONV - 1) + j
            y = y + ext_ref[s0:s0 + tm, :] * w[j:j + 1, :]
        return y + b_ref[...]

    gate = conv(g_ref, gh_ref, wg_ref, bg_ref, eg_ref)
    val = conv(v_ref, vh_ref, wv_ref, bv_ref, ev_ref)
    o_ref[...] = (gate * _sigmoid(gate) * val).astype(o_ref.dtype)


def ffn_conv(up, conv_w, conv_b):
    lp = up.shape[0]
    tm = _pick(lp, (640, 256, 128))
    tc = 512
    nc = D_FF // tc
    kern = functools.partial(_ffn_conv_kernel, tm=tm)

    def main(off):
        return pl.BlockSpec((tm, tc), lambda i, c: (i, c + off))

    def halo(off):
        return pl.BlockSpec((HALO, tc), lambda i, c: (jnp.maximum(i * (tm // HALO) - 1, 0), c + off))

    def wspec(off):
        return pl.BlockSpec((FFN_CONV, tc), lambda i, c: (0, c + off))

    def bspec(off):
        return pl.BlockSpec((1, tc), lambda i, c: (0, c + off))

    b2 = conv_b.reshape(1, 2 * D_FF)
    return pl.pallas_call(
        kern,
        grid=(lp // tm, nc),
        in_specs=[main(0), halo(0), main(nc), halo(nc), wspec(0), wspec(nc), bspec(0), bspec(nc)],
        out_specs=pl.BlockSpec((tm, tc), lambda i, c: (i, c)),
        out_shape=jax.ShapeDtypeStruct((lp, D_FF), bf16),
        scratch_shapes=[pltpu.VMEM((HALO + tm, tc), f32), pltpu.VMEM((HALO + tm, tc), f32)],
        compiler_params=_cparams(("parallel", "parallel")),
        name="ffn_conv",
    )(up, up, up, up, conv_w, conv_w, b2, b2)


def _split_w_in(w):
    o = 0
    parts = {}
    for name, width in (("gq", GDN_QK_W), ("gk", GDN_QK_W), ("gv", GDN_V_W), ("gz", GDN_V_W),
                        ("gb", GDN_V_HEADS), ("ga", GDN_V_HEADS), ("aq", ATT_Q_W), ("ak", ATT_KV_W),
                        ("av", ATT_KV_W), ("iq", IDX_Q_W), ("ik", HEAD_DIM), ("iw", IDX_HEADS),
                        ("gate_gdn", D_MODEL), ("gate_att", D_MODEL)):
        parts[name] = w[:, o:o + width]
        o += width
    big = jnp.concatenate([parts[n] for n in ("gq", "gk", "gv", "gz", "aq", "iq", "gate_gdn",
                                              "gate_att", "ak", "av", "ik")], axis=1).astype(bf16)
    small = jnp.concatenate([parts["gb"], parts["ga"], parts["iw"]], axis=1)
    small = jnp.pad(small, ((0, 0), (0, LANES - small.shape[1]))).astype(bf16)
    return big, small


def _layer(h0, p, lp, topk):
    tm = _pick(lp, (640, 256, 128))
    tm_big = _pick(lp, (1280, 640, 256, 128))
    w_big, w_small = _split_w_in(p["w_in"])

    u1 = rms_rows(h0, p["mix_pre_g"])
    (proj,) = matmul(u1, w_big, tm=tm, tn=1920, tk=D_MODEL, epilogue=_ep_cast,
                     out_shapes=[jax.ShapeDtypeStruct((lp, PROJ_W), bf16)], name="proj_in")
    (small,) = matmul(u1, w_small, tm=tm_big, tn=LANES, tk=D_MODEL, epilogue=_ep_cast,
                      out_shapes=[jax.ShapeDtypeStruct((lp, LANES), f32)], name="proj_small")
    gb = small[:, 0:GDN_V_HEADS]
    ga = small[:, GDN_V_HEADS:2 * GDN_V_HEADS]
    iw = small[:, 2 * GDN_V_HEADS:2 * GDN_V_HEADS + IDX_HEADS]

    qkv = gdn_prep(proj, p["gdn_conv_w"])
    kt = qkv[:, GDN_QK_W:2 * GDN_QK_W].T
    beta, gcum = gdn_gates(gb, ga, p["gdn_a_log"], p["gdn_dt_bias"])
    gcol = gcum.T[:, :, None]
    grow = gcum.T[:, None, :]
    bcol = beta.T[:, :, None]
    brow = beta.T[:, None, :]
    o_gdn = gdn_core(qkv, kt, proj, p["gdn_norm_g"], gcol, grow, bcol, brow)

    tabs = rope_tables(lp)
    aq = rope(proj, COL_AQ, ATT_HEADS, tabs, scale=HEAD_DIM ** -0.5)
    ak = rope(proj, COL_AK, ATT_KV_HEADS, tabs)
    iq = rope(proj, COL_IQ, IDX_HEADS, tabs, head_major=True)
    ik = rope(proj, COL_IK, 1, tabs)
    av = proj[:, COL_AV:COL_AV + ATT_KV_W]
    bias = dsa_select(iq, ik, iw, topk)
    o_att = dsa_attention(aq, ak, av, bias)

    tn = 512
    gate_spec = lambda col0: ((tm_big, tn), (lambda i, j, c=col0 // tn: (i, c + j)))
    (m1,) = matmul(o_gdn, p["w_branch_gdn"].astype(bf16), tm=tm_big, tn=tn, tk=2048,
                   epilogue=_ep_gate, extra=(proj,), extra_specs=(gate_spec(COL_GATE_GDN),),
                   out_shapes=[jax.ShapeDtypeStruct((lp, D_MODEL), f32)], name="branch_gdn")
    (merged,) = matmul(o_att, p["w_branch_att"].astype(bf16), tm=tm_big, tn=tn, tk=2048,
                       epilogue=_ep_gate_add, extra=(proj, m1),
                       extra_specs=(gate_spec(COL_GATE_ATT), ((tm_big, tn), lambda i, j: (i, j))),
                       out_shapes=[jax.ShapeDtypeStruct((lp, D_MODEL), bf16)], name="branch_att")
    row_spec = ((tm, D_MODEL), lambda i, j: (i, 0))
    vec_spec = ((1, D_MODEL), lambda i, j: (0, 0))
    h1, u2 = matmul(merged, p["w_out"].astype(bf16), tm=tm, tn=D_MODEL, tk=1024,
                    epilogue=_ep_res_norm2,
                    extra=(h0, p["mix_post_g"].reshape(1, D_MODEL), p["ffn_pre_g"].reshape(1, D_MODEL)),
                    extra_specs=(row_spec, vec_spec, vec_spec),
                    out_shapes=[jax.ShapeDtypeStruct((lp, D_MODEL), f32),
                                jax.ShapeDtypeStruct((lp, D_MODEL), bf16)], name="w_out")

    (up,) = matmul(u2, p["w_up"].astype(bf16), tm=tm_big, tn=1024, tk=D_MODEL, epilogue=_ep_cast,
                   out_shapes=[jax.ShapeDtypeStruct((lp, 2 * D_FF), bf16)], name="w_up")
    act = ffn_conv(up, p["ffn_conv_w"], p["ffn_conv_b"])
    (h2,) = matmul(act, p["w_down"].astype(bf16), tm=tm, tn=D_MODEL, tk=1024, epilogue=_ep_res_norm,
                   extra=(h1, p["ffn_post_g"].reshape(1, D_MODEL)),
                   extra_specs=(row_spec, vec_spec),
                   out_shapes=[jax.ShapeDtypeStruct((lp, D_MODEL), f32)], name="w_down")
    return h2


def kernel(x, meta_tokens, mix_pre_g, w_in, gdn_conv_w, gdn_a_log, gdn_dt_bias, gdn_norm_g,
           w_branch_gdn, w_branch_att, w_out, mix_post_g, ffn_pre_g, w_up, ffn_conv_w,
           ffn_conv_b, w_down, ffn_post_g):
    batch, seq, d = x.shape
    assert batch == 1 and d == D_MODEL
    lp = FRAME_X0 + seq
    topk = min(TOPK_MAX, (N_META + seq) // 4)
    h = jnp.concatenate([jnp.zeros((FRAME_OFF, d), x.dtype), meta_tokens.astype(x.dtype), x[0]], axis=0)
    for i in range(w_in.shape[0]):
        p = dict(mix_pre_g=mix_pre_g[i], w_in=w_in[i], gdn_conv_w=gdn_conv_w[i], gdn_a_log=gdn_a_log[i],
                 gdn_dt_bias=gdn_dt_bias[i], gdn_norm_g=gdn_norm_g[i], w_branch_gdn=w_branch_gdn[i],
                 w_branch_att=w_branch_att[i], w_out=w_out[i], mix_post_g=mix_post_g[i],
                 ffn_pre_g=ffn_pre_g[i], w_up=w_up[i], ffn_conv_w=ffn_conv_w[i],
                 ffn_conv_b=ffn_conv_b[i], w_down=w_down[i], ffn_post_g=ffn_post_g[i])
        h = _layer(h, p, lp, topk)
    return h[FRAME_X0:][None]
```

```python
import functools
import struct

import jax
import jax.numpy as jnp
from jax import lax
from jax.experimental import pallas as pl
from jax.experimental.pallas import tpu as pltpu

f32 = jnp.float32
bf16 = jnp.bfloat16
i32 = jnp.int32

D_MODEL = 2048
N_META = 16
EPS = 1e-6
GDN_QK_HEADS = 16
GDN_V_HEADS = 32
HEAD_DIM = 128
GDN_CONV = 4
ATT_HEADS = 16
ATT_KV_HEADS = 2
IDX_HEADS = 16
TOPK_MAX = 256
NEG = -1e30
ROPE_THETA = 500000.0
ROPE_DIMS = HEAD_DIM // 4
D_FF = 3 * D_MODEL
FFN_CONV = 3
GDN_QK_W = GDN_QK_HEADS * HEAD_DIM
GDN_V_W = GDN_V_HEADS * HEAD_DIM
ATT_Q_W = ATT_HEADS * HEAD_DIM
ATT_KV_W = ATT_KV_HEADS * HEAD_DIM
IDX_Q_W = IDX_HEADS * HEAD_DIM

FRAME_X0 = 256
FRAME_OFF = FRAME_X0 - N_META
CHUNK = 128

LANES = 128
VMEM_LIMIT = 56 * 1024 * 1024

COL_GQKV = 0
COL_GZ = COL_GQKV + 2 * GDN_QK_W + GDN_V_W
COL_AQ = COL_GZ + GDN_V_W
COL_IQ = COL_AQ + ATT_Q_W
COL_GATE_GDN = COL_IQ + IDX_Q_W
COL_GATE_ATT = COL_GATE_GDN + D_MODEL
COL_AK = COL_GATE_ATT + D_MODEL
COL_AV = COL_AK + ATT_KV_W
COL_IK = COL_AV + ATT_KV_W
PROJ_W = COL_IK + HEAD_DIM


def _cparams(sem):
    return pltpu.CompilerParams(dimension_semantics=sem, vmem_limit_bytes=VMEM_LIMIT)


def _pick(n, cands):
    for c in cands:
        if n % c == 0:
            return c
    raise ValueError(f"no tile for {n} in {cands}")


def _sigmoid(x):
    return 1.0 / (1.0 + jnp.exp(-x))


def _dot(a, b):
    return jnp.dot(a, b, preferred_element_type=f32)


def _dot_nt(a, b):
    return lax.dot_general(a, b, (((1,), (1,)), ((), ())), preferred_element_type=f32)


def _rms_rows_kernel(h_ref, g_ref, o_ref):
    h = h_ref[...]
    y = h * lax.rsqrt(jnp.mean(h * h, axis=-1, keepdims=True) + EPS)
    o_ref[...] = (y * g_ref[...]).astype(o_ref.dtype)


def rms_rows(h, g):
    lp, d = h.shape
    tm = _pick(lp, (640, 256, 128))
    return pl.pallas_call(
        _rms_rows_kernel,
        grid=(lp // tm,),
        in_specs=[pl.BlockSpec((tm, d), lambda i: (i, 0)),
                  pl.BlockSpec((1, d), lambda i: (0, 0))],
        out_specs=pl.BlockSpec((tm, d), lambda i: (i, 0)),
        out_shape=jax.ShapeDtypeStruct((lp, d), bf16),
        compiler_params=_cparams(("parallel",)),
        name="rms_rows",
    )(h, g.reshape(1, d))


def _mm_kernel(*refs, nk, n_extra, n_out, epilogue):
    a_ref, w_ref = refs[0], refs[1]
    extra = refs[2:2 + n_extra]
    outs = refs[2 + n_extra:2 + n_extra + n_out]
    if nk == 1:
        epilogue(_dot(a_ref[...], w_ref[...]), extra, outs)
        return
    acc_ref = refs[-1]
    k = pl.program_id(2)

    @pl.when(k == 0)
    def _():
        acc_ref[...] = jnp.zeros_like(acc_ref)

    acc_ref[...] += _dot(a_ref[...], w_ref[...])

    @pl.when(k == nk - 1)
    def _():
        epilogue(acc_ref[...], extra, outs)


def matmul(a, w, *, tm, tn, tk, epilogue, extra=(), extra_specs=(), out_shapes, a_col0=0, name):
    m = a.shape[0]
    kdim, n = w.shape
    nk = kdim // tk
    assert m % tm == 0 and n % tn == 0 and kdim % tk == 0
    in_specs = [pl.BlockSpec((tm, tk), lambda i, j, k: (i, a_col0 + k)),
                pl.BlockSpec((tk, tn), lambda i, j, k: (k, j))]
    in_specs += [pl.BlockSpec(bs, (lambda i, j, k, f=f: f(i, j))) for bs, f in extra_specs]
    out_specs = [pl.BlockSpec((tm, tn), lambda i, j, k: (i, j)) for _ in out_shapes]
    scratch = [] if nk == 1 else [pltpu.VMEM((tm, tn), f32)]
    kern = functools.partial(_mm_kernel, nk=nk, n_extra=len(extra), n_out=len(out_shapes),
                             epilogue=epilogue)
    res = pl.pallas_call(
        kern,
        grid=(m // tm, n // tn, nk),
        in_specs=in_specs,
        out_specs=out_specs,
        out_shape=out_shapes,
        scratch_shapes=scratch,
        compiler_params=_cparams(("parallel", "parallel", "arbitrary")),
        name=name,
    )(a, w, *extra)
    return res


def _ep_cast(acc, extra, outs):
    outs[0][...] = acc.astype(outs[0].dtype)


def _ep_gate(acc, extra, outs):
    g = extra[0][...].astype(f32)
    outs[0][...] = (_sigmoid(g) * acc).astype(outs[0].dtype)


def _ep_gate_add(acc, extra, outs):
    g = extra[0][...].astype(f32)
    outs[0][...] = (extra[1][...].astype(f32) + _sigmoid(g) * acc).astype(outs[0].dtype)


def _rms(t, g):
    return t * lax.rsqrt(jnp.mean(t * t, axis=-1, keepdims=True) + EPS) * g


def _ep_res_norm2(acc, extra, outs):
    h_ref, g_ref, g2_ref = extra
    h1 = h_ref[...] + _rms(acc, g_ref[...])
    outs[0][...] = h1
    outs[1][...] = _rms(h1, g2_ref[...]).astype(outs[1].dtype)


def _ep_res_norm(acc, extra, outs):
    h_ref, g_ref = extra
    outs[0][...] = h_ref[...] + _rms(acc, g_ref[...])


HALO = 16


def _gdn_prep_kernel(x_ref, halo_ref, w_ref, o_ref, ext_ref, *, tm, tc):
    i = pl.program_id(0)
    c = pl.program_id(1)
    halo = halo_ref[...].astype(f32)
    ext_ref[0:HALO, :] = jnp.where(i > 0, halo, 0.0)
    ext_ref[HALO:HALO + tm, :] = x_ref[...].astype(f32)
    w = w_ref[...]
    y = ext_ref[HALO:HALO + tm, :] * w[GDN_CONV - 1:GDN_CONV, :]
    for j in range(GDN_CONV - 1):
        s0 = HALO - (GDN_CONV - 1) + j
        y = y + ext_ref[s0:s0 + tm, :] * w[j:j + 1, :]
    s = y * _sigmoid(y)
    is_q = c < (GDN_QK_W // tc)
    is_qk = c < (2 * GDN_QK_W // tc)
    qscale = jnp.where(is_q, HEAD_DIM ** -0.5, 1.0).astype(f32)
    for hh in range(tc // HEAD_DIM):
        seg = s[:, hh * HEAD_DIM:(hh + 1) * HEAD_DIM]
        r = lax.rsqrt(jnp.sum(seg * seg, axis=-1, keepdims=True) + EPS) * qscale
        fac = jnp.where(is_qk, r, 1.0)
        o_ref[:, hh * HEAD_DIM:(hh + 1) * HEAD_DIM] = (seg * fac).astype(o_ref.dtype)


def gdn_prep(proj, conv_w):
    lp = proj.shape[0]
    width = 2 * GDN_QK_W + GDN_V_W
    tm = _pick(lp, (640, 256, 128))
    tc = 512
    kern = functools.partial(_gdn_prep_kernel, tm=tm, tc=tc)
    return pl.pallas_call(
        kern,
        grid=(lp // tm, width // tc),
        in_specs=[pl.BlockSpec((tm, tc), lambda i, c: (i, c)),
                  pl.BlockSpec((HALO, tc), lambda i, c: (jnp.maximum(i * (tm // HALO) - 1, 0), c)),
                  pl.BlockSpec((GDN_CONV, tc), lambda i, c: (0, c))],
        out_specs=pl.BlockSpec((tm, tc), lambda i, c: (i, c)),
        out_shape=jax.ShapeDtypeStruct((lp, width), bf16),
        scratch_shapes=[pltpu.VMEM((HALO + tm, tc), f32)],
        compiler_params=_cparams(("parallel", "parallel")),
        name="gdn_prep",
    )(proj, proj, conv_w)


def _gdn_gates_kernel(b_ref, a_ref, alog_ref, dt_ref, beta_ref, gcum_ref):
    i = pl.program_id(0)
    rows = i * CHUNK + lax.broadcasted_iota(i32, (CHUNK, 1), 0)
    valid = rows >= FRAME_OFF
    beta_ref[...] = jnp.where(valid, _sigmoid(b_ref[...]), 0.0)
    a = a_ref[...] + dt_ref[...]
    sp = jnp.maximum(a, 0.0) + jnp.log1p(jnp.exp(-jnp.abs(a)))
    g = jnp.where(valid, -jnp.exp(alog_ref[...]) * sp, 0.0)
    tri = (lax.broadcasted_iota(i32, (CHUNK, CHUNK), 0)
           >= lax.broadcasted_iota(i32, (CHUNK, CHUNK), 1)).astype(f32)
    gcum_ref[...] = jnp.dot(tri, g, preferred_element_type=f32, precision=lax.Precision.HIGHEST)


def gdn_gates(gb, ga, a_log, dt_bias):
    lp, nh = gb.shape
    spec = pl.BlockSpec((CHUNK, nh), lambda i: (i, 0))
    vec = pl.BlockSpec((1, nh), lambda i: (0, 0))
    return pl.pallas_call(
        _gdn_gates_kernel,
        grid=(lp // CHUNK,),
        in_specs=[spec, spec, vec, vec],
        out_specs=[spec, spec],
        out_shape=[jax.ShapeDtypeStruct((lp, nh), f32)] * 2,
        compiler_params=_cparams(("parallel",)),
        name="gdn_gates",
    )(gb, ga, a_log.reshape(1, nh), dt_bias.reshape(1, nh))


def _block_mask(size):
    r = lax.broadcasted_iota(i32, (CHUNK, CHUNK), 0) // size
    c = lax.broadcasted_iota(i32, (CHUNK, CHUNK), 1) // size
    return r == c


def _unit_lower_inverse_many(mats):
    row = lax.broadcasted_iota(i32, (CHUNK, CHUNK), 0)
    col = lax.broadcasted_iota(i32, (CHUNK, CHUNK), 1)
    eye = (row == col).astype(f32)
    base = 8
    m_prev = _block_mask(base)
    ads = [jnp.where(m_prev, a, 0.0) for a in mats]
    adbs = [ad.astype(bf16) for ad in ads]
    a2s = [_dot(x, x) for x in adbs]
    a2bs = [x.astype(bf16) for x in a2s]
    a4s = [_dot(x, x) for x in a2bs]
    xs = [_dot((eye - ad).astype(bf16), (eye + a2).astype(bf16)) for ad, a2 in zip(ads, a2s)]
    xs = [_dot(x.astype(bf16), (eye + a4).astype(bf16)) for x, a4 in zip(xs, a4s)]
    size = base * 2
    while size <= CHUNK:
        m_cur = _block_mask(size)
        off_diag = jnp.logical_and(m_cur, jnp.logical_not(m_prev))
        bs = [jnp.where(off_diag, a, 0.0).astype(bf16) for a in mats]
        xbs = [x.astype(bf16) for x in xs]
        ys = [_dot(xb, b).astype(bf16) for xb, b in zip(xbs, bs)]
        xs = [x - _dot(y, xb) for x, y, xb in zip(xs, ys, xbs)]
        m_prev = m_cur
        size *= 2
    return xs


def _gdn_kernel(q_ref, k_ref, kt_ref, v_ref, z_ref, ng_ref,
                gc0_ref, gc1_ref, gr0_ref, gr1_ref, bc0_ref, bc1_ref, br0_ref, br1_ref,
                o_ref, s_ref, *, rb):
    r = pl.program_id(1)

    @pl.when(r == 0)
    def _():
        s_ref[...] = jnp.zeros_like(s_ref)

    gcs, grs = (gc0_ref, gc1_ref), (gr0_ref, gr1_ref)
    bcs, brs = (bc0_ref, bc1_ref), (br0_ref, br1_ref)
    row = lax.broadcasted_iota(i32, (CHUNK, CHUNK), 0)
    col = lax.broadcasted_iota(i32, (CHUNK, CHUNK), 1)
    incl = row >= col
    strict = row > col
    ng = ng_ref[...]
    nchunk = rb // CHUNK
    heads = range(2)

    def rows(c):
        return slice(c * CHUNK, (c + 1) * CHUNK)

    def lanes(e):
        return slice(e * HEAD_DIM, (e + 1) * HEAD_DIM)

    qs = [q_ref[rows(c), :] for c in range(nchunk)]
    ks = [k_ref[rows(c), :] for c in range(nchunk)]
    kts = [kt_ref[:, rows(c)] for c in range(nchunk)]
    kks = [_dot(k, kt) for k, kt in zip(ks, kts)]
    qks = [_dot(q, kt) for q, kt in zip(qs, kts)]
    chains = [(c, e) for c in range(nchunk) for e in heads]
    gc = {ce: gcs[ce[1]][rows(ce[0]), :] for ce in chains}
    gr = {ce: grs[ce[1]][:, rows(ce[0])] for ce in chains}
    bc = {ce: bcs[ce[1]][rows(ce[0]), :] for ce in chains}
    br = {ce: brs[ce[1]][:, rows(ce[0])] for ce in chains}
    dec = {ce: jnp.where(incl, jnp.exp(jnp.where(incl, gc[ce] - gr[ce], 0.0)), 0.0) for ce in chains}
    amat = [jnp.where(strict, bc[ce] * kks[ce[0]] * dec[ce], 0.0) for ce in chains]
    tinv = dict(zip(chains, _unit_lower_inverse_many(amat)))
    u = {ce: _dot((tinv[ce] * br[ce]).astype(bf16), v_ref[rows(ce[0]), lanes(ce[1])]) for ce in chains}
    w = {ce: _dot((tinv[ce] * (br[ce] * jnp.exp(gr[ce]))).astype(bf16), ks[ce[0]]) for ce in chains}
    g_last = {ce: gr[ce][:, CHUNK - 1:CHUNK] for ce in chains}
    wq = {ce: jnp.concatenate([w[ce].astype(bf16), qs[ce[0]]], axis=0) for ce in chains}
    ak = {ce: jnp.concatenate(
        [jnp.where(incl, qks[ce[0]] * dec[ce], 0.0).astype(bf16),
         (kts[ce[0]].astype(f32) * jnp.exp(g_last[ce] - gr[ce])).astype(bf16)], axis=0) for ce in chains}
    eg = {ce: jnp.exp(gc[ce]) for ce in chains}
    egl = {ce: jnp.exp(g_last[ce]) for ce in chains}

    state = [s_ref[e] for e in heads]
    for c in range(nchunk):
        sb = [state[e].astype(bf16) for e in heads]
        ws = [_dot(wq[(c, e)], sb[e]) for e in heads]
        vb = [(u[(c, e)] - ws[e][:CHUNK]).astype(bf16) for e in heads]
        av = [_dot(ak[(c, e)], vb[e]) for e in heads]
        for e in heads:
            o = eg[(c, e)] * ws[e][CHUNK:] + av[e][:CHUNK]
            state[e] = state[e] * egl[(c, e)] + av[e][CHUNK:]
            z = z_ref[rows(c), lanes(e)].astype(f32)
            o_ref[rows(c), lanes(e)] = (_rms(o, ng) * (z * _sigmoid(z))).astype(o_ref.dtype)
    for e in heads:
        s_ref[e] = state[e]


def gdn_core(qkv, kt, proj, norm_g, gcol, grow, bcol, brow):
    lp = qkv.shape[0]
    rb = _pick(lp, (640, 256, 128))
    nqk = GDN_QK_HEADS
    kcol0 = GDN_QK_W // HEAD_DIM
    vcol0 = 2 * GDN_QK_W // (2 * HEAD_DIM)
    zcol0 = COL_GZ // (2 * HEAD_DIM)

    def colspec(e):
        return pl.BlockSpec((None, rb, 1), lambda j, r: (2 * j + e, r, 0))

    def rowspec(e):
        return pl.BlockSpec((None, 1, rb), lambda j, r: (2 * j + e, 0, r))

    kern = functools.partial(_gdn_kernel, rb=rb)
    return pl.pallas_call(
        kern,
        grid=(nqk, lp // rb),
        in_specs=[pl.BlockSpec((rb, HEAD_DIM), lambda j, r: (r, j)),
                  pl.BlockSpec((rb, HEAD_DIM), lambda j, r: (r, kcol0 + j)),
                  pl.BlockSpec((HEAD_DIM, rb), lambda j, r: (j, r)),
                  pl.BlockSpec((rb, 2 * HEAD_DIM), lambda j, r: (r, vcol0 + j)),
                  pl.BlockSpec((rb, 2 * HEAD_DIM), lambda j, r: (r, zcol0 + j)),
                  pl.BlockSpec((1, HEAD_DIM), lambda j, r: (0, 0)),
                  colspec(0), colspec(1), rowspec(0), rowspec(1),
                  colspec(0), colspec(1), rowspec(0), rowspec(1)],
        out_specs=pl.BlockSpec((rb, 2 * HEAD_DIM), lambda j, r: (r, j)),
        out_shape=jax.ShapeDtypeStruct((lp, GDN_V_W), bf16),
        scratch_shapes=[pltpu.VMEM((2, HEAD_DIM, HEAD_DIM), f32)],
        compiler_params=_cparams(("parallel", "arbitrary")),
        name="gdn_core",
    )(qkv, qkv, kt, qkv, proj, norm_g.reshape(1, HEAD_DIM),
      gcol, gcol, grow, grow, bcol, bcol, brow, brow)


def _rope_kernel(x_ref, c_ref, s1_ref, s2_ref, o_ref, *, nh, scale, head_major):
    cc = c_ref[...]
    s1 = s1_ref[...]
    s2 = s2_ref[...]
    half = ROPE_DIMS // 2
    for h in range(nh):
        seg = x_ref[:, h * HEAD_DIM:(h + 1) * HEAD_DIM].astype(f32)
        y = (seg * cc + pltpu.roll(seg, half, 1) * s1
             + pltpu.roll(seg, HEAD_DIM - half, 1) * s2)
        if scale != 1.0:
            y = y * scale
        if head_major:
            o_ref[h] = y.astype(o_ref.dtype)
        else:
            o_ref[:, h * HEAD_DIM:(h + 1) * HEAD_DIM] = y.astype(o_ref.dtype)


def rope(proj, col0, nh, tabs, *, scale=1.0, head_major=False):
    lp = proj.shape[0]
    tm = _pick(lp, (640, 256, 128))
    w = nh * HEAD_DIM
    assert col0 % w == 0
    tab = pl.BlockSpec((tm, HEAD_DIM), lambda i: (i, 0))
    if head_major:
        out_spec = pl.BlockSpec((nh, tm, HEAD_DIM), lambda i: (0, i, 0))
        out_shape = jax.ShapeDtypeStruct((nh, lp, HEAD_DIM), bf16)
    else:
        out_spec = pl.BlockSpec((tm, w), lambda i: (i, 0))
        out_shape = jax.ShapeDtypeStruct((lp, w), bf16)
    kern = functools.partial(_rope_kernel, nh=nh, scale=scale, head_major=head_major)
    return pl.pallas_call(
        kern,
        grid=(lp // tm,),
        in_specs=[pl.BlockSpec((tm, w), lambda i: (i, col0 // w)), tab, tab, tab],
        out_specs=out_spec,
        out_shape=out_shape,
        compiler_params=_cparams(("parallel",)),
        name=f"rope_{col0}",
    )(proj, *tabs)


def rope_tables(lp):
    half = ROPE_DIMS // 2
    pos = (jnp.arange(lp, dtype=jnp.int32) - FRAME_OFF).astype(f32)
    inv = ROPE_THETA ** (-jnp.arange(half, dtype=f32) / half)
    ang = pos[:, None] * inv[None, :]
    cos, sin = jnp.cos(ang), jnp.sin(ang)
    zeros = jnp.zeros((lp, HEAD_DIM - ROPE_DIMS), f32)
    z16 = jnp.zeros((lp, half), f32)
    cc = jnp.concatenate([cos, cos, jnp.ones_like(zeros)], axis=1)
    s1 = jnp.concatenate([z16, sin, zeros], axis=1)
    s2 = jnp.concatenate([-sin, z16, zeros], axis=1)
    return cc, s1, s2


INT_MIN = -2 ** 31
INT_MAX = 2 ** 31 - 1


def _sortable(x):
    b = lax.bitcast_convert_type(x, i32)
    return b ^ (lax.shift_right_arithmetic(b, 31) & INT_MAX)


_NEG_BITS = struct.unpack("<i", struct.pack("<f", NEG))[0]
_KEY_NEG = _NEG_BITS ^ ((_NEG_BITS >> 31) & INT_MAX)


def _sum_sublane_groups(m):
    parts = [m[j * 8:(j + 1) * 8, :] for j in range(m.shape[0] // 8)]
    while len(parts) > 1:
        nxt = [parts[j] + parts[j + 1] for j in range(0, len(parts) - 1, 2)]
        if len(parts) % 2:
            nxt.append(parts[-1])
        parts = nxt
    return parts[0]


def _select_kernel(iq_ref, ik_ref, iwt_ref, bias_ref, key_ref, *, tq, ts, lp, topk):
    i = pl.program_id(0)
    nkb = ((i + 1) * tq + ts - 1) // ts
    rem = lp - nkb * ts
    iwt = iwt_ref[...] * ((IDX_HEADS ** -0.5) * (HEAD_DIM ** -0.5))
    qpos = i * tq + lax.broadcasted_iota(i32, (1, tq), 1)
    sub = lax.broadcasted_iota(i32, (ts, 1), 0)

    def score_blk(kb, carry):
        k0 = pl.multiple_of(kb * ts, ts)
        ikb = ik_ref[pl.ds(k0, ts), :]
        acc = jnp.zeros((ts, tq), f32)
        for h in range(IDX_HEADS):
            lg = _dot_nt(ikb, iq_ref[h])
            acc = acc + jnp.maximum(lg, 0.0) * iwt[h:h + 1, :]
        kpos = k0 + sub
        sc = jnp.where(kpos <= qpos, acc, NEG)
        sc = jnp.where(kpos < FRAME_OFF, -jnp.inf, sc)
        key_ref[pl.ds(k0, ts), :] = _sortable(sc)
        return carry

    lax.fori_loop(0, nkb, score_blk, 0)

    def count(pred_fn):
        def blk(kb, cnt):
            k0 = pl.multiple_of(kb * ts, ts)
            m = pred_fn(key_ref[pl.ds(k0, ts), :], k0).astype(i32)
            return cnt + _sum_sublane_groups(m)
        cnt = lax.fori_loop(0, nkb, blk, jnp.zeros((8, tq), i32))
        return jnp.sum(cnt, axis=0, keepdims=True)

    def count_ge(cand):
        return count(lambda kv, k0: kv >= cand) + jnp.where(cand <= _KEY_NEG, rem, 0)

    def bis(it, tau):
        bit = lax.shift_left(jnp.int32(1), 31 - it)
        t2 = tau | bit
        return jnp.where(count_ge(t2 ^ INT_MIN) >= topk, t2, tau)

    tau = lax.fori_loop(0, 32, bis, jnp.zeros((1, tq), i32))
    thr = tau ^ INT_MIN
    n_ge = count_ge(thr)
    n_gt = (count(lambda kv, k0: kv > thr) + jnp.where(thr < _KEY_NEG, rem, 0))
    need = topk - n_gt
    has_tie = jnp.max(jnp.where(n_ge > topk, 1, 0)) > 0

    def tie_cut():
        def bis2(it, cut):
            bit = lax.shift_left(jnp.int32(1), 14 - it)
            c2 = cut | bit
            n = count(lambda kv, k0: jnp.logical_and(kv == thr, (k0 + sub) < c2))
            return jnp.where(n < need, c2, cut)
        return lax.fori_loop(0, 15, bis2, jnp.zeros((1, tq), i32))

    cut = lax.cond(has_tie, tie_cut, lambda: jnp.full((1, tq), INT_MAX, i32))

    def bias_blk(kb, carry):
        k0 = pl.multiple_of(kb * ts, ts)
        kv = key_ref[pl.ds(k0, ts), :]
        kpos = k0 + sub
        sel = jnp.logical_or(kv > thr, jnp.logical_and(kv == thr, kpos <= cut))
        vis = jnp.logical_and(kpos <= qpos, kpos >= FRAME_OFF)
        bias_ref[pl.ds(k0, ts), :] = jnp.where(jnp.logical_and(sel, vis), 0.0, NEG)
        return carry

    lax.fori_loop(0, nkb, bias_blk, 0)

    def fill_blk(kb, carry):
        k0 = pl.multiple_of(kb * ts, ts)
        bias_ref[pl.ds(k0, ts), :] = jnp.full((ts, tq), NEG, f32)
        return carry

    lax.fori_loop(nkb, lp // ts, fill_blk, 0)


def dsa_select(iq, ik, iwt, topk):
    nh, lp, _ = iq.shape
    assert lp < 2 ** 15
    tq = 128
    ts = _pick(lp, (640, 256, 128))
    kern = functools.partial(_select_kernel, tq=tq, ts=ts, lp=lp, topk=topk)
    return pl.pallas_call(
        kern,
        grid=(lp // tq,),
        in_specs=[pl.BlockSpec((nh, tq, HEAD_DIM), lambda i: (0, i, 0)),
                  pl.BlockSpec((lp, HEAD_DIM), lambda i: (0, 0)),
                  pl.BlockSpec((nh, tq), lambda i: (0, i))],
        out_specs=pl.BlockSpec((lp, tq), lambda i: (0, i)),
        out_shape=jax.ShapeDtypeStruct((lp, lp), f32),
        scratch_shapes=[pltpu.VMEM((lp, tq), i32)],
        compiler_params=_cparams(("parallel",)),
        name="dsa_select",
    )(iq, ik, iwt)


def _attn_kernel(q_ref, k_ref, vt_ref, b_ref, o_ref, m_ref, l_ref, acc_ref, *, tq, ts):
    i = pl.program_id(0)
    nkb = ((i + 1) * tq + ts - 1) // ts
    group = ATT_HEADS // ATT_KV_HEADS
    qpos = i * tq + lax.broadcasted_iota(i32, (tq, 1), 0)
    for g in range(ATT_KV_HEADS):
        qg = jnp.concatenate(
            [q_ref[:, (g * group + r) * HEAD_DIM:(g * group + r + 1) * HEAD_DIM]
             for r in range(group)], axis=0)
        m_ref[...] = jnp.full(m_ref.shape, -jnp.inf, f32)
        l_ref[...] = jnp.zeros_like(l_ref)
        acc_ref[...] = jnp.zeros_like(acc_ref)

        def blk(kb, carry):
            k0 = pl.multiple_of(kb * ts, ts)
            kb_ = k_ref[pl.ds(k0, ts), g * HEAD_DIM:(g + 1) * HEAD_DIM]
            vt_ = vt_ref[g * HEAD_DIM:(g + 1) * HEAD_DIM, pl.ds(k0, ts)]
            bias = b_ref[pl.ds(k0, ts), :]
            st = _dot_nt(kb_, qg)
            ps, alphas = [], []
            for r in range(group):
                sr = st[:, r * tq:(r + 1) * tq] + bias
                m_old = m_ref[r:r + 1, :]
                m_new = jnp.maximum(m_old, jnp.max(sr, axis=0, keepdims=True))
                alpha = jnp.exp(m_old - m_new)
                p = jnp.exp(sr - m_new)
                l_ref[r:r + 1, :] = alpha * l_ref[r:r + 1, :] + jnp.sum(p, axis=0, keepdims=True)
                m_ref[r:r + 1, :] = m_new
                ps.append(p.astype(bf16))
                alphas.append(alpha)
            pt = jnp.concatenate(ps, axis=1)
            acc_ref[...] = jnp.concatenate(alphas, axis=1) * acc_ref[...] + _dot(vt_, pt)
            return carry

        lax.fori_loop(0, nkb, blk, 0)
        for r in range(group):
            ot = acc_ref[:, r * tq:(r + 1) * tq] / l_ref[r:r + 1, :]
            orr = jnp.where(qpos >= FRAME_OFF, ot.T, 0.0)
            o_ref[:, (g * group + r) * HEAD_DIM:(g * group + r + 1) * HEAD_DIM] = orr.astype(o_ref.dtype)


def dsa_attention(q, k, vt, bias_t):
    lp = q.shape[0]
    tq = 128
    ts = _pick(lp, (640, 256, 128))
    group = ATT_HEADS // ATT_KV_HEADS
    kern = functools.partial(_attn_kernel, tq=tq, ts=ts)
    return pl.pallas_call(
        kern,
        grid=(lp // tq,),
        in_specs=[pl.BlockSpec((tq, ATT_Q_W), lambda i: (i, 0)),
                  pl.BlockSpec((lp, ATT_KV_W), lambda i: (0, 0)),
                  pl.BlockSpec((ATT_KV_W, lp), lambda i: (0, 0)),
                  pl.BlockSpec((lp, tq), lambda i: (0, i))],
        out_specs=pl.BlockSpec((tq, ATT_Q_W), lambda i: (i, 0)),
        out_shape=jax.ShapeDtypeStruct((lp, ATT_Q_W), bf16),
        scratch_shapes=[pltpu.VMEM((group, tq), f32),
                        pltpu.VMEM((group, tq), f32),
                        pltpu.VMEM((HEAD_DIM, group * tq), f32)],
        compiler_params=_cparams(("parallel",)),
        name="dsa_attention",
    )(q, k, vt, bias_t)


def _ffn_conv_kernel(g_ref, gh_ref, v_ref, vh_ref, wg_ref, wv_ref, bg_ref, bv_ref, o_ref,
                     eg_ref, ev_ref, *, tm):
    i = pl.program_id(0)

    def conv(x_ref, h_ref, w_ref, b_ref, ext_ref):
        ext_ref[0:HALO, :] = jnp.where(i > 0, h_ref[...].astype(f32), 0.0)
        ext_ref[HALO:HALO + tm, :] = x_ref[...].astype(f32)
        w = w_ref[...]
        y = ext_ref[HALO:HALO + tm, :] * w[FFN_CONV - 1:FFN_CONV, :]
        for j in range(FFN_CONV - 1):
            s0 = HALO - (FFN_CONV - 1) + j
            y = y + ext_ref[s0:s0 + tm, :] * w[j:j + 1, :]
        return y + b_ref[...]

    gate = conv(g_ref, gh_ref, wg_ref, bg_ref, eg_ref)
    val = conv(v_ref, vh_ref, wv_ref, bv_ref, ev_ref)
    o_ref[...] = (gate * _sigmoid(gate) * val).astype(o_ref.dtype)


def ffn_conv(up, conv_w, conv_b):
    lp = up.shape[0]
    tm = _pick(lp, (640, 256, 128))
    tc = 512
    nc = D_FF // tc
    kern = functools.partial(_ffn_conv_kernel, tm=tm)

    def main(off):
        return pl.BlockSpec((tm, tc), lambda i, c: (i, c + off))

    def halo(off):
        return pl.BlockSpec((HALO, tc), lambda i, c: (jnp.maximum(i * (tm // HALO) - 1, 0), c + off))

    def wspec(off):
        return pl.BlockSpec((FFN_CONV, tc), lambda i, c: (0, c + off))

    def bspec(off):
        return pl.BlockSpec((1, tc), lambda i, c: (0, c + off))

    b2 = conv_b.reshape(1, 2 * D_FF)
    return pl.pallas_call(
        kern,
        grid=(lp // tm, nc),
        in_specs=[main(0), halo(0), main(nc), halo(nc), wspec(0), wspec(nc), bspec(0), bspec(nc)],
        out_specs=pl.BlockSpec((tm, tc), lambda i, c: (i, c)),
        out_shape=jax.ShapeDtypeStruct((lp, D_FF), bf16),
        scratch_shapes=[pltpu.VMEM((HALO + tm, tc), f32), pltpu.VMEM((HALO + tm, tc), f32)],
        compiler_params=_cparams(("parallel", "parallel")),
        name="ffn_conv",
    )(up, up, up, up, conv_w, conv_w, b2, b2)


def _split_w_in(w):
    o = 0
    parts = {}
    for name, width in (("gq", GDN_QK_W), ("gk", GDN_QK_W), ("gv", GDN_V_W), ("gz", GDN_V_W),
                        ("gb", GDN_V_HEADS), ("ga", GDN_V_HEADS), ("aq", ATT_Q_W), ("ak", ATT_KV_W),
                        ("av", ATT_KV_W), ("iq", IDX_Q_W), ("ik", HEAD_DIM), ("iw", IDX_HEADS),
                        ("gate_gdn", D_MODEL), ("gate_att", D_MODEL)):
        parts[name] = w[:, o:o + width]
        o += width
    big = jnp.concatenate([parts[n] for n in ("gq", "gk", "gv", "gz", "aq", "iq", "gate_gdn",
                                              "gate_att", "ak", "av", "ik")], axis=1).astype(bf16)
    small = jnp.concatenate([parts["gb"], parts["ga"], parts["iw"]], axis=1)
    small = jnp.pad(small, ((0, 0), (0, LANES - small.shape[1]))).astype(bf16)
    return big, small


def _layer(h0, p, lp, topk):
    tm = _pick(lp, (640, 256, 128))
    tm_big = _pick(lp, (1280, 640, 256, 128))
    w_big, w_small = _split_w_in(p["w_in"])

    u1 = rms_rows(h0, p["mix_pre_g"])
    (proj,) = matmul(u1, w_big, tm=tm, tn=1920, tk=D_MODEL, epilogue=_ep_cast,
                     out_shapes=[jax.ShapeDtypeStruct((lp, PROJ_W), bf16)], name="proj_in")
    (small,) = matmul(u1, w_small, tm=tm_big, tn=LANES, tk=D_MODEL, epilogue=_ep_cast,
                      out_shapes=[jax.ShapeDtypeStruct((lp, LANES), f32)], name="proj_small")
    gb = small[:, 0:GDN_V_HEADS]
    ga = small[:, GDN_V_HEADS:2 * GDN_V_HEADS]
    iw = small[:, 2 * GDN_V_HEADS:2 * GDN_V_HEADS + IDX_HEADS]

    qkv = gdn_prep(proj, p["gdn_conv_w"])
    kt = qkv[:, GDN_QK_W:2 * GDN_QK_W].T
    beta, gcum = gdn_gates(gb, ga, p["gdn_a_log"], p["gdn_dt_bias"])
    gcol = gcum.T[:, :, None]
    grow = gcum.T[:, None, :]
    bcol = beta.T[:, :, None]
    brow = beta.T[:, None, :]
    o_gdn = gdn_core(qkv, kt, proj, p["gdn_norm_g"], gcol, grow, bcol, brow)

    tabs = rope_tables(lp)
    aq = rope(proj, COL_AQ, ATT_HEADS, tabs, scale=HEAD_DIM ** -0.5)
    ak = rope(proj, COL_AK, ATT_KV_HEADS, tabs)
    iq = rope(proj, COL_IQ, IDX_HEADS, tabs, head_major=True)
    ik = rope(proj, COL_IK, 1, tabs)
    avt = proj[:, COL_AV:COL_AV + ATT_KV_W].T
    bias_t = dsa_select(iq, ik, iw.T, topk)
    o_att = dsa_attention(aq, ak, avt, bias_t)

    tn = 512
    gate_spec = lambda col0: ((tm_big, tn), (lambda i, j, c=col0 // tn: (i, c + j)))
    (m1,) = matmul(o_gdn, p["w_branch_gdn"].astype(bf16), tm=tm_big, tn=tn, tk=2048,
                   epilogue=_ep_gate, extra=(proj,), extra_specs=(gate_spec(COL_GATE_GDN),),
                   out_shapes=[jax.ShapeDtypeStruct((lp, D_MODEL), f32)], name="branch_gdn")
    (merged,) = matmul(o_att, p["w_branch_att"].astype(bf16), tm=tm_big, tn=tn, tk=2048,
                       epilogue=_ep_gate_add, extra=(proj, m1),
                       extra_specs=(gate_spec(COL_GATE_ATT), ((tm_big, tn), lambda i, j: (i, j))),
                       out_shapes=[jax.ShapeDtypeStruct((lp, D_MODEL), bf16)], name="branch_att")
    row_spec = ((tm, D_MODEL), lambda i, j: (i, 0))
    vec_spec = ((1, D_MODEL), lambda i, j: (0, 0))
    h1, u2 = matmul(merged, p["w_out"].astype(bf16), tm=tm, tn=D_MODEL, tk=1024,
                    epilogue=_ep_res_norm2,
                    extra=(h0, p["mix_post_g"].reshape(1, D_MODEL), p["ffn_pre_g"].reshape(1, D_MODEL)),
                    extra_specs=(row_spec, vec_spec, vec_spec),
                    out_shapes=[jax.ShapeDtypeStruct((lp, D_MODEL), f32),
                                jax.ShapeDtypeStruct((lp, D_MODEL), bf16)], name="w_out")

    (up,) = matmul(u2, p["w_up"].astype(bf16), tm=tm_big, tn=1024, tk=D_MODEL, epilogue=_ep_cast,
                   out_shapes=[jax.ShapeDtypeStruct((lp, 2 * D_FF), bf16)], name="w_up")
    act = ffn_conv(up, p["ffn_conv_w"], p["ffn_conv_b"])
    (h2,) = matmul(act, p["w_down"].astype(bf16), tm=tm, tn=D_MODEL, tk=1024, epilogue=_ep_res_norm,
                   extra=(h1, p["ffn_post_g"].reshape(1, D_MODEL)),
                   extra_specs=(row_spec, vec_spec),
                   out_shapes=[jax.ShapeDtypeStruct((lp, D_MODEL), f32)], name="w_down")
    return h2


def kernel(x, meta_tokens, mix_pre_g, w_in, gdn_conv_w, gdn_a_log, gdn_dt_bias, gdn_norm_g,
           w_branch_gdn, w_branch_att, w_out, mix_post_g, ffn_pre_g, w_up, ffn_conv_w,
           ffn_conv_b, w_down, ffn_post_g):
    batch, seq, d = x.shape
    assert batch == 1 and d == D_MODEL
    lp = FRAME_X0 + seq
    topk = min(TOPK_MAX, (N_META + seq) // 4)
    h = jnp.concatenate([jnp.zeros((FRAME_OFF, d), x.dtype), meta_tokens.astype(x.dtype), x[0]], axis=0)
    for i in range(w_in.shape[0]):
        p = dict(mix_pre_g=mix_pre_g[i], w_in=w_in[i], gdn_conv_w=gdn_conv_w[i], gdn_a_log=gdn_a_log[i],
                 gdn_dt_bias=gdn_dt_bias[i], gdn_norm_g=gdn_norm_g[i], w_branch_gdn=w_branch_gdn[i],
                 w_branch_att=w_branch_att[i], w_out=w_out[i], mix_post_g=mix_post_g[i],
                 ffn_pre_g=ffn_pre_g[i], w_up=w_up[i], ffn_conv_w=ffn_conv_w[i],
                 ffn_conv_b=ffn_conv_b[i], w_down=w_down[i], ffn_post_g=ffn_post_g[i])
        h = _layer(h, p, lp, topk)
    return h[FRAME_X0:][None]
```

```python
import functools
import struct

import jax
import jax.numpy as jnp
from jax import lax
from jax.experimental import pallas as pl
from jax.experimental.pallas import tpu as pltpu

f32 = jnp.float32
bf16 = jnp.bfloat16
i32 = jnp.int32

D_MODEL = 2048
N_META = 16
EPS = 1e-6
GDN_QK_HEADS = 16
GDN_V_HEADS = 32
HEAD_DIM = 128
GDN_CONV = 4
ATT_HEADS = 16
ATT_KV_HEADS = 2
IDX_HEADS = 16
TOPK_MAX = 256
NEG = -1e30
LOG2E = 1.4426950408889634
ROPE_THETA = 500000.0
ROPE_DIMS = HEAD_DIM // 4
D_FF = 3 * D_MODEL
FFN_CONV = 3
GDN_QK_W = GDN_QK_HEADS * HEAD_DIM
GDN_V_W = GDN_V_HEADS * HEAD_DIM
ATT_Q_W = ATT_HEADS * HEAD_DIM
ATT_KV_W = ATT_KV_HEADS * HEAD_DIM
IDX_Q_W = IDX_HEADS * HEAD_DIM

FRAME_X0 = 256
FRAME_OFF = FRAME_X0 - N_META
CHUNK = 128

LANES = 128
VMEM_LIMIT = 56 * 1024 * 1024

COL_GQKV = 0
COL_GZ = COL_GQKV + 2 * GDN_QK_W + GDN_V_W
COL_AQ = COL_GZ + GDN_V_W
COL_IQ = COL_AQ + ATT_Q_W
COL_GATE_GDN = COL_IQ + IDX_Q_W
COL_GATE_ATT = COL_GATE_GDN + D_MODEL
COL_AK = COL_GATE_ATT + D_MODEL
COL_AV = COL_AK + ATT_KV_W
COL_IK = COL_AV + ATT_KV_W
PROJ_W = COL_IK + HEAD_DIM


def _cparams(sem):
    return pltpu.CompilerParams(dimension_semantics=sem, vmem_limit_bytes=VMEM_LIMIT)


def _pick(n, cands):
    for c in cands:
        if n % c == 0:
            return c
    raise ValueError(f"no tile for {n} in {cands}")


def _sigmoid(x):
    return 1.0 / (1.0 + jnp.exp(-x))


def _dot(a, b):
    return jnp.dot(a, b, preferred_element_type=f32)


def _dot_nt(a, b):
    return lax.dot_general(a, b, (((1,), (1,)), ((), ())), preferred_element_type=f32)


def _rms_rows_kernel(h_ref, g_ref, o_ref):
    h = h_ref[...]
    y = h * lax.rsqrt(jnp.mean(h * h, axis=-1, keepdims=True) + EPS)
    o_ref[...] = (y * g_ref[...]).astype(o_ref.dtype)


def rms_rows(h, g):
    lp, d = h.shape
    tm = _pick(lp, (640, 256, 128))
    return pl.pallas_call(
        _rms_rows_kernel,
        grid=(lp // tm,),
        in_specs=[pl.BlockSpec((tm, d), lambda i: (i, 0)),
                  pl.BlockSpec((1, d), lambda i: (0, 0))],
        out_specs=pl.BlockSpec((tm, d), lambda i: (i, 0)),
        out_shape=jax.ShapeDtypeStruct((lp, d), bf16),
        compiler_params=_cparams(("parallel",)),
        name="rms_rows",
    )(h, g.reshape(1, d))


def _mm_kernel(*refs, nk, n_extra, n_out, epilogue):
    a_ref, w_ref = refs[0], refs[1]
    extra = refs[2:2 + n_extra]
    outs = refs[2 + n_extra:2 + n_extra + n_out]
    if nk == 1:
        epilogue(_dot(a_ref[...], w_ref[...]), extra, outs)
        return
    acc_ref = refs[-1]
    k = pl.program_id(2)

    @pl.when(k == 0)
    def _():
        acc_ref[...] = jnp.zeros_like(acc_ref)

    acc_ref[...] += _dot(a_ref[...], w_ref[...])

    @pl.when(k == nk - 1)
    def _():
        epilogue(acc_ref[...], extra, outs)


def matmul(a, w, *, tm, tn, tk, epilogue, extra=(), extra_specs=(), out_shapes, a_col0=0, name):
    m = a.shape[0]
    kdim, n = w.shape
    nk = kdim // tk
    assert m % tm == 0 and n % tn == 0 and kdim % tk == 0
    in_specs = [pl.BlockSpec((tm, tk), lambda i, j, k: (i, a_col0 + k)),
                pl.BlockSpec((tk, tn), lambda i, j, k: (k, j))]
    in_specs += [pl.BlockSpec(bs, (lambda i, j, k, f=f: f(i, j))) for bs, f in extra_specs]
    out_specs = [pl.BlockSpec((tm, tn), lambda i, j, k: (i, j)) for _ in out_shapes]
    scratch = [] if nk == 1 else [pltpu.VMEM((tm, tn), f32)]
    kern = functools.partial(_mm_kernel, nk=nk, n_extra=len(extra), n_out=len(out_shapes),
                             epilogue=epilogue)
    res = pl.pallas_call(
        kern,
        grid=(m // tm, n // tn, nk),
        in_specs=in_specs,
        out_specs=out_specs,
        out_shape=out_shapes,
        scratch_shapes=scratch,
        compiler_params=_cparams(("parallel", "parallel", "arbitrary")),
        name=name,
    )(a, w, *extra)
    return res


def _ep_cast(acc, extra, outs):
    outs[0][...] = acc.astype(outs[0].dtype)


def _ep_gate(acc, extra, outs):
    g = extra[0][...].astype(f32)
    outs[0][...] = (_sigmoid(g) * acc).astype(outs[0].dtype)


def _ep_gate_add(acc, extra, outs):
    g = extra[0][...].astype(f32)
    outs[0][...] = (extra[1][...].astype(f32) + _sigmoid(g) * acc).astype(outs[0].dtype)


def _rms(t, g):
    return t * lax.rsqrt(jnp.mean(t * t, axis=-1, keepdims=True) + EPS) * g


def _ep_res_norm2(acc, extra, outs):
    h_ref, g_ref, g2_ref = extra
    h1 = h_ref[...] + _rms(acc, g_ref[...])
    outs[0][...] = h1
    outs[1][...] = _rms(h1, g2_ref[...]).astype(outs[1].dtype)


def _ep_res_norm(acc, extra, outs):
    h_ref, g_ref = extra
    outs[0][...] = h_ref[...] + _rms(acc, g_ref[...])


HALO = 16


def _gdn_prep_kernel(x_ref, halo_ref, w_ref, o_ref, ext_ref, *, tm, tc):
    i = pl.program_id(0)
    c = pl.program_id(1)
    halo = halo_ref[...].astype(f32)
    ext_ref[0:HALO, :] = jnp.where(i > 0, halo, 0.0)
    ext_ref[HALO:HALO + tm, :] = x_ref[...].astype(f32)
    w = w_ref[...]
    y = ext_ref[HALO:HALO + tm, :] * w[GDN_CONV - 1:GDN_CONV, :]
    for j in range(GDN_CONV - 1):
        s0 = HALO - (GDN_CONV - 1) + j
        y = y + ext_ref[s0:s0 + tm, :] * w[j:j + 1, :]
    s = y * _sigmoid(y)
    is_q = c < (GDN_QK_W // tc)
    is_qk = c < (2 * GDN_QK_W // tc)
    qscale = jnp.where(is_q, HEAD_DIM ** -0.5, 1.0).astype(f32)
    for hh in range(tc // HEAD_DIM):
        seg = s[:, hh * HEAD_DIM:(hh + 1) * HEAD_DIM]
        r = lax.rsqrt(jnp.sum(seg * seg, axis=-1, keepdims=True) + EPS) * qscale
        fac = jnp.where(is_qk, r, 1.0)
        o_ref[:, hh * HEAD_DIM:(hh + 1) * HEAD_DIM] = (seg * fac).astype(o_ref.dtype)


def gdn_prep(proj, conv_w):
    lp = proj.shape[0]
    width = 2 * GDN_QK_W + GDN_V_W
    tm = _pick(lp, (640, 256, 128))
    tc = 512
    kern = functools.partial(_gdn_prep_kernel, tm=tm, tc=tc)
    return pl.pallas_call(
        kern,
        grid=(lp // tm, width // tc),
        in_specs=[pl.BlockSpec((tm, tc), lambda i, c: (i, c)),
                  pl.BlockSpec((HALO, tc), lambda i, c: (jnp.maximum(i * (tm // HALO) - 1, 0), c)),
                  pl.BlockSpec((GDN_CONV, tc), lambda i, c: (0, c))],
        out_specs=pl.BlockSpec((tm, tc), lambda i, c: (i, c)),
        out_shape=jax.ShapeDtypeStruct((lp, width), bf16),
        scratch_shapes=[pltpu.VMEM((HALO + tm, tc), f32)],
        compiler_params=_cparams(("parallel", "parallel")),
        name="gdn_prep",
    )(proj, proj, conv_w)


def _gdn_gates_kernel(b_ref, a_ref, alog_ref, dt_ref, beta_ref, gcum_ref):
    i = pl.program_id(0)
    rows = i * CHUNK + lax.broadcasted_iota(i32, (CHUNK, 1), 0)
    valid = rows >= FRAME_OFF
    beta_ref[...] = jnp.where(valid, _sigmoid(b_ref[...]), 0.0)
    a = a_ref[...] + dt_ref[...]
    sp = jnp.maximum(a, 0.0) + jnp.log1p(jnp.exp(-jnp.abs(a)))
    g = jnp.where(valid, -jnp.exp(alog_ref[...]) * sp, 0.0)
    tri = (lax.broadcasted_iota(i32, (CHUNK, CHUNK), 0)
           >= lax.broadcasted_iota(i32, (CHUNK, CHUNK), 1)).astype(f32)
    gcum_ref[...] = jnp.dot(tri, g, preferred_element_type=f32, precision=lax.Precision.HIGHEST)


def gdn_gates(gb, ga, a_log, dt_bias):
    lp, nh = gb.shape
    spec = pl.BlockSpec((CHUNK, nh), lambda i: (i, 0))
    vec = pl.BlockSpec((1, nh), lambda i: (0, 0))
    return pl.pallas_call(
        _gdn_gates_kernel,
        grid=(lp // CHUNK,),
        in_specs=[spec, spec, vec, vec],
        out_specs=[spec, spec],
        out_shape=[jax.ShapeDtypeStruct((lp, nh), f32)] * 2,
        compiler_params=_cparams(("parallel",)),
        name="gdn_gates",
    )(gb, ga, a_log.reshape(1, nh), dt_bias.reshape(1, nh))


def _block_mask(size):
    r = lax.broadcasted_iota(i32, (CHUNK, CHUNK), 0) // size
    c = lax.broadcasted_iota(i32, (CHUNK, CHUNK), 1) // size
    return r == c


def _unit_lower_inverse_many(mats):
    row = lax.broadcasted_iota(i32, (CHUNK, CHUNK), 0)
    col = lax.broadcasted_iota(i32, (CHUNK, CHUNK), 1)
    eye = (row == col).astype(f32)
    base = 8
    m_prev = _block_mask(base)
    ads = [jnp.where(m_prev, a, 0.0) for a in mats]
    adbs = [ad.astype(bf16) for ad in ads]
    a2s = [_dot(x, x) for x in adbs]
    a2bs = [x.astype(bf16) for x in a2s]
    a4s = [_dot(x, x) for x in a2bs]
    xs = [_dot((eye - ad).astype(bf16), (eye + a2).astype(bf16)) for ad, a2 in zip(ads, a2s)]
    xs = [_dot(x.astype(bf16), (eye + a4).astype(bf16)) for x, a4 in zip(xs, a4s)]
    size = base * 2
    while size <= CHUNK:
        m_cur = _block_mask(size)
        off_diag = jnp.logical_and(m_cur, jnp.logical_not(m_prev))
        bs = [jnp.where(off_diag, a, 0.0).astype(bf16) for a in mats]
        xbs = [x.astype(bf16) for x in xs]
        ys = [_dot(xb, b).astype(bf16) for xb, b in zip(xbs, bs)]
        xs = [x - _dot(y, xb) for x, y, xb in zip(xs, ys, xbs)]
        m_prev = m_cur
        size *= 2
    return xs


def _gdn_kernel(q_ref, k_ref, kt_ref, v_ref, z_ref, ng_ref,
                gc0_ref, gc1_ref, gr0_ref, gr1_ref, bc0_ref, bc1_ref, br0_ref, br1_ref,
                o_ref, s_ref, *, rb):
    r = pl.program_id(1)

    @pl.when(r == 0)
    def _():
        s_ref[...] = jnp.zeros_like(s_ref)

    gcs, grs = (gc0_ref, gc1_ref), (gr0_ref, gr1_ref)
    bcs, brs = (bc0_ref, bc1_ref), (br0_ref, br1_ref)
    row = lax.broadcasted_iota(i32, (CHUNK, CHUNK), 0)
    col = lax.broadcasted_iota(i32, (CHUNK, CHUNK), 1)
    incl = row >= col
    strict = row > col
    ng = ng_ref[...]
    nchunk = rb // CHUNK
    heads = range(2)

    def rows(c):
        return slice(c * CHUNK, (c + 1) * CHUNK)

    def lanes(e):
        return slice(e * HEAD_DIM, (e + 1) * HEAD_DIM)

    qs = [q_ref[rows(c), :] for c in range(nchunk)]
    ks = [k_ref[rows(c), :] for c in range(nchunk)]
    kts = [kt_ref[:, rows(c)] for c in range(nchunk)]
    kks = [_dot(k, kt) for k, kt in zip(ks, kts)]
    qks = [_dot(q, kt) for q, kt in zip(qs, kts)]
    chains = [(c, e) for c in range(nchunk) for e in heads]
    gc = {ce: gcs[ce[1]][rows(ce[0]), :] for ce in chains}
    gr = {ce: grs[ce[1]][:, rows(ce[0])] for ce in chains}
    bc = {ce: bcs[ce[1]][rows(ce[0]), :] for ce in chains}
    br = {ce: brs[ce[1]][:, rows(ce[0])] for ce in chains}
    dec = {ce: jnp.where(incl, jnp.exp(jnp.where(incl, gc[ce] - gr[ce], 0.0)), 0.0) for ce in chains}
    amat = [jnp.where(strict, bc[ce] * kks[ce[0]] * dec[ce], 0.0) for ce in chains]
    tinv = dict(zip(chains, _unit_lower_inverse_many(amat)))
    u = {ce: _dot((tinv[ce] * br[ce]).astype(bf16), v_ref[rows(ce[0]), lanes(ce[1])]) for ce in chains}
    w = {ce: _dot((tinv[ce] * (br[ce] * jnp.exp(gr[ce]))).astype(bf16), ks[ce[0]]) for ce in chains}
    g_last = {ce: gr[ce][:, CHUNK - 1:CHUNK] for ce in chains}
    wq = {ce: jnp.concatenate([w[ce].astype(bf16), qs[ce[0]]], axis=0) for ce in chains}
    ak = {ce: jnp.concatenate(
        [jnp.where(incl, qks[ce[0]] * dec[ce], 0.0).astype(bf16),
         (kts[ce[0]].astype(f32) * jnp.exp(g_last[ce] - gr[ce])).astype(bf16)], axis=0) for ce in chains}
    eg = {ce: jnp.exp(gc[ce]) for ce in chains}
    egl = {ce: jnp.exp(g_last[ce]) for ce in chains}

    state = [s_ref[e] for e in heads]
    for c in range(nchunk):
        sb = [state[e].astype(bf16) for e in heads]
        ws = [_dot(wq[(c, e)], sb[e]) for e in heads]
        vb = [(u[(c, e)] - ws[e][:CHUNK]).astype(bf16) for e in heads]
        av = [_dot(ak[(c, e)], vb[e]) for e in heads]
        for e in heads:
            o = eg[(c, e)] * ws[e][CHUNK:] + av[e][:CHUNK]
            state[e] = state[e] * egl[(c, e)] + av[e][CHUNK:]
            z = z_ref[rows(c), lanes(e)].astype(f32)
            o_ref[rows(c), lanes(e)] = (_rms(o, ng) * (z * _sigmoid(z))).astype(o_ref.dtype)
    for e in heads:
        s_ref[e] = state[e]


def gdn_core(qkv, kt, proj, norm_g, gcol, grow, bcol, brow):
    lp = qkv.shape[0]
    rb = _pick(lp, (640, 256, 128))
    nqk = GDN_QK_HEADS
    kcol0 = GDN_QK_W // HEAD_DIM
    vcol0 = 2 * GDN_QK_W // (2 * HEAD_DIM)
    zcol0 = COL_GZ // (2 * HEAD_DIM)

    def colspec(e):
        return pl.BlockSpec((None, rb, 1), lambda j, r: (2 * j + e, r, 0))

    def rowspec(e):
        return pl.BlockSpec((None, 1, rb), lambda j, r: (2 * j + e, 0, r))

    kern = functools.partial(_gdn_kernel, rb=rb)
    return pl.pallas_call(
        kern,
        grid=(nqk, lp // rb),
        in_specs=[pl.BlockSpec((rb, HEAD_DIM), lambda j, r: (r, j)),
                  pl.BlockSpec((rb, HEAD_DIM), lambda j, r: (r, kcol0 + j)),
                  pl.BlockSpec((HEAD_DIM, rb), lambda j, r: (j, r)),
                  pl.BlockSpec((rb, 2 * HEAD_DIM), lambda j, r: (r, vcol0 + j)),
                  pl.BlockSpec((rb, 2 * HEAD_DIM), lambda j, r: (r, zcol0 + j)),
                  pl.BlockSpec((1, HEAD_DIM), lambda j, r: (0, 0)),
                  colspec(0), colspec(1), rowspec(0), rowspec(1),
                  colspec(0), colspec(1), rowspec(0), rowspec(1)],
        out_specs=pl.BlockSpec((rb, 2 * HEAD_DIM), lambda j, r: (r, j)),
        out_shape=jax.ShapeDtypeStruct((lp, GDN_V_W), bf16),
        scratch_shapes=[pltpu.VMEM((2, HEAD_DIM, HEAD_DIM), f32)],
        compiler_params=_cparams(("parallel", "arbitrary")),
        name="gdn_core",
    )(qkv, qkv, kt, qkv, proj, norm_g.reshape(1, HEAD_DIM),
      gcol, gcol, grow, grow, bcol, bcol, brow, brow)


def _rope_kernel(x_ref, c_ref, s1_ref, s2_ref, o_ref, *, nh, scale, head_major):
    cc = c_ref[...]
    s1 = s1_ref[...]
    s2 = s2_ref[...]
    half = ROPE_DIMS // 2
    for h in range(nh):
        seg = x_ref[:, h * HEAD_DIM:(h + 1) * HEAD_DIM].astype(f32)
        y = (seg * cc + pltpu.roll(seg, half, 1) * s1
             + pltpu.roll(seg, HEAD_DIM - half, 1) * s2)
        if scale != 1.0:
            y = y * scale
        if head_major:
            o_ref[h] = y.astype(o_ref.dtype)
        else:
            o_ref[:, h * HEAD_DIM:(h + 1) * HEAD_DIM] = y.astype(o_ref.dtype)


def rope(proj, col0, nh, tabs, *, scale=1.0, head_major=False):
    lp = proj.shape[0]
    tm = _pick(lp, (640, 256, 128))
    w = nh * HEAD_DIM
    assert col0 % w == 0
    tab = pl.BlockSpec((tm, HEAD_DIM), lambda i: (i, 0))
    if head_major:
        out_spec = pl.BlockSpec((nh, tm, HEAD_DIM), lambda i: (0, i, 0))
        out_shape = jax.ShapeDtypeStruct((nh, lp, HEAD_DIM), bf16)
    else:
        out_spec = pl.BlockSpec((tm, w), lambda i: (i, 0))
        out_shape = jax.ShapeDtypeStruct((lp, w), bf16)
    kern = functools.partial(_rope_kernel, nh=nh, scale=scale, head_major=head_major)
    return pl.pallas_call(
        kern,
        grid=(lp // tm,),
        in_specs=[pl.BlockSpec((tm, w), lambda i: (i, col0 // w)), tab, tab, tab],
        out_specs=out_spec,
        out_shape=out_shape,
        compiler_params=_cparams(("parallel",)),
        name=f"rope_{col0}",
    )(proj, *tabs)


def rope_tables(lp):
    half = ROPE_DIMS // 2
    pos = (jnp.arange(lp, dtype=jnp.int32) - FRAME_OFF).astype(f32)
    inv = ROPE_THETA ** (-jnp.arange(half, dtype=f32) / half)
    ang = pos[:, None] * inv[None, :]
    cos, sin = jnp.cos(ang), jnp.sin(ang)
    zeros = jnp.zeros((lp, HEAD_DIM - ROPE_DIMS), f32)
    z16 = jnp.zeros((lp, half), f32)
    cc = jnp.concatenate([cos, cos, jnp.ones_like(zeros)], axis=1)
    s1 = jnp.concatenate([z16, sin, zeros], axis=1)
    s2 = jnp.concatenate([-sin, z16, zeros], axis=1)
    return cc, s1, s2


INT_MIN = -2 ** 31
INT_MAX = 2 ** 31 - 1


def _sortable(x):
    b = lax.bitcast_convert_type(x, i32)
    return b ^ (lax.shift_right_arithmetic(b, 31) & INT_MAX)


_NEG_BITS = struct.unpack("<i", struct.pack("<f", NEG))[0]
_KEY_NEG = _NEG_BITS ^ ((_NEG_BITS >> 31) & INT_MAX)


def _sum_sublane_groups(m):
    parts = [m[j * 8:(j + 1) * 8, :] for j in range(m.shape[0] // 8)]
    while len(parts) > 1:
        nxt = [parts[j] + parts[j + 1] for j in range(0, len(parts) - 1, 2)]
        if len(parts) % 2:
            nxt.append(parts[-1])
        parts = nxt
    return parts[0]


def _select_kernel(iq_ref, ik_ref, iwt_ref, bias_ref, key_ref, *, tq, ts, lp, topk):
    i = pl.program_id(0)
    nkb = ((i + 1) * tq + ts - 1) // ts
    rem = lp - nkb * ts
    iwt = iwt_ref[...] * ((IDX_HEADS ** -0.5) * (HEAD_DIM ** -0.5))
    qpos = i * tq + lax.broadcasted_iota(i32, (1, tq), 1)
    sub = lax.broadcasted_iota(i32, (ts, 1), 0)

    def score_blk(kb, carry):
        k0 = pl.multiple_of(kb * ts, ts)
        ikb = ik_ref[pl.ds(k0, ts), :]
        acc = jnp.zeros((ts, tq), f32)
        for h in range(IDX_HEADS):
            lg = _dot_nt(ikb, iq_ref[h])
            acc = acc + jnp.maximum(lg, 0.0) * iwt[h:h + 1, :]
        kpos = k0 + sub
        sc = jnp.where(kpos <= qpos, acc, NEG)
        sc = jnp.where(kpos < FRAME_OFF, -jnp.inf, sc)
        key_ref[pl.ds(k0, ts), :] = _sortable(sc)
        return carry

    lax.fori_loop(0, nkb, score_blk, 0)

    def count(pred_fn):
        def blk(kb, cnt):
            k0 = pl.multiple_of(kb * ts, ts)
            m = pred_fn(key_ref[pl.ds(k0, ts), :], k0).astype(i32)
            return cnt + _sum_sublane_groups(m)
        cnt = lax.fori_loop(0, nkb, blk, jnp.zeros((8, tq), i32))
        return jnp.sum(cnt, axis=0, keepdims=True)

    def count_ge(cand):
        return count(lambda kv, k0: kv >= cand) + jnp.where(cand <= _KEY_NEG, rem, 0)

    def bis_cond(c):
        it, _, n_tau = c
        return jnp.logical_and(it < 32, jnp.max(jnp.where(n_tau == topk, 0, 1)) > 0)

    def bis(c):
        it, tau, n_tau = c
        bit = lax.shift_left(jnp.int32(1), 31 - it)
        t2 = tau | bit
        n2 = count_ge(t2 ^ INT_MIN)
        ok = n2 >= topk
        return it + 1, jnp.where(ok, t2, tau), jnp.where(ok, n2, n_tau)

    _, tau, n_ge = lax.while_loop(
        bis_cond, bis, (jnp.int32(0), jnp.zeros((1, tq), i32), jnp.full((1, tq), lp, i32)))
    thr = tau ^ INT_MIN
    has_tie = jnp.max(jnp.where(n_ge > topk, 1, 0)) > 0

    def tie_cut():
        n_gt = (count(lambda kv, k0: kv > thr) + jnp.where(thr < _KEY_NEG, rem, 0))
        need = topk - n_gt

        def bis2(it, cut):
            bit = lax.shift_left(jnp.int32(1), 14 - it)
            c2 = cut | bit
            n = count(lambda kv, k0: jnp.logical_and(kv == thr, (k0 + sub) < c2))
            return jnp.where(n < need, c2, cut)
        return lax.fori_loop(0, 15, bis2, jnp.zeros((1, tq), i32))

    cut = lax.cond(has_tie, tie_cut, lambda: jnp.full((1, tq), INT_MAX, i32))

    def bias_blk(kb, carry):
        k0 = pl.multiple_of(kb * ts, ts)
        kv = key_ref[pl.ds(k0, ts), :]
        kpos = k0 + sub
        sel = jnp.logical_or(kv > thr, jnp.logical_and(kv == thr, kpos <= cut))
        vis = jnp.logical_and(kpos <= qpos, kpos >= FRAME_OFF)
        bias_ref[pl.ds(k0, ts), :] = jnp.where(jnp.logical_and(sel, vis), 0.0, NEG).astype(bias_ref.dtype)
        return carry

    lax.fori_loop(0, nkb, bias_blk, 0)

    def fill_blk(kb, carry):
        k0 = pl.multiple_of(kb * ts, ts)
        bias_ref[pl.ds(k0, ts), :] = jnp.full((ts, tq), NEG, bias_ref.dtype)
        return carry

    lax.fori_loop(nkb, lp // ts, fill_blk, 0)


def dsa_select(iq, ik, iwt, topk):
    nh, lp, _ = iq.shape
    assert lp < 2 ** 15
    tq = 256
    ts = _pick(lp, (640, 256, 128))
    kern = functools.partial(_select_kernel, tq=tq, ts=ts, lp=lp, topk=topk)
    return pl.pallas_call(
        kern,
        grid=(lp // tq,),
        in_specs=[pl.BlockSpec((nh, tq, HEAD_DIM), lambda i: (0, i, 0)),
                  pl.BlockSpec((lp, HEAD_DIM), lambda i: (0, 0)),
                  pl.BlockSpec((nh, tq), lambda i: (0, i))],
        out_specs=pl.BlockSpec((lp, tq), lambda i: (0, i)),
        out_shape=jax.ShapeDtypeStruct((lp, lp), bf16),
        scratch_shapes=[pltpu.VMEM((lp, tq), i32)],
        compiler_params=_cparams(("parallel",)),
        name="dsa_select",
    )(iq, ik, iwt)


V_AUG = HEAD_DIM + 16


def _attn_kernel(q_ref, k_ref, vt_ref, b_ref, o_ref, m_ref, acc_ref, *, tq, ts):
    i = pl.program_id(0)
    nkb = ((i + 1) * tq + ts - 1) // ts
    group = ATT_HEADS // ATT_KV_HEADS
    qpos = i * tq + lax.broadcasted_iota(i32, (tq, 1), 0)
    eye = (lax.broadcasted_iota(i32, (tq, tq), 0)
           == lax.broadcasted_iota(i32, (tq, tq), 1)).astype(bf16)
    for g in range(ATT_KV_HEADS):
        qa = jnp.concatenate(
            [jnp.concatenate([q_ref[:, (g * group + r) * HEAD_DIM:(g * group + r + 1) * HEAD_DIM], eye],
                             axis=1) for r in range(group)], axis=0)
        m_ref[...] = jnp.full(m_ref.shape, -jnp.inf, f32)
        acc_ref[...] = jnp.zeros_like(acc_ref)

        def scores(k0):
            ka = jnp.concatenate([k_ref[pl.ds(k0, ts), g * HEAD_DIM:(g + 1) * HEAD_DIM],
                                  b_ref[pl.ds(k0, ts), :]], axis=1)
            return _dot_nt(ka, qa)

        def accumulate(k0, st):
            vt_ = vt_ref[g, :, pl.ds(k0, ts)]
            ps, alphas = [], []
            for r in range(group):
                sr = st[:, r * tq:(r + 1) * tq]
                m_old = m_ref[r:r + 1, :]
                m_new = jnp.maximum(m_old, jnp.max(sr, axis=0, keepdims=True))
                m_ref[r:r + 1, :] = m_new
                ps.append(jnp.exp2(sr - m_new).astype(bf16))
                alphas.append(jnp.exp2(m_old - m_new))
            pt = jnp.concatenate(ps, axis=1)
            acc_ref[...] = jnp.concatenate(alphas, axis=1) * acc_ref[...] + _dot(vt_, pt)

        def blk2(kp, carry):
            k0 = pl.multiple_of(kp * (2 * ts), ts)
            k1 = pl.multiple_of(k0 + ts, ts)
            st0 = scores(k0)
            st1 = scores(k1)
            accumulate(k0, st0)
            accumulate(k1, st1)
            return carry

        lax.fori_loop(0, (nkb + 1) // 2, blk2, 0)
        for r in range(group):
            ot = (acc_ref[0:HEAD_DIM, r * tq:(r + 1) * tq]
                  / acc_ref[HEAD_DIM:HEAD_DIM + 1, r * tq:(r + 1) * tq])
            orr = jnp.where(qpos >= FRAME_OFF, ot.T, 0.0)
            o_ref[:, (g * group + r) * HEAD_DIM:(g * group + r + 1) * HEAD_DIM] = orr.astype(o_ref.dtype)


def dsa_attention(q, k, vt_aug, bias_t):
    lp = q.shape[0]
    tq = 128
    ts = _pick(lp, (640, 256, 128))
    group = ATT_HEADS // ATT_KV_HEADS
    assert (lp // ts) % 2 == 0
    kern = functools.partial(_attn_kernel, tq=tq, ts=ts)
    return pl.pallas_call(
        kern,
        grid=(lp // tq,),
        in_specs=[pl.BlockSpec((tq, ATT_Q_W), lambda i: (i, 0)),
                  pl.BlockSpec((lp, ATT_KV_W), lambda i: (0, 0)),
                  pl.BlockSpec((ATT_KV_HEADS, V_AUG, lp), lambda i: (0, 0, 0)),
                  pl.BlockSpec((lp, tq), lambda i: (0, i))],
        out_specs=pl.BlockSpec((tq, ATT_Q_W), lambda i: (i, 0)),
        out_shape=jax.ShapeDtypeStruct((lp, ATT_Q_W), bf16),
        scratch_shapes=[pltpu.VMEM((group, tq), f32),
                        pltpu.VMEM((V_AUG, group * tq), f32)],
        compiler_params=_cparams(("parallel",)),
        name="dsa_attention",
    )(q, k, vt_aug, bias_t)


def _ffn_conv_kernel(g_ref, gh_ref, v_ref, vh_ref, wg_ref, wv_ref, bg_ref, bv_ref, o_ref,
                     eg_ref, ev_ref, *, tm):
    i = pl.program_id(0)

    def conv(x_ref, h_ref, w_ref, b_ref, ext_ref):
        ext_ref[0:HALO, :] = jnp.where(i > 0, h_ref[...].astype(f32), 0.0)
        ext_ref[HALO:HALO + tm, :] = x_ref[...].astype(f32)
        w = w_ref[...]
        y = ext_ref[HALO:HALO + tm, :] * w[FFN_CONV - 1:FFN_CONV, :]
        for j in range(FFN_CONV - 1):
            s0 = HALO - (FFN_CONV - 1) + j
            y = y + ext_ref[s0:s0 + tm, :] * w[j:j + 1, :]
        return y + b_ref[...]

    gate = conv(g_ref, gh_ref, wg_ref, bg_ref, eg_ref)
    val = conv(v_ref, vh_ref, wv_ref, bv_ref, ev_ref)
    o_ref[...] = (gate * _sigmoid(gate) * val).astype(o_ref.dtype)


def ffn_conv(up, conv_w, conv_b):
    lp = up.shape[0]
    tm = _pick(lp, (640, 256, 128))
    tc = 512
    nc = D_FF // tc
    kern = functools.partial(_ffn_conv_kernel, tm=tm)

    def main(off):
        return pl.BlockSpec((tm, tc), lambda i, c: (i, c + off))

    def halo(off):
        return pl.BlockSpec((HALO, tc), lambda i, c: (jnp.maximum(i * (tm // HALO) - 1, 0), c + off))

    def wspec(off):
        return pl.BlockSpec((FFN_CONV, tc), lambda i, c: (0, c + off))

    def bspec(off):
        return pl.BlockSpec((1, tc), lambda i, c: (0, c + off))

    b2 = conv_b.reshape(1, 2 * D_FF)
    return pl.pallas_call(
        kern,
        grid=(lp // tm, nc),
        in_specs=[main(0), halo(0), main(nc), halo(nc), wspec(0), wspec(nc), bspec(0), bspec(nc)],
        out_specs=pl.BlockSpec((tm, tc), lambda i, c: (i, c)),
        out_shape=jax.ShapeDtypeStruct((lp, D_FF), bf16),
        scratch_shapes=[pltpu.VMEM((HALO + tm, tc), f32), pltpu.VMEM((HALO + tm, tc), f32)],
        compiler_params=_cparams(("parallel", "parallel")),
        name="ffn_conv",
    )(up, up, up, up, conv_w, conv_w, b2, b2)


def _split_w_in(w):
    o = 0
    parts = {}
    for name, width in (("gq", GDN_QK_W), ("gk", GDN_QK_W), ("gv", GDN_V_W), ("gz", GDN_V_W),
                        ("gb", GDN_V_HEADS), ("ga", GDN_V_HEADS), ("aq", ATT_Q_W), ("ak", ATT_KV_W),
                        ("av", ATT_KV_W), ("iq", IDX_Q_W), ("ik", HEAD_DIM), ("iw", IDX_HEADS),
                        ("gate_gdn", D_MODEL), ("gate_att", D_MODEL)):
        parts[name] = w[:, o:o + width]
        o += width
    big = jnp.concatenate([parts[n] for n in ("gq", "gk", "gv", "gz", "aq", "iq", "gate_gdn",
                                              "gate_att", "ak", "av", "ik")], axis=1).astype(bf16)
    small = jnp.concatenate([parts["gb"], parts["ga"], parts["iw"]], axis=1)
    small = jnp.pad(small, ((0, 0), (0, LANES - small.shape[1]))).astype(bf16)
    return big, small


def _layer(h0, p, lp, topk):
    tm = _pick(lp, (640, 256, 128))
    tm_big = _pick(lp, (1280, 640, 256, 128))
    w_big, w_small = _split_w_in(p["w_in"])

    u1 = rms_rows(h0, p["mix_pre_g"])
    (proj,) = matmul(u1, w_big, tm=tm, tn=1920, tk=D_MODEL, epilogue=_ep_cast,
                     out_shapes=[jax.ShapeDtypeStruct((lp, PROJ_W), bf16)], name="proj_in")
    (small,) = matmul(u1, w_small, tm=tm_big, tn=LANES, tk=D_MODEL, epilogue=_ep_cast,
                      out_shapes=[jax.ShapeDtypeStruct((lp, LANES), f32)], name="proj_small")
    gb = small[:, 0:GDN_V_HEADS]
    ga = small[:, GDN_V_HEADS:2 * GDN_V_HEADS]
    iw = small[:, 2 * GDN_V_HEADS:2 * GDN_V_HEADS + IDX_HEADS]

    qkv = gdn_prep(proj, p["gdn_conv_w"])
    kt = qkv[:, GDN_QK_W:2 * GDN_QK_W].T
    beta, gcum = gdn_gates(gb, ga, p["gdn_a_log"], p["gdn_dt_bias"])
    gcol = gcum.T[:, :, None]
    grow = gcum.T[:, None, :]
    bcol = beta.T[:, :, None]
    brow = beta.T[:, None, :]
    o_gdn = gdn_core(qkv, kt, proj, p["gdn_norm_g"], gcol, grow, bcol, brow)

    tabs = rope_tables(lp)
    aq = rope(proj, COL_AQ, ATT_HEADS, tabs, scale=HEAD_DIM ** -0.5 * LOG2E)
    ak = rope(proj, COL_AK, ATT_KV_HEADS, tabs)
    iq = rope(proj, COL_IQ, IDX_HEADS, tabs, head_major=True)
    ik = rope(proj, COL_IK, 1, tabs)
    avt = proj[:, COL_AV:COL_AV + ATT_KV_W].T.reshape(ATT_KV_HEADS, HEAD_DIM, lp)
    avt = jnp.concatenate([avt, jnp.ones((ATT_KV_HEADS, V_AUG - HEAD_DIM, lp), bf16)], axis=1)
    bias_t = dsa_select(iq, ik, iw.T, topk)
    o_att = dsa_attention(aq, ak, avt, bias_t)

    tn = 512
    gate_spec = lambda col0: ((tm_big, tn), (lambda i, j, c=col0 // tn: (i, c + j)))
    (m1,) = matmul(o_gdn, p["w_branch_gdn"].astype(bf16), tm=tm_big, tn=tn, tk=2048,
                   epilogue=_ep_gate, extra=(proj,), extra_specs=(gate_spec(COL_GATE_GDN),),
                   out_shapes=[jax.ShapeDtypeStruct((lp, D_MODEL), f32)], name="branch_gdn")
    (merged,) = matmul(o_att, p["w_branch_att"].astype(bf16), tm=tm_big, tn=tn, tk=2048,
                       epilogue=_ep_gate_add, extra=(proj, m1),
                       extra_specs=(gate_spec(COL_GATE_ATT), ((tm_big, tn), lambda i, j: (i, j))),
                       out_shapes=[jax.ShapeDtypeStruct((lp, D_MODEL), bf16)], name="branch_att")
    row_spec = ((tm, D_MODEL), lambda i, j: (i, 0))
    vec_spec = ((1, D_MODEL), lambda i, j: (0, 0))
    h1, u2 = matmul(merged, p["w_out"].astype(bf16), tm=tm, tn=D_MODEL, tk=1024,
                    epilogue=_ep_res_norm2,
                    extra=(h0, p["mix_post_g"].reshape(1, D_MODEL), p["ffn_pre_g"].reshape(1, D_MODEL)),
                    extra_specs=(row_spec, vec_spec, vec_spec),
                    out_shapes=[jax.ShapeDtypeStruct((lp, D_MODEL), f32),
                                jax.ShapeDtypeStruct((lp, D_MODEL), bf16)], name="w_out")

    (up,) = matmul(u2, p["w_up"].astype(bf16), tm=tm_big, tn=1024, tk=D_MODEL, epilogue=_ep_cast,
                   out_shapes=[jax.ShapeDtypeStruct((lp, 2 * D_FF), bf16)], name="w_up")
    act = ffn_conv(up, p["ffn_conv_w"], p["ffn_conv_b"])
    (h2,) = matmul(act, p["w_down"].astype(bf16), tm=tm, tn=D_MODEL, tk=1024, epilogue=_ep_res_norm,
                   extra=(h1, p["ffn_post_g"].reshape(1, D_MODEL)),
                   extra_specs=(row_spec, vec_spec),
                   out_shapes=[jax.ShapeDtypeStruct((lp, D_MODEL), f32)], name="w_down")
    return h2


def kernel(x, meta_tokens, mix_pre_g, w_in, gdn_conv_w, gdn_a_log, gdn_dt_bias, gdn_norm_g,
           w_branch_gdn, w_branch_att, w_out, mix_post_g, ffn_pre_g, w_up, ffn_conv_w,
           ffn_conv_b, w_down, ffn_post_g):
    batch, seq, d = x.shape
    assert batch == 1 and d == D_MODEL
    lp = FRAME_X0 + seq
    topk = min(TOPK_MAX, (N_META + seq) // 4)
    h = jnp.concatenate([jnp.zeros((FRAME_OFF, d), x.dtype), meta_tokens.astype(x.dtype), x[0]], axis=0)
    for i in range(w_in.shape[0]):
        p = dict(mix_pre_g=mix_pre_g[i], w_in=w_in[i], gdn_conv_w=gdn_conv_w[i], gdn_a_log=gdn_a_log[i],
                 gdn_dt_bias=gdn_dt_bias[i], gdn_norm_g=gdn_norm_g[i], w_branch_gdn=w_branch_gdn[i],
                 w_branch_att=w_branch_att[i], w_out=w_out[i], mix_post_g=mix_post_g[i],
                 ffn_pre_g=ffn_pre_g[i], w_up=w_up[i], ffn_conv_w=ffn_conv_w[i],
                 ffn_conv_b=ffn_conv_b[i], w_down=w_down[i], ffn_post_g=ffn_post_g[i])
        h = _layer(h, p, lp, topk)
    return h[FRAME_X0:][None]
```

```python
import functools
import struct

import jax
import jax.numpy as jnp
from jax import lax
from jax.experimental import pallas as pl
from jax.experimental.pallas import tpu as pltpu

f32 = jnp.float32
bf16 = jnp.bfloat16
i32 = jnp.int32

D_MODEL = 2048
N_META = 16
EPS = 1e-6
GDN_QK_HEADS = 16
GDN_V_HEADS = 32
HEAD_DIM = 128
GDN_CONV = 4
ATT_HEADS = 16
ATT_KV_HEADS = 2
IDX_HEADS = 16
TOPK_MAX = 256
NEG = -1e30
LOG2E = 1.4426950408889634
ROPE_THETA = 500000.0
ROPE_DIMS = HEAD_DIM // 4
D_FF = 3 * D_MODEL
FFN_CONV = 3
GDN_QK_W = GDN_QK_HEADS * HEAD_DIM
GDN_V_W = GDN_V_HEADS * HEAD_DIM
ATT_Q_W = ATT_HEADS * HEAD_DIM
ATT_KV_W = ATT_KV_HEADS * HEAD_DIM
IDX_Q_W = IDX_HEADS * HEAD_DIM

FRAME_X0 = 256
FRAME_OFF = FRAME_X0 - N_META
CHUNK = 128

LANES = 128
VMEM_LIMIT = 56 * 1024 * 1024

COL_GQKV = 0
COL_GZ = COL_GQKV + 2 * GDN_QK_W + GDN_V_W
COL_AQ = COL_GZ + GDN_V_W
COL_IQ = COL_AQ + ATT_Q_W
COL_GATE_GDN = COL_IQ + IDX_Q_W
COL_GATE_ATT = COL_GATE_GDN + D_MODEL
COL_AK = COL_GATE_ATT + D_MODEL
COL_AV = COL_AK + ATT_KV_W
COL_IK = COL_AV + ATT_KV_W
PROJ_W = COL_IK + HEAD_DIM


def _cparams(sem):
    return pltpu.CompilerParams(dimension_semantics=sem, vmem_limit_bytes=VMEM_LIMIT)


def _pick(n, cands):
    for c in cands:
        if n % c == 0:
            return c
    raise ValueError(f"no tile for {n} in {cands}")


def _sigmoid(x):
    return 1.0 / (1.0 + jnp.exp(-x))


def _dot(a, b):
    return jnp.dot(a, b, preferred_element_type=f32)


def _dot_nt(a, b):
    return lax.dot_general(a, b, (((1,), (1,)), ((), ())), preferred_element_type=f32)


def _rms_rows_kernel(h_ref, g_ref, o_ref):
    h = h_ref[...]
    y = h * lax.rsqrt(jnp.mean(h * h, axis=-1, keepdims=True) + EPS)
    o_ref[...] = (y * g_ref[...]).astype(o_ref.dtype)


def rms_rows(h, g):
    lp, d = h.shape
    tm = _pick(lp, (640, 256, 128))
    return pl.pallas_call(
        _rms_rows_kernel,
        grid=(lp // tm,),
        in_specs=[pl.BlockSpec((tm, d), lambda i: (i, 0)),
                  pl.BlockSpec((1, d), lambda i: (0, 0))],
        out_specs=pl.BlockSpec((tm, d), lambda i: (i, 0)),
        out_shape=jax.ShapeDtypeStruct((lp, d), bf16),
        compiler_params=_cparams(("parallel",)),
        name="rms_rows",
    )(h, g.reshape(1, d))


def _mm_kernel(*refs, nk, n_extra, n_out, epilogue):
    a_ref, w_ref = refs[0], refs[1]
    extra = refs[2:2 + n_extra]
    outs = refs[2 + n_extra:2 + n_extra + n_out]
    if nk == 1:
        epilogue(_dot(a_ref[...], w_ref[...]), extra, outs)
        return
    acc_ref = refs[-1]
    k = pl.program_id(2)

    @pl.when(k == 0)
    def _():
        acc_ref[...] = jnp.zeros_like(acc_ref)

    acc_ref[...] += _dot(a_ref[...], w_ref[...])

    @pl.when(k == nk - 1)
    def _():
        epilogue(acc_ref[...], extra, outs)


def matmul(a, w, *, tm, tn, tk, epilogue, extra=(), extra_specs=(), out_shapes, a_col0=0, name):
    m = a.shape[0]
    kdim, n = w.shape
    nk = kdim // tk
    assert m % tm == 0 and n % tn == 0 and kdim % tk == 0
    in_specs = [pl.BlockSpec((tm, tk), lambda i, j, k: (i, a_col0 + k)),
                pl.BlockSpec((tk, tn), lambda i, j, k: (k, j))]
    in_specs += [pl.BlockSpec(bs, (lambda i, j, k, f=f: f(i, j))) for bs, f in extra_specs]
    out_specs = [pl.BlockSpec((tm, tn), lambda i, j, k: (i, j)) for _ in out_shapes]
    scratch = [] if nk == 1 else [pltpu.VMEM((tm, tn), f32)]
    kern = functools.partial(_mm_kernel, nk=nk, n_extra=len(extra), n_out=len(out_shapes),
                             epilogue=epilogue)
    res = pl.pallas_call(
        kern,
        grid=(m // tm, n // tn, nk),
        in_specs=in_specs,
        out_specs=out_specs,
        out_shape=out_shapes,
        scratch_shapes=scratch,
        compiler_params=_cparams(("parallel", "parallel", "arbitrary")),
        name=name,
    )(a, w, *extra)
    return res


def _ep_cast(acc, extra, outs):
    outs[0][...] = acc.astype(outs[0].dtype)


def _ep_gate(acc, extra, outs):
    g = extra[0][...].astype(f32)
    outs[0][...] = (_sigmoid(g) * acc).astype(outs[0].dtype)


def _ep_gate_add(acc, extra, outs):
    g = extra[0][...].astype(f32)
    outs[0][...] = (extra[1][...].astype(f32) + _sigmoid(g) * acc).astype(outs[0].dtype)


def _rms(t, g):
    return t * lax.rsqrt(jnp.mean(t * t, axis=-1, keepdims=True) + EPS) * g


def _ep_res_norm2(acc, extra, outs):
    h_ref, g_ref, g2_ref = extra
    h1 = h_ref[...] + _rms(acc, g_ref[...])
    outs[0][...] = h1
    outs[1][...] = _rms(h1, g2_ref[...]).astype(outs[1].dtype)


def _ep_res_norm(acc, extra, outs):
    h_ref, g_ref = extra
    outs[0][...] = h_ref[...] + _rms(acc, g_ref[...])


HALO = 16


def _gdn_prep_kernel(x_ref, halo_ref, w_ref, o_ref, ext_ref, *, tm, tc):
    i = pl.program_id(0)
    c = pl.program_id(1)
    halo = halo_ref[...].astype(f32)
    ext_ref[0:HALO, :] = jnp.where(i > 0, halo, 0.0)
    ext_ref[HALO:HALO + tm, :] = x_ref[...].astype(f32)
    w = w_ref[...]
    y = ext_ref[HALO:HALO + tm, :] * w[GDN_CONV - 1:GDN_CONV, :]
    for j in range(GDN_CONV - 1):
        s0 = HALO - (GDN_CONV - 1) + j
        y = y + ext_ref[s0:s0 + tm, :] * w[j:j + 1, :]
    s = y * _sigmoid(y)
    is_q = c < (GDN_QK_W // tc)
    is_qk = c < (2 * GDN_QK_W // tc)
    qscale = jnp.where(is_q, HEAD_DIM ** -0.5, 1.0).astype(f32)
    for hh in range(tc // HEAD_DIM):
        seg = s[:, hh * HEAD_DIM:(hh + 1) * HEAD_DIM]
        r = lax.rsqrt(jnp.sum(seg * seg, axis=-1, keepdims=True) + EPS) * qscale
        fac = jnp.where(is_qk, r, 1.0)
        o_ref[:, hh * HEAD_DIM:(hh + 1) * HEAD_DIM] = (seg * fac).astype(o_ref.dtype)


def gdn_prep(proj, conv_w):
    lp = proj.shape[0]
    width = 2 * GDN_QK_W + GDN_V_W
    tm = _pick(lp, (640, 256, 128))
    tc = 512
    kern = functools.partial(_gdn_prep_kernel, tm=tm, tc=tc)
    return pl.pallas_call(
        kern,
        grid=(lp // tm, width // tc),
        in_specs=[pl.BlockSpec((tm, tc), lambda i, c: (i, c)),
                  pl.BlockSpec((HALO, tc), lambda i, c: (jnp.maximum(i * (tm // HALO) - 1, 0), c)),
                  pl.BlockSpec((GDN_CONV, tc), lambda i, c: (0, c))],
        out_specs=pl.BlockSpec((tm, tc), lambda i, c: (i, c)),
        out_shape=jax.ShapeDtypeStruct((lp, width), bf16),
        scratch_shapes=[pltpu.VMEM((HALO + tm, tc), f32)],
        compiler_params=_cparams(("parallel", "parallel")),
        name="gdn_prep",
    )(proj, proj, conv_w)


def _gdn_gates_kernel(b_ref, a_ref, alog_ref, dt_ref, beta_ref, gcum_ref):
    i = pl.program_id(0)
    rows = i * CHUNK + lax.broadcasted_iota(i32, (CHUNK, 1), 0)
    valid = rows >= FRAME_OFF
    beta_ref[...] = jnp.where(valid, _sigmoid(b_ref[...]), 0.0)
    a = a_ref[...] + dt_ref[...]
    sp = jnp.maximum(a, 0.0) + jnp.log1p(jnp.exp(-jnp.abs(a)))
    g = jnp.where(valid, -jnp.exp(alog_ref[...]) * sp, 0.0)
    tri = (lax.broadcasted_iota(i32, (CHUNK, CHUNK), 0)
           >= lax.broadcasted_iota(i32, (CHUNK, CHUNK), 1)).astype(f32)
    gcum_ref[...] = jnp.dot(tri, g, preferred_element_type=f32, precision=lax.Precision.HIGHEST)


def gdn_gates(gb, ga, a_log, dt_bias):
    lp, nh = gb.shape
    spec = pl.BlockSpec((CHUNK, nh), lambda i: (i, 0))
    vec = pl.BlockSpec((1, nh), lambda i: (0, 0))
    return pl.pallas_call(
        _gdn_gates_kernel,
        grid=(lp // CHUNK,),
        in_specs=[spec, spec, vec, vec],
        out_specs=[spec, spec],
        out_shape=[jax.ShapeDtypeStruct((lp, nh), f32)] * 2,
        compiler_params=_cparams(("parallel",)),
        name="gdn_gates",
    )(gb, ga, a_log.reshape(1, nh), dt_bias.reshape(1, nh))


def _block_mask(size):
    r = lax.broadcasted_iota(i32, (CHUNK, CHUNK), 0) // size
    c = lax.broadcasted_iota(i32, (CHUNK, CHUNK), 1) // size
    return r == c


def _unit_lower_inverse_many(mats):
    row = lax.broadcasted_iota(i32, (CHUNK, CHUNK), 0)
    col = lax.broadcasted_iota(i32, (CHUNK, CHUNK), 1)
    eye = (row == col).astype(f32)
    base = 8
    m_prev = _block_mask(base)
    ads = [jnp.where(m_prev, a, 0.0) for a in mats]
    adbs = [ad.astype(bf16) for ad in ads]
    a2s = [_dot(x, x) for x in adbs]
    a2bs = [x.astype(bf16) for x in a2s]
    a4s = [_dot(x, x) for x in a2bs]
    xs = [_dot((eye - ad).astype(bf16), (eye + a2).astype(bf16)) for ad, a2 in zip(ads, a2s)]
    xs = [_dot(x.astype(bf16), (eye + a4).astype(bf16)) for x, a4 in zip(xs, a4s)]
    size = base * 2
    while size <= CHUNK:
        m_cur = _block_mask(size)
        off_diag = jnp.logical_and(m_cur, jnp.logical_not(m_prev))
        bs = [jnp.where(off_diag, a, 0.0).astype(bf16) for a in mats]
        xbs = [x.astype(bf16) for x in xs]
        ys = [_dot(xb, b).astype(bf16) for xb, b in zip(xbs, bs)]
        xs = [x - _dot(y, xb) for x, y, xb in zip(xs, ys, xbs)]
        m_prev = m_cur
        size *= 2
    return xs


def _gdn_kernel(q_ref, k_ref, v_ref, z_ref, ng_ref,
                gc0_ref, gc1_ref, gr0_ref, gr1_ref, bc0_ref, bc1_ref, br0_ref, br1_ref,
                o_ref, s_ref, *, rb):
    r = pl.program_id(1)

    @pl.when(r == 0)
    def _():
        s_ref[...] = jnp.zeros_like(s_ref)

    gcs, grs = (gc0_ref, gc1_ref), (gr0_ref, gr1_ref)
    bcs, brs = (bc0_ref, bc1_ref), (br0_ref, br1_ref)
    row = lax.broadcasted_iota(i32, (CHUNK, CHUNK), 0)
    col = lax.broadcasted_iota(i32, (CHUNK, CHUNK), 1)
    incl = row >= col
    strict = row > col
    ng = ng_ref[...]
    nchunk = rb // CHUNK
    heads = range(2)

    def rows(c):
        return slice(c * CHUNK, (c + 1) * CHUNK)

    def lanes(e):
        return slice(e * HEAD_DIM, (e + 1) * HEAD_DIM)

    qs = [q_ref[rows(c), :] for c in range(nchunk)]
    ks = [k_ref[rows(c), :] for c in range(nchunk)]
    kts = [k.astype(f32).T for k in ks]
    kks = [_dot_nt(k, k) for k in ks]
    qks = [_dot_nt(q, k) for q, k in zip(qs, ks)]
    chains = [(c, e) for c in range(nchunk) for e in heads]
    gc = {ce: gcs[ce[1]][rows(ce[0]), :] for ce in chains}
    gr = {ce: grs[ce[1]][:, rows(ce[0])] for ce in chains}
    bc = {ce: bcs[ce[1]][rows(ce[0]), :] for ce in chains}
    br = {ce: brs[ce[1]][:, rows(ce[0])] for ce in chains}
    dec = {ce: jnp.where(incl, jnp.exp(jnp.where(incl, gc[ce] - gr[ce], 0.0)), 0.0) for ce in chains}
    amat = [jnp.where(strict, bc[ce] * kks[ce[0]] * dec[ce], 0.0) for ce in chains]
    tinv = dict(zip(chains, _unit_lower_inverse_many(amat)))
    u = {ce: _dot((tinv[ce] * br[ce]).astype(bf16), v_ref[rows(ce[0]), lanes(ce[1])]) for ce in chains}
    w = {ce: _dot((tinv[ce] * (br[ce] * jnp.exp(gr[ce]))).astype(bf16), ks[ce[0]]) for ce in chains}
    g_last = {ce: gr[ce][:, CHUNK - 1:CHUNK] for ce in chains}
    wq = {ce: jnp.concatenate([w[ce].astype(bf16), qs[ce[0]]], axis=0) for ce in chains}
    ak = {ce: jnp.concatenate(
        [jnp.where(incl, qks[ce[0]] * dec[ce], 0.0).astype(bf16),
         (kts[ce[0]] * jnp.exp(g_last[ce] - gr[ce])).astype(bf16)], axis=0) for ce in chains}
    eg = {ce: jnp.exp(gc[ce]) for ce in chains}
    egl = {ce: jnp.exp(g_last[ce]) for ce in chains}

    state = [s_ref[e] for e in heads]
    for c in range(nchunk):
        sb = [state[e].astype(bf16) for e in heads]
        ws = [_dot(wq[(c, e)], sb[e]) for e in heads]
        vb = [(u[(c, e)] - ws[e][:CHUNK]).astype(bf16) for e in heads]
        av = [_dot(ak[(c, e)], vb[e]) for e in heads]
        for e in heads:
            o = eg[(c, e)] * ws[e][CHUNK:] + av[e][:CHUNK]
            state[e] = state[e] * egl[(c, e)] + av[e][CHUNK:]
            z = z_ref[rows(c), lanes(e)].astype(f32)
            o_ref[rows(c), lanes(e)] = (_rms(o, ng) * (z * _sigmoid(z))).astype(o_ref.dtype)
    for e in heads:
        s_ref[e] = state[e]


def gdn_core(qkv, proj, norm_g, gcol, grow, bcol, brow):
    lp = qkv.shape[0]
    rb = _pick(lp, (640, 256, 128))
    nqk = GDN_QK_HEADS
    kcol0 = GDN_QK_W // HEAD_DIM
    vcol0 = 2 * GDN_QK_W // (2 * HEAD_DIM)
    zcol0 = COL_GZ // (2 * HEAD_DIM)

    def colspec(e):
        return pl.BlockSpec((None, rb, 1), lambda j, r: (2 * j + e, r, 0))

    def rowspec(e):
        return pl.BlockSpec((None, 1, rb), lambda j, r: (2 * j + e, 0, r))

    kern = functools.partial(_gdn_kernel, rb=rb)
    return pl.pallas_call(
        kern,
        grid=(nqk, lp // rb),
        in_specs=[pl.BlockSpec((rb, HEAD_DIM), lambda j, r: (r, j)),
                  pl.BlockSpec((rb, HEAD_DIM), lambda j, r: (r, kcol0 + j)),
                  pl.BlockSpec((rb, 2 * HEAD_DIM), lambda j, r: (r, vcol0 + j)),
                  pl.BlockSpec((rb, 2 * HEAD_DIM), lambda j, r: (r, zcol0 + j)),
                  pl.BlockSpec((1, HEAD_DIM), lambda j, r: (0, 0)),
                  colspec(0), colspec(1), rowspec(0), rowspec(1),
                  colspec(0), colspec(1), rowspec(0), rowspec(1)],
        out_specs=pl.BlockSpec((rb, 2 * HEAD_DIM), lambda j, r: (r, j)),
        out_shape=jax.ShapeDtypeStruct((lp, GDN_V_W), bf16),
        scratch_shapes=[pltpu.VMEM((2, HEAD_DIM, HEAD_DIM), f32)],
        compiler_params=_cparams(("parallel", "arbitrary")),
        name="gdn_core",
    )(qkv, qkv, qkv, proj, norm_g.reshape(1, HEAD_DIM),
      gcol, gcol, grow, grow, bcol, bcol, brow, brow)


def _rope_kernel(x_ref, c_ref, s1_ref, s2_ref, o_ref, *, nh, scale, head_major):
    cc = c_ref[...]
    s1 = s1_ref[...]
    s2 = s2_ref[...]
    half = ROPE_DIMS // 2
    for h in range(nh):
        seg = x_ref[:, h * HEAD_DIM:(h + 1) * HEAD_DIM].astype(f32)
        y = (seg * cc + pltpu.roll(seg, half, 1) * s1
             + pltpu.roll(seg, HEAD_DIM - half, 1) * s2)
        if scale != 1.0:
            y = y * scale
        if head_major:
            o_ref[h] = y.astype(o_ref.dtype)
        else:
            o_ref[:, h * HEAD_DIM:(h + 1) * HEAD_DIM] = y.astype(o_ref.dtype)


def rope(proj, col0, nh, tabs, *, scale=1.0, head_major=False):
    lp = proj.shape[0]
    tm = _pick(lp, (640, 256, 128))
    w = nh * HEAD_DIM
    assert col0 % w == 0
    tab = pl.BlockSpec((tm, HEAD_DIM), lambda i: (i, 0))
    if head_major:
        out_spec = pl.BlockSpec((nh, tm, HEAD_DIM), lambda i: (0, i, 0))
        out_shape = jax.ShapeDtypeStruct((nh, lp, HEAD_DIM), bf16)
    else:
        out_spec = pl.BlockSpec((tm, w), lambda i: (i, 0))
        out_shape = jax.ShapeDtypeStruct((lp, w), bf16)
    kern = functools.partial(_rope_kernel, nh=nh, scale=scale, head_major=head_major)
    return pl.pallas_call(
        kern,
        grid=(lp // tm,),
        in_specs=[pl.BlockSpec((tm, w), lambda i: (i, col0 // w)), tab, tab, tab],
        out_specs=out_spec,
        out_shape=out_shape,
        compiler_params=_cparams(("parallel",)),
        name=f"rope_{col0}",
    )(proj, *tabs)


def rope_tables(lp):
    half = ROPE_DIMS // 2
    pos = (jnp.arange(lp, dtype=jnp.int32) - FRAME_OFF).astype(f32)
    inv = ROPE_THETA ** (-jnp.arange(half, dtype=f32) / half)
    ang = pos[:, None] * inv[None, :]
    cos, sin = jnp.cos(ang), jnp.sin(ang)
    zeros = jnp.zeros((lp, HEAD_DIM - ROPE_DIMS), f32)
    z16 = jnp.zeros((lp, half), f32)
    cc = jnp.concatenate([cos, cos, jnp.ones_like(zeros)], axis=1)
    s1 = jnp.concatenate([z16, sin, zeros], axis=1)
    s2 = jnp.concatenate([-sin, z16, zeros], axis=1)
    return cc, s1, s2


INT_MIN = -2 ** 31
INT_MAX = 2 ** 31 - 1


def _sortable(x):
    b = lax.bitcast_convert_type(x, i32)
    return b ^ (lax.shift_right_arithmetic(b, 31) & INT_MAX)


_NEG_BITS = struct.unpack("<i", struct.pack("<f", NEG))[0]
_KEY_NEG = _NEG_BITS ^ ((_NEG_BITS >> 31) & INT_MAX)


def _sum_sublane_groups(m):
    parts = [m[j * 8:(j + 1) * 8, :] for j in range(m.shape[0] // 8)]
    while len(parts) > 1:
        nxt = [parts[j] + parts[j + 1] for j in range(0, len(parts) - 1, 2)]
        if len(parts) % 2:
            nxt.append(parts[-1])
        parts = nxt
    return parts[0]


def _select_kernel(iq_ref, ik_ref, iwt_ref, bias_ref, key_ref, *, tq, ts, lp, topk):
    i = pl.program_id(0)
    nkb = ((i + 1) * tq + ts - 1) // ts
    rem = lp - nkb * ts
    iwt = iwt_ref[...] * ((IDX_HEADS ** -0.5) * (HEAD_DIM ** -0.5))
    qpos = i * tq + lax.broadcasted_iota(i32, (1, tq), 1)
    sub = lax.broadcasted_iota(i32, (ts, 1), 0)

    def score_blk(kb, carry):
        k0 = pl.multiple_of(kb * ts, ts)
        ikb = ik_ref[pl.ds(k0, ts), :]
        acc = jnp.zeros((ts, tq), f32)
        for h in range(IDX_HEADS):
            lg = _dot_nt(ikb, iq_ref[h])
            acc = acc + jnp.maximum(lg, 0.0) * iwt[h:h + 1, :]
        kpos = k0 + sub
        sc = jnp.where(kpos <= qpos, acc, NEG)
        sc = jnp.where(kpos < FRAME_OFF, -jnp.inf, sc)
        key_ref[pl.ds(k0, ts), :] = _sortable(sc)
        return carry

    lax.fori_loop(0, nkb, score_blk, 0)

    sub8 = lax.broadcasted_iota(i32, (8, 1), 0)
    n_acc = 4

    def count(pred_fn):
        def blk(kb, cnts):
            k0 = pl.multiple_of(kb * ts, ts)
            cnts = list(cnts)
            blk_ref = key_ref.at[pl.ds(k0, ts)]
            for j in range(ts // 8):
                kv = blk_ref[j * 8:(j + 1) * 8, :]
                cnts[j % n_acc] = cnts[j % n_acc] + jnp.where(pred_fn(kv, k0 + j * 8 + sub8), 1, 0)
            return tuple(cnts)
        cnts = lax.fori_loop(0, nkb, blk, tuple(jnp.zeros((8, tq), i32) for _ in range(n_acc)))
        return jnp.sum(sum(cnts[1:], cnts[0]), axis=0, keepdims=True)

    def count_ge(cand):
        return count(lambda kv, kpos: kv >= cand) + jnp.where(cand <= _KEY_NEG, rem, 0)

    def bis_cond(c):
        it, _, n_tau = c
        return jnp.logical_and(it < 32, jnp.max(jnp.where(n_tau == topk, 0, 1)) > 0)

    def bis(c):
        it, tau, n_tau = c
        bit = lax.shift_left(jnp.int32(1), 31 - it)
        t2 = tau | bit
        n2 = count_ge(t2 ^ INT_MIN)
        ok = n2 >= topk
        return it + 1, jnp.where(ok, t2, tau), jnp.where(ok, n2, n_tau)

    _, tau, n_ge = lax.while_loop(
        bis_cond, bis, (jnp.int32(0), jnp.zeros((1, tq), i32), jnp.full((1, tq), lp, i32)))
    thr = tau ^ INT_MIN
    has_tie = jnp.max(jnp.where(n_ge > topk, 1, 0)) > 0

    def tie_cut():
        n_gt = (count(lambda kv, kpos: kv > thr) + jnp.where(thr < _KEY_NEG, rem, 0))
        need = topk - n_gt

        def bis2(it, cut):
            bit = lax.shift_left(jnp.int32(1), 14 - it)
            c2 = cut | bit
            n = count(lambda kv, kpos: jnp.logical_and(kv == thr, kpos < c2))
            return jnp.where(n < need, c2, cut)
        return lax.fori_loop(0, 15, bis2, jnp.zeros((1, tq), i32))

    cut = lax.cond(has_tie, tie_cut, lambda: jnp.full((1, tq), INT_MAX, i32))

    def bias_blk(kb, carry):
        k0 = pl.multiple_of(kb * ts, ts)
        kv = key_ref[pl.ds(k0, ts), :]
        kpos = k0 + sub
        sel = jnp.logical_or(kv > thr, jnp.logical_and(kv == thr, kpos <= cut))
        vis = jnp.logical_and(kpos <= qpos, kpos >= FRAME_OFF)
        bias_ref[pl.ds(k0, ts), :] = jnp.where(jnp.logical_and(sel, vis), 0.0, NEG).astype(bias_ref.dtype)
        return carry

    lax.fori_loop(0, nkb, bias_blk, 0)

    def fill_blk(kb, carry):
        k0 = pl.multiple_of(kb * ts, ts)
        bias_ref[pl.ds(k0, ts), :] = jnp.full((ts, tq), NEG, bias_ref.dtype)
        return carry

    lax.fori_loop(nkb, lp // ts, fill_blk, 0)


def dsa_select(iq, ik, iwt, topk):
    nh, lp, _ = iq.shape
    assert lp < 2 ** 15
    tq = 256
    ts = _pick(lp, (640, 256, 128))
    kern = functools.partial(_select_kernel, tq=tq, ts=ts, lp=lp, topk=topk)
    return pl.pallas_call(
        kern,
        grid=(lp // tq,),
        in_specs=[pl.BlockSpec((nh, tq, HEAD_DIM), lambda i: (0, i, 0)),
                  pl.BlockSpec((lp, HEAD_DIM), lambda i: (0, 0)),
                  pl.BlockSpec((nh, tq), lambda i: (0, i))],
        out_specs=pl.BlockSpec((lp, tq), lambda i: (0, i)),
        out_shape=jax.ShapeDtypeStruct((lp, lp), bf16),
        scratch_shapes=[pltpu.VMEM((lp, tq), i32)],
        compiler_params=_cparams(("parallel",)),
        name="dsa_select",
    )(iq, ik, iwt)


V_AUG = HEAD_DIM + 16


def _attn_kernel(q_ref, k_ref, vt_ref, b_ref, o_ref, m_ref, acc_ref, *, tq, ts):
    i = pl.program_id(0)
    nkb = ((i + 1) * tq + ts - 1) // ts
    group = ATT_HEADS // ATT_KV_HEADS
    qpos = i * tq + lax.broadcasted_iota(i32, (tq, 1), 0)
    eye = (lax.broadcasted_iota(i32, (tq, tq), 0)
           == lax.broadcasted_iota(i32, (tq, tq), 1)).astype(bf16)
    for g in range(ATT_KV_HEADS):
        qa = jnp.concatenate(
            [jnp.concatenate([q_ref[:, (g * group + r) * HEAD_DIM:(g * group + r + 1) * HEAD_DIM], eye],
                             axis=1) for r in range(group)], axis=0)
        m_ref[...] = jnp.full(m_ref.shape, -jnp.inf, f32)
        acc_ref[...] = jnp.zeros_like(acc_ref)

        def scores(k0):
            ka = jnp.concatenate([k_ref[pl.ds(k0, ts), g * HEAD_DIM:(g + 1) * HEAD_DIM],
                                  b_ref[pl.ds(k0, ts), :]], axis=1)
            return _dot_nt(ka, qa)

        def accumulate(k0, st):
            vt_ = vt_ref[g, :, pl.ds(k0, ts)]
            ps, alphas = [], []
            for r in range(group):
                sr = st[:, r * tq:(r + 1) * tq]
                m_old = m_ref[r:r + 1, :]
                m_new = jnp.maximum(m_old, jnp.max(sr, axis=0, keepdims=True))
                m_ref[r:r + 1, :] = m_new
                ps.append(jnp.exp2(sr - m_new).astype(bf16))
                alphas.append(jnp.exp2(m_old - m_new))
            pt = jnp.concatenate(ps, axis=1)
            acc_ref[...] = jnp.concatenate(alphas, axis=1) * acc_ref[...] + _dot(vt_, pt)

        def blk2(kp, carry):
            k0 = pl.multiple_of(kp * (2 * ts), ts)
            k1 = pl.multiple_of(k0 + ts, ts)
            st0 = scores(k0)
            st1 = scores(k1)
            accumulate(k0, st0)
            accumulate(k1, st1)
            return carry

        lax.fori_loop(0, (nkb + 1) // 2, blk2, 0)
        for r in range(group):
            ot = (acc_ref[0:HEAD_DIM, r * tq:(r + 1) * tq]
                  / acc_ref[HEAD_DIM:HEAD_DIM + 1, r * tq:(r + 1) * tq])
            orr = jnp.where(qpos >= FRAME_OFF, ot.T, 0.0)
            o_ref[:, (g * group + r) * HEAD_DIM:(g * group + r + 1) * HEAD_DIM] = orr.astype(o_ref.dtype)


def dsa_attention(q, k, vt_aug, bias_t):
    lp = q.shape[0]
    tq = 128
    ts = _pick(lp, (640, 256, 128))
    group = ATT_HEADS // ATT_KV_HEADS
    assert (lp // ts) % 2 == 0
    kern = functools.partial(_attn_kernel, tq=tq, ts=ts)
    return pl.pallas_call(
        kern,
        grid=(lp // tq,),
        in_specs=[pl.BlockSpec((tq, ATT_Q_W), lambda i: (i, 0)),
                  pl.BlockSpec((lp, ATT_KV_W), lambda i: (0, 0)),
                  pl.BlockSpec((ATT_KV_HEADS, V_AUG, lp), lambda i: (0, 0, 0)),
                  pl.BlockSpec((lp, tq), lambda i: (0, i))],
        out_specs=pl.BlockSpec((tq, ATT_Q_W), lambda i: (i, 0)),
        out_shape=jax.ShapeDtypeStruct((lp, ATT_Q_W), bf16),
        scratch_shapes=[pltpu.VMEM((group, tq), f32),
                        pltpu.VMEM((V_AUG, group * tq), f32)],
        compiler_params=_cparams(("parallel",)),
        name="dsa_attention",
    )(q, k, vt_aug, bias_t)


def _ffn_up_kernel(a_ref, ah_ref, wg_ref, wv_ref, cg_ref, cv_ref, bg_ref, bv_ref, o_ref,
                   eg_ref, ev_ref, *, tm):
    i = pl.program_id(0)
    halo = ah_ref[...]
    halo = jnp.where(i > 0, halo, jnp.zeros_like(halo))
    a = jnp.concatenate([halo, a_ref[...]], axis=0)

    eg_ref[...] = _dot(a, wg_ref[...])
    ev_ref[...] = _dot(a, wv_ref[...])

    def conv(ext_ref, w_ref, b_ref):
        w = w_ref[...]
        y = ext_ref[HALO:HALO + tm, :] * w[FFN_CONV - 1:FFN_CONV, :]
        for j in range(FFN_CONV - 1):
            s0 = HALO - (FFN_CONV - 1) + j
            y = y + ext_ref[s0:s0 + tm, :] * w[j:j + 1, :]
        return y + b_ref[...]

    gate = conv(eg_ref, cg_ref, bg_ref)
    val = conv(ev_ref, cv_ref, bv_ref)
    o_ref[...] = (gate * _sigmoid(gate) * val).astype(o_ref.dtype)


def ffn_up(u, w_up, conv_w, conv_b):
    lp, kdim = u.shape
    tm = _pick(lp, (1280, 640, 256, 128))
    tn = 512
    nc = D_FF // tn
    kern = functools.partial(_ffn_up_kernel, tm=tm)

    def wspec(off):
        return pl.BlockSpec((kdim, tn), lambda i, c: (0, c + off))

    def cspec(off):
        return pl.BlockSpec((FFN_CONV, tn), lambda i, c: (0, c + off))

    def bspec(off):
        return pl.BlockSpec((1, tn), lambda i, c: (0, c + off))

    b2 = conv_b.reshape(1, 2 * D_FF)
    return pl.pallas_call(
        kern,
        grid=(lp // tm, nc),
        in_specs=[pl.BlockSpec((tm, kdim), lambda i, c: (i, 0)),
                  pl.BlockSpec((HALO, kdim), lambda i, c: (jnp.maximum(i * (tm // HALO) - 1, 0), 0)),
                  wspec(0), wspec(nc), cspec(0), cspec(nc), bspec(0), bspec(nc)],
        out_specs=pl.BlockSpec((tm, tn), lambda i, c: (i, c)),
        out_shape=jax.ShapeDtypeStruct((lp, D_FF), bf16),
        scratch_shapes=[pltpu.VMEM((HALO + tm, tn), f32), pltpu.VMEM((HALO + tm, tn), f32)],
        compiler_params=_cparams(("parallel", "arbitrary")),
        name="ffn_up",
    )(u, u, w_up, w_up, conv_w, conv_w, b2, b2)


def _split_w_in(w):
    o = 0
    parts = {}
    for name, width in (("gq", GDN_QK_W), ("gk", GDN_QK_W), ("gv", GDN_V_W), ("gz", GDN_V_W),
                        ("gb", GDN_V_HEADS), ("ga", GDN_V_HEADS), ("aq", ATT_Q_W), ("ak", ATT_KV_W),
                        ("av", ATT_KV_W), ("iq", IDX_Q_W), ("ik", HEAD_DIM), ("iw", IDX_HEADS),
                        ("gate_gdn", D_MODEL), ("gate_att", D_MODEL)):
        parts[name] = w[:, o:o + width]
        o += width
    big = jnp.concatenate([parts[n] for n in ("gq", "gk", "gv", "gz", "aq", "iq", "gate_gdn",
                                              "gate_att", "ak", "av", "ik")], axis=1).astype(bf16)
    small = jnp.concatenate([parts["gb"], parts["ga"], parts["iw"]], axis=1)
    small = jnp.pad(small, ((0, 0), (0, LANES - small.shape[1]))).astype(bf16)
    return big, small


def _layer(h0, p, lp, topk):
    tm = _pick(lp, (640, 256, 128))
    tm_big = _pick(lp, (1280, 640, 256, 128))
    w_big, w_small = _split_w_in(p["w_in"])

    u1 = rms_rows(h0, p["mix_pre_g"])
    (proj,) = matmul(u1, w_big, tm=tm, tn=1920, tk=D_MODEL, epilogue=_ep_cast,
                     out_shapes=[jax.ShapeDtypeStruct((lp, PROJ_W), bf16)], name="proj_in")
    (small,) = matmul(u1, w_small, tm=tm_big, tn=LANES, tk=D_MODEL, epilogue=_ep_cast,
                      out_shapes=[jax.ShapeDtypeStruct((lp, LANES), f32)], name="proj_small")
    gb = small[:, 0:GDN_V_HEADS]
    ga = small[:, GDN_V_HEADS:2 * GDN_V_HEADS]
    iw = small[:, 2 * GDN_V_HEADS:2 * GDN_V_HEADS + IDX_HEADS]

    qkv = gdn_prep(proj, p["gdn_conv_w"])
    beta, gcum = gdn_gates(gb, ga, p["gdn_a_log"], p["gdn_dt_bias"])
    gcol = gcum.T[:, :, None]
    grow = gcum.T[:, None, :]
    bcol = beta.T[:, :, None]
    brow = beta.T[:, None, :]
    o_gdn = gdn_core(qkv, proj, p["gdn_norm_g"], gcol, grow, bcol, brow)

    tabs = rope_tables(lp)
    aq = rope(proj, COL_AQ, ATT_HEADS, tabs, scale=HEAD_DIM ** -0.5 * LOG2E)
    ak = rope(proj, COL_AK, ATT_KV_HEADS, tabs)
    iq = rope(proj, COL_IQ, IDX_HEADS, tabs, head_major=True)
    ik = rope(proj, COL_IK, 1, tabs)
    avt = proj[:, COL_AV:COL_AV + ATT_KV_W].T.reshape(ATT_KV_HEADS, HEAD_DIM, lp)
    avt = jnp.concatenate([avt, jnp.ones((ATT_KV_HEADS, V_AUG - HEAD_DIM, lp), bf16)], axis=1)
    bias_t = dsa_select(iq, ik, iw.T, topk)
    o_att = dsa_attention(aq, ak, avt, bias_t)

    tn = 512
    gate_spec = lambda col0: ((tm_big, tn), (lambda i, j, c=col0 // tn: (i, c + j)))
    (m1,) = matmul(o_gdn, p["w_branch_gdn"].astype(bf16), tm=tm_big, tn=tn, tk=2048,
                   epilogue=_ep_gate, extra=(proj,), extra_specs=(gate_spec(COL_GATE_GDN),),
                   out_shapes=[jax.ShapeDtypeStruct((lp, D_MODEL), f32)], name="branch_gdn")
    (merged,) = matmul(o_att, p["w_branch_att"].astype(bf16), tm=tm_big, tn=tn, tk=2048,
                       epilogue=_ep_gate_add, extra=(proj, m1),
                       extra_specs=(gate_spec(COL_GATE_ATT), ((tm_big, tn), lambda i, j: (i, j))),
                       out_shapes=[jax.ShapeDtypeStruct((lp, D_MODEL), bf16)], name="branch_att")
    row_spec = ((tm, D_MODEL), lambda i, j: (i, 0))
    vec_spec = ((1, D_MODEL), lambda i, j: (0, 0))
    h1, u2 = matmul(merged, p["w_out"].astype(bf16), tm=tm, tn=D_MODEL, tk=1024,
                    epilogue=_ep_res_norm2,
                    extra=(h0, p["mix_post_g"].reshape(1, D_MODEL), p["ffn_pre_g"].reshape(1, D_MODEL)),
                    extra_specs=(row_spec, vec_spec, vec_spec),
                    out_shapes=[jax.ShapeDtypeStruct((lp, D_MODEL), f32),
                                jax.ShapeDtypeStruct((lp, D_MODEL), bf16)], name="w_out")

    act = ffn_up(u2, p["w_up"].astype(bf16), p["ffn_conv_w"], p["ffn_conv_b"])
    (h2,) = matmul(act, p["w_down"].astype(bf16), tm=tm, tn=D_MODEL, tk=1024, epilogue=_ep_res_norm,
                   extra=(h1, p["ffn_post_g"].reshape(1, D_MODEL)),
                   extra_specs=(row_spec, vec_spec),
                   out_shapes=[jax.ShapeDtypeStruct((lp, D_MODEL), f32)], name="w_down")
    return h2


def kernel(x, meta_tokens, mix_pre_g, w_in, gdn_conv_w, gdn_a_log, gdn_dt_bias, gdn_norm_g,
           w_branch_gdn, w_branch_att, w_out, mix_post_g, ffn_pre_g, w_up, ffn_conv_w,
           ffn_conv_b, w_down, ffn_post_g):
    batch, seq, d = x.shape
    assert batch == 1 and d == D_MODEL
    lp = FRAME_X0 + seq
    topk = min(TOPK_MAX, (N_META + seq) // 4)
    h = jnp.concatenate([jnp.zeros((FRAME_OFF, d), x.dtype), meta_tokens.astype(x.dtype), x[0]], axis=0)
    for i in range(w_in.shape[0]):
        p = dict(mix_pre_g=mix_pre_g[i], w_in=w_in[i], gdn_conv_w=gdn_conv_w[i], gdn_a_log=gdn_a_log[i],
                 gdn_dt_bias=gdn_dt_bias[i], gdn_norm_g=gdn_norm_g[i], w_branch_gdn=w_branch_gdn[i],
                 w_branch_att=w_branch_att[i], w_out=w_out[i], mix_post_g=mix_post_g[i],
                 ffn_pre_g=ffn_pre_g[i], w_up=w_up[i], ffn_conv_w=ffn_conv_w[i],
                 ffn_conv_b=ffn_conv_b[i], w_down=w_down[i], ffn_post_g=ffn_post_g[i])
        h = _layer(h, p, lp, topk)
    return h[FRAME_X0:][None]
```

```python
import functools
import struct

import jax
import jax.numpy as jnp
from jax import lax
from jax.experimental import pallas as pl
from jax.experimental.pallas import tpu as pltpu

f32 = jnp.float32
bf16 = jnp.bfloat16
i32 = jnp.int32
i16 = jnp.int16

D_MODEL = 2048
N_META = 16
EPS = 1e-6
GDN_QK_HEADS = 16
GDN_V_HEADS = 32
HEAD_DIM = 128
GDN_CONV = 4
ATT_HEADS = 16
ATT_KV_HEADS = 2
IDX_HEADS = 16
TOPK_MAX = 256
NEG = -1e30
LOG2E = 1.4426950408889634
ROPE_THETA = 500000.0
ROPE_DIMS = HEAD_DIM // 4
D_FF = 3 * D_MODEL
FFN_CONV = 3
GDN_QK_W = GDN_QK_HEADS * HEAD_DIM
GDN_V_W = GDN_V_HEADS * HEAD_DIM
ATT_Q_W = ATT_HEADS * HEAD_DIM
ATT_KV_W = ATT_KV_HEADS * HEAD_DIM
IDX_Q_W = IDX_HEADS * HEAD_DIM

FRAME_X0 = 256
FRAME_OFF = FRAME_X0 - N_META
CHUNK = 128

LANES = 128
VMEM_LIMIT = 56 * 1024 * 1024

COL_GQKV = 0
COL_GZ = COL_GQKV + 2 * GDN_QK_W + GDN_V_W
COL_AQ = COL_GZ + GDN_V_W
COL_IQ = COL_AQ + ATT_Q_W
COL_GATE_GDN = COL_IQ + IDX_Q_W
COL_GATE_ATT = COL_GATE_GDN + D_MODEL
COL_AK = COL_GATE_ATT + D_MODEL
COL_AV = COL_AK + ATT_KV_W
COL_IK = COL_AV + ATT_KV_W
PROJ_W = COL_IK + HEAD_DIM


def _cparams(sem):
    return pltpu.CompilerParams(dimension_semantics=sem, vmem_limit_bytes=VMEM_LIMIT)


def _pick(n, cands):
    for c in cands:
        if n % c == 0:
            return c
    raise ValueError(f"no tile for {n} in {cands}")


def _sigmoid(x):
    return 1.0 / (1.0 + jnp.exp(-x))


def _dot(a, b):
    return jnp.dot(a, b, preferred_element_type=f32)


def _dot_nt(a, b):
    return lax.dot_general(a, b, (((1,), (1,)), ((), ())), preferred_element_type=f32)


def _rms_rows_kernel(h_ref, g_ref, o_ref):
    h = h_ref[...]
    y = h * lax.rsqrt(jnp.mean(h * h, axis=-1, keepdims=True) + EPS)
    o_ref[...] = (y * g_ref[...]).astype(o_ref.dtype)


def rms_rows(h, g):
    lp, d = h.shape
    tm = _pick(lp, (640, 256, 128))
    return pl.pallas_call(
        _rms_rows_kernel,
        grid=(lp // tm,),
        in_specs=[pl.BlockSpec((tm, d), lambda i: (i, 0)),
                  pl.BlockSpec((1, d), lambda i: (0, 0))],
        out_specs=pl.BlockSpec((tm, d), lambda i: (i, 0)),
        out_shape=jax.ShapeDtypeStruct((lp, d), bf16),
        compiler_params=_cparams(("parallel",)),
        name="rms_rows",
    )(h, g.reshape(1, d))


def _mm_kernel(*refs, nk, n_extra, n_out, epilogue):
    a_ref, w_ref = refs[0], refs[1]
    extra = refs[2:2 + n_extra]
    outs = refs[2 + n_extra:2 + n_extra + n_out]
    if nk == 1:
        epilogue(_dot(a_ref[...], w_ref[...]), extra, outs)
        return
    acc_ref = refs[-1]
    k = pl.program_id(2)

    @pl.when(k == 0)
    def _():
        acc_ref[...] = jnp.zeros_like(acc_ref)

    acc_ref[...] += _dot(a_ref[...], w_ref[...])

    @pl.when(k == nk - 1)
    def _():
        epilogue(acc_ref[...], extra, outs)


def matmul(a, w, *, tm, tn, tk, epilogue, extra=(), extra_specs=(), out_shapes, a_col0=0, name):
    m = a.shape[0]
    kdim, n = w.shape
    nk = kdim // tk
    assert m % tm == 0 and n % tn == 0 and kdim % tk == 0
    in_specs = [pl.BlockSpec((tm, tk), lambda i, j, k: (i, a_col0 + k)),
                pl.BlockSpec((tk, tn), lambda i, j, k: (k, j))]
    in_specs += [pl.BlockSpec(bs, (lambda i, j, k, f=f: f(i, j))) for bs, f in extra_specs]
    out_specs = [pl.BlockSpec((tm, tn), lambda i, j, k: (i, j)) for _ in out_shapes]
    scratch = [] if nk == 1 else [pltpu.VMEM((tm, tn), f32)]
    kern = functools.partial(_mm_kernel, nk=nk, n_extra=len(extra), n_out=len(out_shapes),
                             epilogue=epilogue)
    res = pl.pallas_call(
        kern,
        grid=(m // tm, n // tn, nk),
        in_specs=in_specs,
        out_specs=out_specs,
        out_shape=out_shapes,
        scratch_shapes=scratch,
        compiler_params=_cparams(("parallel", "parallel", "arbitrary")),
        name=name,
    )(a, w, *extra)
    return res


def _ep_cast(acc, extra, outs):
    outs[0][...] = acc.astype(outs[0].dtype)


def _ep_gate(acc, extra, outs):
    g = extra[0][...].astype(f32)
    outs[0][...] = (_sigmoid(g) * acc).astype(outs[0].dtype)


def _ep_gate_add(acc, extra, outs):
    g = extra[0][...].astype(f32)
    outs[0][...] = (extra[1][...].astype(f32) + _sigmoid(g) * acc).astype(outs[0].dtype)


def _rms(t, g):
    return t * lax.rsqrt(jnp.mean(t * t, axis=-1, keepdims=True) + EPS) * g


def _ep_res_norm2(acc, extra, outs):
    h_ref, g_ref, g2_ref = extra
    h1 = h_ref[...] + _rms(acc, g_ref[...])
    outs[0][...] = h1
    outs[1][...] = _rms(h1, g2_ref[...]).astype(outs[1].dtype)


def _ep_res_norm(acc, extra, outs):
    h_ref, g_ref = extra
    outs[0][...] = h_ref[...] + _rms(acc, g_ref[...])


HALO = 16


def _gdn_prep_kernel(x_ref, halo_ref, w_ref, o_ref, ext_ref, *, tm, tc):
    i = pl.program_id(0)
    c = pl.program_id(1)
    halo = halo_ref[...].astype(f32)
    ext_ref[0:HALO, :] = jnp.where(i > 0, halo, 0.0)
    ext_ref[HALO:HALO + tm, :] = x_ref[...].astype(f32)
    w = w_ref[...]
    y = ext_ref[HALO:HALO + tm, :] * w[GDN_CONV - 1:GDN_CONV, :]
    for j in range(GDN_CONV - 1):
        s0 = HALO - (GDN_CONV - 1) + j
        y = y + ext_ref[s0:s0 + tm, :] * w[j:j + 1, :]
    s = y * _sigmoid(y)
    is_q = c < (GDN_QK_W // tc)
    is_qk = c < (2 * GDN_QK_W // tc)
    qscale = jnp.where(is_q, HEAD_DIM ** -0.5, 1.0).astype(f32)
    for hh in range(tc // HEAD_DIM):
        seg = s[:, hh * HEAD_DIM:(hh + 1) * HEAD_DIM]
        r = lax.rsqrt(jnp.sum(seg * seg, axis=-1, keepdims=True) + EPS) * qscale
        fac = jnp.where(is_qk, r, 1.0)
        o_ref[:, hh * HEAD_DIM:(hh + 1) * HEAD_DIM] = (seg * fac).astype(o_ref.dtype)


def gdn_prep(proj, conv_w):
    lp = proj.shape[0]
    width = 2 * GDN_QK_W + GDN_V_W
    tm = _pick(lp, (640, 256, 128))
    tc = 512
    kern = functools.partial(_gdn_prep_kernel, tm=tm, tc=tc)
    return pl.pallas_call(
        kern,
        grid=(lp // tm, width // tc),
        in_specs=[pl.BlockSpec((tm, tc), lambda i, c: (i, c)),
                  pl.BlockSpec((HALO, tc), lambda i, c: (jnp.maximum(i * (tm // HALO) - 1, 0), c)),
                  pl.BlockSpec((GDN_CONV, tc), lambda i, c: (0, c))],
        out_specs=pl.BlockSpec((tm, tc), lambda i, c: (i, c)),
        out_shape=jax.ShapeDtypeStruct((lp, width), bf16),
        scratch_shapes=[pltpu.VMEM((HALO + tm, tc), f32)],
        compiler_params=_cparams(("parallel", "parallel")),
        name="gdn_prep",
    )(proj, proj, conv_w)


def _gdn_gates_kernel(b_ref, a_ref, alog_ref, dt_ref, beta_ref, gcum_ref):
    i = pl.program_id(0)
    rows = i * CHUNK + lax.broadcasted_iota(i32, (CHUNK, 1), 0)
    valid = rows >= FRAME_OFF
    beta_ref[...] = jnp.where(valid, _sigmoid(b_ref[...]), 0.0)
    a = a_ref[...] + dt_ref[...]
    sp = jnp.maximum(a, 0.0) + jnp.log1p(jnp.exp(-jnp.abs(a)))
    g = jnp.where(valid, -jnp.exp(alog_ref[...]) * sp, 0.0)
    tri = (lax.broadcasted_iota(i32, (CHUNK, CHUNK), 0)
           >= lax.broadcasted_iota(i32, (CHUNK, CHUNK), 1)).astype(f32)
    gcum_ref[...] = jnp.dot(tri, g, preferred_element_type=f32, precision=lax.Precision.HIGHEST)


def gdn_gates(gb, ga, a_log, dt_bias):
    lp, nh = gb.shape
    spec = pl.BlockSpec((CHUNK, nh), lambda i: (i, 0))
    vec = pl.BlockSpec((1, nh), lambda i: (0, 0))
    return pl.pallas_call(
        _gdn_gates_kernel,
        grid=(lp // CHUNK,),
        in_specs=[spec, spec, vec, vec],
        out_specs=[spec, spec],
        out_shape=[jax.ShapeDtypeStruct((lp, nh), f32)] * 2,
        compiler_params=_cparams(("parallel",)),
        name="gdn_gates",
    )(gb, ga, a_log.reshape(1, nh), dt_bias.reshape(1, nh))


def _block_mask(size):
    r = lax.broadcasted_iota(i32, (CHUNK, CHUNK), 0) // size
    c = lax.broadcasted_iota(i32, (CHUNK, CHUNK), 1) // size
    return r == c


def _unit_lower_inverse_many(mats):
    row = lax.broadcasted_iota(i32, (CHUNK, CHUNK), 0)
    col = lax.broadcasted_iota(i32, (CHUNK, CHUNK), 1)
    eye = (row == col).astype(f32)
    base = 8
    m_prev = _block_mask(base)
    ads = [jnp.where(m_prev, a, 0.0) for a in mats]
    adbs = [ad.astype(bf16) for ad in ads]
    a2s = [_dot(x, x) for x in adbs]
    a2bs = [x.astype(bf16) for x in a2s]
    a4s = [_dot(x, x) for x in a2bs]
    xs = [_dot((eye - ad).astype(bf16), (eye + a2).astype(bf16)) for ad, a2 in zip(ads, a2s)]
    xs = [_dot(x.astype(bf16), (eye + a4).astype(bf16)) for x, a4 in zip(xs, a4s)]
    size = base * 2
    while size <= CHUNK:
        m_cur = _block_mask(size)
        off_diag = jnp.logical_and(m_cur, jnp.logical_not(m_prev))
        bs = [jnp.where(off_diag, a, 0.0).astype(bf16) for a in mats]
        xbs = [x.astype(bf16) for x in xs]
        ys = [_dot(xb, b).astype(bf16) for xb, b in zip(xbs, bs)]
        xs = [x - _dot(y, xb) for x, y, xb in zip(xs, ys, xbs)]
        m_prev = m_cur
        size *= 2
    return xs


def _gdn_kernel(q_ref, k_ref, v_ref, z_ref, ng_ref,
                gc0_ref, gc1_ref, gr0_ref, gr1_ref, bc0_ref, bc1_ref, br0_ref, br1_ref,
                o_ref, s_ref, *, rb):
    r = pl.program_id(1)

    @pl.when(r == 0)
    def _():
        s_ref[...] = jnp.zeros_like(s_ref)

    gcs, grs = (gc0_ref, gc1_ref), (gr0_ref, gr1_ref)
    bcs, brs = (bc0_ref, bc1_ref), (br0_ref, br1_ref)
    row = lax.broadcasted_iota(i32, (CHUNK, CHUNK), 0)
    col = lax.broadcasted_iota(i32, (CHUNK, CHUNK), 1)
    incl = row >= col
    strict = row > col
    ng = ng_ref[...]
    nchunk = rb // CHUNK
    heads = range(2)

    def rows(c):
        return slice(c * CHUNK, (c + 1) * CHUNK)

    def lanes(e):
        return slice(e * HEAD_DIM, (e + 1) * HEAD_DIM)

    qs = [q_ref[rows(c), :] for c in range(nchunk)]
    ks = [k_ref[rows(c), :] for c in range(nchunk)]
    kts = [k.astype(f32).T for k in ks]
    kks = [_dot_nt(k, k) for k in ks]
    qks = [_dot_nt(q, k) for q, k in zip(qs, ks)]
    chains = [(c, e) for c in range(nchunk) for e in heads]
    gc = {ce: gcs[ce[1]][rows(ce[0]), :] for ce in chains}
    gr = {ce: grs[ce[1]][:, rows(ce[0])] for ce in chains}
    bc = {ce: bcs[ce[1]][rows(ce[0]), :] for ce in chains}
    br = {ce: brs[ce[1]][:, rows(ce[0])] for ce in chains}
    dec = {ce: jnp.where(incl, jnp.exp(jnp.where(incl, gc[ce] - gr[ce], 0.0)), 0.0) for ce in chains}
    amat = [jnp.where(strict, bc[ce] * kks[ce[0]] * dec[ce], 0.0) for ce in chains]
    tinv = dict(zip(chains, _unit_lower_inverse_many(amat)))
    u = {ce: _dot((tinv[ce] * br[ce]).astype(bf16), v_ref[rows(ce[0]), lanes(ce[1])]) for ce in chains}
    w = {ce: _dot((tinv[ce] * (br[ce] * jnp.exp(gr[ce]))).astype(bf16), ks[ce[0]]) for ce in chains}
    g_last = {ce: gr[ce][:, CHUNK - 1:CHUNK] for ce in chains}
    wq = {ce: jnp.concatenate([w[ce].astype(bf16), qs[ce[0]]], axis=0) for ce in chains}
    ak = {ce: jnp.concatenate(
        [jnp.where(incl, qks[ce[0]] * dec[ce], 0.0).astype(bf16),
         (kts[ce[0]] * jnp.exp(g_last[ce] - gr[ce])).astype(bf16)], axis=0) for ce in chains}
    eg = {ce: jnp.exp(gc[ce]) for ce in chains}
    egl = {ce: jnp.exp(g_last[ce]) for ce in chains}

    state = [s_ref[e] for e in heads]
    for c in range(nchunk):
        sb = [state[e].astype(bf16) for e in heads]
        ws = [_dot(wq[(c, e)], sb[e]) for e in heads]
        vb = [(u[(c, e)] - ws[e][:CHUNK]).astype(bf16) for e in heads]
        av = [_dot(ak[(c, e)], vb[e]) for e in heads]
        for e in heads:
            o = eg[(c, e)] * ws[e][CHUNK:] + av[e][:CHUNK]
            state[e] = state[e] * egl[(c, e)] + av[e][CHUNK:]
            z = z_ref[rows(c), lanes(e)].astype(f32)
            o_ref[rows(c), lanes(e)] = (_rms(o, ng) * (z * _sigmoid(z))).astype(o_ref.dtype)
    for e in heads:
        s_ref[e] = state[e]


def gdn_core(qkv, proj, norm_g, gcol, grow, bcol, brow):
    lp = qkv.shape[0]
    rb = _pick(lp, (640, 256, 128))
    nqk = GDN_QK_HEADS
    kcol0 = GDN_QK_W // HEAD_DIM
    vcol0 = 2 * GDN_QK_W // (2 * HEAD_DIM)
    zcol0 = COL_GZ // (2 * HEAD_DIM)

    def colspec(e):
        return pl.BlockSpec((None, rb, 1), lambda j, r: (2 * j + e, r, 0))

    def rowspec(e):
        return pl.BlockSpec((None, 1, rb), lambda j, r: (2 * j + e, 0, r))

    kern = functools.partial(_gdn_kernel, rb=rb)
    return pl.pallas_call(
        kern,
        grid=(nqk, lp // rb),
        in_specs=[pl.BlockSpec((rb, HEAD_DIM), lambda j, r: (r, j)),
                  pl.BlockSpec((rb, HEAD_DIM), lambda j, r: (r, kcol0 + j)),
                  pl.BlockSpec((rb, 2 * HEAD_DIM), lambda j, r: (r, vcol0 + j)),
                  pl.BlockSpec((rb, 2 * HEAD_DIM), lambda j, r: (r, zcol0 + j)),
                  pl.BlockSpec((1, HEAD_DIM), lambda j, r: (0, 0)),
                  colspec(0), colspec(1), rowspec(0), rowspec(1),
                  colspec(0), colspec(1), rowspec(0), rowspec(1)],
        out_specs=pl.BlockSpec((rb, 2 * HEAD_DIM), lambda j, r: (r, j)),
        out_shape=jax.ShapeDtypeStruct((lp, GDN_V_W), bf16),
        scratch_shapes=[pltpu.VMEM((2, HEAD_DIM, HEAD_DIM), f32)],
        compiler_params=_cparams(("parallel", "arbitrary")),
        name="gdn_core",
    )(qkv, qkv, qkv, proj, norm_g.reshape(1, HEAD_DIM),
      gcol, gcol, grow, grow, bcol, bcol, brow, brow)


def _rope_kernel(x_ref, c_ref, s1_ref, s2_ref, o_ref, *, nh, scale, head_major):
    cc = c_ref[...]
    s1 = s1_ref[...]
    s2 = s2_ref[...]
    half = ROPE_DIMS // 2
    for h in range(nh):
        seg = x_ref[:, h * HEAD_DIM:(h + 1) * HEAD_DIM].astype(f32)
        y = (seg * cc + pltpu.roll(seg, half, 1) * s1
             + pltpu.roll(seg, HEAD_DIM - half, 1) * s2)
        if scale != 1.0:
            y = y * scale
        if head_major:
            o_ref[h] = y.astype(o_ref.dtype)
        else:
            o_ref[:, h * HEAD_DIM:(h + 1) * HEAD_DIM] = y.astype(o_ref.dtype)


def rope(proj, col0, nh, tabs, *, scale=1.0, head_major=False):
    lp = proj.shape[0]
    tm = _pick(lp, (640, 256, 128))
    w = nh * HEAD_DIM
    assert col0 % w == 0
    tab = pl.BlockSpec((tm, HEAD_DIM), lambda i: (i, 0))
    if head_major:
        out_spec = pl.BlockSpec((nh, tm, HEAD_DIM), lambda i: (0, i, 0))
        out_shape = jax.ShapeDtypeStruct((nh, lp, HEAD_DIM), bf16)
    else:
        out_spec = pl.BlockSpec((tm, w), lambda i: (i, 0))
        out_shape = jax.ShapeDtypeStruct((lp, w), bf16)
    kern = functools.partial(_rope_kernel, nh=nh, scale=scale, head_major=head_major)
    return pl.pallas_call(
        kern,
        grid=(lp // tm,),
        in_specs=[pl.BlockSpec((tm, w), lambda i: (i, col0 // w)), tab, tab, tab],
        out_specs=out_spec,
        out_shape=out_shape,
        compiler_params=_cparams(("parallel",)),
        name=f"rope_{col0}",
    )(proj, *tabs)


def rope_tables(lp):
    half = ROPE_DIMS // 2
    pos = (jnp.arange(lp, dtype=jnp.int32) - FRAME_OFF).astype(f32)
    inv = ROPE_THETA ** (-jnp.arange(half, dtype=f32) / half)
    ang = pos[:, None] * inv[None, :]
    cos, sin = jnp.cos(ang), jnp.sin(ang)
    zeros = jnp.zeros((lp, HEAD_DIM - ROPE_DIMS), f32)
    z16 = jnp.zeros((lp, half), f32)
    cc = jnp.concatenate([cos, cos, jnp.ones_like(zeros)], axis=1)
    s1 = jnp.concatenate([z16, sin, zeros], axis=1)
    s2 = jnp.concatenate([-sin, z16, zeros], axis=1)
    return cc, s1, s2


INT_MAX = 2 ** 31 - 1
HALF_MIN, HALF_MAX, HALF_SPAN = -2 ** 15, 2 ** 15 - 1, 2 ** 16


def _sortable(x):
    b = lax.bitcast_convert_type(x, i32)
    return b ^ (lax.shift_right_arithmetic(b, 31) & INT_MAX)


_NEG_BITS = struct.unpack("<i", struct.pack("<f", NEG))[0]
_KEY_NEG = _NEG_BITS ^ ((_NEG_BITS >> 31) & INT_MAX)
_KEY_NEG_HI = _KEY_NEG >> 16
_KEY_NEG_LO = (_KEY_NEG & (HALF_SPAN - 1)) + HALF_MIN


def _select_kernel(iq_ref, ik_ref, iwt_ref, bias_ref, key_ref, half_ref, *, tq, ts, lp, topk):
    i = pl.program_id(0)
    nkb = ((i + 1) * tq + ts - 1) // ts
    rem = lp - nkb * ts
    iwt = iwt_ref[...] * ((IDX_HEADS ** -0.5) * (HEAD_DIM ** -0.5))
    qpos = i * tq + lax.broadcasted_iota(i32, (1, tq), 1)
    sub = lax.broadcasted_iota(i32, (ts, 1), 0)

    def score_blk(kb, carry):
        k0 = pl.multiple_of(kb * ts, ts)
        ikb = ik_ref[pl.ds(k0, ts), :]
        acc = jnp.zeros((ts, tq), f32)
        for h in range(IDX_HEADS):
            lg = _dot_nt(ikb, iq_ref[h])
            acc = acc + jnp.maximum(lg, 0.0) * iwt[h:h + 1, :]
        kpos = k0 + sub
        sc = jnp.where(kpos <= qpos, acc, NEG)
        sc = jnp.where(kpos < FRAME_OFF, -jnp.inf, sc)
        key = _sortable(sc)
        key_ref[pl.ds(k0, ts), :] = key
        half_ref[pl.ds(k0, ts), :] = lax.shift_right_arithmetic(key, 16).astype(i16)
        return carry

    lax.fori_loop(0, nkb, score_blk, 0)

    n_acc = 4

    def count(pred_fn):
        sub8 = lax.broadcasted_iota(i32, (8, 1), 0)

        def blk(kb, cnts):
            k0 = pl.multiple_of(kb * ts, ts)
            cnts = list(cnts)
            blk_ref = key_ref.at[pl.ds(k0, ts)]
            for j in range(ts // 8):
                kv = blk_ref[j * 8:(j + 1) * 8, :]
                cnts[j % n_acc] = cnts[j % n_acc] + jnp.where(pred_fn(kv, k0 + j * 8 + sub8), 1, 0)
            return tuple(cnts)
        cnts = lax.fori_loop(0, nkb, blk, tuple(jnp.zeros((8, tq), i32) for _ in range(n_acc)))
        return jnp.sum(sum(cnts[1:], cnts[0]), axis=0, keepdims=True)

    def count_half_ge(cand):
        one, zero = jnp.int16(1), jnp.int16(0)

        def blk(kb, cnts):
            k0 = pl.multiple_of(kb * ts, ts)
            cnts = list(cnts)
            blk_ref = half_ref.at[pl.ds(k0, ts)]
            for j in range(ts // 16):
                hv = blk_ref[j * 16:(j + 1) * 16, :]
                cnts[j % n_acc] = cnts[j % n_acc] + jnp.where(hv >= cand, one, zero)
            return tuple(cnts)
        cnts = lax.fori_loop(0, nkb, blk, tuple(jnp.zeros((16, tq), i16) for _ in range(n_acc)))
        return jnp.sum(sum(cnts[1:], cnts[0]).astype(i32), axis=0, keepdims=True)

    def search_half(neg_half, n_start):
        def body(it, c):
            tau, n_tau = c
            t2 = tau | lax.shift_left(jnp.int32(1), 15 - it)
            cand = t2 + HALF_MIN
            n2 = count_half_ge(cand.astype(i16)) + jnp.where(cand <= neg_half, rem, 0)
            ok = n2 >= topk
            return jnp.where(ok, t2, tau), jnp.where(ok, n2, n_tau)

        return lax.fori_loop(0, 16, body, (jnp.zeros((1, tq), i32), n_start))

    tau_hi, n_hi = search_half(jnp.full((1, tq), _KEY_NEG_HI, i32), jnp.full((1, tq), lp, i32))
    t_hi = tau_hi + HALF_MIN

    def low_blk(kb, carry):
        k0 = pl.multiple_of(kb * ts, ts)
        key = key_ref[pl.ds(k0, ts), :]
        hi = lax.shift_right_arithmetic(key, 16)
        lo = (key & (HALF_SPAN - 1)) + HALF_MIN
        lo = jnp.where(hi > t_hi, HALF_MAX, jnp.where(hi == t_hi, lo, HALF_MIN))
        half_ref[pl.ds(k0, ts), :] = lo.astype(i16)
        return carry

    lax.fori_loop(0, nkb, low_blk, 0)
    neg_lo = jnp.where(_KEY_NEG_HI > t_hi, HALF_MAX, jnp.where(_KEY_NEG_HI == t_hi, _KEY_NEG_LO, HALF_MIN))
    tau_lo, n_ge = search_half(neg_lo, n_hi)
    thr = lax.shift_left(t_hi, 16) | tau_lo
    has_tie = jnp.max(jnp.where(n_ge > topk, 1, 0)) > 0

    def tie_cut():
        n_gt = (count(lambda kv, kpos: kv > thr) + jnp.where(thr < _KEY_NEG, rem, 0))
        need = topk - n_gt

        def bis2(it, cut):
            bit = lax.shift_left(jnp.int32(1), 14 - it)
            c2 = cut | bit
            n = count(lambda kv, kpos: jnp.logical_and(kv == thr, kpos < c2))
            return jnp.where(n < need, c2, cut)
        return lax.fori_loop(0, 15, bis2, jnp.zeros((1, tq), i32))

    cut = lax.cond(has_tie, tie_cut, lambda: jnp.full((1, tq), INT_MAX, i32))

    def bias_blk(kb, carry):
        k0 = pl.multiple_of(kb * ts, ts)
        kv = key_ref[pl.ds(k0, ts), :]
        interior = jnp.logical_and(k0 >= FRAME_OFF, k0 + ts - 1 <= i * tq)

        @pl.when(jnp.logical_and(interior, jnp.logical_not(has_tie)))
        def _():
            bias_ref[pl.ds(k0, ts), :] = jnp.where(kv >= thr, 0.0, NEG).astype(bias_ref.dtype)

        @pl.when(jnp.logical_not(jnp.logical_and(interior, jnp.logical_not(has_tie))))
        def _():
            kpos = k0 + sub
            sel = jnp.logical_or(kv > thr, jnp.logical_and(kv == thr, kpos <= cut))
            vis = jnp.logical_and(kpos <= qpos, kpos >= FRAME_OFF)
            bias_ref[pl.ds(k0, ts), :] = jnp.where(jnp.logical_and(sel, vis), 0.0, NEG).astype(bias_ref.dtype)
        return carry

    lax.fori_loop(0, nkb, bias_blk, 0)

    def fill_blk(kb, carry):
        k0 = pl.multiple_of(kb * ts, ts)
        bias_ref[pl.ds(k0, ts), :] = jnp.full((ts, tq), NEG, bias_ref.dtype)
        return carry

    lax.fori_loop(nkb, lp // ts, fill_blk, 0)


def dsa_select(iq, ik, iwt, topk):
    nh, lp, _ = iq.shape
    assert lp < 2 ** 15
    tq = 256
    ts = _pick(lp, (640, 256, 128))
    kern = functools.partial(_select_kernel, tq=tq, ts=ts, lp=lp, topk=topk)
    return pl.pallas_call(
        kern,
        grid=(lp // tq,),
        in_specs=[pl.BlockSpec((nh, tq, HEAD_DIM), lambda i: (0, i, 0)),
                  pl.BlockSpec((lp, HEAD_DIM), lambda i: (0, 0), pipeline_mode=pl.Buffered(1)),
                  pl.BlockSpec((nh, tq), lambda i: (0, i))],
        out_specs=pl.BlockSpec((lp, tq), lambda i: (0, i)),
        out_shape=jax.ShapeDtypeStruct((lp, lp), bf16),
        scratch_shapes=[pltpu.VMEM((lp, tq), i32), pltpu.VMEM((lp, tq), i16)],
        compiler_params=_cparams(("parallel",)),
        name="dsa_select",
    )(iq, ik, iwt)


V_AUG = HEAD_DIM + 16


def _attn_kernel(q_ref, k_ref, vt_ref, b_ref, o_ref, m_ref, acc_ref, *, tq, ts):
    i = pl.program_id(0)
    nkb = ((i + 1) * tq + ts - 1) // ts
    group = ATT_HEADS // ATT_KV_HEADS
    qpos = i * tq + lax.broadcasted_iota(i32, (tq, 1), 0)
    eye = (lax.broadcasted_iota(i32, (tq, tq), 0)
           == lax.broadcasted_iota(i32, (tq, tq), 1)).astype(bf16)
    for g in range(ATT_KV_HEADS):
        qa = jnp.concatenate(
            [jnp.concatenate([q_ref[:, (g * group + r) * HEAD_DIM:(g * group + r + 1) * HEAD_DIM], eye],
                             axis=1) for r in range(group)], axis=0)
        m_ref[...] = jnp.full(m_ref.shape, -jnp.inf, f32)
        acc_ref[...] = jnp.zeros_like(acc_ref)

        def scores(k0):
            ka = jnp.concatenate([k_ref[pl.ds(k0, ts), g * HEAD_DIM:(g + 1) * HEAD_DIM],
                                  b_ref[pl.ds(k0, ts), :]], axis=1)
            return _dot_nt(ka, qa)

        def accumulate(k0, st):
            vt_ = vt_ref[g, :, pl.ds(k0, ts)]
            ps, alphas = [], []
            for r in range(group):
                sr = st[:, r * tq:(r + 1) * tq]
                m_old = m_ref[r:r + 1, :]
                m_new = jnp.maximum(m_old, jnp.max(sr, axis=0, keepdims=True))
                m_ref[r:r + 1, :] = m_new
                ps.append(jnp.exp2(sr - m_new).astype(bf16))
                alphas.append(jnp.exp2(m_old - m_new))
            pt = jnp.concatenate(ps, axis=1)
            acc_ref[...] = jnp.concatenate(alphas, axis=1) * acc_ref[...] + _dot(vt_, pt)

        def blk2(kp, carry):
            k0 = pl.multiple_of(kp * (2 * ts), ts)
            k1 = pl.multiple_of(k0 + ts, ts)
            st0 = scores(k0)
            st1 = scores(k1)
            accumulate(k0, st0)
            accumulate(k1, st1)
            return carry

        lax.fori_loop(0, (nkb + 1) // 2, blk2, 0)
        for r in range(group):
            ot = (acc_ref[0:HEAD_DIM, r * tq:(r + 1) * tq]
                  / acc_ref[HEAD_DIM:HEAD_DIM + 1, r * tq:(r + 1) * tq])
            orr = jnp.where(qpos >= FRAME_OFF, ot.T, 0.0)
            o_ref[:, (g * group + r) * HEAD_DIM:(g * group + r + 1) * HEAD_DIM] = orr.astype(o_ref.dtype)


def dsa_attention(q, k, vt_aug, bias_t):
    lp = q.shape[0]
    tq = 128
    ts = _pick(lp, (640, 256, 128))
    group = ATT_HEADS // ATT_KV_HEADS
    assert (lp // ts) % 2 == 0
    kern = functools.partial(_attn_kernel, tq=tq, ts=ts)
    return pl.pallas_call(
        kern,
        grid=(lp // tq,),
        in_specs=[pl.BlockSpec((tq, ATT_Q_W), lambda i: (i, 0)),
                  pl.BlockSpec((lp, ATT_KV_W), lambda i: (0, 0)),
                  pl.BlockSpec((ATT_KV_HEADS, V_AUG, lp), lambda i: (0, 0, 0)),
                  pl.BlockSpec((lp, tq), lambda i: (0, i))],
        out_specs=pl.BlockSpec((tq, ATT_Q_W), lambda i: (i, 0)),
        out_shape=jax.ShapeDtypeStruct((lp, ATT_Q_W), bf16),
        scratch_shapes=[pltpu.VMEM((group, tq), f32),
                        pltpu.VMEM((V_AUG, group * tq), f32)],
        compiler_params=_cparams(("parallel",)),
        name="dsa_attention",
    )(q, k, vt_aug, bias_t)


def _ffn_up_kernel(a_ref, ah_ref, wg_ref, wv_ref, cg_ref, cv_ref, bg_ref, bv_ref, o_ref,
                   eg_ref, ev_ref, *, tm):
    i = pl.program_id(0)
    halo = ah_ref[...]
    halo = jnp.where(i > 0, halo, jnp.zeros_like(halo))
    a = jnp.concatenate([halo, a_ref[...]], axis=0)

    eg_ref[...] = _dot(a, wg_ref[...])
    ev_ref[...] = _dot(a, wv_ref[...])

    def conv(ext_ref, w_ref, b_ref):
        w = w_ref[...]
        y = ext_ref[HALO:HALO + tm, :] * w[FFN_CONV - 1:FFN_CONV, :]
        for j in range(FFN_CONV - 1):
            s0 = HALO - (FFN_CONV - 1) + j
            y = y + ext_ref[s0:s0 + tm, :] * w[j:j + 1, :]
        return y + b_ref[...]

    gate = conv(eg_ref, cg_ref, bg_ref)
    val = conv(ev_ref, cv_ref, bv_ref)
    o_ref[...] = (gate * _sigmoid(gate) * val).astype(o_ref.dtype)


def ffn_up(u, w_up, conv_w, conv_b):
    lp, kdim = u.shape
    tm = _pick(lp, (1280, 640, 256, 128))
    tn = 512
    nc = D_FF // tn
    kern = functools.partial(_ffn_up_kernel, tm=tm)

    def wspec(off):
        return pl.BlockSpec((kdim, tn), lambda i, c: (0, c + off))

    def cspec(off):
        return pl.BlockSpec((FFN_CONV, tn), lambda i, c: (0, c + off))

    def bspec(off):
        return pl.BlockSpec((1, tn), lambda i, c: (0, c + off))

    b2 = conv_b.reshape(1, 2 * D_FF)
    return pl.pallas_call(
        kern,
        grid=(lp // tm, nc),
        in_specs=[pl.BlockSpec((tm, kdim), lambda i, c: (i, 0)),
                  pl.BlockSpec((HALO, kdim), lambda i, c: (jnp.maximum(i * (tm // HALO) - 1, 0), 0)),
                  wspec(0), wspec(nc), cspec(0), cspec(nc), bspec(0), bspec(nc)],
        out_specs=pl.BlockSpec((tm, tn), lambda i, c: (i, c)),
        out_shape=jax.ShapeDtypeStruct((lp, D_FF), bf16),
        scratch_shapes=[pltpu.VMEM((HALO + tm, tn), f32), pltpu.VMEM((HALO + tm, tn), f32)],
        compiler_params=_cparams(("parallel", "arbitrary")),
        name="ffn_up",
    )(u, u, w_up, w_up, conv_w, conv_w, b2, b2)


def _split_w_in(w):
    o = 0
    parts = {}
    for name, width in (("gq", GDN_QK_W), ("gk", GDN_QK_W), ("gv", GDN_V_W), ("gz", GDN_V_W),
                        ("gb", GDN_V_HEADS), ("ga", GDN_V_HEADS), ("aq", ATT_Q_W), ("ak", ATT_KV_W),
                        ("av", ATT_KV_W), ("iq", IDX_Q_W), ("ik", HEAD_DIM), ("iw", IDX_HEADS),
                        ("gate_gdn", D_MODEL), ("gate_att", D_MODEL)):
        parts[name] = w[:, o:o + width]
        o += width
    big = jnp.concatenate([parts[n] for n in ("gq", "gk", "gv", "gz", "aq", "iq", "gate_gdn",
                                              "gate_att", "ak", "av", "ik")], axis=1).astype(bf16)
    small = jnp.concatenate([parts["gb"], parts["ga"], parts["iw"]], axis=1)
    small = jnp.pad(small, ((0, 0), (0, LANES - small.shape[1]))).astype(bf16)
    return big, small


def _layer(h0, p, lp, topk):
    tm = _pick(lp, (640, 256, 128))
    tm_big = _pick(lp, (1280, 640, 256, 128))
    w_big, w_small = _split_w_in(p["w_in"])

    u1 = rms_rows(h0, p["mix_pre_g"])
    (proj,) = matmul(u1, w_big, tm=tm, tn=1920, tk=D_MODEL, epilogue=_ep_cast,
                     out_shapes=[jax.ShapeDtypeStruct((lp, PROJ_W), bf16)], name="proj_in")
    (small,) = matmul(u1, w_small, tm=tm_big, tn=LANES, tk=D_MODEL, epilogue=_ep_cast,
                      out_shapes=[jax.ShapeDtypeStruct((lp, LANES), f32)], name="proj_small")
    gb = small[:, 0:GDN_V_HEADS]
    ga = small[:, GDN_V_HEADS:2 * GDN_V_HEADS]
    iw = small[:, 2 * GDN_V_HEADS:2 * GDN_V_HEADS + IDX_HEADS]

    qkv = gdn_prep(proj, p["gdn_conv_w"])
    beta, gcum = gdn_gates(gb, ga, p["gdn_a_log"], p["gdn_dt_bias"])
    gcol = gcum.T[:, :, None]
    grow = gcum.T[:, None, :]
    bcol = beta.T[:, :, None]
    brow = beta.T[:, None, :]
    o_gdn = gdn_core(qkv, proj, p["gdn_norm_g"], gcol, grow, bcol, brow)

    tabs = rope_tables(lp)
    aq = rope(proj, COL_AQ, ATT_HEADS, tabs, scale=HEAD_DIM ** -0.5 * LOG2E)
    ak = rope(proj, COL_AK, ATT_KV_HEADS, tabs)
    iq = rope(proj, COL_IQ, IDX_HEADS, tabs, head_major=True)
    ik = rope(proj, COL_IK, 1, tabs)
    avt = proj[:, COL_AV:COL_AV + ATT_KV_W].T.reshape(ATT_KV_HEADS, HEAD_DIM, lp)
    avt = jnp.concatenate([avt, jnp.ones((ATT_KV_HEADS, V_AUG - HEAD_DIM, lp), bf16)], axis=1)
    bias_t = dsa_select(iq, ik, iw.T, topk)
    o_att = dsa_attention(aq, ak, avt, bias_t)

    tn = 512
    gate_spec = lambda col0: ((tm_big, tn), (lambda i, j, c=col0 // tn: (i, c + j)))
    (m1,) = matmul(o_gdn, p["w_branch_gdn"].astype(bf16), tm=tm_big, tn=tn, tk=2048,
                   epilogue=_ep_gate, extra=(proj,), extra_specs=(gate_spec(COL_GATE_GDN),),
                   out_shapes=[jax.ShapeDtypeStruct((lp, D_MODEL), f32)], name="branch_gdn")
    (merged,) = matmul(o_att, p["w_branch_att"].astype(bf16), tm=tm_big, tn=tn, tk=2048,
                       epilogue=_ep_gate_add, extra=(proj, m1),
                       extra_specs=(gate_spec(COL_GATE_ATT), ((tm_big, tn), lambda i, j: (i, j))),
                       out_shapes=[jax.ShapeDtypeStruct((lp, D_MODEL), bf16)], name="branch_att")
    row_spec = ((tm, D_MODEL), lambda i, j: (i, 0))
    vec_spec = ((1, D_MODEL), lambda i, j: (0, 0))
    h1, u2 = matmul(merged, p["w_out"].astype(bf16), tm=tm, tn=D_MODEL, tk=1024,
                    epilogue=_ep_res_norm2,
                    extra=(h0, p["mix_post_g"].reshape(1, D_MODEL), p["ffn_pre_g"].reshape(1, D_MODEL)),
                    extra_specs=(row_spec, vec_spec, vec_spec),
                    out_shapes=[jax.ShapeDtypeStruct((lp, D_MODEL), f32),
                                jax.ShapeDtypeStruct((lp, D_MODEL), bf16)], name="w_out")

    act = ffn_up(u2, p["w_up"].astype(bf16), p["ffn_conv_w"], p["ffn_conv_b"])
    (h2,) = matmul(act, p["w_down"].astype(bf16), tm=tm, tn=D_MODEL, tk=1024, epilogue=_ep_res_norm,
                   extra=(h1, p["ffn_post_g"].reshape(1, D_MODEL)),
                   extra_specs=(row_spec, vec_spec),
                   out_shapes=[jax.ShapeDtypeStruct((lp, D_MODEL), f32)], name="w_down")
    return h2


def kernel(x, meta_tokens, mix_pre_g, w_in, gdn_conv_w, gdn_a_log, gdn_dt_bias, gdn_norm_g,
           w_branch_gdn, w_branch_att, w_out, mix_post_g, ffn_pre_g, w_up, ffn_conv_w,
           ffn_conv_b, w_down, ffn_post_g):
    batch, seq, d = x.shape
    assert batch == 1 and d == D_MODEL
    lp = FRAME_X0 + seq
    topk = min(TOPK_MAX, (N_META + seq) // 4)
    h = jnp.concatenate([jnp.zeros((FRAME_OFF, d), x.dtype), meta_tokens.astype(x.dtype), x[0]], axis=0)
    for i in range(w_in.shape[0]):
        p = dict(mix_pre_g=mix_pre_g[i], w_in=w_in[i], gdn_conv_w=gdn_conv_w[i], gdn_a_log=gdn_a_log[i],
                 gdn_dt_bias=gdn_dt_bias[i], gdn_norm_g=gdn_norm_g[i], w_branch_gdn=w_branch_gdn[i],
                 w_branch_att=w_branch_att[i], w_out=w_out[i], mix_post_g=mix_post_g[i],
                 ffn_pre_g=ffn_pre_g[i], w_up=w_up[i], ffn_conv_w=ffn_conv_w[i],
                 ffn_conv_b=ffn_conv_b[i], w_down=w_down[i], ffn_post_g=ffn_post_g[i])
        h = _layer(h, p, lp, topk)
    return h[FRAME_X0:][None]
```

```python
import functools
import struct

import jax
import jax.numpy as jnp
from jax import lax
from jax.experimental import pallas as pl
from jax.experimental.pallas import tpu as pltpu

f32 = jnp.float32
bf16 = jnp.bfloat16
i32 = jnp.int32
i16 = jnp.int16

D_MODEL = 2048
N_META = 16
EPS = 1e-6
GDN_QK_HEADS = 16
GDN_V_HEADS = 32
HEAD_DIM = 128
GDN_CONV = 4
ATT_HEADS = 16
ATT_KV_HEADS = 2
IDX_HEADS = 16
TOPK_MAX = 256
NEG = -1e30
LOG2E = 1.4426950408889634
ROPE_THETA = 500000.0
ROPE_DIMS = HEAD_DIM // 4
D_FF = 3 * D_MODEL
FFN_CONV = 3
GDN_QK_W = GDN_QK_HEADS * HEAD_DIM
GDN_V_W = GDN_V_HEADS * HEAD_DIM
ATT_Q_W = ATT_HEADS * HEAD_DIM
ATT_KV_W = ATT_KV_HEADS * HEAD_DIM
IDX_Q_W = IDX_HEADS * HEAD_DIM

FRAME_X0 = 256
FRAME_OFF = FRAME_X0 - N_META
CHUNK = 128

LANES = 128
VMEM_LIMIT = 56 * 1024 * 1024

COL_GZ = 0
COL_GATE_GDN = COL_GZ + GDN_V_W
COL_GATE_ATT = COL_GATE_GDN + D_MODEL
COL_AV = COL_GATE_ATT + D_MODEL
PLAIN_W = COL_AV + ATT_KV_W


def _cparams(sem):
    return pltpu.CompilerParams(dimension_semantics=sem, vmem_limit_bytes=VMEM_LIMIT)


def _pick(n, cands):
    for c in cands:
        if n % c == 0:
            return c
    raise ValueError(f"no tile for {n} in {cands}")


def _sigmoid(x):
    return 1.0 / (1.0 + jnp.exp(-x))


def _dot(a, b):
    return jnp.dot(a, b, preferred_element_type=f32)


def _dot_nt(a, b):
    return lax.dot_general(a, b, (((1,), (1,)), ((), ())), preferred_element_type=f32)


def _rms_rows_kernel(h_ref, g_ref, o_ref):
    h = h_ref[...]
    y = h * lax.rsqrt(jnp.mean(h * h, axis=-1, keepdims=True) + EPS)
    o_ref[...] = (y * g_ref[...]).astype(o_ref.dtype)


def rms_rows(h, g):
    lp, d = h.shape
    tm = _pick(lp, (640, 256, 128))
    return pl.pallas_call(
        _rms_rows_kernel,
        grid=(lp // tm,),
        in_specs=[pl.BlockSpec((tm, d), lambda i: (i, 0)),
                  pl.BlockSpec((1, d), lambda i: (0, 0))],
        out_specs=pl.BlockSpec((tm, d), lambda i: (i, 0)),
        out_shape=jax.ShapeDtypeStruct((lp, d), bf16),
        compiler_params=_cparams(("parallel",)),
        name="rms_rows",
    )(h, g.reshape(1, d))


def _mm_kernel(*refs, nk, n_extra, n_out, epilogue):
    a_ref, w_ref = refs[0], refs[1]
    extra = refs[2:2 + n_extra]
    outs = refs[2 + n_extra:2 + n_extra + n_out]
    if nk == 1:
        epilogue(_dot(a_ref[...], w_ref[...]), extra, outs)
        return
    acc_ref = refs[-1]
    k = pl.program_id(2)

    @pl.when(k == 0)
    def _():
        acc_ref[...] = jnp.zeros_like(acc_ref)

    acc_ref[...] += _dot(a_ref[...], w_ref[...])

    @pl.when(k == nk - 1)
    def _():
        epilogue(acc_ref[...], extra, outs)


def matmul(a, w, *, tm, tn, tk, epilogue, extra=(), extra_specs=(), out_shapes, out_block_specs=None,
           a_col0=0, name):
    m = a.shape[0]
    kdim, n = w.shape
    nk = kdim // tk
    assert m % tm == 0 and n % tn == 0 and kdim % tk == 0
    in_specs = [pl.BlockSpec((tm, tk), lambda i, j, k: (i, a_col0 + k)),
                pl.BlockSpec((tk, tn), lambda i, j, k: (k, j))]
    in_specs += [pl.BlockSpec(bs, (lambda i, j, k, f=f: f(i, j))) for bs, f in extra_specs]
    if out_block_specs is None:
        out_block_specs = [((tm, tn), lambda i, j: (i, j))] * len(out_shapes)
    out_specs = [pl.BlockSpec(bs, (lambda i, j, k, f=f: f(i, j))) for bs, f in out_block_specs]
    scratch = [] if nk == 1 else [pltpu.VMEM((tm, tn), f32)]
    kern = functools.partial(_mm_kernel, nk=nk, n_extra=len(extra), n_out=len(out_shapes),
                             epilogue=epilogue)
    res = pl.pallas_call(
        kern,
        grid=(m // tm, n // tn, nk),
        in_specs=in_specs,
        out_specs=out_specs,
        out_shape=out_shapes,
        scratch_shapes=scratch,
        compiler_params=_cparams(("parallel", "parallel", "arbitrary")),
        name=name,
    )(a, w, *extra)
    return res


def _ep_cast(acc, extra, outs):
    outs[0][...] = acc.astype(outs[0].dtype)


def _ep_gate(acc, extra, outs):
    g = extra[0][...].astype(f32)
    outs[0][...] = (_sigmoid(g) * acc).astype(outs[0].dtype)


def _ep_gate_add(acc, extra, outs):
    g = extra[0][...].astype(f32)
    outs[0][...] = (extra[1][...].astype(f32) + _sigmoid(g) * acc).astype(outs[0].dtype)


def _rms(t, g):
    return t * lax.rsqrt(jnp.mean(t * t, axis=-1, keepdims=True) + EPS) * g


def _ep_res_norm2(acc, extra, outs):
    h_ref, g_ref, g2_ref = extra
    h1 = h_ref[...] + _rms(acc, g_ref[...])
    outs[0][...] = h1
    outs[1][...] = _rms(h1, g2_ref[...]).astype(outs[1].dtype)


def _ep_res_norm(acc, extra, outs):
    h_ref, g_ref = extra
    outs[0][...] = h_ref[...] + _rms(acc, g_ref[...])


HALO = 16


def _gdn_proj_kernel(a_ref, ah_ref, w_ref, cw_ref, o_ref, ext_ref, *, tm, tn):
    i = pl.program_id(0)
    c = pl.program_id(1)
    halo = ah_ref[...]
    halo = jnp.where(i > 0, halo, jnp.zeros_like(halo))
    a = jnp.concatenate([halo, a_ref[...]], axis=0)
    ext_ref[...] = _dot(a, w_ref[...])
    w = cw_ref[...]
    y = ext_ref[HALO:HALO + tm, :] * w[GDN_CONV - 1:GDN_CONV, :]
    for j in range(GDN_CONV - 1):
        s0 = HALO - (GDN_CONV - 1) + j
        y = y + ext_ref[s0:s0 + tm, :] * w[j:j + 1, :]
    s = y * _sigmoid(y)
    is_q = c < (GDN_QK_W // tn)
    is_qk = c < (2 * GDN_QK_W // tn)
    qscale = jnp.where(is_q, HEAD_DIM ** -0.5, 1.0).astype(f32)
    for hh in range(tn // HEAD_DIM):
        seg = s[:, hh * HEAD_DIM:(hh + 1) * HEAD_DIM]
        r = lax.rsqrt(jnp.sum(seg * seg, axis=-1, keepdims=True) + EPS) * qscale
        fac = jnp.where(is_qk, r, 1.0)
        o_ref[:, hh * HEAD_DIM:(hh + 1) * HEAD_DIM] = (seg * fac).astype(o_ref.dtype)


def gdn_proj(u, w_qkv, conv_w):
    lp, kdim = u.shape
    width = w_qkv.shape[1]
    tm = _pick(lp, (1280, 640, 256, 128))
    tn = 1024
    kern = functools.partial(_gdn_proj_kernel, tm=tm, tn=tn)
    return pl.pallas_call(
        kern,
        grid=(lp // tm, width // tn),
        in_specs=[pl.BlockSpec((tm, kdim), lambda i, c: (i, 0)),
                  pl.BlockSpec((HALO, kdim), lambda i, c: (jnp.maximum(i * (tm // HALO) - 1, 0), 0)),
                  pl.BlockSpec((kdim, tn), lambda i, c: (0, c)),
                  pl.BlockSpec((GDN_CONV, tn), lambda i, c: (0, c))],
        out_specs=pl.BlockSpec((tm, tn), lambda i, c: (i, c)),
        out_shape=jax.ShapeDtypeStruct((lp, width), bf16),
        scratch_shapes=[pltpu.VMEM((HALO + tm, tn), f32)],
        compiler_params=_cparams(("parallel", "arbitrary")),
        name="gdn_proj",
    )(u, u, w_qkv, conv_w)


def _gdn_gates_kernel(b_ref, a_ref, alog_ref, dt_ref, beta_ref, gcum_ref):
    i = pl.program_id(0)
    rows = i * CHUNK + lax.broadcasted_iota(i32, (CHUNK, 1), 0)
    valid = rows >= FRAME_OFF
    beta_ref[...] = jnp.where(valid, _sigmoid(b_ref[...]), 0.0)
    a = a_ref[...] + dt_ref[...]
    sp = jnp.maximum(a, 0.0) + jnp.log1p(jnp.exp(-jnp.abs(a)))
    g = jnp.where(valid, -jnp.exp(alog_ref[...]) * sp, 0.0)
    tri = (lax.broadcasted_iota(i32, (CHUNK, CHUNK), 0)
           >= lax.broadcasted_iota(i32, (CHUNK, CHUNK), 1)).astype(f32)
    gcum_ref[...] = jnp.dot(tri, g, preferred_element_type=f32, precision=lax.Precision.HIGHEST)


def gdn_gates(gb, ga, a_log, dt_bias):
    lp, nh = gb.shape
    spec = pl.BlockSpec((CHUNK, nh), lambda i: (i, 0))
    vec = pl.BlockSpec((1, nh), lambda i: (0, 0))
    return pl.pallas_call(
        _gdn_gates_kernel,
        grid=(lp // CHUNK,),
        in_specs=[spec, spec, vec, vec],
        out_specs=[spec, spec],
        out_shape=[jax.ShapeDtypeStruct((lp, nh), f32)] * 2,
        compiler_params=_cparams(("parallel",)),
        name="gdn_gates",
    )(gb, ga, a_log.reshape(1, nh), dt_bias.reshape(1, nh))


def _block_mask(size):
    r = lax.broadcasted_iota(i32, (CHUNK, CHUNK), 0) // size
    c = lax.broadcasted_iota(i32, (CHUNK, CHUNK), 1) // size
    return r == c


def _unit_lower_inverse_many(mats):
    row = lax.broadcasted_iota(i32, (CHUNK, CHUNK), 0)
    col = lax.broadcasted_iota(i32, (CHUNK, CHUNK), 1)
    eye = (row == col).astype(f32)
    base = 8
    m_prev = _block_mask(base)
    ads = [jnp.where(m_prev, a, 0.0) for a in mats]
    adbs = [ad.astype(bf16) for ad in ads]
    a2s = [_dot(x, x) for x in adbs]
    a2bs = [x.astype(bf16) for x in a2s]
    a4s = [_dot(x, x) for x in a2bs]
    xs = [_dot((eye - ad).astype(bf16), (eye + a2).astype(bf16)) for ad, a2 in zip(ads, a2s)]
    xs = [_dot(x.astype(bf16), (eye + a4).astype(bf16)) for x, a4 in zip(xs, a4s)]
    size = base * 2
    while size <= CHUNK:
        m_cur = _block_mask(size)
        off_diag = jnp.logical_and(m_cur, jnp.logical_not(m_prev))
        bs = [jnp.where(off_diag, a, 0.0).astype(bf16) for a in mats]
        xbs = [x.astype(bf16) for x in xs]
        ys = [_dot(xb, b).astype(bf16) for xb, b in zip(xbs, bs)]
        xs = [x - _dot(y, xb) for x, y, xb in zip(xs, ys, xbs)]
        m_prev = m_cur
        size *= 2
    return xs


def _gdn_kernel(q_ref, k_ref, v_ref, z_ref, ng_ref,
                gc0_ref, gc1_ref, gr0_ref, gr1_ref, bc0_ref, bc1_ref, br0_ref, br1_ref,
                o_ref, s_ref, *, rb):
    r = pl.program_id(1)

    @pl.when(r == 0)
    def _():
        s_ref[...] = jnp.zeros_like(s_ref)

    gcs, grs = (gc0_ref, gc1_ref), (gr0_ref, gr1_ref)
    bcs, brs = (bc0_ref, bc1_ref), (br0_ref, br1_ref)
    row = lax.broadcasted_iota(i32, (CHUNK, CHUNK), 0)
    col = lax.broadcasted_iota(i32, (CHUNK, CHUNK), 1)
    incl = row >= col
    strict = row > col
    ng = ng_ref[...]
    nchunk = rb // CHUNK
    heads = range(2)

    def rows(c):
        return slice(c * CHUNK, (c + 1) * CHUNK)

    def lanes(e):
        return slice(e * HEAD_DIM, (e + 1) * HEAD_DIM)

    qs = [q_ref[rows(c), :] for c in range(nchunk)]
    ks = [k_ref[rows(c), :] for c in range(nchunk)]
    kts = [k.astype(f32).T for k in ks]
    kks = [_dot_nt(k, k) for k in ks]
    qks = [_dot_nt(q, k) for q, k in zip(qs, ks)]
    chains = [(c, e) for c in range(nchunk) for e in heads]
    gc = {ce: gcs[ce[1]][rows(ce[0]), :] for ce in chains}
    gr = {ce: grs[ce[1]][:, rows(ce[0])] for ce in chains}
    bc = {ce: bcs[ce[1]][rows(ce[0]), :] for ce in chains}
    br = {ce: brs[ce[1]][:, rows(ce[0])] for ce in chains}
    dec = {ce: jnp.where(incl, jnp.exp(jnp.where(incl, gc[ce] - gr[ce], 0.0)), 0.0) for ce in chains}
    amat = [jnp.where(strict, bc[ce] * kks[ce[0]] * dec[ce], 0.0) for ce in chains]
    tinv = dict(zip(chains, _unit_lower_inverse_many(amat)))
    u = {ce: _dot((tinv[ce] * br[ce]).astype(bf16), v_ref[rows(ce[0]), lanes(ce[1])]) for ce in chains}
    w = {ce: _dot((tinv[ce] * (br[ce] * jnp.exp(gr[ce]))).astype(bf16), ks[ce[0]]) for ce in chains}
    g_last = {ce: gr[ce][:, CHUNK - 1:CHUNK] for ce in chains}
    wq = {ce: jnp.concatenate([w[ce].astype(bf16), qs[ce[0]]], axis=0) for ce in chains}
    ak = {ce: jnp.concatenate(
        [jnp.where(incl, qks[ce[0]] * dec[ce], 0.0).astype(bf16),
         (kts[ce[0]] * jnp.exp(g_last[ce] - gr[ce])).astype(bf16)], axis=0) for ce in chains}
    eg = {ce: jnp.exp(gc[ce]) for ce in chains}
    egl = {ce: jnp.exp(g_last[ce]) for ce in chains}

    state = [s_ref[e] for e in heads]
    for c in range(nchunk):
        sb = [state[e].astype(bf16) for e in heads]
        ws = [_dot(wq[(c, e)], sb[e]) for e in heads]
        vb = [(u[(c, e)] - ws[e][:CHUNK]).astype(bf16) for e in heads]
        av = [_dot(ak[(c, e)], vb[e]) for e in heads]
        for e in heads:
            o = eg[(c, e)] * ws[e][CHUNK:] + av[e][:CHUNK]
            state[e] = state[e] * egl[(c, e)] + av[e][CHUNK:]
            z = z_ref[rows(c), lanes(e)].astype(f32)
            o_ref[rows(c), lanes(e)] = (_rms(o, ng) * (z * _sigmoid(z))).astype(o_ref.dtype)
    for e in heads:
        s_ref[e] = state[e]


def gdn_core(qkv, plain, norm_g, gcol, grow, bcol, brow):
    lp = qkv.shape[0]
    rb = _pick(lp, (640, 256, 128))
    nqk = GDN_QK_HEADS
    kcol0 = GDN_QK_W // HEAD_DIM
    vcol0 = 2 * GDN_QK_W // (2 * HEAD_DIM)
    zcol0 = COL_GZ // (2 * HEAD_DIM)

    def colspec(e):
        return pl.BlockSpec((None, rb, 1), lambda j, r: (2 * j + e, r, 0))

    def rowspec(e):
        return pl.BlockSpec((None, 1, rb), lambda j, r: (2 * j + e, 0, r))

    kern = functools.partial(_gdn_kernel, rb=rb)
    return pl.pallas_call(
        kern,
        grid=(nqk, lp // rb),
        in_specs=[pl.BlockSpec((rb, HEAD_DIM), lambda j, r: (r, j)),
                  pl.BlockSpec((rb, HEAD_DIM), lambda j, r: (r, kcol0 + j)),
                  pl.BlockSpec((rb, 2 * HEAD_DIM), lambda j, r: (r, vcol0 + j)),
                  pl.BlockSpec((rb, 2 * HEAD_DIM), lambda j, r: (r, zcol0 + j)),
                  pl.BlockSpec((1, HEAD_DIM), lambda j, r: (0, 0)),
                  colspec(0), colspec(1), rowspec(0), rowspec(1),
                  colspec(0), colspec(1), rowspec(0), rowspec(1)],
        out_specs=pl.BlockSpec((rb, 2 * HEAD_DIM), lambda j, r: (r, j)),
        out_shape=jax.ShapeDtypeStruct((lp, GDN_V_W), bf16),
        scratch_shapes=[pltpu.VMEM((2, HEAD_DIM, HEAD_DIM), f32)],
        compiler_params=_cparams(("parallel", "arbitrary")),
        name="gdn_core",
    )(qkv, qkv, qkv, plain, norm_g.reshape(1, HEAD_DIM),
      gcol, gcol, grow, grow, bcol, bcol, brow, brow)


def _rope_head(seg, cc, s1, s2):
    half = ROPE_DIMS // 2
    return seg * cc + pltpu.roll(seg, half, 1) * s1 + pltpu.roll(seg, HEAD_DIM - half, 1) * s2


def _make_ep_rope(scale, head_major):
    def ep(acc, extra, outs):
        cc, s1, s2 = (r[...] for r in extra)
        for h in range(acc.shape[1] // HEAD_DIM):
            y = _rope_head(acc[:, h * HEAD_DIM:(h + 1) * HEAD_DIM], cc, s1, s2)
            if scale != 1.0:
                y = y * scale
            if head_major:
                outs[0][h] = y.astype(outs[0].dtype)
            else:
                outs[0][:, h * HEAD_DIM:(h + 1) * HEAD_DIM] = y.astype(outs[0].dtype)
    return ep


def _ep_rope_kv(acc, extra, outs):
    cc, s1, s2 = (r[...] for r in extra)
    for h in range(ATT_KV_HEADS):
        y = _rope_head(acc[:, h * HEAD_DIM:(h + 1) * HEAD_DIM], cc, s1, s2)
        outs[0][:, h * HEAD_DIM:(h + 1) * HEAD_DIM] = y.astype(outs[0].dtype)
    y = _rope_head(acc[:, ATT_KV_W:ATT_KV_W + HEAD_DIM], cc, s1, s2)
    outs[1][...] = y.astype(outs[1].dtype)


def rope_tables(lp):
    half = ROPE_DIMS // 2
    pos = (jnp.arange(lp, dtype=jnp.int32) - FRAME_OFF).astype(f32)
    inv = ROPE_THETA ** (-jnp.arange(half, dtype=f32) / half)
    ang = pos[:, None] * inv[None, :]
    cos, sin = jnp.cos(ang), jnp.sin(ang)
    zeros = jnp.zeros((lp, HEAD_DIM - ROPE_DIMS), f32)
    z16 = jnp.zeros((lp, half), f32)
    cc = jnp.concatenate([cos, cos, jnp.ones_like(zeros)], axis=1)
    s1 = jnp.concatenate([z16, sin, zeros], axis=1)
    s2 = jnp.concatenate([-sin, z16, zeros], axis=1)
    return cc, s1, s2


INT_MAX = 2 ** 31 - 1
HALF_MIN, HALF_MAX, HALF_SPAN = -2 ** 15, 2 ** 15 - 1, 2 ** 16


def _sortable(x):
    b = lax.bitcast_convert_type(x, i32)
    return b ^ (lax.shift_right_arithmetic(b, 31) & INT_MAX)


_NEG_BITS = struct.unpack("<i", struct.pack("<f", NEG))[0]
_KEY_NEG = _NEG_BITS ^ ((_NEG_BITS >> 31) & INT_MAX)
_KEY_NEG_HI = _KEY_NEG >> 16
_KEY_NEG_LO = (_KEY_NEG & (HALF_SPAN - 1)) + HALF_MIN


def _select_kernel(iq_ref, ik_ref, iwt_ref, bias_ref, key_ref, half_ref, *, tq, ts, lp, topk):
    i = pl.program_id(0)
    nkb = ((i + 1) * tq + ts - 1) // ts
    rem = lp - nkb * ts
    iwt = iwt_ref[...] * ((IDX_HEADS ** -0.5) * (HEAD_DIM ** -0.5))
    qpos = i * tq + lax.broadcasted_iota(i32, (1, tq), 1)
    sub = lax.broadcasted_iota(i32, (ts, 1), 0)

    def score_blk(kb, carry):
        k0 = pl.multiple_of(kb * ts, ts)
        ikb = ik_ref[pl.ds(k0, ts), :]
        acc = jnp.zeros((ts, tq), f32)
        for h in range(IDX_HEADS):
            lg = _dot_nt(ikb, iq_ref[h])
            acc = acc + jnp.maximum(lg, 0.0) * iwt[h:h + 1, :]
        kpos = k0 + sub
        sc = jnp.where(kpos <= qpos, acc, NEG)
        sc = jnp.where(kpos < FRAME_OFF, -jnp.inf, sc)
        key = _sortable(sc)
        key_ref[pl.ds(k0, ts), :] = key
        half_ref[pl.ds(k0, ts), :] = lax.shift_right_arithmetic(key, 16).astype(i16)
        return carry

    lax.fori_loop(0, nkb, score_blk, 0)

    n_acc = 4

    def count(pred_fn):
        sub8 = lax.broadcasted_iota(i32, (8, 1), 0)

        def blk(kb, cnts):
            k0 = pl.multiple_of(kb * ts, ts)
            cnts = list(cnts)
            blk_ref = key_ref.at[pl.ds(k0, ts)]
            for j in range(ts // 8):
                kv = blk_ref[j * 8:(j + 1) * 8, :]
                cnts[j % n_acc] = cnts[j % n_acc] + jnp.where(pred_fn(kv, k0 + j * 8 + sub8), 1, 0)
            return tuple(cnts)
        cnts = lax.fori_loop(0, nkb, blk, tuple(jnp.zeros((8, tq), i32) for _ in range(n_acc)))
        return jnp.sum(sum(cnts[1:], cnts[0]), axis=0, keepdims=True)

    def count_half_ge(cand):
        one, zero = jnp.int16(1), jnp.int16(0)

        def blk(kb, cnts):
            k0 = pl.multiple_of(kb * ts, ts)
            cnts = list(cnts)
            blk_ref = half_ref.at[pl.ds(k0, ts)]
            for j in range(ts // 16):
                hv = blk_ref[j * 16:(j + 1) * 16, :]
                cnts[j % n_acc] = cnts[j % n_acc] + jnp.where(hv >= cand, one, zero)
            return tuple(cnts)
        cnts = lax.fori_loop(0, nkb, blk, tuple(jnp.zeros((16, tq), i16) for _ in range(n_acc)))
        return jnp.sum(sum(cnts[1:], cnts[0]).astype(i32), axis=0, keepdims=True)

    def search_half(neg_half, n_start):
        def body(it, c):
            tau, n_tau = c
            t2 = tau | lax.shift_left(jnp.int32(1), 15 - it)
            cand = t2 + HALF_MIN
            n2 = count_half_ge(cand.astype(i16)) + jnp.where(cand <= neg_half, rem, 0)
            ok = n2 >= topk
            return jnp.where(ok, t2, tau), jnp.where(ok, n2, n_tau)

        return lax.fori_loop(0, 16, body, (jnp.zeros((1, tq), i32), n_start))

    tau_hi, n_hi = search_half(jnp.full((1, tq), _KEY_NEG_HI, i32), jnp.full((1, tq), lp, i32))
    t_hi = tau_hi + HALF_MIN

    def low_blk(kb, carry):
        k0 = pl.multiple_of(kb * ts, ts)
        key = key_ref[pl.ds(k0, ts), :]
        hi = lax.shift_right_arithmetic(key, 16)
        lo = (key & (HALF_SPAN - 1)) + HALF_MIN
        lo = jnp.where(hi > t_hi, HALF_MAX, jnp.where(hi == t_hi, lo, HALF_MIN))
        half_ref[pl.ds(k0, ts), :] = lo.astype(i16)
        return carry

    lax.fori_loop(0, nkb, low_blk, 0)
    neg_lo = jnp.where(_KEY_NEG_HI > t_hi, HALF_MAX, jnp.where(_KEY_NEG_HI == t_hi, _KEY_NEG_LO, HALF_MIN))
    tau_lo, n_ge = search_half(neg_lo, n_hi)
    thr = lax.shift_left(t_hi, 16) | tau_lo
    has_tie = jnp.max(jnp.where(n_ge > topk, 1, 0)) > 0

    def tie_cut():
        n_gt = (count(lambda kv, kpos: kv > thr) + jnp.where(thr < _KEY_NEG, rem, 0))
        need = topk - n_gt

        def bis2(it, cut):
            bit = lax.shift_left(jnp.int32(1), 14 - it)
            c2 = cut | bit
            n = count(lambda kv, kpos: jnp.logical_and(kv == thr, kpos < c2))
            return jnp.where(n < need, c2, cut)
        return lax.fori_loop(0, 15, bis2, jnp.zeros((1, tq), i32))

    cut = lax.cond(has_tie, tie_cut, lambda: jnp.full((1, tq), INT_MAX, i32))

    def bias_blk(kb, carry):
        k0 = pl.multiple_of(kb * ts, ts)
        kv = key_ref[pl.ds(k0, ts), :]
        interior = jnp.logical_and(k0 >= FRAME_OFF, k0 + ts - 1 <= i * tq)

        @pl.when(jnp.logical_and(interior, jnp.logical_not(has_tie)))
        def _():
            bias_ref[pl.ds(k0, ts), :] = jnp.where(kv >= thr, 0.0, NEG).astype(bias_ref.dtype)

        @pl.when(jnp.logical_not(jnp.logical_and(interior, jnp.logical_not(has_tie))))
        def _():
            kpos = k0 + sub
            sel = jnp.logical_or(kv > thr, jnp.logical_and(kv == thr, kpos <= cut))
            vis = jnp.logical_and(kpos <= qpos, kpos >= FRAME_OFF)
            bias_ref[pl.ds(k0, ts), :] = jnp.where(jnp.logical_and(sel, vis), 0.0, NEG).astype(bias_ref.dtype)
        return carry

    lax.fori_loop(0, nkb, bias_blk, 0)

    def fill_blk(kb, carry):
        k0 = pl.multiple_of(kb * ts, ts)
        bias_ref[pl.ds(k0, ts), :] = jnp.full((ts, tq), NEG, bias_ref.dtype)
        return carry

    lax.fori_loop(nkb, lp // ts, fill_blk, 0)


def dsa_select(iq, ik, iwt, topk):
    nh, lp, _ = iq.shape
    assert lp < 2 ** 15
    tq = 256
    ts = _pick(lp, (640, 256, 128))
    kern = functools.partial(_select_kernel, tq=tq, ts=ts, lp=lp, topk=topk)
    return pl.pallas_call(
        kern,
        grid=(lp // tq,),
        in_specs=[pl.BlockSpec((nh, tq, HEAD_DIM), lambda i: (0, i, 0)),
                  pl.BlockSpec((lp, HEAD_DIM), lambda i: (0, 0), pipeline_mode=pl.Buffered(1)),
                  pl.BlockSpec((nh, tq), lambda i: (0, i))],
        out_specs=pl.BlockSpec((lp, tq), lambda i: (0, i)),
        out_shape=jax.ShapeDtypeStruct((lp, lp), bf16),
        scratch_shapes=[pltpu.VMEM((lp, tq), i32), pltpu.VMEM((lp, tq), i16)],
        compiler_params=_cparams(("parallel",)),
        name="dsa_select",
    )(iq, ik, iwt)


V_AUG = HEAD_DIM + 16


def _attn_kernel(q_ref, k_ref, vt_ref, b_ref, o_ref, m_ref, acc_ref, *, tq, ts):
    i = pl.program_id(0)
    nkb = ((i + 1) * tq + ts - 1) // ts
    group = ATT_HEADS // ATT_KV_HEADS
    qpos = i * tq + lax.broadcasted_iota(i32, (tq, 1), 0)
    eye = (lax.broadcasted_iota(i32, (tq, tq), 0)
           == lax.broadcasted_iota(i32, (tq, tq), 1)).astype(bf16)
    for g in range(ATT_KV_HEADS):
        qa = jnp.concatenate(
            [jnp.concatenate([q_ref[:, (g * group + r) * HEAD_DIM:(g * group + r + 1) * HEAD_DIM], eye],
                             axis=1) for r in range(group)], axis=0)
        m_ref[...] = jnp.full(m_ref.shape, -jnp.inf, f32)
        acc_ref[...] = jnp.zeros_like(acc_ref)

        def scores(k0):
            ka = jnp.concatenate([k_ref[pl.ds(k0, ts), g * HEAD_DIM:(g + 1) * HEAD_DIM],
                                  b_ref[pl.ds(k0, ts), :]], axis=1)
            return _dot_nt(ka, qa)

        def accumulate(k0, st):
            vt_ = vt_ref[g, :, pl.ds(k0, ts)]
            ps, alphas = [], []
            for r in range(group):
                sr = st[:, r * tq:(r + 1) * tq]
                m_old = m_ref[r:r + 1, :]
                m_new = jnp.maximum(m_old, jnp.max(sr, axis=0, keepdims=True))
                m_ref[r:r + 1, :] = m_new
                ps.append(jnp.exp2(sr - m_new).astype(bf16))
                alphas.append(jnp.exp2(m_old - m_new))
            pt = jnp.concatenate(ps, axis=1)
            acc_ref[...] = jnp.concatenate(alphas, axis=1) * acc_ref[...] + _dot(vt_, pt)

        def blk2(kp, carry):
            k0 = pl.multiple_of(kp * (2 * ts), ts)
            k1 = pl.multiple_of(k0 + ts, ts)
            st0 = scores(k0)
            st1 = scores(k1)
            accumulate(k0, st0)
            accumulate(k1, st1)
            return carry

        lax.fori_loop(0, (nkb + 1) // 2, blk2, 0)
        for r in range(group):
            ot = (acc_ref[0:HEAD_DIM, r * tq:(r + 1) * tq]
                  / acc_ref[HEAD_DIM:HEAD_DIM + 1, r * tq:(r + 1) * tq])
            orr = jnp.where(qpos >= FRAME_OFF, ot.T, 0.0)
            o_ref[:, (g * group + r) * HEAD_DIM:(g * group + r + 1) * HEAD_DIM] = orr.astype(o_ref.dtype)


def dsa_attention(q, k, vt_aug, bias_t):
    lp = q.shape[0]
    tq = 128
    ts = _pick(lp, (640, 256, 128))
    group = ATT_HEADS // ATT_KV_HEADS
    assert (lp // ts) % 2 == 0
    kern = functools.partial(_attn_kernel, tq=tq, ts=ts)
    return pl.pallas_call(
        kern,
        grid=(lp // tq,),
        in_specs=[pl.BlockSpec((tq, ATT_Q_W), lambda i: (i, 0)),
                  pl.BlockSpec((lp, ATT_KV_W), lambda i: (0, 0)),
                  pl.BlockSpec((ATT_KV_HEADS, V_AUG, lp), lambda i: (0, 0, 0)),
                  pl.BlockSpec((lp, tq), lambda i: (0, i))],
        out_specs=pl.BlockSpec((tq, ATT_Q_W), lambda i: (i, 0)),
        out_shape=jax.ShapeDtypeStruct((lp, ATT_Q_W), bf16),
        scratch_shapes=[pltpu.VMEM((group, tq), f32),
                        pltpu.VMEM((V_AUG, group * tq), f32)],
        compiler_params=_cparams(("parallel",)),
        name="dsa_attention",
    )(q, k, vt_aug, bias_t)


def _ffn_up_kernel(a_ref, ah_ref, wg_ref, wv_ref, cg_ref, cv_ref, bg_ref, bv_ref, o_ref,
                   eg_ref, ev_ref, *, tm):
    i = pl.program_id(0)
    halo = ah_ref[...]
    halo = jnp.where(i > 0, halo, jnp.zeros_like(halo))
    a = jnp.concatenate([halo, a_ref[...]], axis=0)

    eg_ref[...] = _dot(a, wg_ref[...])
    ev_ref[...] = _dot(a, wv_ref[...])

    def conv(ext_ref, w_ref, b_ref):
        w = w_ref[...]
        y = ext_ref[HALO:HALO + tm, :] * w[FFN_CONV - 1:FFN_CONV, :]
        for j in range(FFN_CONV - 1):
            s0 = HALO - (FFN_CONV - 1) + j
            y = y + ext_ref[s0:s0 + tm, :] * w[j:j + 1, :]
        return y + b_ref[...]

    gate = conv(eg_ref, cg_ref, bg_ref)
    val = conv(ev_ref, cv_ref, bv_ref)
    o_ref[...] = (gate * _sigmoid(gate) * val).astype(o_ref.dtype)


def ffn_up(u, w_up, conv_w, conv_b):
    lp, kdim = u.shape
    tm = _pick(lp, (1280, 640, 256, 128))
    tn = 512
    nc = D_FF // tn
    kern = functools.partial(_ffn_up_kernel, tm=tm)

    def wspec(off):
        return pl.BlockSpec((kdim, tn), lambda i, c: (0, c + off))

    def cspec(off):
        return pl.BlockSpec((FFN_CONV, tn), lambda i, c: (0, c + off))

    def bspec(off):
        return pl.BlockSpec((1, tn), lambda i, c: (0, c + off))

    b2 = conv_b.reshape(1, 2 * D_FF)
    return pl.pallas_call(
        kern,
        grid=(lp // tm, nc),
        in_specs=[pl.BlockSpec((tm, kdim), lambda i, c: (i, 0)),
                  pl.BlockSpec((HALO, kdim), lambda i, c: (jnp.maximum(i * (tm // HALO) - 1, 0), 0)),
                  wspec(0), wspec(nc), cspec(0), cspec(nc), bspec(0), bspec(nc)],
        out_specs=pl.BlockSpec((tm, tn), lambda i, c: (i, c)),
        out_shape=jax.ShapeDtypeStruct((lp, D_FF), bf16),
        scratch_shapes=[pltpu.VMEM((HALO + tm, tn), f32), pltpu.VMEM((HALO + tm, tn), f32)],
        compiler_params=_cparams(("parallel", "arbitrary")),
        name="ffn_up",
    )(u, u, w_up, w_up, conv_w, conv_w, b2, b2)


def _split_w_in(w):
    o = 0
    parts = {}
    for name, width in (("gq", GDN_QK_W), ("gk", GDN_QK_W), ("gv", GDN_V_W), ("gz", GDN_V_W),
                        ("gb", GDN_V_HEADS), ("ga", GDN_V_HEADS), ("aq", ATT_Q_W), ("ak", ATT_KV_W),
                        ("av", ATT_KV_W), ("iq", IDX_Q_W), ("ik", HEAD_DIM), ("iw", IDX_HEADS),
                        ("gate_gdn", D_MODEL), ("gate_att", D_MODEL)):
        parts[name] = w[:, o:o + width]
        o += width
    cat = lambda names: jnp.concatenate([parts[n] for n in names], axis=1)
    small = cat(("gb", "ga", "iw"))
    small = jnp.pad(small, ((0, 0), (0, LANES - small.shape[1])))
    groups = dict(qkv=cat(("gq", "gk", "gv")), aq=parts["aq"], iq=parts["iq"], kv=cat(("ak", "ik")),
                  plain=cat(("gz", "gate_gdn", "gate_att", "av")), small=small)
    return {k: v.astype(bf16) for k, v in groups.items()}


def _layer(h0, p, lp, topk):
    tm = _pick(lp, (640, 256, 128))
    tm_big = _pick(lp, (1280, 640, 256, 128))
    wp = _split_w_in(p["w_in"])

    def proj(w, **kw):
        return matmul(u1, w, tm=tm_big, tk=D_MODEL, **kw)

    u1 = rms_rows(h0, p["mix_pre_g"])
    tabs = rope_tables(lp)
    tab_specs = (((tm_big, HEAD_DIM), lambda i, j: (i, 0)),) * 3
    qkv = gdn_proj(u1, wp["qkv"], p["gdn_conv_w"])
    (aq,) = proj(wp["aq"], tn=1024, epilogue=_make_ep_rope(HEAD_DIM ** -0.5 * LOG2E, False),
                 extra=tabs, extra_specs=tab_specs,
                 out_shapes=[jax.ShapeDtypeStruct((lp, ATT_Q_W), bf16)], name="proj_aq")
    iq_heads = 1024 // HEAD_DIM
    (iq,) = proj(wp["iq"], tn=1024, epilogue=_make_ep_rope(1.0, True), extra=tabs, extra_specs=tab_specs,
                 out_shapes=[jax.ShapeDtypeStruct((IDX_HEADS, lp, HEAD_DIM), bf16)],
                 out_block_specs=[((iq_heads, tm_big, HEAD_DIM), lambda i, j: (j, i, 0))], name="proj_iq")
    ak, ik = proj(wp["kv"], tn=ATT_KV_W + HEAD_DIM, epilogue=_ep_rope_kv, extra=tabs, extra_specs=tab_specs,
                  out_shapes=[jax.ShapeDtypeStruct((lp, ATT_KV_W), bf16),
                              jax.ShapeDtypeStruct((lp, HEAD_DIM), bf16)],
                  out_block_specs=[((tm_big, ATT_KV_W), lambda i, j: (i, 0)),
                                   ((tm_big, HEAD_DIM), lambda i, j: (i, 0))], name="proj_kv")
    (plain,) = proj(wp["plain"], tn=768, epilogue=_ep_cast,
                    out_shapes=[jax.ShapeDtypeStruct((lp, PLAIN_W), bf16)], name="proj_plain")
    (small,) = proj(wp["small"], tn=LANES, epilogue=_ep_cast,
                    out_shapes=[jax.ShapeDtypeStruct((lp, LANES), f32)], name="proj_small")
    gb = small[:, 0:GDN_V_HEADS]
    ga = small[:, GDN_V_HEADS:2 * GDN_V_HEADS]
    iw = small[:, 2 * GDN_V_HEADS:2 * GDN_V_HEADS + IDX_HEADS]

    beta, gcum = gdn_gates(gb, ga, p["gdn_a_log"], p["gdn_dt_bias"])
    gcol = gcum.T[:, :, None]
    grow = gcum.T[:, None, :]
    bcol = beta.T[:, :, None]
    brow = beta.T[:, None, :]
    o_gdn = gdn_core(qkv, plain, p["gdn_norm_g"], gcol, grow, bcol, brow)

    avt = plain[:, COL_AV:COL_AV + ATT_KV_W].T.reshape(ATT_KV_HEADS, HEAD_DIM, lp)
    avt = jnp.concatenate([avt, jnp.ones((ATT_KV_HEADS, V_AUG - HEAD_DIM, lp), bf16)], axis=1)
    bias_t = dsa_select(iq, ik, iw.T, topk)
    o_att = dsa_attention(aq, ak, avt, bias_t)

    tn = 512
    gate_spec = lambda col0: ((tm_big, tn), (lambda i, j, c=col0 // tn: (i, c + j)))
    (m1,) = matmul(o_gdn, p["w_branch_gdn"].astype(bf16), tm=tm_big, tn=tn, tk=2048,
                   epilogue=_ep_gate, extra=(plain,), extra_specs=(gate_spec(COL_GATE_GDN),),
                   out_shapes=[jax.ShapeDtypeStruct((lp, D_MODEL), f32)], name="branch_gdn")
    (merged,) = matmul(o_att, p["w_branch_att"].astype(bf16), tm=tm_big, tn=tn, tk=2048,
                       epilogue=_ep_gate_add, extra=(plain, m1),
                       extra_specs=(gate_spec(COL_GATE_ATT), ((tm_big, tn), lambda i, j: (i, j))),
                       out_shapes=[jax.ShapeDtypeStruct((lp, D_MODEL), bf16)], name="branch_att")
    row_spec = ((tm, D_MODEL), lambda i, j: (i, 0))
    vec_spec = ((1, D_MODEL), lambda i, j: (0, 0))
    h1, u2 = matmul(merged, p["w_out"].astype(bf16), tm=tm, tn=D_MODEL, tk=1024,
                    epilogue=_ep_res_norm2,
                    extra=(h0, p["mix_post_g"].reshape(1, D_MODEL), p["ffn_pre_g"].reshape(1, D_MODEL)),
                    extra_specs=(row_spec, vec_spec, vec_spec),
                    out_shapes=[jax.ShapeDtypeStruct((lp, D_MODEL), f32),
                                jax.ShapeDtypeStruct((lp, D_MODEL), bf16)], name="w_out")

    act = ffn_up(u2, p["w_up"].astype(bf16), p["ffn_conv_w"], p["ffn_conv_b"])
    (h2,) = matmul(act, p["w_down"].astype(bf16), tm=tm, tn=D_MODEL, tk=1024, epilogue=_ep_res_norm,
                   extra=(h1, p["ffn_post_g"].reshape(1, D_MODEL)),
                   extra_specs=(row_spec, vec_spec),
                   out_shapes=[jax.ShapeDtypeStruct((lp, D_MODEL), f32)], name="w_down")
    return h2


def kernel(x, meta_tokens, mix_pre_g, w_in, gdn_conv_w, gdn_a_log, gdn_dt_bias, gdn_norm_g,
           w_branch_gdn, w_branch_att, w_out, mix_post_g, ffn_pre_g, w_up, ffn_conv_w,
           ffn_conv_b, w_down, ffn_post_g):
    batch, seq, d = x.shape
    assert batch == 1 and d == D_MODEL
    lp = FRAME_X0 + seq
    topk = min(TOPK_MAX, (N_META + seq) // 4)
    h = jnp.concatenate([jnp.zeros((FRAME_OFF, d), x.dtype), meta_tokens.astype(x.dtype), x[0]], axis=0)
    for i in range(w_in.shape[0]):
        p = dict(mix_pre_g=mix_pre_g[i], w_in=w_in[i], gdn_conv_w=gdn_conv_w[i], gdn_a_log=gdn_a_log[i],
                 gdn_dt_bias=gdn_dt_bias[i], gdn_norm_g=gdn_norm_g[i], w_branch_gdn=w_branch_gdn[i],
                 w_branch_att=w_branch_att[i], w_out=w_out[i], mix_post_g=mix_post_g[i],
                 ffn_pre_g=ffn_pre_g[i], w_up=w_up[i], ffn_conv_w=ffn_conv_w[i],
                 ffn_conv_b=ffn_conv_b[i], w_down=w_down[i], ffn_post_g=ffn_post_g[i])
        h = _layer(h, p, lp, topk)
    return h[FRAME_X0:][None]
```

```python
import functools
import struct

import jax
import jax.numpy as jnp
from jax import lax
from jax.experimental import pallas as pl
from jax.experimental.pallas import tpu as pltpu

f32 = jnp.float32
bf16 = jnp.bfloat16
i32 = jnp.int32
i16 = jnp.int16

D_MODEL = 2048
N_META = 16
EPS = 1e-6
GDN_QK_HEADS = 16
GDN_V_HEADS = 32
HEAD_DIM = 128
GDN_CONV = 4
ATT_HEADS = 16
ATT_KV_HEADS = 2
IDX_HEADS = 16
TOPK_MAX = 256
NEG = -1e30
LOG2E = 1.4426950408889634
ROPE_THETA = 500000.0
ROPE_DIMS = HEAD_DIM // 4
D_FF = 3 * D_MODEL
FFN_CONV = 3
GDN_QK_W = GDN_QK_HEADS * HEAD_DIM
GDN_V_W = GDN_V_HEADS * HEAD_DIM
ATT_Q_W = ATT_HEADS * HEAD_DIM
ATT_KV_W = ATT_KV_HEADS * HEAD_DIM
IDX_Q_W = IDX_HEADS * HEAD_DIM

FRAME_X0 = 256
FRAME_OFF = FRAME_X0 - N_META
CHUNK = 128

LANES = 128
VMEM_LIMIT = 56 * 1024 * 1024

COL_GZ = 0
COL_GATE_GDN = COL_GZ + GDN_V_W
COL_GATE_ATT = COL_GATE_GDN + D_MODEL
COL_AV = COL_GATE_ATT + D_MODEL
PLAIN_W = COL_AV + ATT_KV_W


def _cparams(sem):
    return pltpu.CompilerParams(dimension_semantics=sem, vmem_limit_bytes=VMEM_LIMIT)


def _pick(n, cands):
    for c in cands:
        if n % c == 0:
            return c
    raise ValueError(f"no tile for {n} in {cands}")


def _sigmoid(x):
    return 1.0 / (1.0 + jnp.exp(-x))


def _dot(a, b):
    return jnp.dot(a, b, preferred_element_type=f32)


def _dot_nt(a, b):
    return lax.dot_general(a, b, (((1,), (1,)), ((), ())), preferred_element_type=f32)


def _rms_rows_kernel(h_ref, g_ref, o_ref):
    h = h_ref[...]
    y = h * lax.rsqrt(jnp.mean(h * h, axis=-1, keepdims=True) + EPS)
    o_ref[...] = (y * g_ref[...]).astype(o_ref.dtype)


def rms_rows(h, g):
    lp, d = h.shape
    tm = _pick(lp, (640, 256, 128))
    return pl.pallas_call(
        _rms_rows_kernel,
        grid=(lp // tm,),
        in_specs=[pl.BlockSpec((tm, d), lambda i: (i, 0)),
                  pl.BlockSpec((1, d), lambda i: (0, 0))],
        out_specs=pl.BlockSpec((tm, d), lambda i: (i, 0)),
        out_shape=jax.ShapeDtypeStruct((lp, d), bf16),
        compiler_params=_cparams(("parallel",)),
        name="rms_rows",
    )(h, g.reshape(1, d))


def _mm_kernel(*refs, nk, n_extra, n_out, epilogue):
    a_ref, w_ref = refs[0], refs[1]
    extra = refs[2:2 + n_extra]
    outs = refs[2 + n_extra:2 + n_extra + n_out]
    if nk == 1:
        epilogue(_dot(a_ref[...], w_ref[...]), extra, outs)
        return
    acc_ref = refs[-1]
    k = pl.program_id(2)

    @pl.when(k == 0)
    def _():
        acc_ref[...] = jnp.zeros_like(acc_ref)

    acc_ref[...] += _dot(a_ref[...], w_ref[...])

    @pl.when(k == nk - 1)
    def _():
        epilogue(acc_ref[...], extra, outs)


def matmul(a, w, *, tm, tn, tk, epilogue, extra=(), extra_specs=(), out_shapes, out_block_specs=None,
           a_col0=0, name):
    m = a.shape[0]
    kdim, n = w.shape
    nk = kdim // tk
    assert m % tm == 0 and n % tn == 0 and kdim % tk == 0
    in_specs = [pl.BlockSpec((tm, tk), lambda i, j, k: (i, a_col0 + k)),
                pl.BlockSpec((tk, tn), lambda i, j, k: (k, j))]
    in_specs += [pl.BlockSpec(bs, (lambda i, j, k, f=f: f(i, j))) for bs, f in extra_specs]
    if out_block_specs is None:
        out_block_specs = [((tm, tn), lambda i, j: (i, j))] * len(out_shapes)
    out_specs = [pl.BlockSpec(bs, (lambda i, j, k, f=f: f(i, j))) for bs, f in out_block_specs]
    scratch = [] if nk == 1 else [pltpu.VMEM((tm, tn), f32)]
    kern = functools.partial(_mm_kernel, nk=nk, n_extra=len(extra), n_out=len(out_shapes),
                             epilogue=epilogue)
    res = pl.pallas_call(
        kern,
        grid=(m // tm, n // tn, nk),
        in_specs=in_specs,
        out_specs=out_specs,
        out_shape=out_shapes,
        scratch_shapes=scratch,
        compiler_params=_cparams(("parallel", "parallel", "arbitrary")),
        name=name,
    )(a, w, *extra)
    return res


def _ep_cast(acc, extra, outs):
    outs[0][...] = acc.astype(outs[0].dtype)


def _ep_gate(acc, extra, outs):
    g = extra[0][...].astype(f32)
    outs[0][...] = (_sigmoid(g) * acc).astype(outs[0].dtype)


def _ep_gate_add(acc, extra, outs):
    g = extra[0][...].astype(f32)
    outs[0][...] = (extra[1][...].astype(f32) + _sigmoid(g) * acc).astype(outs[0].dtype)


def _rms(t, g):
    return t * lax.rsqrt(jnp.mean(t * t, axis=-1, keepdims=True) + EPS) * g


def _ep_res_norm2(acc, extra, outs):
    h_ref, g_ref, g2_ref = extra
    h1 = h_ref[...] + _rms(acc, g_ref[...])
    outs[0][...] = h1
    outs[1][...] = _rms(h1, g2_ref[...]).astype(outs[1].dtype)


def _ep_res_norm(acc, extra, outs):
    h_ref, g_ref = extra
    outs[0][...] = h_ref[...] + _rms(acc, g_ref[...])


HALO = 16


def _gdn_proj_kernel(a_ref, ah_ref, w_ref, cw_ref, o_ref, ext_ref, *, tm, tn):
    i = pl.program_id(0)
    c = pl.program_id(1)
    halo = ah_ref[...]
    halo = jnp.where(i > 0, halo, jnp.zeros_like(halo))
    a = jnp.concatenate([halo, a_ref[...]], axis=0)
    ext_ref[...] = _dot(a, w_ref[...])
    w = cw_ref[...]
    y = ext_ref[HALO:HALO + tm, :] * w[GDN_CONV - 1:GDN_CONV, :]
    for j in range(GDN_CONV - 1):
        s0 = HALO - (GDN_CONV - 1) + j
        y = y + ext_ref[s0:s0 + tm, :] * w[j:j + 1, :]
    s = y * _sigmoid(y)
    is_q = c < (GDN_QK_W // tn)
    is_qk = c < (2 * GDN_QK_W // tn)
    qscale = jnp.where(is_q, HEAD_DIM ** -0.5, 1.0).astype(f32)
    for hh in range(tn // HEAD_DIM):
        seg = s[:, hh * HEAD_DIM:(hh + 1) * HEAD_DIM]
        r = lax.rsqrt(jnp.sum(seg * seg, axis=-1, keepdims=True) + EPS) * qscale
        fac = jnp.where(is_qk, r, 1.0)
        o_ref[:, hh * HEAD_DIM:(hh + 1) * HEAD_DIM] = (seg * fac).astype(o_ref.dtype)


def gdn_proj(u, w_qkv, conv_w):
    lp, kdim = u.shape
    width = w_qkv.shape[1]
    tm = _pick(lp, (1280, 640, 256, 128))
    tn = 1024
    kern = functools.partial(_gdn_proj_kernel, tm=tm, tn=tn)
    return pl.pallas_call(
        kern,
        grid=(lp // tm, width // tn),
        in_specs=[pl.BlockSpec((tm, kdim), lambda i, c: (i, 0)),
                  pl.BlockSpec((HALO, kdim), lambda i, c: (jnp.maximum(i * (tm // HALO) - 1, 0), 0)),
                  pl.BlockSpec((kdim, tn), lambda i, c: (0, c)),
                  pl.BlockSpec((GDN_CONV, tn), lambda i, c: (0, c))],
        out_specs=pl.BlockSpec((tm, tn), lambda i, c: (i, c)),
        out_shape=jax.ShapeDtypeStruct((lp, width), bf16),
        scratch_shapes=[pltpu.VMEM((HALO + tm, tn), f32)],
        compiler_params=_cparams(("parallel", "arbitrary")),
        name="gdn_proj",
    )(u, u, w_qkv, conv_w)


def _gdn_gates_kernel(b_ref, a_ref, alog_ref, dt_ref, beta_ref, gcum_ref):
    i = pl.program_id(0)
    rows = i * CHUNK + lax.broadcasted_iota(i32, (CHUNK, 1), 0)
    valid = rows >= FRAME_OFF
    beta_ref[...] = jnp.where(valid, _sigmoid(b_ref[...]), 0.0)
    a = a_ref[...] + dt_ref[...]
    sp = jnp.maximum(a, 0.0) + jnp.log1p(jnp.exp(-jnp.abs(a)))
    g = jnp.where(valid, -jnp.exp(alog_ref[...]) * sp, 0.0)
    tri = (lax.broadcasted_iota(i32, (CHUNK, CHUNK), 0)
           >= lax.broadcasted_iota(i32, (CHUNK, CHUNK), 1)).astype(f32)
    gcum_ref[...] = jnp.dot(tri, g, preferred_element_type=f32, precision=lax.Precision.HIGHEST)


def gdn_gates(gb, ga, a_log, dt_bias):
    lp, nh = gb.shape
    spec = pl.BlockSpec((CHUNK, nh), lambda i: (i, 0))
    vec = pl.BlockSpec((1, nh), lambda i: (0, 0))
    return pl.pallas_call(
        _gdn_gates_kernel,
        grid=(lp // CHUNK,),
        in_specs=[spec, spec, vec, vec],
        out_specs=[spec, spec],
        out_shape=[jax.ShapeDtypeStruct((lp, nh), f32)] * 2,
        compiler_params=_cparams(("parallel",)),
        name="gdn_gates",
    )(gb, ga, a_log.reshape(1, nh), dt_bias.reshape(1, nh))


def _block_mask(size):
    r = lax.broadcasted_iota(i32, (CHUNK, CHUNK), 0) // size
    c = lax.broadcasted_iota(i32, (CHUNK, CHUNK), 1) // size
    return r == c


def _unit_lower_inverse_many(mats, tick):
    row = lax.broadcasted_iota(i32, (CHUNK, CHUNK), 0)
    col = lax.broadcasted_iota(i32, (CHUNK, CHUNK), 1)
    eye = (row == col).astype(f32)
    base = 8
    m_prev = _block_mask(base)
    ads = [jnp.where(m_prev, a, 0.0) for a in mats]
    adbs = [ad.astype(bf16) for ad in ads]
    a2s = [_dot(x, x) for x in adbs]
    tick()
    a2bs = [x.astype(bf16) for x in a2s]
    a4s = [_dot(x, x) for x in a2bs]
    tick()
    xs = [_dot((eye - ad).astype(bf16), (eye + a2).astype(bf16)) for ad, a2 in zip(ads, a2s)]
    tick()
    xs = [_dot(x.astype(bf16), (eye + a4).astype(bf16)) for x, a4 in zip(xs, a4s)]
    tick()
    size = base * 2
    while size <= CHUNK:
        m_cur = _block_mask(size)
        off_diag = jnp.logical_and(m_cur, jnp.logical_not(m_prev))
        bs = [jnp.where(off_diag, a, 0.0).astype(bf16) for a in mats]
        xbs = [x.astype(bf16) for x in xs]
        ys = [_dot(xb, b).astype(bf16) for xb, b in zip(xbs, bs)]
        tick()
        xs = [x - _dot(y, xb) for x, y, xb in zip(xs, ys, xbs)]
        tick()
        m_prev = m_cur
        size *= 2
    return xs


def _gdn_kernel(q_ref, k_ref, v_ref, z_ref, ng_ref,
                gc0_ref, gc1_ref, gr0_ref, gr1_ref, bc0_ref, bc1_ref, br0_ref, br1_ref,
                o_ref, s_ref, u_ref, wq_ref, ak_ref, eg_ref, egl_ref, *, rb):
    r = pl.program_id(1)

    @pl.when(r == 0)
    def _():
        for ref in (s_ref, u_ref, wq_ref, ak_ref, eg_ref, egl_ref):
            ref[...] = jnp.zeros_like(ref)

    gcs, grs = (gc0_ref, gc1_ref), (gr0_ref, gr1_ref)
    bcs, brs = (bc0_ref, bc1_ref), (br0_ref, br1_ref)
    row = lax.broadcasted_iota(i32, (CHUNK, CHUNK), 0)
    col = lax.broadcasted_iota(i32, (CHUNK, CHUNK), 1)
    incl = row >= col
    strict = row > col
    ng = ng_ref[...]
    nchunk = rb // CHUNK
    heads = range(2)

    def rows(c):
        return slice(c * CHUNK, (c + 1) * CHUNK)

    def lanes(e):
        return slice(e * HEAD_DIM, (e + 1) * HEAD_DIM)

    chains = [(c, e) for c in range(nchunk) for e in heads]

    def recurrence():
        state = [s_ref[e] for e in heads]
        for c in range(nchunk):
            n = [chains.index((c, e)) for e in heads]
            sb = [state[e].astype(bf16) for e in heads]
            ws = [_dot(wq_ref[n[e]], sb[e]) for e in heads]
            yield
            vb = [(u_ref[n[e]] - ws[e][:CHUNK]).astype(bf16) for e in heads]
            av = [_dot(ak_ref[n[e]], vb[e]) for e in heads]
            yield
            for e in heads:
                o = eg_ref[n[e]] * ws[e][CHUNK:] + av[e][:CHUNK]
                state[e] = state[e] * egl_ref[n[e]] + av[e][CHUNK:]
                z = z_ref[rows(c), lanes(e)].astype(f32)
                o_ref[rows(c), lanes(e)] = (_rms(o, ng) * (z * _sigmoid(z))).astype(o_ref.dtype)
        for e in heads:
            s_ref[e] = state[e]

    rec = recurrence()

    def tick():
        next(rec, None)

    qs = [q_ref[rows(c), :] for c in range(nchunk)]
    ks = [k_ref[rows(c), :] for c in range(nchunk)]
    kts = [k.astype(f32).T for k in ks]
    kks = [_dot_nt(k, k) for k in ks]
    qks = [_dot_nt(q, k) for q, k in zip(qs, ks)]
    tick()
    gc = {ce: gcs[ce[1]][rows(ce[0]), :] for ce in chains}
    gr = {ce: grs[ce[1]][:, rows(ce[0])] for ce in chains}
    bc = {ce: bcs[ce[1]][rows(ce[0]), :] for ce in chains}
    br = {ce: brs[ce[1]][:, rows(ce[0])] for ce in chains}
    dec = {ce: jnp.where(incl, jnp.exp(jnp.where(incl, gc[ce] - gr[ce], 0.0)), 0.0) for ce in chains}
    amat = [jnp.where(strict, bc[ce] * kks[ce[0]] * dec[ce], 0.0) for ce in chains]
    tinv = dict(zip(chains, _unit_lower_inverse_many(amat, tick)))
    u = {ce: _dot((tinv[ce] * br[ce]).astype(bf16), v_ref[rows(ce[0]), lanes(ce[1])]) for ce in chains}
    tick()
    w = {ce: _dot((tinv[ce] * (br[ce] * jnp.exp(gr[ce]))).astype(bf16), ks[ce[0]]) for ce in chains}
    tick()
    g_last = {ce: gr[ce][:, CHUNK - 1:CHUNK] for ce in chains}
    wq = {ce: jnp.concatenate([w[ce].astype(bf16), qs[ce[0]]], axis=0) for ce in chains}
    ak = {ce: jnp.concatenate(
        [jnp.where(incl, qks[ce[0]] * dec[ce], 0.0).astype(bf16),
         (kts[ce[0]] * jnp.exp(g_last[ce] - gr[ce])).astype(bf16)], axis=0) for ce in chains}
    for _ in rec:
        pass
    for n, ce in enumerate(chains):
        u_ref[n] = u[ce]
        wq_ref[n] = wq[ce]
        ak_ref[n] = ak[ce]
        eg_ref[n] = jnp.broadcast_to(jnp.exp(gc[ce]), (CHUNK, HEAD_DIM))
        egl_ref[n] = jnp.broadcast_to(jnp.exp(g_last[ce]), (1, HEAD_DIM))


def gdn_core(qkv, plain, norm_g, gcol, grow, bcol, brow):
    lp = qkv.shape[0]
    rb = _pick(lp, (640, 256, 128))
    nqk = GDN_QK_HEADS
    kcol0 = GDN_QK_W // HEAD_DIM
    vcol0 = 2 * GDN_QK_W // (2 * HEAD_DIM)
    zcol0 = COL_GZ // (2 * HEAD_DIM)

    nr = lp // rb
    n_chain = 2 * (rb // CHUNK)

    def cur(r):
        return jnp.minimum(r, nr - 1)

    def prev(r):
        return jnp.maximum(r - 1, 0)

    def colspec(e):
        return pl.BlockSpec((None, rb, 1), lambda j, r: (2 * j + e, cur(r), 0))

    def rowspec(e):
        return pl.BlockSpec((None, 1, rb), lambda j, r: (2 * j + e, 0, cur(r)))

    kern = functools.partial(_gdn_kernel, rb=rb)
    return pl.pallas_call(
        kern,
        grid=(nqk, nr + 1),
        in_specs=[pl.BlockSpec((rb, HEAD_DIM), lambda j, r: (cur(r), j)),
                  pl.BlockSpec((rb, HEAD_DIM), lambda j, r: (cur(r), kcol0 + j)),
                  pl.BlockSpec((rb, 2 * HEAD_DIM), lambda j, r: (cur(r), vcol0 + j)),
                  pl.BlockSpec((rb, 2 * HEAD_DIM), lambda j, r: (prev(r), zcol0 + j)),
                  pl.BlockSpec((1, HEAD_DIM), lambda j, r: (0, 0)),
                  colspec(0), colspec(1), rowspec(0), rowspec(1),
                  colspec(0), colspec(1), rowspec(0), rowspec(1)],
        out_specs=pl.BlockSpec((rb, 2 * HEAD_DIM), lambda j, r: (prev(r), j)),
        out_shape=jax.ShapeDtypeStruct((lp, GDN_V_W), bf16),
        scratch_shapes=[pltpu.VMEM((2, HEAD_DIM, HEAD_DIM), f32),
                        pltpu.VMEM((n_chain, CHUNK, HEAD_DIM), f32),
                        pltpu.VMEM((n_chain, 2 * CHUNK, HEAD_DIM), bf16),
                        pltpu.VMEM((n_chain, 2 * CHUNK, CHUNK), bf16),
                        pltpu.VMEM((n_chain, CHUNK, HEAD_DIM), f32),
                        pltpu.VMEM((n_chain, 1, HEAD_DIM), f32)],
        compiler_params=_cparams(("parallel", "arbitrary")),
        name="gdn_core",
    )(qkv, qkv, qkv, plain, norm_g.reshape(1, HEAD_DIM),
      gcol, gcol, grow, grow, bcol, bcol, brow, brow)


def _rope_head(seg, cc, s1, s2):
    half = ROPE_DIMS // 2
    return seg * cc + pltpu.roll(seg, half, 1) * s1 + pltpu.roll(seg, HEAD_DIM - half, 1) * s2


def _make_ep_rope(scale, head_major):
    def ep(acc, extra, outs):
        cc, s1, s2 = (r[...] for r in extra)
        for h in range(acc.shape[1] // HEAD_DIM):
            y = _rope_head(acc[:, h * HEAD_DIM:(h + 1) * HEAD_DIM], cc, s1, s2)
            if scale != 1.0:
                y = y * scale
            if head_major:
                outs[0][h] = y.astype(outs[0].dtype)
            else:
                outs[0][:, h * HEAD_DIM:(h + 1) * HEAD_DIM] = y.astype(outs[0].dtype)
    return ep


def _ep_rope_kv(acc, extra, outs):
    cc, s1, s2 = (r[...] for r in extra)
    for h in range(ATT_KV_HEADS):
        y = _rope_head(acc[:, h * HEAD_DIM:(h + 1) * HEAD_DIM], cc, s1, s2)
        outs[0][:, h * HEAD_DIM:(h + 1) * HEAD_DIM] = y.astype(outs[0].dtype)
    y = _rope_head(acc[:, ATT_KV_W:ATT_KV_W + HEAD_DIM], cc, s1, s2)
    outs[1][...] = y.astype(outs[1].dtype)


def rope_tables(lp):
    half = ROPE_DIMS // 2
    pos = (jnp.arange(lp, dtype=jnp.int32) - FRAME_OFF).astype(f32)
    inv = ROPE_THETA ** (-jnp.arange(half, dtype=f32) / half)
    ang = pos[:, None] * inv[None, :]
    cos, sin = jnp.cos(ang), jnp.sin(ang)
    zeros = jnp.zeros((lp, HEAD_DIM - ROPE_DIMS), f32)
    z16 = jnp.zeros((lp, half), f32)
    cc = jnp.concatenate([cos, cos, jnp.ones_like(zeros)], axis=1)
    s1 = jnp.concatenate([z16, sin, zeros], axis=1)
    s2 = jnp.concatenate([-sin, z16, zeros], axis=1)
    return cc, s1, s2


INT_MAX = 2 ** 31 - 1
HALF_MIN, HALF_MAX, HALF_SPAN = -2 ** 15, 2 ** 15 - 1, 2 ** 16


def _sortable(x):
    b = lax.bitcast_convert_type(x, i32)
    return b ^ (lax.shift_right_arithmetic(b, 31) & INT_MAX)


_NEG_BITS = struct.unpack("<i", struct.pack("<f", NEG))[0]
_KEY_NEG = _NEG_BITS ^ ((_NEG_BITS >> 31) & INT_MAX)
_KEY_NEG_HI = _KEY_NEG >> 16
_KEY_NEG_LO = (_KEY_NEG & (HALF_SPAN - 1)) + HALF_MIN


def _select_kernel(iq_ref, ik_ref, iwt_ref, bias_ref, key_ref, half_ref, *, tq, ts, lp, topk):
    i = pl.program_id(0)
    nkb = ((i + 1) * tq + ts - 1) // ts
    rem = lp - nkb * ts
    iwt = iwt_ref[...] * ((IDX_HEADS ** -0.5) * (HEAD_DIM ** -0.5))
    qpos = i * tq + lax.broadcasted_iota(i32, (1, tq), 1)
    sub = lax.broadcasted_iota(i32, (ts, 1), 0)

    def score_blk(kb, carry):
        k0 = pl.multiple_of(kb * ts, ts)
        ikb = ik_ref[pl.ds(k0, ts), :]
        acc = jnp.zeros((ts, tq), f32)
        for h in range(IDX_HEADS):
            lg = _dot_nt(ikb, iq_ref[h])
            acc = acc + jnp.maximum(lg, 0.0) * iwt[h:h + 1, :]
        kpos = k0 + sub
        sc = jnp.where(kpos <= qpos, acc, NEG)
        sc = jnp.where(kpos < FRAME_OFF, -jnp.inf, sc)
        key = _sortable(sc)
        key_ref[pl.ds(k0, ts), :] = key
        half_ref[pl.ds(k0, ts), :] = lax.shift_right_arithmetic(key, 16).astype(i16)
        return carry

    lax.fori_loop(0, nkb, score_blk, 0)

    n_acc = 4

    def count(pred_fn):
        sub8 = lax.broadcasted_iota(i32, (8, 1), 0)

        def blk(kb, cnts):
            k0 = pl.multiple_of(kb * ts, ts)
            cnts = list(cnts)
            blk_ref = key_ref.at[pl.ds(k0, ts)]
            for j in range(ts // 8):
                kv = blk_ref[j * 8:(j + 1) * 8, :]
                cnts[j % n_acc] = cnts[j % n_acc] + jnp.where(pred_fn(kv, k0 + j * 8 + sub8), 1, 0)
            return tuple(cnts)
        cnts = lax.fori_loop(0, nkb, blk, tuple(jnp.zeros((8, tq), i32) for _ in range(n_acc)))
        return jnp.sum(sum(cnts[1:], cnts[0]), axis=0, keepdims=True)

    def count_half_ge(cand):
        one, zero = jnp.int16(1), jnp.int16(0)

        def blk(kb, cnts):
            k0 = pl.multiple_of(kb * ts, ts)
            cnts = list(cnts)
            blk_ref = half_ref.at[pl.ds(k0, ts)]
            for j in range(ts // 16):
                hv = blk_ref[j * 16:(j + 1) * 16, :]
                cnts[j % n_acc] = cnts[j % n_acc] + jnp.where(hv >= cand, one, zero)
            return tuple(cnts)
        cnts = lax.fori_loop(0, nkb, blk, tuple(jnp.zeros((16, tq), i16) for _ in range(n_acc)))
        return jnp.sum(sum(cnts[1:], cnts[0]).astype(i32), axis=0, keepdims=True)

    def search_half(neg_half, n_start):
        def body(it, c):
            tau, n_tau = c
            t2 = tau | lax.shift_left(jnp.int32(1), 15 - it)
            cand = t2 + HALF_MIN
            n2 = count_half_ge(cand.astype(i16)) + jnp.where(cand <= neg_half, rem, 0)
            ok = n2 >= topk
            return jnp.where(ok, t2, tau), jnp.where(ok, n2, n_tau)

        return lax.fori_loop(0, 16, body, (jnp.zeros((1, tq), i32), n_start))

    tau_hi, n_hi = search_half(jnp.full((1, tq), _KEY_NEG_HI, i32), jnp.full((1, tq), lp, i32))
    t_hi = tau_hi + HALF_MIN

    def low_blk(kb, carry):
        k0 = pl.multiple_of(kb * ts, ts)
        key = key_ref[pl.ds(k0, ts), :]
        hi = lax.shift_right_arithmetic(key, 16)
        lo = (key & (HALF_SPAN - 1)) + HALF_MIN
        lo = jnp.where(hi > t_hi, HALF_MAX, jnp.where(hi == t_hi, lo, HALF_MIN))
        half_ref[pl.ds(k0, ts), :] = lo.astype(i16)
        return carry

    lax.fori_loop(0, nkb, low_blk, 0)
    neg_lo = jnp.where(_KEY_NEG_HI > t_hi, HALF_MAX, jnp.where(_KEY_NEG_HI == t_hi, _KEY_NEG_LO, HALF_MIN))
    tau_lo, n_ge = search_half(neg_lo, n_hi)
    thr = lax.shift_left(t_hi, 16) | tau_lo
    has_tie = jnp.max(jnp.where(n_ge > topk, 1, 0)) > 0

    def tie_cut():
        n_gt = (count(lambda kv, kpos: kv > thr) + jnp.where(thr < _KEY_NEG, rem, 0))
        need = topk - n_gt

        def bis2(it, cut):
            bit = lax.shift_left(jnp.int32(1), 14 - it)
            c2 = cut | bit
            n = count(lambda kv, kpos: jnp.logical_and(kv == thr, kpos < c2))
            return jnp.where(n < need, c2, cut)
        return lax.fori_loop(0, 15, bis2, jnp.zeros((1, tq), i32))

    cut = lax.cond(has_tie, tie_cut, lambda: jnp.full((1, tq), INT_MAX, i32))

    def bias_blk(kb, carry):
        k0 = pl.multiple_of(kb * ts, ts)
        kv = key_ref[pl.ds(k0, ts), :]
        interior = jnp.logical_and(k0 >= FRAME_OFF, k0 + ts - 1 <= i * tq)

        @pl.when(jnp.logical_and(interior, jnp.logical_not(has_tie)))
        def _():
            bias_ref[pl.ds(k0, ts), :] = jnp.where(kv >= thr, 0.0, NEG).astype(bias_ref.dtype)

        @pl.when(jnp.logical_not(jnp.logical_and(interior, jnp.logical_not(has_tie))))
        def _():
            kpos = k0 + sub
            sel = jnp.logical_or(kv > thr, jnp.logical_and(kv == thr, kpos <= cut))
            vis = jnp.logical_and(kpos <= qpos, kpos >= FRAME_OFF)
            bias_ref[pl.ds(k0, ts), :] = jnp.where(jnp.logical_and(sel, vis), 0.0, NEG).astype(bias_ref.dtype)
        return carry

    lax.fori_loop(0, nkb, bias_blk, 0)

    def fill_blk(kb, carry):
        k0 = pl.multiple_of(kb * ts, ts)
        bias_ref[pl.ds(k0, ts), :] = jnp.full((ts, tq), NEG, bias_ref.dtype)
        return carry

    lax.fori_loop(nkb, lp // ts, fill_blk, 0)


def dsa_select(iq, ik, iwt, topk):
    nh, lp, _ = iq.shape
    assert lp < 2 ** 15
    tq = 256
    ts = _pick(lp, (640, 256, 128))
    kern = functools.partial(_select_kernel, tq=tq, ts=ts, lp=lp, topk=topk)
    return pl.pallas_call(
        kern,
        grid=(lp // tq,),
        in_specs=[pl.BlockSpec((nh, tq, HEAD_DIM), lambda i: (0, i, 0)),
                  pl.BlockSpec((lp, HEAD_DIM), lambda i: (0, 0), pipeline_mode=pl.Buffered(1)),
                  pl.BlockSpec((nh, tq), lambda i: (0, i))],
        out_specs=pl.BlockSpec((lp, tq), lambda i: (0, i)),
        out_shape=jax.ShapeDtypeStruct((lp, lp), bf16),
        scratch_shapes=[pltpu.VMEM((lp, tq), i32), pltpu.VMEM((lp, tq), i16)],
        compiler_params=_cparams(("parallel",)),
        name="dsa_select",
    )(iq, ik, iwt)


V_AUG = HEAD_DIM + 16


def _attn_kernel(q_ref, k_ref, vt_ref, b_ref, o_ref, m_ref, acc_ref, *, tq, ts, bpt):
    i = pl.program_id(0)
    nkb = ((i + 1) * tq + ts - 1) // ts
    group = ATT_HEADS // ATT_KV_HEADS
    kv_groups = range(ATT_KV_HEADS)
    qpos = i * tq + lax.broadcasted_iota(i32, (tq, 1), 0)
    eye = (lax.broadcasted_iota(i32, (tq, tq), 0)
           == lax.broadcasted_iota(i32, (tq, tq), 1)).astype(bf16)
    qa = [jnp.concatenate(
        [jnp.concatenate([q_ref[:, (g * group + r) * HEAD_DIM:(g * group + r + 1) * HEAD_DIM], eye],
                         axis=1) for r in range(group)], axis=0) for g in kv_groups]
    m_ref[...] = jnp.full(m_ref.shape, -jnp.inf, f32)
    acc_ref[...] = jnp.zeros_like(acc_ref)

    def scores(g, k0):
        ka = jnp.concatenate([k_ref[pl.ds(k0, ts), g * HEAD_DIM:(g + 1) * HEAD_DIM],
                              b_ref[pl.ds(k0, ts), :]], axis=1)
        return _dot_nt(ka, qa[g])

    def accumulate(g, k0, st):
        vt_ = vt_ref[g, :, pl.ds(k0, ts)]
        ps, alphas = [], []
        for r in range(group):
            sr = st[:, r * tq:(r + 1) * tq]
            m_old = m_ref[g, r:r + 1, :]
            m_new = jnp.maximum(m_old, jnp.max(sr, axis=0, keepdims=True))
            m_ref[g, r:r + 1, :] = m_new
            ps.append(jnp.exp2(sr - m_new).astype(bf16))
            alphas.append(jnp.exp2(m_old - m_new))
        pt = jnp.concatenate(ps, axis=1)
        acc_ref[g] = jnp.concatenate(alphas, axis=1) * acc_ref[g] + _dot(vt_, pt)

    def trip(kp, carry):
        k0s = [pl.multiple_of((kp * bpt + b) * ts, ts) for b in range(bpt)]
        work = [(g, kk) for g in kv_groups for kk in k0s]
        st = scores(*work[0])
        for n in range(len(work)):
            st_next = scores(*work[n + 1]) if n + 1 < len(work) else None
            accumulate(*work[n], st)
            st = st_next
        return carry

    lax.fori_loop(0, (nkb + bpt - 1) // bpt, trip, 0)
    for g in kv_groups:
        for r in range(group):
            ot = (acc_ref[g, 0:HEAD_DIM, r * tq:(r + 1) * tq]
                  / acc_ref[g, HEAD_DIM:HEAD_DIM + 1, r * tq:(r + 1) * tq])
            orr = jnp.where(qpos >= FRAME_OFF, ot.T, 0.0)
            o_ref[:, (g * group + r) * HEAD_DIM:(g * group + r + 1) * HEAD_DIM] = orr.astype(o_ref.dtype)


def dsa_attention(q, k, vt_aug, bias_t):
    lp = q.shape[0]
    tq = 128
    ts = _pick(lp, (1280, 640, 256, 128))
    bpt = 1
    group = ATT_HEADS // ATT_KV_HEADS
    assert (lp // ts) % bpt == 0
    kern = functools.partial(_attn_kernel, tq=tq, ts=ts, bpt=bpt)
    resident = pl.Buffered(1)
    return pl.pallas_call(
        kern,
        grid=(lp // tq,),
        in_specs=[pl.BlockSpec((tq, ATT_Q_W), lambda i: (i, 0)),
                  pl.BlockSpec((lp, ATT_KV_W), lambda i: (0, 0), pipeline_mode=resident),
                  pl.BlockSpec((ATT_KV_HEADS, V_AUG, lp), lambda i: (0, 0, 0), pipeline_mode=resident),
                  pl.BlockSpec((lp, tq), lambda i: (0, i))],
        out_specs=pl.BlockSpec((tq, ATT_Q_W), lambda i: (i, 0)),
        out_shape=jax.ShapeDtypeStruct((lp, ATT_Q_W), bf16),
        scratch_shapes=[pltpu.VMEM((ATT_KV_HEADS, group, tq), f32),
                        pltpu.VMEM((ATT_KV_HEADS, V_AUG, group * tq), f32)],
        compiler_params=_cparams(("parallel",)),
        name="dsa_attention",
    )(q, k, vt_aug, bias_t)


def _ffn_up_kernel(a_ref, ah_ref, wg_ref, wv_ref, cg_ref, cv_ref, bg_ref, bv_ref, o_ref,
                   eg_ref, ev_ref, *, tm):
    i = pl.program_id(0)
    halo = ah_ref[...]
    halo = jnp.where(i > 0, halo, jnp.zeros_like(halo))
    a = jnp.concatenate([halo, a_ref[...]], axis=0)

    eg_ref[...] = _dot(a, wg_ref[...])
    ev_ref[...] = _dot(a, wv_ref[...])

    def conv(ext_ref, w_ref, b_ref):
        w = w_ref[...]
        y = ext_ref[HALO:HALO + tm, :] * w[FFN_CONV - 1:FFN_CONV, :]
        for j in range(FFN_CONV - 1):
            s0 = HALO - (FFN_CONV - 1) + j
            y = y + ext_ref[s0:s0 + tm, :] * w[j:j + 1, :]
        return y + b_ref[...]

    gate = conv(eg_ref, cg_ref, bg_ref)
    val = conv(ev_ref, cv_ref, bv_ref)
    o_ref[...] = (gate * _sigmoid(gate) * val).astype(o_ref.dtype)


def ffn_up(u, w_up, conv_w, conv_b):
    lp, kdim = u.shape
    tm = _pick(lp, (1280, 640, 256, 128))
    tn = 512
    nc = D_FF // tn
    kern = functools.partial(_ffn_up_kernel, tm=tm)

    def wspec(off):
        return pl.BlockSpec((kdim, tn), lambda i, c: (0, c + off))

    def cspec(off):
        return pl.BlockSpec((FFN_CONV, tn), lambda i, c: (0, c + off))

    def bspec(off):
        return pl.BlockSpec((1, tn), lambda i, c: (0, c + off))

    b2 = conv_b.reshape(1, 2 * D_FF)
    return pl.pallas_call(
        kern,
        grid=(lp // tm, nc),
        in_specs=[pl.BlockSpec((tm, kdim), lambda i, c: (i, 0)),
                  pl.BlockSpec((HALO, kdim), lambda i, c: (jnp.maximum(i * (tm // HALO) - 1, 0), 0)),
                  wspec(0), wspec(nc), cspec(0), cspec(nc), bspec(0), bspec(nc)],
        out_specs=pl.BlockSpec((tm, tn), lambda i, c: (i, c)),
        out_shape=jax.ShapeDtypeStruct((lp, D_FF), bf16),
        scratch_shapes=[pltpu.VMEM((HALO + tm, tn), f32), pltpu.VMEM((HALO + tm, tn), f32)],
        compiler_params=_cparams(("parallel", "arbitrary")),
        name="ffn_up",
    )(u, u, w_up, w_up, conv_w, conv_w, b2, b2)


def _split_w_in(w):
    o = 0
    parts = {}
    for name, width in (("gq", GDN_QK_W), ("gk", GDN_QK_W), ("gv", GDN_V_W), ("gz", GDN_V_W),
                        ("gb", GDN_V_HEADS), ("ga", GDN_V_HEADS), ("aq", ATT_Q_W), ("ak", ATT_KV_W),
                        ("av", ATT_KV_W), ("iq", IDX_Q_W), ("ik", HEAD_DIM), ("iw", IDX_HEADS),
                        ("gate_gdn", D_MODEL), ("gate_att", D_MODEL)):
        parts[name] = w[:, o:o + width]
        o += width
    cat = lambda names: jnp.concatenate([parts[n] for n in names], axis=1)
    small = cat(("gb", "ga", "iw"))
    small = jnp.pad(small, ((0, 0), (0, LANES - small.shape[1])))
    groups = dict(qkv=cat(("gq", "gk", "gv")), aq=parts["aq"], iq=parts["iq"], kv=cat(("ak", "ik")),
                  plain=cat(("gz", "gate_gdn", "gate_att", "av")), small=small)
    return {k: v.astype(bf16) for k, v in groups.items()}


def _layer(h0, p, lp, topk):
    tm = _pick(lp, (640, 256, 128))
    tm_big = _pick(lp, (1280, 640, 256, 128))
    wp = _split_w_in(p["w_in"])

    def proj(w, **kw):
        return matmul(u1, w, tm=tm_big, tk=D_MODEL, **kw)

    u1 = rms_rows(h0, p["mix_pre_g"])
    tabs = rope_tables(lp)
    tab_specs = (((tm_big, HEAD_DIM), lambda i, j: (i, 0)),) * 3
    qkv = gdn_proj(u1, wp["qkv"], p["gdn_conv_w"])
    (aq,) = proj(wp["aq"], tn=1024, epilogue=_make_ep_rope(HEAD_DIM ** -0.5 * LOG2E, False),
                 extra=tabs, extra_specs=tab_specs,
                 out_shapes=[jax.ShapeDtypeStruct((lp, ATT_Q_W), bf16)], name="proj_aq")
    iq_heads = 1024 // HEAD_DIM
    (iq,) = proj(wp["iq"], tn=1024, epilogue=_make_ep_rope(1.0, True), extra=tabs, extra_specs=tab_specs,
                 out_shapes=[jax.ShapeDtypeStruct((IDX_HEADS, lp, HEAD_DIM), bf16)],
                 out_block_specs=[((iq_heads, tm_big, HEAD_DIM), lambda i, j: (j, i, 0))], name="proj_iq")
    ak, ik = proj(wp["kv"], tn=ATT_KV_W + HEAD_DIM, epilogue=_ep_rope_kv, extra=tabs, extra_specs=tab_specs,
                  out_shapes=[jax.ShapeDtypeStruct((lp, ATT_KV_W), bf16),
                              jax.ShapeDtypeStruct((lp, HEAD_DIM), bf16)],
                  out_block_specs=[((tm_big, ATT_KV_W), lambda i, j: (i, 0)),
                                   ((tm_big, HEAD_DIM), lambda i, j: (i, 0))], name="proj_kv")
    (plain,) = proj(wp["plain"], tn=768, epilogue=_ep_cast,
                    out_shapes=[jax.ShapeDtypeStruct((lp, PLAIN_W), bf16)], name="proj_plain")
    (small,) = proj(wp["small"], tn=LANES, epilogue=_ep_cast,
                    out_shapes=[jax.ShapeDtypeStruct((lp, LANES), f32)], name="proj_small")
    gb = small[:, 0:GDN_V_HEADS]
    ga = small[:, GDN_V_HEADS:2 * GDN_V_HEADS]
    iw = small[:, 2 * GDN_V_HEADS:2 * GDN_V_HEADS + IDX_HEADS]

    beta, gcum = gdn_gates(gb, ga, p["gdn_a_log"], p["gdn_dt_bias"])
    gcol = gcum.T[:, :, None]
    grow = gcum.T[:, None, :]
    bcol = beta.T[:, :, None]
    brow = beta.T[:, None, :]
    o_gdn = gdn_core(qkv, plain, p["gdn_norm_g"], gcol, grow, bcol, brow)

    avt = plain[:, COL_AV:COL_AV + ATT_KV_W].T.reshape(ATT_KV_HEADS, HEAD_DIM, lp)
    avt = jnp.concatenate([avt, jnp.ones((ATT_KV_HEADS, V_AUG - HEAD_DIM, lp), bf16)], axis=1)
    bias_t = dsa_select(iq, ik, iw.T, topk)
    o_att = dsa_attention(aq, ak, avt, bias_t)

    tn = 512
    gate_spec = lambda col0: ((tm_big, tn), (lambda i, j, c=col0 // tn: (i, c + j)))
    (m1,) = matmul(o_gdn, p["w_branch_gdn"].astype(bf16), tm=tm_big, tn=tn, tk=2048,
                   epilogue=_ep_gate, extra=(plain,), extra_specs=(gate_spec(COL_GATE_GDN),),
                   out_shapes=[jax.ShapeDtypeStruct((lp, D_MODEL), f32)], name="branch_gdn")
    (merged,) = matmul(o_att, p["w_branch_att"].astype(bf16), tm=tm_big, tn=tn, tk=2048,
                       epilogue=_ep_gate_add, extra=(plain, m1),
                       extra_specs=(gate_spec(COL_GATE_ATT), ((tm_big, tn), lambda i, j: (i, j))),
                       out_shapes=[jax.ShapeDtypeStruct((lp, D_MODEL), bf16)], name="branch_att")
    row_spec = ((tm, D_MODEL), lambda i, j: (i, 0))
    vec_spec = ((1, D_MODEL), lambda i, j: (0, 0))
    h1, u2 = matmul(merged, p["w_out"].astype(bf16), tm=tm, tn=D_MODEL, tk=1024,
                    epilogue=_ep_res_norm2,
                    extra=(h0, p["mix_post_g"].reshape(1, D_MODEL), p["ffn_pre_g"].reshape(1, D_MODEL)),
                    extra_specs=(row_spec, vec_spec, vec_spec),
                    out_shapes=[jax.ShapeDtypeStruct((lp, D_MODEL), f32),
                                jax.ShapeDtypeStruct((lp, D_MODEL), bf16)], name="w_out")

    act = ffn_up(u2, p["w_up"].astype(bf16), p["ffn_conv_w"], p["ffn_conv_b"])
    (h2,) = matmul(act, p["w_down"].astype(bf16), tm=tm, tn=D_MODEL, tk=1024, epilogue=_ep_res_norm,
                   extra=(h1, p["ffn_post_g"].reshape(1, D_MODEL)),
                   extra_specs=(row_spec, vec_spec),
                   out_shapes=[jax.ShapeDtypeStruct((lp, D_MODEL), f32)], name="w_down")
    return h2


def kernel(x, meta_tokens, mix_pre_g, w_in, gdn_conv_w, gdn_a_log, gdn_dt_bias, gdn_norm_g,
           w_branch_gdn, w_branch_att, w_out, mix_post_g, ffn_pre_g, w_up, ffn_conv_w,
           ffn_conv_b, w_down, ffn_post_g):
    batch, seq, d = x.shape
    assert batch == 1 and d == D_MODEL
    lp = FRAME_X0 + seq
    topk = min(TOPK_MAX, (N_META + seq) // 4)
    h = jnp.concatenate([jnp.zeros((FRAME_OFF, d), x.dtype), meta_tokens.astype(x.dtype), x[0]], axis=0)
    for i in range(w_in.shape[0]):
        p = dict(mix_pre_g=mix_pre_g[i], w_in=w_in[i], gdn_conv_w=gdn_conv_w[i], gdn_a_log=gdn_a_log[i],
                 gdn_dt_bias=gdn_dt_bias[i], gdn_norm_g=gdn_norm_g[i], w_branch_gdn=w_branch_gdn[i],
                 w_branch_att=w_branch_att[i], w_out=w_out[i], mix_post_g=mix_post_g[i],
                 ffn_pre_g=ffn_pre_g[i], w_up=w_up[i], ffn_conv_w=ffn_conv_w[i],
                 ffn_conv_b=ffn_conv_b[i], w_down=w_down[i], ffn_post_g=ffn_post_g[i])
        h = _layer(h, p, lp, topk)
    return h[FRAME_X0:][None]
```

```python
import functools
import struct

import jax
import jax.numpy as jnp
from jax import lax
from jax.experimental import pallas as pl
from jax.experimental.pallas import tpu as pltpu

f32 = jnp.float32
bf16 = jnp.bfloat16
i32 = jnp.int32
i16 = jnp.int16

D_MODEL = 2048
N_META = 16
EPS = 1e-6
GDN_QK_HEADS = 16
GDN_V_HEADS = 32
HEAD_DIM = 128
GDN_CONV = 4
ATT_HEADS = 16
ATT_KV_HEADS = 2
IDX_HEADS = 16
TOPK_MAX = 256
NEG = -1e30
LOG2E = 1.4426950408889634
ROPE_THETA = 500000.0
ROPE_DIMS = HEAD_DIM // 4
D_FF = 3 * D_MODEL
FFN_CONV = 3
GDN_QK_W = GDN_QK_HEADS * HEAD_DIM
GDN_V_W = GDN_V_HEADS * HEAD_DIM
ATT_Q_W = ATT_HEADS * HEAD_DIM
ATT_KV_W = ATT_KV_HEADS * HEAD_DIM
IDX_Q_W = IDX_HEADS * HEAD_DIM

FRAME_X0 = 256
FRAME_OFF = FRAME_X0 - N_META
CHUNK = 128

LANES = 128
VMEM_LIMIT = 56 * 1024 * 1024

COL_GZ = 0
COL_GATE_GDN = COL_GZ + GDN_V_W
COL_GATE_ATT = COL_GATE_GDN + D_MODEL
COL_AV = COL_GATE_ATT + D_MODEL
PLAIN_W = COL_AV + ATT_KV_W


def _cparams(sem):
    return pltpu.CompilerParams(dimension_semantics=sem, vmem_limit_bytes=VMEM_LIMIT)


def _pick(n, cands):
    for c in cands:
        if n % c == 0:
            return c
    raise ValueError(f"no tile for {n} in {cands}")


def _sigmoid(x):
    return 1.0 / (1.0 + jnp.exp(-x))


def _dot(a, b):
    return jnp.dot(a, b, preferred_element_type=f32)


def _dot_nt(a, b):
    return lax.dot_general(a, b, (((1,), (1,)), ((), ())), preferred_element_type=f32)


def _rms_rows_kernel(h_ref, g_ref, o_ref):
    h = h_ref[...]
    y = h * lax.rsqrt(jnp.mean(h * h, axis=-1, keepdims=True) + EPS)
    o_ref[...] = (y * g_ref[...]).astype(o_ref.dtype)


def rms_rows(h, g):
    lp, d = h.shape
    tm = _pick(lp, (640, 256, 128))
    return pl.pallas_call(
        _rms_rows_kernel,
        grid=(lp // tm,),
        in_specs=[pl.BlockSpec((tm, d), lambda i: (i, 0)),
                  pl.BlockSpec((1, d), lambda i: (0, 0))],
        out_specs=pl.BlockSpec((tm, d), lambda i: (i, 0)),
        out_shape=jax.ShapeDtypeStruct((lp, d), bf16),
        compiler_params=_cparams(("parallel",)),
        name="rms_rows",
    )(h, g.reshape(1, d))


def _mm_kernel(*refs, nk, n_extra, n_out, epilogue):
    a_ref, w_ref = refs[0], refs[1]
    extra = refs[2:2 + n_extra]
    outs = refs[2 + n_extra:2 + n_extra + n_out]
    if nk == 1:
        epilogue(_dot(a_ref[...], w_ref[...]), extra, outs)
        return
    acc_ref = refs[-1]
    k = pl.program_id(2)

    @pl.when(k == 0)
    def _():
        acc_ref[...] = jnp.zeros_like(acc_ref)

    acc_ref[...] += _dot(a_ref[...], w_ref[...])

    @pl.when(k == nk - 1)
    def _():
        epilogue(acc_ref[...], extra, outs)


def matmul(a, w, *, tm, tn, tk, epilogue, extra=(), extra_specs=(), out_shapes, out_block_specs=None,
           a_col0=0, name):
    m = a.shape[0]
    kdim, n = w.shape
    nk = kdim // tk
    assert m % tm == 0 and n % tn == 0 and kdim % tk == 0
    w_mode = dict(pipeline_mode=pl.Buffered(1)) if (nk == 1 and n == tn) else {}
    in_specs = [pl.BlockSpec((tm, tk), lambda i, j, k: (i, a_col0 + k)),
                pl.BlockSpec((tk, tn), lambda i, j, k: (k, j), **w_mode)]
    in_specs += [pl.BlockSpec(bs, (lambda i, j, k, f=f: f(i, j))) for bs, f in extra_specs]
    if out_block_specs is None:
        out_block_specs = [((tm, tn), lambda i, j: (i, j))] * len(out_shapes)
    out_specs = [pl.BlockSpec(bs, (lambda i, j, k, f=f: f(i, j))) for bs, f in out_block_specs]
    scratch = [] if nk == 1 else [pltpu.VMEM((tm, tn), f32)]
    kern = functools.partial(_mm_kernel, nk=nk, n_extra=len(extra), n_out=len(out_shapes),
                             epilogue=epilogue)
    res = pl.pallas_call(
        kern,
        grid=(m // tm, n // tn, nk),
        in_specs=in_specs,
        out_specs=out_specs,
        out_shape=out_shapes,
        scratch_shapes=scratch,
        compiler_params=_cparams(("parallel", "parallel", "arbitrary")),
        name=name,
    )(a, w, *extra)
    return res


def _ep_cast(acc, extra, outs):
    outs[0][...] = acc.astype(outs[0].dtype)


def _ep_gate(acc, extra, outs):
    g = extra[0][...].astype(f32)
    outs[0][...] = (_sigmoid(g) * acc).astype(outs[0].dtype)


def _ep_gate_add(acc, extra, outs):
    g = extra[0][...].astype(f32)
    outs[0][...] = (extra[1][...].astype(f32) + _sigmoid(g) * acc).astype(outs[0].dtype)


def _rms(t, g):
    return t * lax.rsqrt(jnp.mean(t * t, axis=-1, keepdims=True) + EPS) * g


def _ep_res_norm2(acc, extra, outs):
    h_ref, g_ref, g2_ref = extra
    h1 = h_ref[...] + _rms(acc, g_ref[...])
    outs[0][...] = h1
    outs[1][...] = _rms(h1, g2_ref[...]).astype(outs[1].dtype)


def _ep_res_norm(acc, extra, outs):
    h_ref, g_ref = extra
    outs[0][...] = h_ref[...] + _rms(acc, g_ref[...])


HALO = 16


def _pipelined_steps(c, nc, produce, consume):
    @pl.when(c == 0)
    def _():
        produce(0)

    for par in (0, 1):
        @pl.when(jnp.logical_and(jnp.logical_and(c > 0, c < nc), c % 2 == par))
        def _(par=par):
            produce(par)
            consume(1 - par)

    @pl.when(c == nc)
    def _():
        consume((nc - 1) % 2)


def _halo_rows(a_ref, ah_ref, i):
    halo = ah_ref[...]
    halo = jnp.where(i > 0, halo, jnp.zeros_like(halo))
    return jnp.concatenate([halo, a_ref[...]], axis=0)


def _gdn_proj_kernel(a_ref, ah_ref, w_ref, cw_ref, o_ref, ext0_ref, ext1_ref, *, tm, tn, nc):
    i = pl.program_id(0)
    c = pl.program_id(1)
    exts = (ext0_ref, ext1_ref)

    def produce(slot):
        exts[slot][...] = _dot(_halo_rows(a_ref, ah_ref, i), w_ref[...])

    def consume(slot):
        ext_ref = exts[slot]
        ct = c - 1
        w = cw_ref[...]
        y = ext_ref[HALO:HALO + tm, :] * w[GDN_CONV - 1:GDN_CONV, :]
        for j in range(GDN_CONV - 1):
            s0 = HALO - (GDN_CONV - 1) + j
            y = y + ext_ref[s0:s0 + tm, :] * w[j:j + 1, :]
        s = y * _sigmoid(y)
        is_q = ct < (GDN_QK_W // tn)
        is_qk = ct < (2 * GDN_QK_W // tn)
        qscale = jnp.where(is_q, HEAD_DIM ** -0.5, 1.0).astype(f32)
        for hh in range(tn // HEAD_DIM):
            seg = s[:, hh * HEAD_DIM:(hh + 1) * HEAD_DIM]
            r = lax.rsqrt(jnp.sum(seg * seg, axis=-1, keepdims=True) + EPS) * qscale
            fac = jnp.where(is_qk, r, 1.0)
            o_ref[:, hh * HEAD_DIM:(hh + 1) * HEAD_DIM] = (seg * fac).astype(o_ref.dtype)

    _pipelined_steps(c, nc, produce, consume)


def gdn_proj(u, w_qkv, conv_w):
    lp, kdim = u.shape
    width = w_qkv.shape[1]
    tm = _pick(lp, (1280, 640, 256, 128))
    tn = 1024
    nc = width // tn
    kern = functools.partial(_gdn_proj_kernel, tm=tm, tn=tn, nc=nc)
    made = lambda c: jnp.minimum(c, nc - 1)
    done = lambda c: jnp.maximum(c - 1, 0)
    return pl.pallas_call(
        kern,
        grid=(lp // tm, nc + 1),
        in_specs=[pl.BlockSpec((tm, kdim), lambda i, c: (i, 0)),
                  pl.BlockSpec((HALO, kdim), lambda i, c: (jnp.maximum(i * (tm // HALO) - 1, 0), 0)),
                  pl.BlockSpec((kdim, tn), lambda i, c: (0, made(c))),
                  pl.BlockSpec((GDN_CONV, tn), lambda i, c: (0, done(c)))],
        out_specs=pl.BlockSpec((tm, tn), lambda i, c: (i, done(c))),
        out_shape=jax.ShapeDtypeStruct((lp, width), bf16),
        scratch_shapes=[pltpu.VMEM((HALO + tm, tn), f32)] * 2,
        compiler_params=_cparams(("parallel", "arbitrary")),
        name="gdn_proj",
    )(u, u, w_qkv, conv_w)


def _gdn_gates_kernel(b_ref, a_ref, alog_ref, dt_ref, beta_ref, gcum_ref, *, tm):
    i = pl.program_id(0)
    rows = i * tm + lax.broadcasted_iota(i32, (tm, 1), 0)
    valid = rows >= FRAME_OFF
    beta_ref[...] = jnp.where(valid, _sigmoid(b_ref[...]), 0.0)
    a = a_ref[...] + dt_ref[...]
    sp = jnp.maximum(a, 0.0) + jnp.log1p(jnp.exp(-jnp.abs(a)))
    g = jnp.where(valid, -jnp.exp(alog_ref[...]) * sp, 0.0)
    tri = (lax.broadcasted_iota(i32, (CHUNK, CHUNK), 0)
           >= lax.broadcasted_iota(i32, (CHUNK, CHUNK), 1)).astype(f32)
    for c in range(tm // CHUNK):
        rs = slice(c * CHUNK, (c + 1) * CHUNK)
        gcum_ref[rs, :] = jnp.dot(tri, g[rs, :], preferred_element_type=f32,
                                  precision=lax.Precision.HIGHEST)


def gdn_gates(gb, ga, a_log, dt_bias):
    lp, nh = gb.shape
    tm = _pick(lp, (1280, 640, 256, 128))
    spec = pl.BlockSpec((tm, nh), lambda i: (i, 0))
    vec = pl.BlockSpec((1, nh), lambda i: (0, 0))
    return pl.pallas_call(
        functools.partial(_gdn_gates_kernel, tm=tm),
        grid=(lp // tm,),
        in_specs=[spec, spec, vec, vec],
        out_specs=[spec, spec],
        out_shape=[jax.ShapeDtypeStruct((lp, nh), f32)] * 2,
        compiler_params=_cparams(("parallel",)),
        name="gdn_gates",
    )(gb, ga, a_log.reshape(1, nh), dt_bias.reshape(1, nh))


def _block_mask(size):
    r = lax.broadcasted_iota(i32, (CHUNK, CHUNK), 0) // size
    c = lax.broadcasted_iota(i32, (CHUNK, CHUNK), 1) // size
    return r == c


def _unit_lower_inverse_many(mats, tick):
    row = lax.broadcasted_iota(i32, (CHUNK, CHUNK), 0)
    col = lax.broadcasted_iota(i32, (CHUNK, CHUNK), 1)
    eye = (row == col).astype(f32)
    base = 8
    m_prev = _block_mask(base)
    ads = [jnp.where(m_prev, a, 0.0) for a in mats]
    adbs = [ad.astype(bf16) for ad in ads]
    a2s = [_dot(x, x) for x in adbs]
    tick()
    a2bs = [x.astype(bf16) for x in a2s]
    a4s = [_dot(x, x) for x in a2bs]
    tick()
    xs = [_dot((eye - ad).astype(bf16), (eye + a2).astype(bf16)) for ad, a2 in zip(ads, a2s)]
    tick()
    xs = [_dot(x.astype(bf16), (eye + a4).astype(bf16)) for x, a4 in zip(xs, a4s)]
    tick()
    size = base * 2
    while size <= CHUNK:
        m_cur = _block_mask(size)
        off_diag = jnp.logical_and(m_cur, jnp.logical_not(m_prev))
        bs = [jnp.where(off_diag, a, 0.0).astype(bf16) for a in mats]
        xbs = [x.astype(bf16) for x in xs]
        ys = [_dot(xb, b).astype(bf16) for xb, b in zip(xbs, bs)]
        tick()
        xs = [x - _dot(y, xb) for x, y, xb in zip(xs, ys, xbs)]
        tick()
        m_prev = m_cur
        size *= 2
    return xs


def _gdn_kernel(q_ref, k_ref, v_ref, z_ref, ng_ref,
                gc0_ref, gc1_ref, gr0_ref, gr1_ref, bc0_ref, bc1_ref, br0_ref, br1_ref,
                o_ref, s_ref, u_ref, wq_ref, ak_ref, eg_ref, egl_ref, *, rb):
    r = pl.program_id(1)

    @pl.when(r == 0)
    def _():
        for ref in (s_ref, u_ref, wq_ref, ak_ref, eg_ref, egl_ref):
            ref[...] = jnp.zeros_like(ref)

    gcs, grs = (gc0_ref, gc1_ref), (gr0_ref, gr1_ref)
    bcs, brs = (bc0_ref, bc1_ref), (br0_ref, br1_ref)
    row = lax.broadcasted_iota(i32, (CHUNK, CHUNK), 0)
    col = lax.broadcasted_iota(i32, (CHUNK, CHUNK), 1)
    incl = row >= col
    strict = row > col
    ng = ng_ref[...]
    nchunk = rb // CHUNK
    heads = range(2)

    def rows(c):
        return slice(c * CHUNK, (c + 1) * CHUNK)

    def lanes(e):
        return slice(e * HEAD_DIM, (e + 1) * HEAD_DIM)

    chains = [(c, e) for c in range(nchunk) for e in heads]

    def recurrence():
        state = [s_ref[e] for e in heads]
        for c in range(nchunk):
            n = [chains.index((c, e)) for e in heads]
            sb = [state[e].astype(bf16) for e in heads]
            ws = [_dot(wq_ref[n[e]], sb[e]) for e in heads]
            yield
            vb = [(u_ref[n[e]] - ws[e][:CHUNK]).astype(bf16) for e in heads]
            av = [_dot(ak_ref[n[e]], vb[e]) for e in heads]
            yield
            for e in heads:
                o = eg_ref[n[e]] * ws[e][CHUNK:] + av[e][:CHUNK]
                state[e] = state[e] * egl_ref[n[e]] + av[e][CHUNK:]
                z = z_ref[rows(c), lanes(e)].astype(f32)
                o_ref[rows(c), lanes(e)] = (_rms(o, ng) * (z * _sigmoid(z))).astype(o_ref.dtype)
        for e in heads:
            s_ref[e] = state[e]

    rec = recurrence()

    def tick():
        next(rec, None)

    qs = [q_ref[rows(c), :] for c in range(nchunk)]
    ks = [k_ref[rows(c), :] for c in range(nchunk)]
    kts = [k.astype(f32).T for k in ks]
    kks = [_dot_nt(k, k) for k in ks]
    qks = [_dot_nt(q, k) for q, k in zip(qs, ks)]
    tick()
    gc = {ce: gcs[ce[1]][rows(ce[0]), :] for ce in chains}
    gr = {ce: grs[ce[1]][:, rows(ce[0])] for ce in chains}
    bc = {ce: bcs[ce[1]][rows(ce[0]), :] for ce in chains}
    br = {ce: brs[ce[1]][:, rows(ce[0])] for ce in chains}
    dec = {ce: jnp.where(incl, jnp.exp(jnp.where(incl, gc[ce] - gr[ce], 0.0)), 0.0) for ce in chains}
    amat = [jnp.where(strict, bc[ce] * kks[ce[0]] * dec[ce], 0.0) for ce in chains]
    tinv = dict(zip(chains, _unit_lower_inverse_many(amat, tick)))
    u = {ce: _dot((tinv[ce] * br[ce]).astype(bf16), v_ref[rows(ce[0]), lanes(ce[1])]) for ce in chains}
    tick()
    w = {ce: _dot((tinv[ce] * (br[ce] * jnp.exp(gr[ce]))).astype(bf16), ks[ce[0]]) for ce in chains}
    tick()
    g_last = {ce: gr[ce][:, CHUNK - 1:CHUNK] for ce in chains}
    wq = {ce: jnp.concatenate([w[ce].astype(bf16), qs[ce[0]]], axis=0) for ce in chains}
    ak = {ce: jnp.concatenate(
        [jnp.where(incl, qks[ce[0]] * dec[ce], 0.0).astype(bf16),
         (kts[ce[0]] * jnp.exp(g_last[ce] - gr[ce])).astype(bf16)], axis=0) for ce in chains}
    for _ in rec:
        pass
    for n, ce in enumerate(chains):
        u_ref[n] = u[ce]
        wq_ref[n] = wq[ce]
        ak_ref[n] = ak[ce]
        eg_ref[n] = jnp.broadcast_to(jnp.exp(gc[ce]), (CHUNK, HEAD_DIM))
        egl_ref[n] = jnp.broadcast_to(jnp.exp(g_last[ce]), (1, HEAD_DIM))


def gdn_core(qkv, plain, norm_g, gcol, grow, bcol, brow):
    lp = qkv.shape[0]
    rb = _pick(lp, (640, 256, 128))
    nqk = GDN_QK_HEADS
    kcol0 = GDN_QK_W // HEAD_DIM
    vcol0 = 2 * GDN_QK_W // (2 * HEAD_DIM)
    zcol0 = COL_GZ // (2 * HEAD_DIM)

    nr = lp // rb
    n_chain = 2 * (rb // CHUNK)

    def cur(r):
        return jnp.minimum(r, nr - 1)

    def prev(r):
        return jnp.maximum(r - 1, 0)

    def colspec(e):
        return pl.BlockSpec((None, rb, 1), lambda j, r: (2 * j + e, cur(r), 0))

    def rowspec(e):
        return pl.BlockSpec((None, 1, rb), lambda j, r: (2 * j + e, 0, cur(r)))

    kern = functools.partial(_gdn_kernel, rb=rb)
    return pl.pallas_call(
        kern,
        grid=(nqk, nr + 1),
        in_specs=[pl.BlockSpec((rb, HEAD_DIM), lambda j, r: (cur(r), j)),
                  pl.BlockSpec((rb, HEAD_DIM), lambda j, r: (cur(r), kcol0 + j)),
                  pl.BlockSpec((rb, 2 * HEAD_DIM), lambda j, r: (cur(r), vcol0 + j)),
                  pl.BlockSpec((rb, 2 * HEAD_DIM), lambda j, r: (prev(r), zcol0 + j)),
                  pl.BlockSpec((1, HEAD_DIM), lambda j, r: (0, 0)),
                  colspec(0), colspec(1), rowspec(0), rowspec(1),
                  colspec(0), colspec(1), rowspec(0), rowspec(1)],
        out_specs=pl.BlockSpec((rb, 2 * HEAD_DIM), lambda j, r: (prev(r), j)),
        out_shape=jax.ShapeDtypeStruct((lp, GDN_V_W), bf16),
        scratch_shapes=[pltpu.VMEM((2, HEAD_DIM, HEAD_DIM), f32),
                        pltpu.VMEM((n_chain, CHUNK, HEAD_DIM), f32),
                        pltpu.VMEM((n_chain, 2 * CHUNK, HEAD_DIM), bf16),
                        pltpu.VMEM((n_chain, 2 * CHUNK, CHUNK), bf16),
                        pltpu.VMEM((n_chain, CHUNK, HEAD_DIM), f32),
                        pltpu.VMEM((n_chain, 1, HEAD_DIM), f32)],
        compiler_params=_cparams(("parallel", "arbitrary")),
        name="gdn_core",
    )(qkv, qkv, qkv, plain, norm_g.reshape(1, HEAD_DIM),
      gcol, gcol, grow, grow, bcol, bcol, brow, brow)


def _rope_head(seg, cc, s1, s2):
    half = ROPE_DIMS // 2
    return seg * cc + pltpu.roll(seg, half, 1) * s1 + pltpu.roll(seg, HEAD_DIM - half, 1) * s2


def _make_ep_rope(scale, head_major):
    def ep(acc, extra, outs):
        cc, s1, s2 = (r[...] for r in extra)
        for h in range(acc.shape[1] // HEAD_DIM):
            y = _rope_head(acc[:, h * HEAD_DIM:(h + 1) * HEAD_DIM], cc, s1, s2)
            if scale != 1.0:
                y = y * scale
            if head_major:
                outs[0][h] = y.astype(outs[0].dtype)
            else:
                outs[0][:, h * HEAD_DIM:(h + 1) * HEAD_DIM] = y.astype(outs[0].dtype)
    return ep


def _ep_rope_kv(acc, extra, outs):
    cc, s1, s2 = (r[...] for r in extra)
    for h in range(ATT_KV_HEADS):
        y = _rope_head(acc[:, h * HEAD_DIM:(h + 1) * HEAD_DIM], cc, s1, s2)
        outs[0][:, h * HEAD_DIM:(h + 1) * HEAD_DIM] = y.astype(outs[0].dtype)
    y = _rope_head(acc[:, ATT_KV_W:ATT_KV_W + HEAD_DIM], cc, s1, s2)
    outs[1][...] = y.astype(outs[1].dtype)


def rope_tables(lp):
    half = ROPE_DIMS // 2
    pos = (jnp.arange(lp, dtype=jnp.int32) - FRAME_OFF).astype(f32)
    inv = ROPE_THETA ** (-jnp.arange(half, dtype=f32) / half)
    ang = pos[:, None] * inv[None, :]
    cos, sin = jnp.cos(ang), jnp.sin(ang)
    zeros = jnp.zeros((lp, HEAD_DIM - ROPE_DIMS), f32)
    z16 = jnp.zeros((lp, half), f32)
    cc = jnp.concatenate([cos, cos, jnp.ones_like(zeros)], axis=1)
    s1 = jnp.concatenate([z16, sin, zeros], axis=1)
    s2 = jnp.concatenate([-sin, z16, zeros], axis=1)
    return cc, s1, s2


INT_MAX = 2 ** 31 - 1
HALF_MIN, HALF_MAX, HALF_SPAN = -2 ** 15, 2 ** 15 - 1, 2 ** 16


def _sortable(x):
    b = lax.bitcast_convert_type(x, i32)
    return b ^ (lax.shift_right_arithmetic(b, 31) & INT_MAX)


_NEG_BITS = struct.unpack("<i", struct.pack("<f", NEG))[0]
_KEY_NEG = _NEG_BITS ^ ((_NEG_BITS >> 31) & INT_MAX)
_KEY_NEG_HI = _KEY_NEG >> 16
_KEY_NEG_LO = (_KEY_NEG & (HALF_SPAN - 1)) + HALF_MIN


def _select_kernel(iq_ref, ik_ref, iwt_ref, bias_ref, key_ref, half_ref, *, tq, ts, lp, topk):
    i = pl.program_id(0)
    nkb = ((i + 1) * tq + ts - 1) // ts
    rem = lp - nkb * ts
    iwt = iwt_ref[...] * ((IDX_HEADS ** -0.5) * (HEAD_DIM ** -0.5))
    qpos = i * tq + lax.broadcasted_iota(i32, (1, tq), 1)
    sub = lax.broadcasted_iota(i32, (ts, 1), 0)

    def score_blk(kb, carry):
        k0 = pl.multiple_of(kb * ts, ts)
        ikb = ik_ref[pl.ds(k0, ts), :]
        acc = jnp.zeros((ts, tq), f32)
        for h in range(IDX_HEADS):
            lg = _dot_nt(ikb, iq_ref[h])
            acc = acc + jnp.maximum(lg, 0.0) * iwt[h:h + 1, :]
        kpos = k0 + sub
        sc = jnp.where(kpos <= qpos, acc, NEG)
        sc = jnp.where(kpos < FRAME_OFF, -jnp.inf, sc)
        key = _sortable(sc)
        key_ref[pl.ds(k0, ts), :] = key
        half_ref[pl.ds(k0, ts), :] = lax.shift_right_arithmetic(key, 16).astype(i16)
        return carry

    lax.fori_loop(0, nkb, score_blk, 0)

    n_acc = 4

    def count(pred_fn):
        sub8 = lax.broadcasted_iota(i32, (8, 1), 0)

        def blk(kb, cnts):
            k0 = pl.multiple_of(kb * ts, ts)
            cnts = list(cnts)
            blk_ref = key_ref.at[pl.ds(k0, ts)]
            for j in range(ts // 8):
                kv = blk_ref[j * 8:(j + 1) * 8, :]
                cnts[j % n_acc] = cnts[j % n_acc] + jnp.where(pred_fn(kv, k0 + j * 8 + sub8), 1, 0)
            return tuple(cnts)
        cnts = lax.fori_loop(0, nkb, blk, tuple(jnp.zeros((8, tq), i32) for _ in range(n_acc)))
        return jnp.sum(sum(cnts[1:], cnts[0]), axis=0, keepdims=True)

    def count_half_ge(cand):
        one, zero = jnp.int16(1), jnp.int16(0)

        def blk(kb, cnts):
            k0 = pl.multiple_of(kb * ts, ts)
            cnts = list(cnts)
            blk_ref = half_ref.at[pl.ds(k0, ts)]
            for j in range(ts // 16):
                hv = blk_ref[j * 16:(j + 1) * 16, :]
                cnts[j % n_acc] = cnts[j % n_acc] + jnp.where(hv >= cand, one, zero)
            return tuple(cnts)
        cnts = lax.fori_loop(0, nkb, blk, tuple(jnp.zeros((16, tq), i16) for _ in range(n_acc)))
        return jnp.sum(sum(cnts[1:], cnts[0]).astype(i32), axis=0, keepdims=True)

    def search_half(neg_half, n_start):
        def body(it, c):
            tau, n_tau = c
            t2 = tau | lax.shift_left(jnp.int32(1), 15 - it)
            cand = t2 + HALF_MIN
            n2 = count_half_ge(cand.astype(i16)) + jnp.where(cand <= neg_half, rem, 0)
            ok = n2 >= topk
            return jnp.where(ok, t2, tau), jnp.where(ok, n2, n_tau)

        return lax.fori_loop(0, 16, body, (jnp.zeros((1, tq), i32), n_start))

    tau_hi, n_hi = search_half(jnp.full((1, tq), _KEY_NEG_HI, i32), jnp.full((1, tq), lp, i32))
    t_hi = tau_hi + HALF_MIN

    def low_blk(kb, carry):
        k0 = pl.multiple_of(kb * ts, ts)
        key = key_ref[pl.ds(k0, ts), :]
        hi = lax.shift_right_arithmetic(key, 16)
        lo = (key & (HALF_SPAN - 1)) + HALF_MIN
        lo = jnp.where(hi > t_hi, HALF_MAX, jnp.where(hi == t_hi, lo, HALF_MIN))
        half_ref[pl.ds(k0, ts), :] = lo.astype(i16)
        return carry

    lax.fori_loop(0, nkb, low_blk, 0)
    neg_lo = jnp.where(_KEY_NEG_HI > t_hi, HALF_MAX, jnp.where(_KEY_NEG_HI == t_hi, _KEY_NEG_LO, HALF_MIN))
    tau_lo, n_ge = search_half(neg_lo, n_hi)
    thr = lax.shift_left(t_hi, 16) | tau_lo
    has_tie = jnp.max(jnp.where(n_ge > topk, 1, 0)) > 0

    def tie_cut():
        n_gt = (count(lambda kv, kpos: kv > thr) + jnp.where(thr < _KEY_NEG, rem, 0))
        need = topk - n_gt

        def bis2(it, cut):
            bit = lax.shift_left(jnp.int32(1), 14 - it)
            c2 = cut | bit
            n = count(lambda kv, kpos: jnp.logical_and(kv == thr, kpos < c2))
            return jnp.where(n < need, c2, cut)
        return lax.fori_loop(0, 15, bis2, jnp.zeros((1, tq), i32))

    cut = lax.cond(has_tie, tie_cut, lambda: jnp.full((1, tq), INT_MAX, i32))

    def bias_blk(kb, carry):
        k0 = pl.multiple_of(kb * ts, ts)
        kv = key_ref[pl.ds(k0, ts), :]
        interior = jnp.logical_and(k0 >= FRAME_OFF, k0 + ts - 1 <= i * tq)

        @pl.when(jnp.logical_and(interior, jnp.logical_not(has_tie)))
        def _():
            bias_ref[pl.ds(k0, ts), :] = jnp.where(kv >= thr, 0.0, NEG).astype(bias_ref.dtype)

        @pl.when(jnp.logical_not(jnp.logical_and(interior, jnp.logical_not(has_tie))))
        def _():
            kpos = k0 + sub
            sel = jnp.logical_or(kv > thr, jnp.logical_and(kv == thr, kpos <= cut))
            vis = jnp.logical_and(kpos <= qpos, kpos >= FRAME_OFF)
            bias_ref[pl.ds(k0, ts), :] = jnp.where(jnp.logical_and(sel, vis), 0.0, NEG).astype(bias_ref.dtype)
        return carry

    lax.fori_loop(0, nkb, bias_blk, 0)

    def fill_blk(kb, carry):
        k0 = pl.multiple_of(kb * ts, ts)
        bias_ref[pl.ds(k0, ts), :] = jnp.full((ts, tq), NEG, bias_ref.dtype)
        return carry

    lax.fori_loop(nkb, lp // ts, fill_blk, 0)


def dsa_select(iq, ik, iwt, topk):
    nh, lp, _ = iq.shape
    assert lp < 2 ** 15
    tq = 256
    ts = _pick(lp, (640, 256, 128))
    kern = functools.partial(_select_kernel, tq=tq, ts=ts, lp=lp, topk=topk)
    return pl.pallas_call(
        kern,
        grid=(lp // tq,),
        in_specs=[pl.BlockSpec((nh, tq, HEAD_DIM), lambda i: (0, i, 0)),
                  pl.BlockSpec((lp, HEAD_DIM), lambda i: (0, 0), pipeline_mode=pl.Buffered(1)),
                  pl.BlockSpec((nh, tq), lambda i: (0, i))],
        out_specs=pl.BlockSpec((lp, tq), lambda i: (0, i)),
        out_shape=jax.ShapeDtypeStruct((lp, lp), bf16),
        scratch_shapes=[pltpu.VMEM((lp, tq), i32), pltpu.VMEM((lp, tq), i16)],
        compiler_params=_cparams(("parallel",)),
        name="dsa_select",
    )(iq, ik, iwt)


V_AUG = HEAD_DIM + 16


def _attn_kernel(q_ref, k_ref, vt_ref, b_ref, o_ref, m_ref, acc_ref, *, tq, ts, bpt):
    i = pl.program_id(0)
    nkb = ((i + 1) * tq + ts - 1) // ts
    group = ATT_HEADS // ATT_KV_HEADS
    kv_groups = range(ATT_KV_HEADS)
    qpos = i * tq + lax.broadcasted_iota(i32, (tq, 1), 0)
    eye = (lax.broadcasted_iota(i32, (tq, tq), 0)
           == lax.broadcasted_iota(i32, (tq, tq), 1)).astype(bf16)
    qa = [jnp.concatenate(
        [jnp.concatenate([q_ref[:, (g * group + r) * HEAD_DIM:(g * group + r + 1) * HEAD_DIM], eye],
                         axis=1) for r in range(group)], axis=0) for g in kv_groups]
    m_ref[...] = jnp.full(m_ref.shape, -jnp.inf, f32)
    acc_ref[...] = jnp.zeros_like(acc_ref)

    def scores(g, k0):
        ka = jnp.concatenate([k_ref[pl.ds(k0, ts), g * HEAD_DIM:(g + 1) * HEAD_DIM],
                              b_ref[pl.ds(k0, ts), :]], axis=1)
        return _dot_nt(ka, qa[g])

    def accumulate(g, k0, st):
        vt_ = vt_ref[g, :, pl.ds(k0, ts)]
        ps, alphas = [], []
        for r in range(group):
            sr = st[:, r * tq:(r + 1) * tq]
            m_old = m_ref[g, r:r + 1, :]
            m_new = jnp.maximum(m_old, jnp.max(sr, axis=0, keepdims=True))
            m_ref[g, r:r + 1, :] = m_new
            ps.append(jnp.exp2(sr - m_new).astype(bf16))
            alphas.append(jnp.exp2(m_old - m_new))
        pt = jnp.concatenate(ps, axis=1)
        acc_ref[g] = jnp.concatenate(alphas, axis=1) * acc_ref[g] + _dot(vt_, pt)

    def trip(kp, carry):
        k0s = [pl.multiple_of((kp * bpt + b) * ts, ts) for b in range(bpt)]
        work = [(g, kk) for g in kv_groups for kk in k0s]
        st = scores(*work[0])
        for n in range(len(work)):
            st_next = scores(*work[n + 1]) if n + 1 < len(work) else None
            accumulate(*work[n], st)
            st = st_next
        return carry

    lax.fori_loop(0, (nkb + bpt - 1) // bpt, trip, 0)
    for g in kv_groups:
        for r in range(group):
            ot = (acc_ref[g, 0:HEAD_DIM, r * tq:(r + 1) * tq]
                  / acc_ref[g, HEAD_DIM:HEAD_DIM + 1, r * tq:(r + 1) * tq])
            orr = jnp.where(qpos >= FRAME_OFF, ot.T, 0.0)
            o_ref[:, (g * group + r) * HEAD_DIM:(g * group + r + 1) * HEAD_DIM] = orr.astype(o_ref.dtype)


def dsa_attention(q, k, vt_aug, bias_t):
    lp = q.shape[0]
    tq = 128
    ts = _pick(lp, (1280, 640, 256, 128))
    bpt = 1
    group = ATT_HEADS // ATT_KV_HEADS
    assert (lp // ts) % bpt == 0
    kern = functools.partial(_attn_kernel, tq=tq, ts=ts, bpt=bpt)
    resident = pl.Buffered(1)
    return pl.pallas_call(
        kern,
        grid=(lp // tq,),
        in_specs=[pl.BlockSpec((tq, ATT_Q_W), lambda i: (i, 0)),
                  pl.BlockSpec((lp, ATT_KV_W), lambda i: (0, 0), pipeline_mode=resident),
                  pl.BlockSpec((ATT_KV_HEADS, V_AUG, lp), lambda i: (0, 0, 0), pipeline_mode=resident),
                  pl.BlockSpec((lp, tq), lambda i: (0, i))],
        out_specs=pl.BlockSpec((tq, ATT_Q_W), lambda i: (i, 0)),
        out_shape=jax.ShapeDtypeStruct((lp, ATT_Q_W), bf16),
        scratch_shapes=[pltpu.VMEM((ATT_KV_HEADS, group, tq), f32),
                        pltpu.VMEM((ATT_KV_HEADS, V_AUG, group * tq), f32)],
        compiler_params=_cparams(("parallel",)),
        name="dsa_attention",
    )(q, k, vt_aug, bias_t)


def _ffn_up_kernel(a_ref, ah_ref, wg_ref, wv_ref, cg_ref, cv_ref, bg_ref, bv_ref, o_ref,
                   eg_ref, ev_ref, *, tm):
    a = _halo_rows(a_ref, ah_ref, pl.program_id(0))
    eg_ref[...] = _dot(a, wg_ref[...])
    ev_ref[...] = _dot(a, wv_ref[...])

    def conv(ext_ref, w_ref, b_ref):
        w = w_ref[...]
        y = ext_ref[HALO:HALO + tm, :] * w[FFN_CONV - 1:FFN_CONV, :]
        for j in range(FFN_CONV - 1):
            s0 = HALO - (FFN_CONV - 1) + j
            y = y + ext_ref[s0:s0 + tm, :] * w[j:j + 1, :]
        return y + b_ref[...]

    gate = conv(eg_ref, cg_ref, bg_ref)
    val = conv(ev_ref, cv_ref, bv_ref)
    o_ref[...] = (gate * _sigmoid(gate) * val).astype(o_ref.dtype)


def ffn_up(u, w_up, conv_w, conv_b):
    lp, kdim = u.shape
    tm = _pick(lp, (1280, 640, 256, 128))
    tn = 512
    nc = D_FF // tn
    kern = functools.partial(_ffn_up_kernel, tm=tm)

    def wspec(off):
        return pl.BlockSpec((kdim, tn), lambda i, c: (0, c + off))

    def cspec(off):
        return pl.BlockSpec((FFN_CONV, tn), lambda i, c: (0, c + off))

    def bspec(off):
        return pl.BlockSpec((1, tn), lambda i, c: (0, c + off))

    b2 = conv_b.reshape(1, 2 * D_FF)
    return pl.pallas_call(
        kern,
        grid=(lp // tm, nc),
        in_specs=[pl.BlockSpec((tm, kdim), lambda i, c: (i, 0)),
                  pl.BlockSpec((HALO, kdim), lambda i, c: (jnp.maximum(i * (tm // HALO) - 1, 0), 0)),
                  wspec(0), wspec(nc), cspec(0), cspec(nc), bspec(0), bspec(nc)],
        out_specs=pl.BlockSpec((tm, tn), lambda i, c: (i, c)),
        out_shape=jax.ShapeDtypeStruct((lp, D_FF), bf16),
        scratch_shapes=[pltpu.VMEM((HALO + tm, tn), f32)] * 2,
        compiler_params=_cparams(("parallel", "arbitrary")),
        name="ffn_up",
    )(u, u, w_up, w_up, conv_w, conv_w, b2, b2)


def _split_w_in(w):
    o = 0
    parts = {}
    for name, width in (("gq", GDN_QK_W), ("gk", GDN_QK_W), ("gv", GDN_V_W), ("gz", GDN_V_W),
                        ("gb", GDN_V_HEADS), ("ga", GDN_V_HEADS), ("aq", ATT_Q_W), ("ak", ATT_KV_W),
                        ("av", ATT_KV_W), ("iq", IDX_Q_W), ("ik", HEAD_DIM), ("iw", IDX_HEADS),
                        ("gate_gdn", D_MODEL), ("gate_att", D_MODEL)):
        parts[name] = w[:, o:o + width]
        o += width
    cat = lambda names: jnp.concatenate([parts[n] for n in names], axis=1)
    small = cat(("gb", "ga", "iw"))
    small = jnp.pad(small, ((0, 0), (0, LANES - small.shape[1])))
    groups = dict(qkv=cat(("gq", "gk", "gv")), aq=parts["aq"], iq=parts["iq"], kv=cat(("ak", "ik")),
                  plain=cat(("gz", "gate_gdn", "gate_att", "av")), small=small)
    return {k: v.astype(bf16) for k, v in groups.items()}


def _layer(h0, p, lp, topk):
    tm = _pick(lp, (640, 256, 128))
    tm_big = _pick(lp, (1280, 640, 256, 128))
    wp = _split_w_in(p["w_in"])

    def proj(w, **kw):
        return matmul(u1, w, tm=tm_big, tk=D_MODEL, **kw)

    u1 = rms_rows(h0, p["mix_pre_g"])
    tabs = rope_tables(lp)
    tab_specs = (((tm_big, HEAD_DIM), lambda i, j: (i, 0)),) * 3
    qkv = gdn_proj(u1, wp["qkv"], p["gdn_conv_w"])
    (aq,) = proj(wp["aq"], tn=1024, epilogue=_make_ep_rope(HEAD_DIM ** -0.5 * LOG2E, False),
                 extra=tabs, extra_specs=tab_specs,
                 out_shapes=[jax.ShapeDtypeStruct((lp, ATT_Q_W), bf16)], name="proj_aq")
    iq_heads = 1024 // HEAD_DIM
    (iq,) = proj(wp["iq"], tn=1024, epilogue=_make_ep_rope(1.0, True), extra=tabs, extra_specs=tab_specs,
                 out_shapes=[jax.ShapeDtypeStruct((IDX_HEADS, lp, HEAD_DIM), bf16)],
                 out_block_specs=[((iq_heads, tm_big, HEAD_DIM), lambda i, j: (j, i, 0))], name="proj_iq")
    ak, ik = proj(wp["kv"], tn=ATT_KV_W + HEAD_DIM, epilogue=_ep_rope_kv, extra=tabs, extra_specs=tab_specs,
                  out_shapes=[jax.ShapeDtypeStruct((lp, ATT_KV_W), bf16),
                              jax.ShapeDtypeStruct((lp, HEAD_DIM), bf16)],
                  out_block_specs=[((tm_big, ATT_KV_W), lambda i, j: (i, 0)),
                                   ((tm_big, HEAD_DIM), lambda i, j: (i, 0))], name="proj_kv")
    (plain,) = proj(wp["plain"], tn=768, epilogue=_ep_cast,
                    out_shapes=[jax.ShapeDtypeStruct((lp, PLAIN_W), bf16)], name="proj_plain")
    (small,) = proj(wp["small"], tn=LANES, epilogue=_ep_cast,
                    out_shapes=[jax.ShapeDtypeStruct((lp, LANES), f32)], name="proj_small")
    gb = small[:, 0:GDN_V_HEADS]
    ga = small[:, GDN_V_HEADS:2 * GDN_V_HEADS]
    iw = small[:, 2 * GDN_V_HEADS:2 * GDN_V_HEADS + IDX_HEADS]

    beta, gcum = gdn_gates(gb, ga, p["gdn_a_log"], p["gdn_dt_bias"])
    gcol = gcum.T[:, :, None]
    grow = gcum.T[:, None, :]
    bcol = beta.T[:, :, None]
    brow = beta.T[:, None, :]
    o_gdn = gdn_core(qkv, plain, p["gdn_norm_g"], gcol, grow, bcol, brow)

    avt = plain[:, COL_AV:COL_AV + ATT_KV_W].T.reshape(ATT_KV_HEADS, HEAD_DIM, lp)
    avt = jnp.concatenate([avt, jnp.ones((ATT_KV_HEADS, V_AUG - HEAD_DIM, lp), bf16)], axis=1)
    bias_t = dsa_select(iq, ik, iw.T, topk)
    o_att = dsa_attention(aq, ak, avt, bias_t)

    tn = 1024
    gate_spec = lambda col0: ((tm_big, tn), (lambda i, j, c=col0 // tn: (i, c + j)))
    (m1,) = matmul(o_gdn, p["w_branch_gdn"].astype(bf16), tm=tm_big, tn=tn, tk=2048,
                   epilogue=_ep_gate, extra=(plain,), extra_specs=(gate_spec(COL_GATE_GDN),),
                   out_shapes=[jax.ShapeDtypeStruct((lp, D_MODEL), f32)], name="branch_gdn")
    (merged,) = matmul(o_att, p["w_branch_att"].astype(bf16), tm=tm_big, tn=tn, tk=2048,
                       epilogue=_ep_gate_add, extra=(plain, m1),
                       extra_specs=(gate_spec(COL_GATE_ATT), ((tm_big, tn), lambda i, j: (i, j))),
                       out_shapes=[jax.ShapeDtypeStruct((lp, D_MODEL), bf16)], name="branch_att")
    row_spec = ((tm, D_MODEL), lambda i, j: (i, 0))
    vec_spec = ((1, D_MODEL), lambda i, j: (0, 0))
    h1, u2 = matmul(merged, p["w_out"].astype(bf16), tm=tm, tn=D_MODEL, tk=D_MODEL,
                    epilogue=_ep_res_norm2,
                    extra=(h0, p["mix_post_g"].reshape(1, D_MODEL), p["ffn_pre_g"].reshape(1, D_MODEL)),
                    extra_specs=(row_spec, vec_spec, vec_spec),
                    out_shapes=[jax.ShapeDtypeStruct((lp, D_MODEL), f32),
                                jax.ShapeDtypeStruct((lp, D_MODEL), bf16)], name="w_out")

    act = ffn_up(u2, p["w_up"].astype(bf16), p["ffn_conv_w"], p["ffn_conv_b"])
    (h2,) = matmul(act, p["w_down"].astype(bf16), tm=tm, tn=D_MODEL, tk=2048, epilogue=_ep_res_norm,
                   extra=(h1, p["ffn_post_g"].reshape(1, D_MODEL)),
                   extra_specs=(row_spec, vec_spec),
                   out_shapes=[jax.ShapeDtypeStruct((lp, D_MODEL), f32)], name="w_down")
    return h2


def kernel(x, meta_tokens, mix_pre_g, w_in, gdn_conv_w, gdn_a_log, gdn_dt_bias, gdn_norm_g,
           w_branch_gdn, w_branch_att, w_out, mix_post_g, ffn_pre_g, w_up, ffn_conv_w,
           ffn_conv_b, w_down, ffn_post_g):
    batch, seq, d = x.shape
    assert batch == 1 and d == D_MODEL
    lp = FRAME_X0 + seq
    topk = min(TOPK_MAX, (N_META + seq) // 4)
    h = jnp.concatenate([jnp.zeros((FRAME_OFF, d), x.dtype), meta_tokens.astype(x.dtype), x[0]], axis=0)
    for i in range(w_in.shape[0]):
        p = dict(mix_pre_g=mix_pre_g[i], w_in=w_in[i], gdn_conv_w=gdn_conv_w[i], gdn_a_log=gdn_a_log[i],
                 gdn_dt_bias=gdn_dt_bias[i], gdn_norm_g=gdn_norm_g[i], w_branch_gdn=w_branch_gdn[i],
                 w_branch_att=w_branch_att[i], w_out=w_out[i], mix_post_g=mix_post_g[i],
                 ffn_pre_g=ffn_pre_g[i], w_up=w_up[i], ffn_conv_w=ffn_conv_w[i],
                 ffn_conv_b=ffn_conv_b[i], w_down=w_down[i], ffn_post_g=ffn_post_g[i])
        h = _layer(h, p, lp, topk)
    return h[FRAME_X0:][None]
```

```python
import functools
import struct

import jax
import jax.numpy as jnp
from jax import lax
from jax.experimental import pallas as pl
from jax.experimental.pallas import tpu as pltpu

f32 = jnp.float32
bf16 = jnp.bfloat16
i32 = jnp.int32
i16 = jnp.int16

D_MODEL = 2048
N_META = 16
EPS = 1e-6
GDN_QK_HEADS = 16
GDN_V_HEADS = 32
HEAD_DIM = 128
GDN_CONV = 4
ATT_HEADS = 16
ATT_KV_HEADS = 2
IDX_HEADS = 16
TOPK_MAX = 256
NEG = -1e30
LOG2E = 1.4426950408889634
ROPE_THETA = 500000.0
ROPE_DIMS = HEAD_DIM // 4
D_FF = 3 * D_MODEL
FFN_CONV = 3
GDN_QK_W = GDN_QK_HEADS * HEAD_DIM
GDN_V_W = GDN_V_HEADS * HEAD_DIM
ATT_Q_W = ATT_HEADS * HEAD_DIM
ATT_KV_W = ATT_KV_HEADS * HEAD_DIM
IDX_Q_W = IDX_HEADS * HEAD_DIM

FRAME_X0 = 256
FRAME_OFF = FRAME_X0 - N_META
CHUNK = 128

LANES = 128
VMEM_LIMIT = 56 * 1024 * 1024

COL_GZ = 0
COL_GATE_GDN = COL_GZ + GDN_V_W
COL_GATE_ATT = COL_GATE_GDN + D_MODEL
COL_AV = COL_GATE_ATT + D_MODEL
PLAIN_W = COL_AV + ATT_KV_W


def _cparams(sem):
    return pltpu.CompilerParams(dimension_semantics=sem, vmem_limit_bytes=VMEM_LIMIT)


def _pick(n, cands):
    for c in cands:
        if n % c == 0:
            return c
    raise ValueError(f"no tile for {n} in {cands}")


def _sigmoid(x):
    return 1.0 / (1.0 + jnp.exp(-x))


def _dot(a, b):
    return jnp.dot(a, b, preferred_element_type=f32)


def _dot_nt(a, b):
    return lax.dot_general(a, b, (((1,), (1,)), ((), ())), preferred_element_type=f32)


def _rms_rows_kernel(h_ref, g_ref, o_ref):
    h = h_ref[...]
    y = h * lax.rsqrt(jnp.mean(h * h, axis=-1, keepdims=True) + EPS)
    o_ref[...] = (y * g_ref[...]).astype(o_ref.dtype)


def rms_rows(h, g):
    lp, d = h.shape
    tm = _pick(lp, (640, 256, 128))
    return pl.pallas_call(
        _rms_rows_kernel,
        grid=(lp // tm,),
        in_specs=[pl.BlockSpec((tm, d), lambda i: (i, 0)),
                  pl.BlockSpec((1, d), lambda i: (0, 0))],
        out_specs=pl.BlockSpec((tm, d), lambda i: (i, 0)),
        out_shape=jax.ShapeDtypeStruct((lp, d), bf16),
        compiler_params=_cparams(("parallel",)),
        name="rms_rows",
    )(h, g.reshape(1, d))


def _mm_kernel(*refs, nk, n_extra, n_out, epilogue):
    a_ref, w_ref = refs[0], refs[1]
    extra = refs[2:2 + n_extra]
    outs = refs[2 + n_extra:2 + n_extra + n_out]
    if nk == 1:
        epilogue(_dot(a_ref[...], w_ref[...]), extra, outs)
        return
    acc_ref = refs[-1]
    k = pl.program_id(2)

    @pl.when(k == 0)
    def _():
        acc_ref[...] = jnp.zeros_like(acc_ref)

    acc_ref[...] += _dot(a_ref[...], w_ref[...])

    @pl.when(k == nk - 1)
    def _():
        epilogue(acc_ref[...], extra, outs)


def matmul(a, w, *, tm, tn, tk, epilogue, extra=(), extra_specs=(), out_shapes, out_block_specs=None,
           a_col0=0, name):
    m = a.shape[0]
    kdim, n = w.shape
    nk = kdim // tk
    assert m % tm == 0 and n % tn == 0 and kdim % tk == 0
    w_mode = dict(pipeline_mode=pl.Buffered(1)) if (nk == 1 and n == tn) else {}
    in_specs = [pl.BlockSpec((tm, tk), lambda i, j, k: (i, a_col0 + k)),
                pl.BlockSpec((tk, tn), lambda i, j, k: (k, j), **w_mode)]
    in_specs += [pl.BlockSpec(bs, (lambda i, j, k, f=f: f(i, j))) for bs, f in extra_specs]
    if out_block_specs is None:
        out_block_specs = [((tm, tn), lambda i, j: (i, j))] * len(out_shapes)
    out_specs = [pl.BlockSpec(bs, (lambda i, j, k, f=f: f(i, j))) for bs, f in out_block_specs]
    scratch = [] if nk == 1 else [pltpu.VMEM((tm, tn), f32)]
    kern = functools.partial(_mm_kernel, nk=nk, n_extra=len(extra), n_out=len(out_shapes),
                             epilogue=epilogue)
    res = pl.pallas_call(
        kern,
        grid=(m // tm, n // tn, nk),
        in_specs=in_specs,
        out_specs=out_specs,
        out_shape=out_shapes,
        scratch_shapes=scratch,
        compiler_params=_cparams(("parallel", "parallel", "arbitrary")),
        name=name,
    )(a, w, *extra)
    return res


def _ep_cast(acc, extra, outs):
    outs[0][...] = acc.astype(outs[0].dtype)


def _ep_gate(acc, extra, outs):
    g = extra[0][...].astype(f32)
    outs[0][...] = (_sigmoid(g) * acc).astype(outs[0].dtype)


def _ep_gate_add(acc, extra, outs):
    g = extra[0][...].astype(f32)
    outs[0][...] = (extra[1][...].astype(f32) + _sigmoid(g) * acc).astype(outs[0].dtype)


def _rms(t, g):
    return t * lax.rsqrt(jnp.mean(t * t, axis=-1, keepdims=True) + EPS) * g


def _ep_res_norm2(acc, extra, outs):
    h_ref, g_ref, g2_ref = extra
    h1 = h_ref[...] + _rms(acc, g_ref[...])
    outs[0][...] = h1
    outs[1][...] = _rms(h1, g2_ref[...]).astype(outs[1].dtype)


def _ep_res_norm(acc, extra, outs):
    h_ref, g_ref = extra
    outs[0][...] = h_ref[...] + _rms(acc, g_ref[...])


HALO = 16


def _halo_rows(a_ref, ah_ref, i):
    halo = ah_ref[...]
    halo = jnp.where(i > 0, halo, jnp.zeros_like(halo))
    return jnp.concatenate([halo, a_ref[...]], axis=0)


def _gdn_proj_kernel(a_ref, ah_ref, w_ref, cw_ref, o_ref, ext_ref, *, tm, tn):
    c = pl.program_id(1)
    ext_ref[...] = _dot(_halo_rows(a_ref, ah_ref, pl.program_id(0)), w_ref[...])
    w = cw_ref[...]
    y = ext_ref[HALO:HALO + tm, :] * w[GDN_CONV - 1:GDN_CONV, :]
    for j in range(GDN_CONV - 1):
        s0 = HALO - (GDN_CONV - 1) + j
        y = y + ext_ref[s0:s0 + tm, :] * w[j:j + 1, :]
    s = y * _sigmoid(y)
    is_q = c < (GDN_QK_W // tn)
    is_qk = c < (2 * GDN_QK_W // tn)
    qscale = jnp.where(is_q, HEAD_DIM ** -0.5, 1.0).astype(f32)
    for hh in range(tn // HEAD_DIM):
        seg = s[:, hh * HEAD_DIM:(hh + 1) * HEAD_DIM]
        r = lax.rsqrt(jnp.sum(seg * seg, axis=-1, keepdims=True) + EPS) * qscale
        fac = jnp.where(is_qk, r, 1.0)
        o_ref[:, hh * HEAD_DIM:(hh + 1) * HEAD_DIM] = (seg * fac).astype(o_ref.dtype)


def gdn_proj(u, w_qkv, conv_w):
    lp, kdim = u.shape
    width = w_qkv.shape[1]
    tm = _pick(lp, (1280, 640, 256, 128))
    tn = 1024
    kern = functools.partial(_gdn_proj_kernel, tm=tm, tn=tn)
    return pl.pallas_call(
        kern,
        grid=(lp // tm, width // tn),
        in_specs=[pl.BlockSpec((tm, kdim), lambda i, c: (i, 0)),
                  pl.BlockSpec((HALO, kdim), lambda i, c: (jnp.maximum(i * (tm // HALO) - 1, 0), 0)),
                  pl.BlockSpec((kdim, tn), lambda i, c: (0, c)),
                  pl.BlockSpec((GDN_CONV, tn), lambda i, c: (0, c))],
        out_specs=pl.BlockSpec((tm, tn), lambda i, c: (i, c)),
        out_shape=jax.ShapeDtypeStruct((lp, width), bf16),
        scratch_shapes=[pltpu.VMEM((HALO + tm, tn), f32)],
        compiler_params=_cparams(("parallel", "arbitrary")),
        name="gdn_proj",
    )(u, u, w_qkv, conv_w)


def _gdn_gates_kernel(b_ref, a_ref, alog_ref, dt_ref, beta_ref, gcum_ref, *, tm):
    i = pl.program_id(0)
    rows = i * tm + lax.broadcasted_iota(i32, (tm, 1), 0)
    valid = rows >= FRAME_OFF
    beta_ref[...] = jnp.where(valid, _sigmoid(b_ref[...]), 0.0)
    a = a_ref[...] + dt_ref[...]
    sp = jnp.maximum(a, 0.0) + jnp.log1p(jnp.exp(-jnp.abs(a)))
    g = jnp.where(valid, -jnp.exp(alog_ref[...]) * sp, 0.0)
    tri = (lax.broadcasted_iota(i32, (CHUNK, CHUNK), 0)
           >= lax.broadcasted_iota(i32, (CHUNK, CHUNK), 1)).astype(f32)
    for c in range(tm // CHUNK):
        rs = slice(c * CHUNK, (c + 1) * CHUNK)
        gcum_ref[rs, :] = jnp.dot(tri, g[rs, :], preferred_element_type=f32,
                                  precision=lax.Precision.HIGHEST)


def gdn_gates(gb, ga, a_log, dt_bias):
    lp, nh = gb.shape
    tm = _pick(lp, (1280, 640, 256, 128))
    spec = pl.BlockSpec((tm, nh), lambda i: (i, 0))
    vec = pl.BlockSpec((1, nh), lambda i: (0, 0))
    return pl.pallas_call(
        functools.partial(_gdn_gates_kernel, tm=tm),
        grid=(lp // tm,),
        in_specs=[spec, spec, vec, vec],
        out_specs=[spec, spec],
        out_shape=[jax.ShapeDtypeStruct((lp, nh), f32)] * 2,
        compiler_params=_cparams(("parallel",)),
        name="gdn_gates",
    )(gb, ga, a_log.reshape(1, nh), dt_bias.reshape(1, nh))


def _block_mask(size):
    r = lax.broadcasted_iota(i32, (CHUNK, CHUNK), 0) // size
    c = lax.broadcasted_iota(i32, (CHUNK, CHUNK), 1) // size
    return r == c


def _unit_lower_inverse_many(mats, tick):
    row = lax.broadcasted_iota(i32, (CHUNK, CHUNK), 0)
    col = lax.broadcasted_iota(i32, (CHUNK, CHUNK), 1)
    eye = (row == col).astype(f32)
    base = 8
    m_prev = _block_mask(base)
    ads = [jnp.where(m_prev, a, 0.0) for a in mats]
    adbs = [ad.astype(bf16) for ad in ads]
    a2s = [_dot(x, x) for x in adbs]
    tick()
    a2bs = [x.astype(bf16) for x in a2s]
    a4s = [_dot(x, x) for x in a2bs]
    tick()
    xs = [_dot((eye - ad).astype(bf16), (eye + a2).astype(bf16)) for ad, a2 in zip(ads, a2s)]
    tick()
    xs = [_dot(x.astype(bf16), (eye + a4).astype(bf16)) for x, a4 in zip(xs, a4s)]
    tick()
    size = base * 2
    while size <= CHUNK:
        m_cur = _block_mask(size)
        off_diag = jnp.logical_and(m_cur, jnp.logical_not(m_prev))
        bs = [jnp.where(off_diag, a, 0.0).astype(bf16) for a in mats]
        xbs = [x.astype(bf16) for x in xs]
        ys = [_dot(xb, b).astype(bf16) for xb, b in zip(xbs, bs)]
        tick()
        xs = [x - _dot(y, xb) for x, y, xb in zip(xs, ys, xbs)]
        tick()
        m_prev = m_cur
        size *= 2
    return xs


def _gdn_kernel(q_ref, k_ref, v_ref, z_ref, ng_ref, gr0_ref, gr1_ref, br0_ref, br1_ref,
                o_ref, s_ref, u_ref, wq_ref, ak_ref, eg_ref, egl_ref, *, rb):
    r = pl.program_id(1)

    @pl.when(r == 0)
    def _():
        for ref in (s_ref, u_ref, wq_ref, ak_ref, eg_ref, egl_ref):
            ref[...] = jnp.zeros_like(ref)

    grs, brs = (gr0_ref, gr1_ref), (br0_ref, br1_ref)
    row = lax.broadcasted_iota(i32, (CHUNK, CHUNK), 0)
    col = lax.broadcasted_iota(i32, (CHUNK, CHUNK), 1)
    incl = row >= col
    strict = row > col
    ng = ng_ref[...]
    nchunk = rb // CHUNK
    heads = range(2)

    def rows(c):
        return slice(c * CHUNK, (c + 1) * CHUNK)

    def lanes(e):
        return slice(e * HEAD_DIM, (e + 1) * HEAD_DIM)

    chains = [(c, e) for c in range(nchunk) for e in heads]

    def recurrence():
        state = [s_ref[e] for e in heads]
        for c in range(nchunk):
            n = [chains.index((c, e)) for e in heads]
            sb = [state[e].astype(bf16) for e in heads]
            ws = [_dot(wq_ref[n[e]], sb[e]) for e in heads]
            yield
            vb = [(u_ref[n[e]] - ws[e][:CHUNK]).astype(bf16) for e in heads]
            av = [_dot(ak_ref[n[e]], vb[e]) for e in heads]
            yield
            for e in heads:
                o = eg_ref[n[e]] * ws[e][CHUNK:] + av[e][:CHUNK]
                state[e] = state[e] * egl_ref[n[e]] + av[e][CHUNK:]
                z = z_ref[rows(c), lanes(e)].astype(f32)
                o_ref[rows(c), lanes(e)] = (_rms(o, ng) * (z * _sigmoid(z))).astype(o_ref.dtype)
        for e in heads:
            s_ref[e] = state[e]

    rec = recurrence()

    def tick():
        next(rec, None)

    qs = [q_ref[rows(c), :] for c in range(nchunk)]
    ks = [k_ref[rows(c), :] for c in range(nchunk)]
    kts = [k.astype(f32).T for k in ks]
    kks = [_dot_nt(k, k) for k in ks]
    qks = [_dot_nt(q, k) for q, k in zip(qs, ks)]
    tick()
    def per_row(v):
        return jnp.broadcast_to(v, (CHUNK, CHUNK)).T

    gr = {ce: grs[ce[1]][:, rows(ce[0])] for ce in chains}
    br = {ce: brs[ce[1]][:, rows(ce[0])] for ce in chains}
    gc = {ce: per_row(gr[ce]) for ce in chains}
    bc = {ce: per_row(br[ce]) for ce in chains}
    dec = {ce: jnp.where(incl, jnp.exp(jnp.where(incl, gc[ce] - gr[ce], 0.0)), 0.0) for ce in chains}
    amat = [jnp.where(strict, bc[ce] * kks[ce[0]] * dec[ce], 0.0) for ce in chains]
    tinv = dict(zip(chains, _unit_lower_inverse_many(amat, tick)))
    u = {ce: _dot((tinv[ce] * br[ce]).astype(bf16), v_ref[rows(ce[0]), lanes(ce[1])]) for ce in chains}
    tick()
    w = {ce: _dot((tinv[ce] * (br[ce] * jnp.exp(gr[ce]))).astype(bf16), ks[ce[0]]) for ce in chains}
    tick()
    g_last = {ce: gr[ce][:, CHUNK - 1:CHUNK] for ce in chains}
    wq = {ce: jnp.concatenate([w[ce].astype(bf16), qs[ce[0]]], axis=0) for ce in chains}
    ak = {ce: jnp.concatenate(
        [jnp.where(incl, qks[ce[0]] * dec[ce], 0.0).astype(bf16),
         (kts[ce[0]] * jnp.exp(g_last[ce] - gr[ce])).astype(bf16)], axis=0) for ce in chains}
    for _ in rec:
        pass
    for n, ce in enumerate(chains):
        u_ref[n] = u[ce]
        wq_ref[n] = wq[ce]
        ak_ref[n] = ak[ce]
        eg_ref[n] = jnp.exp(gc[ce])
        egl_ref[n] = jnp.broadcast_to(jnp.exp(g_last[ce]), (1, HEAD_DIM))


def gdn_core(qkv, plain, norm_g, grow, brow):
    assert CHUNK == HEAD_DIM
    lp = qkv.shape[0]
    rb = _pick(lp, (640, 256, 128))
    nqk = GDN_QK_HEADS
    kcol0 = GDN_QK_W // HEAD_DIM
    vcol0 = 2 * GDN_QK_W // (2 * HEAD_DIM)
    zcol0 = COL_GZ // (2 * HEAD_DIM)

    nr = lp // rb
    n_chain = 2 * (rb // CHUNK)

    def cur(r):
        return jnp.minimum(r, nr - 1)

    def prev(r):
        return jnp.maximum(r - 1, 0)

    def rowspec(e):
        return pl.BlockSpec((None, 1, rb), lambda j, r: (2 * j + e, 0, cur(r)))

    kern = functools.partial(_gdn_kernel, rb=rb)
    return pl.pallas_call(
        kern,
        grid=(nqk, nr + 1),
        in_specs=[pl.BlockSpec((rb, HEAD_DIM), lambda j, r: (cur(r), j)),
                  pl.BlockSpec((rb, HEAD_DIM), lambda j, r: (cur(r), kcol0 + j)),
                  pl.BlockSpec((rb, 2 * HEAD_DIM), lambda j, r: (cur(r), vcol0 + j)),
                  pl.BlockSpec((rb, 2 * HEAD_DIM), lambda j, r: (prev(r), zcol0 + j)),
                  pl.BlockSpec((1, HEAD_DIM), lambda j, r: (0, 0)),
                  rowspec(0), rowspec(1), rowspec(0), rowspec(1)],
        out_specs=pl.BlockSpec((rb, 2 * HEAD_DIM), lambda j, r: (prev(r), j)),
        out_shape=jax.ShapeDtypeStruct((lp, GDN_V_W), bf16),
        scratch_shapes=[pltpu.VMEM((2, HEAD_DIM, HEAD_DIM), f32),
                        pltpu.VMEM((n_chain, CHUNK, HEAD_DIM), f32),
                        pltpu.VMEM((n_chain, 2 * CHUNK, HEAD_DIM), bf16),
                        pltpu.VMEM((n_chain, 2 * CHUNK, CHUNK), bf16),
                        pltpu.VMEM((n_chain, CHUNK, HEAD_DIM), f32),
                        pltpu.VMEM((n_chain, 1, HEAD_DIM), f32)],
        compiler_params=_cparams(("parallel", "arbitrary")),
        name="gdn_core",
    )(qkv, qkv, qkv, plain, norm_g.reshape(1, HEAD_DIM), grow, grow, brow, brow)


def _rope_head(seg, cc, s1, s2):
    half = ROPE_DIMS // 2
    return seg * cc + pltpu.roll(seg, half, 1) * s1 + pltpu.roll(seg, HEAD_DIM - half, 1) * s2


def _make_ep_rope(scale, head_major):
    def ep(acc, extra, outs):
        cc, s1, s2 = (r[...] for r in extra)
        for h in range(acc.shape[1] // HEAD_DIM):
            y = _rope_head(acc[:, h * HEAD_DIM:(h + 1) * HEAD_DIM], cc, s1, s2)
            if scale != 1.0:
                y = y * scale
            if head_major:
                outs[0][h] = y.astype(outs[0].dtype)
            else:
                outs[0][:, h * HEAD_DIM:(h + 1) * HEAD_DIM] = y.astype(outs[0].dtype)
    return ep


def _ep_rope_kv(acc, extra, outs):
    cc, s1, s2 = (r[...] for r in extra)
    for h in range(ATT_KV_HEADS):
        y = _rope_head(acc[:, h * HEAD_DIM:(h + 1) * HEAD_DIM], cc, s1, s2)
        outs[0][:, h * HEAD_DIM:(h + 1) * HEAD_DIM] = y.astype(outs[0].dtype)
    y = _rope_head(acc[:, ATT_KV_W:ATT_KV_W + HEAD_DIM], cc, s1, s2)
    outs[1][...] = y.astype(outs[1].dtype)


def rope_tables(lp):
    half = ROPE_DIMS // 2
    pos = (jnp.arange(lp, dtype=jnp.int32) - FRAME_OFF).astype(f32)
    inv = ROPE_THETA ** (-jnp.arange(half, dtype=f32) / half)
    ang = pos[:, None] * inv[None, :]
    cos, sin = jnp.cos(ang), jnp.sin(ang)
    zeros = jnp.zeros((lp, HEAD_DIM - ROPE_DIMS), f32)
    z16 = jnp.zeros((lp, half), f32)
    cc = jnp.concatenate([cos, cos, jnp.ones_like(zeros)], axis=1)
    s1 = jnp.concatenate([z16, sin, zeros], axis=1)
    s2 = jnp.concatenate([-sin, z16, zeros], axis=1)
    return cc, s1, s2


INT_MAX = 2 ** 31 - 1
HALF_MIN, HALF_MAX, HALF_SPAN = -2 ** 15, 2 ** 15 - 1, 2 ** 16


def _sortable(x):
    b = lax.bitcast_convert_type(x, i32)
    return b ^ (lax.shift_right_arithmetic(b, 31) & INT_MAX)


_NEG_BITS = struct.unpack("<i", struct.pack("<f", NEG))[0]
_KEY_NEG = _NEG_BITS ^ ((_NEG_BITS >> 31) & INT_MAX)
_KEY_NEG_HI = _KEY_NEG >> 16
_KEY_NEG_LO = (_KEY_NEG & (HALF_SPAN - 1)) + HALF_MIN


def _select_kernel(iq_ref, ik_ref, iwt_ref, bias_ref, key_ref, half_ref, *, tq, ts, lp, topk):
    i = pl.program_id(0)
    nkb = ((i + 1) * tq + ts - 1) // ts
    rem = lp - nkb * ts
    iwt = iwt_ref[...] * ((IDX_HEADS ** -0.5) * (HEAD_DIM ** -0.5))
    qpos = i * tq + lax.broadcasted_iota(i32, (1, tq), 1)
    sub = lax.broadcasted_iota(i32, (ts, 1), 0)

    def score_blk(kb, carry):
        k0 = pl.multiple_of(kb * ts, ts)
        ikb = ik_ref[pl.ds(k0, ts), :]
        acc = jnp.zeros((ts, tq), f32)
        for h in range(IDX_HEADS):
            lg = _dot_nt(ikb, iq_ref[h])
            acc = acc + jnp.maximum(lg, 0.0) * iwt[h:h + 1, :]
        kpos = k0 + sub
        sc = jnp.where(kpos <= qpos, acc, NEG)
        sc = jnp.where(kpos < FRAME_OFF, -jnp.inf, sc)
        key = _sortable(sc)
        key_ref[pl.ds(k0, ts), :] = key
        half_ref[pl.ds(k0, ts), :] = lax.shift_right_arithmetic(key, 16).astype(i16)
        return carry

    lax.fori_loop(0, nkb, score_blk, 0)

    n_acc = 4

    def count(pred_fn):
        sub8 = lax.broadcasted_iota(i32, (8, 1), 0)

        def blk(kb, cnts):
            k0 = pl.multiple_of(kb * ts, ts)
            cnts = list(cnts)
            blk_ref = key_ref.at[pl.ds(k0, ts)]
            for j in range(ts // 8):
                kv = blk_ref[j * 8:(j + 1) * 8, :]
                cnts[j % n_acc] = cnts[j % n_acc] + jnp.where(pred_fn(kv, k0 + j * 8 + sub8), 1, 0)
            return tuple(cnts)
        cnts = lax.fori_loop(0, nkb, blk, tuple(jnp.zeros((8, tq), i32) for _ in range(n_acc)))
        return jnp.sum(sum(cnts[1:], cnts[0]), axis=0, keepdims=True)

    def count_half_ge(cand):
        one, zero = jnp.int16(1), jnp.int16(0)

        def blk(kb, cnts):
            k0 = pl.multiple_of(kb * ts, ts)
            cnts = list(cnts)
            blk_ref = half_ref.at[pl.ds(k0, ts)]
            for j in range(ts // 16):
                hv = blk_ref[j * 16:(j + 1) * 16, :]
                cnts[j % n_acc] = cnts[j % n_acc] + jnp.where(hv >= cand, one, zero)
            return tuple(cnts)
        cnts = lax.fori_loop(0, nkb, blk, tuple(jnp.zeros((16, tq), i16) for _ in range(n_acc)))
        return jnp.sum(sum(cnts[1:], cnts[0]).astype(i32), axis=0, keepdims=True)

    def search_half(neg_half, n_start):
        def body(it, c):
            tau, n_tau = c
            t2 = tau | lax.shift_left(jnp.int32(1), 15 - it)
            cand = t2 + HALF_MIN
            n2 = count_half_ge(cand.astype(i16)) + jnp.where(cand <= neg_half, rem, 0)
            ok = n2 >= topk
            return jnp.where(ok, t2, tau), jnp.where(ok, n2, n_tau)

        return lax.fori_loop(0, 16, body, (jnp.zeros((1, tq), i32), n_start))

    tau_hi, n_hi = search_half(jnp.full((1, tq), _KEY_NEG_HI, i32), jnp.full((1, tq), lp, i32))
    t_hi = tau_hi + HALF_MIN

    def low_blk(kb, carry):
        k0 = pl.multiple_of(kb * ts, ts)
        key = key_ref[pl.ds(k0, ts), :]
        hi = lax.shift_right_arithmetic(key, 16)
        lo = (key & (HALF_SPAN - 1)) + HALF_MIN
        lo = jnp.where(hi > t_hi, HALF_MAX, jnp.where(hi == t_hi, lo, HALF_MIN))
        half_ref[pl.ds(k0, ts), :] = lo.astype(i16)
        return carry

    lax.fori_loop(0, nkb, low_blk, 0)
    neg_lo = jnp.where(_KEY_NEG_HI > t_hi, HALF_MAX, jnp.where(_KEY_NEG_HI == t_hi, _KEY_NEG_LO, HALF_MIN))
    tau_lo, n_ge = search_half(neg_lo, n_hi)
    thr = lax.shift_left(t_hi, 16) | tau_lo
    has_tie = jnp.max(jnp.where(n_ge > topk, 1, 0)) > 0

    def tie_cut():
        n_gt = (count(lambda kv, kpos: kv > thr) + jnp.where(thr < _KEY_NEG, rem, 0))
        need = topk - n_gt

        def bis2(it, cut):
            bit = lax.shift_left(jnp.int32(1), 14 - it)
            c2 = cut | bit
            n = count(lambda kv, kpos: jnp.logical_and(kv == thr, kpos < c2))
            return jnp.where(n < need, c2, cut)
        return lax.fori_loop(0, 15, bis2, jnp.zeros((1, tq), i32))

    cut = lax.cond(has_tie, tie_cut, lambda: jnp.full((1, tq), INT_MAX, i32))

    def bias_blk(kb, carry):
        k0 = pl.multiple_of(kb * ts, ts)
        kv = key_ref[pl.ds(k0, ts), :]
        interior = jnp.logical_and(k0 >= FRAME_OFF, k0 + ts - 1 <= i * tq)

        @pl.when(jnp.logical_and(interior, jnp.logical_not(has_tie)))
        def _():
            bias_ref[pl.ds(k0, ts), :] = jnp.where(kv >= thr, 0.0, NEG).astype(bias_ref.dtype)

        @pl.when(jnp.logical_not(jnp.logical_and(interior, jnp.logical_not(has_tie))))
        def _():
            kpos = k0 + sub
            sel = jnp.logical_or(kv > thr, jnp.logical_and(kv == thr, kpos <= cut))
            vis = jnp.logical_and(kpos <= qpos, kpos >= FRAME_OFF)
            bias_ref[pl.ds(k0, ts), :] = jnp.where(jnp.logical_and(sel, vis), 0.0, NEG).astype(bias_ref.dtype)
        return carry

    lax.fori_loop(0, nkb, bias_blk, 0)

    def fill_blk(kb, carry):
        k0 = pl.multiple_of(kb * ts, ts)
        bias_ref[pl.ds(k0, ts), :] = jnp.full((ts, tq), NEG, bias_ref.dtype)
        return carry

    lax.fori_loop(nkb, lp // ts, fill_blk, 0)


def dsa_select(iq, ik, iwt, topk):
    nh, lp, _ = iq.shape
    assert lp < 2 ** 15
    tq = 256
    ts = _pick(lp, (640, 256, 128))
    kern = functools.partial(_select_kernel, tq=tq, ts=ts, lp=lp, topk=topk)
    return pl.pallas_call(
        kern,
        grid=(lp // tq,),
        in_specs=[pl.BlockSpec((nh, tq, HEAD_DIM), lambda i: (0, i, 0)),
                  pl.BlockSpec((lp, HEAD_DIM), lambda i: (0, 0), pipeline_mode=pl.Buffered(1)),
                  pl.BlockSpec((nh, tq), lambda i: (0, i))],
        out_specs=pl.BlockSpec((lp, tq), lambda i: (0, i)),
        out_shape=jax.ShapeDtypeStruct((lp, lp), bf16),
        scratch_shapes=[pltpu.VMEM((lp, tq), i32), pltpu.VMEM((lp, tq), i16)],
        compiler_params=_cparams(("parallel",)),
        name="dsa_select",
    )(iq, ik, iwt)


V_AUG = HEAD_DIM + 16


def _attn_kernel(q_ref, k_ref, vt_ref, b_ref, o_ref, m_ref, acc_ref, *, tq, ts, bpt):
    i = pl.program_id(0)
    nkb = ((i + 1) * tq + ts - 1) // ts
    group = ATT_HEADS // ATT_KV_HEADS
    kv_groups = range(ATT_KV_HEADS)
    qpos = i * tq + lax.broadcasted_iota(i32, (tq, 1), 0)
    eye = (lax.broadcasted_iota(i32, (tq, tq), 0)
           == lax.broadcasted_iota(i32, (tq, tq), 1)).astype(bf16)
    qa = [jnp.concatenate(
        [jnp.concatenate([q_ref[:, (g * group + r) * HEAD_DIM:(g * group + r + 1) * HEAD_DIM], eye],
                         axis=1) for r in range(group)], axis=0) for g in kv_groups]
    m_ref[...] = jnp.full(m_ref.shape, -jnp.inf, f32)
    acc_ref[...] = jnp.zeros_like(acc_ref)

    def scores(g, k0):
        ka = jnp.concatenate([k_ref[pl.ds(k0, ts), g * HEAD_DIM:(g + 1) * HEAD_DIM],
                              b_ref[pl.ds(k0, ts), :]], axis=1)
        return _dot_nt(ka, qa[g])

    def accumulate(g, k0, st):
        vt_ = vt_ref[g, :, pl.ds(k0, ts)]
        ps, alphas = [], []
        for r in range(group):
            sr = st[:, r * tq:(r + 1) * tq]
            m_old = m_ref[g, r:r + 1, :]
            m_new = jnp.maximum(m_old, jnp.max(sr, axis=0, keepdims=True))
            m_ref[g, r:r + 1, :] = m_new
            ps.append(jnp.exp2(sr - m_new).astype(bf16))
            alphas.append(jnp.exp2(m_old - m_new))
        pt = jnp.concatenate(ps, axis=1)
        acc_ref[g] = jnp.concatenate(alphas, axis=1) * acc_ref[g] + _dot(vt_, pt)

    def trip(kp, carry):
        k0s = [pl.multiple_of((kp * bpt + b) * ts, ts) for b in range(bpt)]
        work = [(g, kk) for g in kv_groups for kk in k0s]
        st = scores(*work[0])
        for n in range(len(work)):
            st_next = scores(*work[n + 1]) if n + 1 < len(work) else None
            accumulate(*work[n], st)
            st = st_next
        return carry

    lax.fori_loop(0, (nkb + bpt - 1) // bpt, trip, 0)
    for g in kv_groups:
        for r in range(group):
            ot = (acc_ref[g, 0:HEAD_DIM, r * tq:(r + 1) * tq]
                  / acc_ref[g, HEAD_DIM:HEAD_DIM + 1, r * tq:(r + 1) * tq])
            orr = jnp.where(qpos >= FRAME_OFF, ot.T, 0.0)
            o_ref[:, (g * group + r) * HEAD_DIM:(g * group + r + 1) * HEAD_DIM] = orr.astype(o_ref.dtype)


def dsa_attention(q, k, vt_aug, bias_t):
    lp = q.shape[0]
    tq = 128
    ts = _pick(lp, (1280, 640, 256, 128))
    bpt = 1
    group = ATT_HEADS // ATT_KV_HEADS
    assert (lp // ts) % bpt == 0
    kern = functools.partial(_attn_kernel, tq=tq, ts=ts, bpt=bpt)
    resident = pl.Buffered(1)
    return pl.pallas_call(
        kern,
        grid=(lp // tq,),
        in_specs=[pl.BlockSpec((tq, ATT_Q_W), lambda i: (i, 0)),
                  pl.BlockSpec((lp, ATT_KV_W), lambda i: (0, 0), pipeline_mode=resident),
                  pl.BlockSpec((ATT_KV_HEADS, V_AUG, lp), lambda i: (0, 0, 0), pipeline_mode=resident),
                  pl.BlockSpec((lp, tq), lambda i: (0, i))],
        out_specs=pl.BlockSpec((tq, ATT_Q_W), lambda i: (i, 0)),
        out_shape=jax.ShapeDtypeStruct((lp, ATT_Q_W), bf16),
        scratch_shapes=[pltpu.VMEM((ATT_KV_HEADS, group, tq), f32),
                        pltpu.VMEM((ATT_KV_HEADS, V_AUG, group * tq), f32)],
        compiler_params=_cparams(("parallel",)),
        name="dsa_attention",
    )(q, k, vt_aug, bias_t)


def _ffn_up_kernel(a_ref, ah_ref, wg_ref, wv_ref, cg_ref, cv_ref, bg_ref, bv_ref, o_ref,
                   eg_ref, ev_ref, *, tm):
    a = _halo_rows(a_ref, ah_ref, pl.program_id(0))
    eg_ref[...] = _dot(a, wg_ref[...])
    ev_ref[...] = _dot(a, wv_ref[...])

    def conv(ext_ref, w_ref, b_ref):
        w = w_ref[...]
        y = ext_ref[HALO:HALO + tm, :] * w[FFN_CONV - 1:FFN_CONV, :]
        for j in range(FFN_CONV - 1):
            s0 = HALO - (FFN_CONV - 1) + j
            y = y + ext_ref[s0:s0 + tm, :] * w[j:j + 1, :]
        return y + b_ref[...]

    gate = conv(eg_ref, cg_ref, bg_ref)
    val = conv(ev_ref, cv_ref, bv_ref)
    o_ref[...] = (gate * _sigmoid(gate) * val).astype(o_ref.dtype)


def ffn_up(u, w_up, conv_w, conv_b):
    lp, kdim = u.shape
    tm = _pick(lp, (1280, 640, 256, 128))
    tn = 512
    nc = D_FF // tn
    kern = functools.partial(_ffn_up_kernel, tm=tm)

    def wspec(off):
        return pl.BlockSpec((kdim, tn), lambda i, c: (0, c + off))

    def cspec(off):
        return pl.BlockSpec((FFN_CONV, tn), lambda i, c: (0, c + off))

    def bspec(off):
        return pl.BlockSpec((1, tn), lambda i, c: (0, c + off))

    b2 = conv_b.reshape(1, 2 * D_FF)
    return pl.pallas_call(
        kern,
        grid=(lp // tm, nc),
        in_specs=[pl.BlockSpec((tm, kdim), lambda i, c: (i, 0)),
                  pl.BlockSpec((HALO, kdim), lambda i, c: (jnp.maximum(i * (tm // HALO) - 1, 0), 0)),
                  wspec(0), wspec(nc), cspec(0), cspec(nc), bspec(0), bspec(nc)],
        out_specs=pl.BlockSpec((tm, tn), lambda i, c: (i, c)),
        out_shape=jax.ShapeDtypeStruct((lp, D_FF), bf16),
        scratch_shapes=[pltpu.VMEM((HALO + tm, tn), f32)] * 2,
        compiler_params=_cparams(("parallel", "arbitrary")),
        name="ffn_up",
    )(u, u, w_up, w_up, conv_w, conv_w, b2, b2)


def _split_w_in(w):
    o = 0
    parts = {}
    for name, width in (("gq", GDN_QK_W), ("gk", GDN_QK_W), ("gv", GDN_V_W), ("gz", GDN_V_W),
                        ("gb", GDN_V_HEADS), ("ga", GDN_V_HEADS), ("aq", ATT_Q_W), ("ak", ATT_KV_W),
                        ("av", ATT_KV_W), ("iq", IDX_Q_W), ("ik", HEAD_DIM), ("iw", IDX_HEADS),
                        ("gate_gdn", D_MODEL), ("gate_att", D_MODEL)):
        parts[name] = w[:, o:o + width]
        o += width
    cat = lambda names: jnp.concatenate([parts[n] for n in names], axis=1)
    small = cat(("gb", "ga", "iw"))
    small = jnp.pad(small, ((0, 0), (0, LANES - small.shape[1])))
    groups = dict(qkv=cat(("gq", "gk", "gv")), aq=parts["aq"], iq=parts["iq"], kv=cat(("ak", "ik")),
                  plain=cat(("gz", "gate_gdn", "gate_att", "av")), small=small)
    return {k: v.astype(bf16) for k, v in groups.items()}


def _layer(h0, p, lp, topk):
    tm = _pick(lp, (640, 256, 128))
    tm_big = _pick(lp, (1280, 640, 256, 128))
    wp = _split_w_in(p["w_in"])

    def proj(w, **kw):
        return matmul(u1, w, tm=tm_big, tk=D_MODEL, **kw)

    u1 = rms_rows(h0, p["mix_pre_g"])
    tabs = rope_tables(lp)
    tab_specs = (((tm_big, HEAD_DIM), lambda i, j: (i, 0)),) * 3
    qkv = gdn_proj(u1, wp["qkv"], p["gdn_conv_w"])
    (aq,) = proj(wp["aq"], tn=1024, epilogue=_make_ep_rope(HEAD_DIM ** -0.5 * LOG2E, False),
                 extra=tabs, extra_specs=tab_specs,
                 out_shapes=[jax.ShapeDtypeStruct((lp, ATT_Q_W), bf16)], name="proj_aq")
    iq_heads = 1024 // HEAD_DIM
    (iq,) = proj(wp["iq"], tn=1024, epilogue=_make_ep_rope(1.0, True), extra=tabs, extra_specs=tab_specs,
                 out_shapes=[jax.ShapeDtypeStruct((IDX_HEADS, lp, HEAD_DIM), bf16)],
                 out_block_specs=[((iq_heads, tm_big, HEAD_DIM), lambda i, j: (j, i, 0))], name="proj_iq")
    ak, ik = proj(wp["kv"], tn=ATT_KV_W + HEAD_DIM, epilogue=_ep_rope_kv, extra=tabs, extra_specs=tab_specs,
                  out_shapes=[jax.ShapeDtypeStruct((lp, ATT_KV_W), bf16),
                              jax.ShapeDtypeStruct((lp, HEAD_DIM), bf16)],
                  out_block_specs=[((tm_big, ATT_KV_W), lambda i, j: (i, 0)),
                                   ((tm_big, HEAD_DIM), lambda i, j: (i, 0))], name="proj_kv")
    (plain,) = proj(wp["plain"], tn=768, epilogue=_ep_cast,
                    out_shapes=[jax.ShapeDtypeStruct((lp, PLAIN_W), bf16)], name="proj_plain")
    (small,) = proj(wp["small"], tn=LANES, epilogue=_ep_cast,
                    out_shapes=[jax.ShapeDtypeStruct((lp, LANES), f32)], name="proj_small")
    gb = small[:, 0:GDN_V_HEADS]
    ga = small[:, GDN_V_HEADS:2 * GDN_V_HEADS]
    iw = small[:, 2 * GDN_V_HEADS:2 * GDN_V_HEADS + IDX_HEADS]

    beta, gcum = gdn_gates(gb, ga, p["gdn_a_log"], p["gdn_dt_bias"])
    o_gdn = gdn_core(qkv, plain, p["gdn_norm_g"], gcum.T[:, None, :], beta.T[:, None, :])

    avt = plain[:, COL_AV:COL_AV + ATT_KV_W].T.reshape(ATT_KV_HEADS, HEAD_DIM, lp)
    avt = jnp.concatenate([avt, jnp.ones((ATT_KV_HEADS, V_AUG - HEAD_DIM, lp), bf16)], axis=1)
    bias_t = dsa_select(iq, ik, iw.T, topk)
    o_att = dsa_attention(aq, ak, avt, bias_t)

    tn = 1024
    gate_spec = lambda col0: ((tm_big, tn), (lambda i, j, c=col0 // tn: (i, c + j)))
    (m1,) = matmul(o_gdn, p["w_branch_gdn"].astype(bf16), tm=tm_big, tn=tn, tk=2048,
                   epilogue=_ep_gate, extra=(plain,), extra_specs=(gate_spec(COL_GATE_GDN),),
                   out_shapes=[jax.ShapeDtypeStruct((lp, D_MODEL), f32)], name="branch_gdn")
    (merged,) = matmul(o_att, p["w_branch_att"].astype(bf16), tm=tm_big, tn=tn, tk=2048,
                       epilogue=_ep_gate_add, extra=(plain, m1),
                       extra_specs=(gate_spec(COL_GATE_ATT), ((tm_big, tn), lambda i, j: (i, j))),
                       out_shapes=[jax.ShapeDtypeStruct((lp, D_MODEL), bf16)], name="branch_att")
    row_spec = ((tm, D_MODEL), lambda i, j: (i, 0))
    vec_spec = ((1, D_MODEL), lambda i, j: (0, 0))
    h1, u2 = matmul(merged, p["w_out"].astype(bf16), tm=tm, tn=D_MODEL, tk=D_MODEL,
                    epilogue=_ep_res_norm2,
                    extra=(h0, p["mix_post_g"].reshape(1, D_MODEL), p["ffn_pre_g"].reshape(1, D_MODEL)),
                    extra_specs=(row_spec, vec_spec, vec_spec),
                    out_shapes=[jax.ShapeDtypeStruct((lp, D_MODEL), f32),
                                jax.ShapeDtypeStruct((lp, D_MODEL), bf16)], name="w_out")

    act = ffn_up(u2, p["w_up"].astype(bf16), p["ffn_conv_w"], p["ffn_conv_b"])
    (h2,) = matmul(act, p["w_down"].astype(bf16), tm=tm, tn=D_MODEL, tk=2048, epilogue=_ep_res_norm,
                   extra=(h1, p["ffn_post_g"].reshape(1, D_MODEL)),
                   extra_specs=(row_spec, vec_spec),
                   out_shapes=[jax.ShapeDtypeStruct((lp, D_MODEL), f32)], name="w_down")
    return h2


def kernel(x, meta_tokens, mix_pre_g, w_in, gdn_conv_w, gdn_a_log, gdn_dt_bias, gdn_norm_g,
           w_branch_gdn, w_branch_att, w_out, mix_post_g, ffn_pre_g, w_up, ffn_conv_w,
           ffn_conv_b, w_down, ffn_post_g):
    batch, seq, d = x.shape
    assert batch == 1 and d == D_MODEL
    lp = FRAME_X0 + seq
    topk = min(TOPK_MAX, (N_META + seq) // 4)
    h = jnp.concatenate([jnp.zeros((FRAME_OFF, d), x.dtype), meta_tokens.astype(x.dtype), x[0]], axis=0)
    for i in range(w_in.shape[0]):
        p = dict(mix_pre_g=mix_pre_g[i], w_in=w_in[i], gdn_conv_w=gdn_conv_w[i], gdn_a_log=gdn_a_log[i],
                 gdn_dt_bias=gdn_dt_bias[i], gdn_norm_g=gdn_norm_g[i], w_branch_gdn=w_branch_gdn[i],
                 w_branch_att=w_branch_att[i], w_out=w_out[i], mix_post_g=mix_post_g[i],
                 ffn_pre_g=ffn_pre_g[i], w_up=w_up[i], ffn_conv_w=ffn_conv_w[i],
                 ffn_conv_b=ffn_conv_b[i], w_down=w_down[i], ffn_post_g=ffn_post_g[i])
        h = _layer(h, p, lp, topk)
    return h[FRAME_X0:][None]
```

```python
import functools
import struct

import jax
import jax.numpy as jnp
from jax import lax
from jax.experimental import pallas as pl
from jax.experimental.pallas import tpu as pltpu

f32 = jnp.float32
bf16 = jnp.bfloat16
i32 = jnp.int32
i16 = jnp.int16

D_MODEL = 2048
N_META = 16
EPS = 1e-6
GDN_QK_HEADS = 16
GDN_V_HEADS = 32
HEAD_DIM = 128
GDN_CONV = 4
ATT_HEADS = 16
ATT_KV_HEADS = 2
IDX_HEADS = 16
TOPK_MAX = 256
NEG = -1e30
LOG2E = 1.4426950408889634
ROPE_THETA = 500000.0
ROPE_DIMS = HEAD_DIM // 4
D_FF = 3 * D_MODEL
FFN_CONV = 3
GDN_QK_W = GDN_QK_HEADS * HEAD_DIM
GDN_V_W = GDN_V_HEADS * HEAD_DIM
ATT_Q_W = ATT_HEADS * HEAD_DIM
ATT_KV_W = ATT_KV_HEADS * HEAD_DIM
IDX_Q_W = IDX_HEADS * HEAD_DIM

FRAME_X0 = 256
FRAME_OFF = FRAME_X0 - N_META
CHUNK = 128

LANES = 128
VMEM_LIMIT = 56 * 1024 * 1024

COL_GZ = 0
COL_GATE_GDN = COL_GZ + GDN_V_W
COL_GATE_ATT = COL_GATE_GDN + D_MODEL
COL_AV = COL_GATE_ATT + D_MODEL
PLAIN_W = COL_AV + ATT_KV_W


def _cparams(sem):
    return pltpu.CompilerParams(dimension_semantics=sem, vmem_limit_bytes=VMEM_LIMIT)


def _pick(n, cands):
    for c in cands:
        if n % c == 0:
            return c
    raise ValueError(f"no tile for {n} in {cands}")


def _sigmoid(x):
    return 1.0 / (1.0 + jnp.exp(-x))


def _dot(a, b):
    return jnp.dot(a, b, preferred_element_type=f32)


def _dot_nt(a, b):
    return lax.dot_general(a, b, (((1,), (1,)), ((), ())), preferred_element_type=f32)


def _rms_rows_kernel(h_ref, g_ref, o_ref):
    h = h_ref[...]
    y = h * lax.rsqrt(jnp.mean(h * h, axis=-1, keepdims=True) + EPS)
    o_ref[...] = (y * g_ref[...]).astype(o_ref.dtype)


def rms_rows(h, g):
    lp, d = h.shape
    tm = _pick(lp, (640, 256, 128))
    return pl.pallas_call(
        _rms_rows_kernel,
        grid=(lp // tm,),
        in_specs=[pl.BlockSpec((tm, d), lambda i: (i, 0)),
                  pl.BlockSpec((1, d), lambda i: (0, 0))],
        out_specs=pl.BlockSpec((tm, d), lambda i: (i, 0)),
        out_shape=jax.ShapeDtypeStruct((lp, d), bf16),
        compiler_params=_cparams(("parallel",)),
        name="rms_rows",
    )(h, g.reshape(1, d))


def _mm_kernel(*refs, nk, n_extra, n_out, epilogue):
    a_ref, w_ref = refs[0], refs[1]
    extra = refs[2:2 + n_extra]
    outs = refs[2 + n_extra:2 + n_extra + n_out]
    if nk == 1:
        epilogue(_dot(a_ref[...], w_ref[...]), extra, outs)
        return
    acc_ref = refs[-1]
    k = pl.program_id(2)

    @pl.when(k == 0)
    def _():
        acc_ref[...] = jnp.zeros_like(acc_ref)

    acc_ref[...] += _dot(a_ref[...], w_ref[...])

    @pl.when(k == nk - 1)
    def _():
        epilogue(acc_ref[...], extra, outs)


def matmul(a, w, *, tm, tn, tk, epilogue, extra=(), extra_specs=(), out_shapes, out_block_specs=None,
           a_col0=0, name):
    m = a.shape[0]
    kdim, n = w.shape
    nk = kdim // tk
    assert m % tm == 0 and n % tn == 0 and kdim % tk == 0
    w_mode = dict(pipeline_mode=pl.Buffered(1)) if (nk == 1 and n == tn) else {}
    in_specs = [pl.BlockSpec((tm, tk), lambda i, j, k: (i, a_col0 + k)),
                pl.BlockSpec((tk, tn), lambda i, j, k: (k, j), **w_mode)]
    in_specs += [pl.BlockSpec(bs, (lambda i, j, k, f=f: f(i, j))) for bs, f in extra_specs]
    if out_block_specs is None:
        out_block_specs = [((tm, tn), lambda i, j: (i, j))] * len(out_shapes)
    out_specs = [pl.BlockSpec(bs, (lambda i, j, k, f=f: f(i, j))) for bs, f in out_block_specs]
    scratch = [] if nk == 1 else [pltpu.VMEM((tm, tn), f32)]
    kern = functools.partial(_mm_kernel, nk=nk, n_extra=len(extra), n_out=len(out_shapes),
                             epilogue=epilogue)
    res = pl.pallas_call(
        kern,
        grid=(m // tm, n // tn, nk),
        in_specs=in_specs,
        out_specs=out_specs,
        out_shape=out_shapes,
        scratch_shapes=scratch,
        compiler_params=_cparams(("parallel", "parallel", "arbitrary")),
        name=name,
    )(a, w, *extra)
    return res


def _ep_cast(acc, extra, outs):
    outs[0][...] = acc.astype(outs[0].dtype)


def _ep_gate(acc, extra, outs):
    g = extra[0][...].astype(f32)
    outs[0][...] = (_sigmoid(g) * acc).astype(outs[0].dtype)


def _ep_gate_add(acc, extra, outs):
    g = extra[0][...].astype(f32)
    outs[0][...] = (extra[1][...].astype(f32) + _sigmoid(g) * acc).astype(outs[0].dtype)


def _rms(t, g):
    return t * lax.rsqrt(jnp.mean(t * t, axis=-1, keepdims=True) + EPS) * g


def _ep_res_norm2(acc, extra, outs):
    h_ref, g_ref, g2_ref = extra
    h1 = h_ref[...] + _rms(acc, g_ref[...])
    outs[0][...] = h1
    outs[1][...] = _rms(h1, g2_ref[...]).astype(outs[1].dtype)


def _ep_res_norm(acc, extra, outs):
    h_ref, g_ref = extra
    outs[0][...] = h_ref[...] + _rms(acc, g_ref[...])


HALO = 16


def _halo_rows(a_ref, ah_ref, i):
    halo = ah_ref[...]
    halo = jnp.where(i > 0, halo, jnp.zeros_like(halo))
    return jnp.concatenate([halo, a_ref[...]], axis=0)


def _gdn_proj_kernel(a_ref, ah_ref, w_ref, cw_ref, o_ref, ext_ref, *, tm, tn):
    c = pl.program_id(1)
    ext_ref[...] = _dot(_halo_rows(a_ref, ah_ref, pl.program_id(0)), w_ref[...])
    w = cw_ref[...]
    y = ext_ref[HALO:HALO + tm, :] * w[GDN_CONV - 1:GDN_CONV, :]
    for j in range(GDN_CONV - 1):
        s0 = HALO - (GDN_CONV - 1) + j
        y = y + ext_ref[s0:s0 + tm, :] * w[j:j + 1, :]
    s = y * _sigmoid(y)
    is_q = c < (GDN_QK_W // tn)
    is_qk = c < (2 * GDN_QK_W // tn)
    qscale = jnp.where(is_q, HEAD_DIM ** -0.5, 1.0).astype(f32)
    for hh in range(tn // HEAD_DIM):
        seg = s[:, hh * HEAD_DIM:(hh + 1) * HEAD_DIM]
        r = lax.rsqrt(jnp.sum(seg * seg, axis=-1, keepdims=True) + EPS) * qscale
        fac = jnp.where(is_qk, r, 1.0)
        o_ref[:, hh * HEAD_DIM:(hh + 1) * HEAD_DIM] = (seg * fac).astype(o_ref.dtype)


def gdn_proj(u, w_qkv, conv_w):
    lp, kdim = u.shape
    width = w_qkv.shape[1]
    tm = _pick(lp, (1280, 640, 256, 128))
    tn = 1024
    kern = functools.partial(_gdn_proj_kernel, tm=tm, tn=tn)
    return pl.pallas_call(
        kern,
        grid=(lp // tm, width // tn),
        in_specs=[pl.BlockSpec((tm, kdim), lambda i, c: (i, 0)),
                  pl.BlockSpec((HALO, kdim), lambda i, c: (jnp.maximum(i * (tm // HALO) - 1, 0), 0)),
                  pl.BlockSpec((kdim, tn), lambda i, c: (0, c)),
                  pl.BlockSpec((GDN_CONV, tn), lambda i, c: (0, c))],
        out_specs=pl.BlockSpec((tm, tn), lambda i, c: (i, c)),
        out_shape=jax.ShapeDtypeStruct((lp, width), bf16),
        scratch_shapes=[pltpu.VMEM((HALO + tm, tn), f32)],
        compiler_params=_cparams(("parallel", "arbitrary")),
        name="gdn_proj",
    )(u, u, w_qkv, conv_w)


def _gdn_gates_kernel(b_ref, a_ref, alog_ref, dt_ref, beta_ref, gcum_ref, *, tm):
    i = pl.program_id(0)
    rows = i * tm + lax.broadcasted_iota(i32, (tm, 1), 0)
    valid = rows >= FRAME_OFF
    beta_ref[...] = jnp.where(valid, _sigmoid(b_ref[...]), 0.0)
    a = a_ref[...] + dt_ref[...]
    sp = jnp.maximum(a, 0.0) + jnp.log1p(jnp.exp(-jnp.abs(a)))
    g = jnp.where(valid, -jnp.exp(alog_ref[...]) * sp, 0.0)
    tri = (lax.broadcasted_iota(i32, (CHUNK, CHUNK), 0)
           >= lax.broadcasted_iota(i32, (CHUNK, CHUNK), 1)).astype(f32)
    for c in range(tm // CHUNK):
        rs = slice(c * CHUNK, (c + 1) * CHUNK)
        gcum_ref[rs, :] = jnp.dot(tri, g[rs, :], preferred_element_type=f32,
                                  precision=lax.Precision.HIGHEST)


def gdn_gates(gb, ga, a_log, dt_bias):
    lp, nh = gb.shape
    tm = _pick(lp, (1280, 640, 256, 128))
    spec = pl.BlockSpec((tm, nh), lambda i: (i, 0))
    vec = pl.BlockSpec((1, nh), lambda i: (0, 0))
    return pl.pallas_call(
        functools.partial(_gdn_gates_kernel, tm=tm),
        grid=(lp // tm,),
        in_specs=[spec, spec, vec, vec],
        out_specs=[spec, spec],
        out_shape=[jax.ShapeDtypeStruct((lp, nh), f32)] * 2,
        compiler_params=_cparams(("parallel",)),
        name="gdn_gates",
    )(gb, ga, a_log.reshape(1, nh), dt_bias.reshape(1, nh))


def _block_mask(size):
    r = lax.broadcasted_iota(i32, (CHUNK, CHUNK), 0) // size
    c = lax.broadcasted_iota(i32, (CHUNK, CHUNK), 1) // size
    return r == c


def _unit_lower_inverse_many(mats, tick):
    row = lax.broadcasted_iota(i32, (CHUNK, CHUNK), 0)
    col = lax.broadcasted_iota(i32, (CHUNK, CHUNK), 1)
    eye = (row == col).astype(f32)
    base = 8
    m_prev = _block_mask(base)
    ads = [jnp.where(m_prev, a, 0.0) for a in mats]
    adbs = [ad.astype(bf16) for ad in ads]
    a2s = [_dot(x, x) for x in adbs]
    tick()
    a2bs = [x.astype(bf16) for x in a2s]
    a4s = [_dot(x, x) for x in a2bs]
    tick()
    xs = [_dot((eye - ad).astype(bf16), (eye + a2).astype(bf16)) for ad, a2 in zip(ads, a2s)]
    tick()
    xs = [_dot(x.astype(bf16), (eye + a4).astype(bf16)) for x, a4 in zip(xs, a4s)]
    tick()
    size = base * 2
    while size <= CHUNK:
        m_cur = _block_mask(size)
        off_diag = jnp.logical_and(m_cur, jnp.logical_not(m_prev))
        bs = [jnp.where(off_diag, a, 0.0).astype(bf16) for a in mats]
        xbs = [x.astype(bf16) for x in xs]
        ys = [_dot(xb, b).astype(bf16) for xb, b in zip(xbs, bs)]
        tick()
        xs = [x - _dot(y, xb) for x, y, xb in zip(xs, ys, xbs)]
        tick()
        m_prev = m_cur
        size *= 2
    return xs


def _gdn_kernel(q_ref, k_ref, v_ref, z_ref, ng_ref, gr0_ref, gr1_ref, br0_ref, br1_ref,
                o_ref, s_ref, u_ref, wq_ref, ak_ref, eg_ref, egl_ref, *, rb):
    r = pl.program_id(1)

    @pl.when(r == 0)
    def _():
        for ref in (s_ref, u_ref, wq_ref, ak_ref, eg_ref, egl_ref):
            ref[...] = jnp.zeros_like(ref)

    grs, brs = (gr0_ref, gr1_ref), (br0_ref, br1_ref)
    row = lax.broadcasted_iota(i32, (CHUNK, CHUNK), 0)
    col = lax.broadcasted_iota(i32, (CHUNK, CHUNK), 1)
    incl = row >= col
    strict = row > col
    ng = ng_ref[...]
    nchunk = rb // CHUNK
    heads = range(2)

    def rows(c):
        return slice(c * CHUNK, (c + 1) * CHUNK)

    def lanes(e):
        return slice(e * HEAD_DIM, (e + 1) * HEAD_DIM)

    chains = [(c, e) for c in range(nchunk) for e in heads]

    def recurrence():
        state = [s_ref[e] for e in heads]
        for c in range(nchunk):
            n = [chains.index((c, e)) for e in heads]
            sb = [state[e].astype(bf16) for e in heads]
            ws = [_dot(wq_ref[n[e]], sb[e]) for e in heads]
            yield
            vb = [(u_ref[n[e]] - ws[e][:CHUNK]).astype(bf16) for e in heads]
            av = [_dot(ak_ref[n[e]], vb[e]) for e in heads]
            yield
            for e in heads:
                o = eg_ref[n[e]] * ws[e][CHUNK:] + av[e][:CHUNK]
                state[e] = state[e] * egl_ref[n[e]] + av[e][CHUNK:]
                z = z_ref[rows(c), lanes(e)].astype(f32)
                o_ref[rows(c), lanes(e)] = (_rms(o, ng) * (z * _sigmoid(z))).astype(o_ref.dtype)
        for e in heads:
            s_ref[e] = state[e]

    rec = recurrence()

    def tick():
        next(rec, None)

    qs = [q_ref[rows(c), :] for c in range(nchunk)]
    ks = [k_ref[rows(c), :] for c in range(nchunk)]
    kts = [k.astype(f32).T for k in ks]
    kks = [_dot_nt(k, k) for k in ks]
    qks = [_dot_nt(q, k) for q, k in zip(qs, ks)]
    tick()
    def per_row(v):
        return jnp.broadcast_to(v, (CHUNK, CHUNK)).T

    gr = {ce: grs[ce[1]][:, rows(ce[0])] for ce in chains}
    br = {ce: brs[ce[1]][:, rows(ce[0])] for ce in chains}
    gc = {ce: per_row(gr[ce]) for ce in chains}
    bc = {ce: per_row(br[ce]) for ce in chains}
    dec = {ce: jnp.where(incl, jnp.exp(jnp.where(incl, gc[ce] - gr[ce], 0.0)), 0.0) for ce in chains}
    amat = [jnp.where(strict, bc[ce] * kks[ce[0]] * dec[ce], 0.0) for ce in chains]
    tinv = dict(zip(chains, _unit_lower_inverse_many(amat, tick)))
    u = {ce: _dot((tinv[ce] * br[ce]).astype(bf16), v_ref[rows(ce[0]), lanes(ce[1])]) for ce in chains}
    tick()
    w = {ce: _dot((tinv[ce] * (br[ce] * jnp.exp(gr[ce]))).astype(bf16), ks[ce[0]]) for ce in chains}
    tick()
    g_last = {ce: gr[ce][:, CHUNK - 1:CHUNK] for ce in chains}
    wq = {ce: jnp.concatenate([w[ce].astype(bf16), qs[ce[0]]], axis=0) for ce in chains}
    ak = {ce: jnp.concatenate(
        [jnp.where(incl, qks[ce[0]] * dec[ce], 0.0).astype(bf16),
         (kts[ce[0]] * jnp.exp(g_last[ce] - gr[ce])).astype(bf16)], axis=0) for ce in chains}
    for _ in rec:
        pass
    for n, ce in enumerate(chains):
        u_ref[n] = u[ce]
        wq_ref[n] = wq[ce]
        ak_ref[n] = ak[ce]
        eg_ref[n] = jnp.exp(gc[ce])
        egl_ref[n] = jnp.broadcast_to(jnp.exp(g_last[ce]), (1, HEAD_DIM))


def gdn_core(qkv, plain, norm_g, grow, brow):
    assert CHUNK == HEAD_DIM
    lp = qkv.shape[0]
    rb = _pick(lp, (640, 256, 128))
    nqk = GDN_QK_HEADS
    kcol0 = GDN_QK_W // HEAD_DIM
    vcol0 = 2 * GDN_QK_W // (2 * HEAD_DIM)
    zcol0 = COL_GZ // (2 * HEAD_DIM)

    nr = lp // rb
    n_chain = 2 * (rb // CHUNK)

    def cur(r):
        return jnp.minimum(r, nr - 1)

    def prev(r):
        return jnp.maximum(r - 1, 0)

    def rowspec(e):
        return pl.BlockSpec((None, 1, rb), lambda j, r: (2 * j + e, 0, cur(r)))

    kern = functools.partial(_gdn_kernel, rb=rb)
    return pl.pallas_call(
        kern,
        grid=(nqk, nr + 1),
        in_specs=[pl.BlockSpec((rb, HEAD_DIM), lambda j, r: (cur(r), j)),
                  pl.BlockSpec((rb, HEAD_DIM), lambda j, r: (cur(r), kcol0 + j)),
                  pl.BlockSpec((rb, 2 * HEAD_DIM), lambda j, r: (cur(r), vcol0 + j)),
                  pl.BlockSpec((rb, 2 * HEAD_DIM), lambda j, r: (prev(r), zcol0 + j)),
                  pl.BlockSpec((1, HEAD_DIM), lambda j, r: (0, 0)),
                  rowspec(0), rowspec(1), rowspec(0), rowspec(1)],
        out_specs=pl.BlockSpec((rb, 2 * HEAD_DIM), lambda j, r: (prev(r), j)),
        out_shape=jax.ShapeDtypeStruct((lp, GDN_V_W), bf16),
        scratch_shapes=[pltpu.VMEM((2, HEAD_DIM, HEAD_DIM), f32),
                        pltpu.VMEM((n_chain, CHUNK, HEAD_DIM), f32),
                        pltpu.VMEM((n_chain, 2 * CHUNK, HEAD_DIM), bf16),
                        pltpu.VMEM((n_chain, 2 * CHUNK, CHUNK), bf16),
                        pltpu.VMEM((n_chain, CHUNK, HEAD_DIM), f32),
                        pltpu.VMEM((n_chain, 1, HEAD_DIM), f32)],
        compiler_params=_cparams(("parallel", "arbitrary")),
        name="gdn_core",
    )(qkv, qkv, qkv, plain, norm_g.reshape(1, HEAD_DIM), grow, grow, brow, brow)


def _rope_head(seg, cc, s1, s2):
    half = ROPE_DIMS // 2
    return seg * cc + pltpu.roll(seg, half, 1) * s1 + pltpu.roll(seg, HEAD_DIM - half, 1) * s2


def _make_ep_rope(scale, head_major):
    def ep(acc, extra, outs):
        cc, s1, s2 = (r[...] for r in extra)
        for h in range(acc.shape[1] // HEAD_DIM):
            y = _rope_head(acc[:, h * HEAD_DIM:(h + 1) * HEAD_DIM], cc, s1, s2)
            if scale != 1.0:
                y = y * scale
            if head_major:
                outs[0][h] = y.astype(outs[0].dtype)
            else:
                outs[0][:, h * HEAD_DIM:(h + 1) * HEAD_DIM] = y.astype(outs[0].dtype)
    return ep


def _ep_rope_kv(acc, extra, outs):
    cc, s1, s2 = (r[...] for r in extra)
    for h in range(ATT_KV_HEADS):
        y = _rope_head(acc[:, h * HEAD_DIM:(h + 1) * HEAD_DIM], cc, s1, s2)
        outs[0][:, h * HEAD_DIM:(h + 1) * HEAD_DIM] = y.astype(outs[0].dtype)
    y = _rope_head(acc[:, ATT_KV_W:ATT_KV_W + HEAD_DIM], cc, s1, s2)
    outs[1][...] = y.astype(outs[1].dtype)


def rope_tables(lp):
    half = ROPE_DIMS // 2
    pos = (jnp.arange(lp, dtype=jnp.int32) - FRAME_OFF).astype(f32)
    inv = ROPE_THETA ** (-jnp.arange(half, dtype=f32) / half)
    ang = pos[:, None] * inv[None, :]
    cos, sin = jnp.cos(ang), jnp.sin(ang)
    zeros = jnp.zeros((lp, HEAD_DIM - ROPE_DIMS), f32)
    z16 = jnp.zeros((lp, half), f32)
    cc = jnp.concatenate([cos, cos, jnp.ones_like(zeros)], axis=1)
    s1 = jnp.concatenate([z16, sin, zeros], axis=1)
    s2 = jnp.concatenate([-sin, z16, zeros], axis=1)
    return cc, s1, s2


INT_MAX = 2 ** 31 - 1
HALF_MIN, HALF_MAX, HALF_SPAN = -2 ** 15, 2 ** 15 - 1, 2 ** 16


def _sortable(x):
    b = lax.bitcast_convert_type(x, i32)
    return b ^ (lax.shift_right_arithmetic(b, 31) & INT_MAX)


_NEG_BITS = struct.unpack("<i", struct.pack("<f", NEG))[0]
_KEY_NEG = _NEG_BITS ^ ((_NEG_BITS >> 31) & INT_MAX)
_KEY_NEG_HI = _KEY_NEG >> 16
_KEY_NEG_LO = (_KEY_NEG & (HALF_SPAN - 1)) + HALF_MIN


def _select_kernel(iq_ref, ik_ref, iwt_ref, bias_ref, key_ref, half_ref, *, tq, ts, lp, topk):
    i = pl.program_id(0)
    nkb = ((i + 1) * tq + ts - 1) // ts
    rem = lp - nkb * ts
    iwt = iwt_ref[...] * ((IDX_HEADS ** -0.5) * (HEAD_DIM ** -0.5))
    qpos = i * tq + lax.broadcasted_iota(i32, (1, tq), 1)
    sub = lax.broadcasted_iota(i32, (ts, 1), 0)

    def score_blk(kb, carry):
        k0 = pl.multiple_of(kb * ts, ts)
        ikb = ik_ref[pl.ds(k0, ts), :]
        acc = jnp.zeros((ts, tq), f32)
        for h in range(IDX_HEADS):
            lg = _dot_nt(ikb, iq_ref[h])
            acc = acc + jnp.maximum(lg, 0.0) * iwt[h:h + 1, :]
        kpos = k0 + sub
        sc = jnp.where(kpos <= qpos, acc, NEG)
        sc = jnp.where(kpos < FRAME_OFF, -jnp.inf, sc)
        key = _sortable(sc)
        key_ref[pl.ds(k0, ts), :] = key
        half_ref[pl.ds(k0, ts), :] = lax.shift_right_arithmetic(key, 16).astype(i16)
        return carry

    lax.fori_loop(0, nkb, score_blk, 0)

    n_acc = 4

    def count(pred_fn):
        sub8 = lax.broadcasted_iota(i32, (8, 1), 0)

        def blk(kb, cnts):
            k0 = pl.multiple_of(kb * ts, ts)
            cnts = list(cnts)
            blk_ref = key_ref.at[pl.ds(k0, ts)]
            for j in range(ts // 8):
                kv = blk_ref[j * 8:(j + 1) * 8, :]
                cnts[j % n_acc] = cnts[j % n_acc] + jnp.where(pred_fn(kv, k0 + j * 8 + sub8), 1, 0)
            return tuple(cnts)
        cnts = lax.fori_loop(0, nkb, blk, tuple(jnp.zeros((8, tq), i32) for _ in range(n_acc)))
        return jnp.sum(sum(cnts[1:], cnts[0]), axis=0, keepdims=True)

    def count_half_ge(cand):
        one, zero = jnp.int16(1), jnp.int16(0)

        def blk(kb, cnts):
            k0 = pl.multiple_of(kb * ts, ts)
            cnts = list(cnts)
            blk_ref = half_ref.at[pl.ds(k0, ts)]
            for j in range(ts // 16):
                hv = blk_ref[j * 16:(j + 1) * 16, :]
                cnts[j % n_acc] = cnts[j % n_acc] + jnp.where(hv >= cand, one, zero)
            return tuple(cnts)
        cnts = lax.fori_loop(0, nkb, blk, tuple(jnp.zeros((16, tq), i16) for _ in range(n_acc)))
        return jnp.sum(sum(cnts[1:], cnts[0]).astype(i32), axis=0, keepdims=True)

    def search_half(neg_half, n_start):
        def body(it, c):
            tau, n_tau = c
            t2 = tau | lax.shift_left(jnp.int32(1), 15 - it)
            cand = t2 + HALF_MIN
            n2 = count_half_ge(cand.astype(i16)) + jnp.where(cand <= neg_half, rem, 0)
            ok = n2 >= topk
            return jnp.where(ok, t2, tau), jnp.where(ok, n2, n_tau)

        return lax.fori_loop(0, 16, body, (jnp.zeros((1, tq), i32), n_start))

    tau_hi, n_hi = search_half(jnp.full((1, tq), _KEY_NEG_HI, i32), jnp.full((1, tq), lp, i32))
    t_hi = tau_hi + HALF_MIN

    def low_blk(kb, carry):
        k0 = pl.multiple_of(kb * ts, ts)
        key = key_ref[pl.ds(k0, ts), :]
        hi = lax.shift_right_arithmetic(key, 16)
        lo = (key & (HALF_SPAN - 1)) + HALF_MIN
        lo = jnp.where(hi > t_hi, HALF_MAX, jnp.where(hi == t_hi, lo, HALF_MIN))
        half_ref[pl.ds(k0, ts), :] = lo.astype(i16)
        return carry

    lax.fori_loop(0, nkb, low_blk, 0)
    neg_lo = jnp.where(_KEY_NEG_HI > t_hi, HALF_MAX, jnp.where(_KEY_NEG_HI == t_hi, _KEY_NEG_LO, HALF_MIN))
    tau_lo, n_ge = search_half(neg_lo, n_hi)
    thr = lax.shift_left(t_hi, 16) | tau_lo
    has_tie = jnp.max(jnp.where(n_ge > topk, 1, 0)) > 0

    def tie_cut():
        n_gt = (count(lambda kv, kpos: kv > thr) + jnp.where(thr < _KEY_NEG, rem, 0))
        need = topk - n_gt

        def bis2(it, cut):
            bit = lax.shift_left(jnp.int32(1), 14 - it)
            c2 = cut | bit
            n = count(lambda kv, kpos: jnp.logical_and(kv == thr, kpos < c2))
            return jnp.where(n < need, c2, cut)
        return lax.fori_loop(0, 15, bis2, jnp.zeros((1, tq), i32))

    cut = lax.cond(has_tie, tie_cut, lambda: jnp.full((1, tq), INT_MAX, i32))

    def bias_blk(kb, carry):
        k0 = pl.multiple_of(kb * ts, ts)
        kv = key_ref[pl.ds(k0, ts), :]
        interior = jnp.logical_and(k0 >= FRAME_OFF, k0 + ts - 1 <= i * tq)

        @pl.when(jnp.logical_and(interior, jnp.logical_not(has_tie)))
        def _():
            bias_ref[pl.ds(k0, ts), :] = jnp.where(kv >= thr, 0.0, NEG).astype(bias_ref.dtype)

        @pl.when(jnp.logical_not(jnp.logical_and(interior, jnp.logical_not(has_tie))))
        def _():
            kpos = k0 + sub
            sel = jnp.logical_or(kv > thr, jnp.logical_and(kv == thr, kpos <= cut))
            vis = jnp.logical_and(kpos <= qpos, kpos >= FRAME_OFF)
            bias_ref[pl.ds(k0, ts), :] = jnp.where(jnp.logical_and(sel, vis), 0.0, NEG).astype(bias_ref.dtype)
        return carry

    lax.fori_loop(0, nkb, bias_blk, 0)

    def fill_blk(kb, carry):
        k0 = pl.multiple_of(kb * ts, ts)
        bias_ref[pl.ds(k0, ts), :] = jnp.full((ts, tq), NEG, bias_ref.dtype)
        return carry

    lax.fori_loop(nkb, lp // ts, fill_blk, 0)


def dsa_select(iq, ik, iwt, topk):
    nh, lp, _ = iq.shape
    assert lp < 2 ** 15
    tq = 256
    ts = _pick(lp, (640, 256, 128))
    kern = functools.partial(_select_kernel, tq=tq, ts=ts, lp=lp, topk=topk)
    return pl.pallas_call(
        kern,
        grid=(lp // tq,),
        in_specs=[pl.BlockSpec((nh, tq, HEAD_DIM), lambda i: (0, i, 0)),
                  pl.BlockSpec((lp, HEAD_DIM), lambda i: (0, 0), pipeline_mode=pl.Buffered(1)),
                  pl.BlockSpec((nh, tq), lambda i: (0, i))],
        out_specs=pl.BlockSpec((lp, tq), lambda i: (0, i)),
        out_shape=jax.ShapeDtypeStruct((lp, lp), bf16),
        scratch_shapes=[pltpu.VMEM((lp, tq), i32), pltpu.VMEM((lp, tq), i16)],
        compiler_params=_cparams(("parallel",)),
        name="dsa_select",
    )(iq, ik, iwt)


V_AUG = HEAD_DIM + 16


def _attn_kernel(q_ref, k_ref, vt_ref, b_ref, o_ref, m_ref, acc_ref, *, tq, ts, bpt):
    i = pl.program_id(0)
    nkb = ((i + 1) * tq + ts - 1) // ts
    group = ATT_HEADS // ATT_KV_HEADS
    kv_groups = range(ATT_KV_HEADS)
    qpos = i * tq + lax.broadcasted_iota(i32, (tq, 1), 0)
    eye = (lax.broadcasted_iota(i32, (tq, tq), 0)
           == lax.broadcasted_iota(i32, (tq, tq), 1)).astype(bf16)
    qa = [jnp.concatenate(
        [jnp.concatenate([q_ref[:, (g * group + r) * HEAD_DIM:(g * group + r + 1) * HEAD_DIM], eye],
                         axis=1) for r in range(group)], axis=0) for g in kv_groups]
    m_ref[...] = jnp.full(m_ref.shape, -jnp.inf, f32)
    acc_ref[...] = jnp.zeros_like(acc_ref)

    def scores(g, k0):
        ka = jnp.concatenate([k_ref[pl.ds(k0, ts), g * HEAD_DIM:(g + 1) * HEAD_DIM],
                              b_ref[pl.ds(k0, ts), :]], axis=1)
        return _dot_nt(ka, qa[g]).astype(bf16)

    def accumulate(g, k0, st):
        vt_ = vt_ref[g, :, pl.ds(k0, ts)]
        ps, alphas = [], []
        for r in range(group):
            sr = st[:, r * tq:(r + 1) * tq]
            m_old = m_ref[g, r:r + 1, :]
            m_new = jnp.maximum(m_old, jnp.max(sr, axis=0, keepdims=True).astype(f32))
            m_ref[g, r:r + 1, :] = m_new
            ps.append(jnp.exp2(sr - m_new.astype(bf16)))
            alphas.append(jnp.exp2(m_old - m_new))
        pt = jnp.concatenate(ps, axis=1)
        acc_ref[g] = jnp.concatenate(alphas, axis=1) * acc_ref[g] + _dot(vt_, pt)

    def make_trip(nblk, first_block):
        def trip(kp, carry):
            k0s = [pl.multiple_of((first_block + kp * nblk + b) * ts, ts) for b in range(nblk)]
            work = [(g, kk) for kk in k0s for g in kv_groups]
            st = scores(*work[0])
            for n in range(len(work)):
                st_next = scores(*work[n + 1]) if n + 1 < len(work) else None
                accumulate(*work[n], st)
                st = st_next
            return carry
        return trip

    n_full = nkb // bpt
    lax.fori_loop(0, n_full, make_trip(bpt, 0), 0)
    if bpt > 1:
        lax.fori_loop(0, nkb - n_full * bpt, make_trip(1, n_full * bpt), 0)
    for g in kv_groups:
        for r in range(group):
            ot = (acc_ref[g, 0:HEAD_DIM, r * tq:(r + 1) * tq]
                  / acc_ref[g, HEAD_DIM:HEAD_DIM + 1, r * tq:(r + 1) * tq])
            orr = jnp.where(qpos >= FRAME_OFF, ot.T, 0.0)
            o_ref[:, (g * group + r) * HEAD_DIM:(g * group + r + 1) * HEAD_DIM] = orr.astype(o_ref.dtype)


def dsa_attention(q, k, vt_aug, bias_t):
    lp = q.shape[0]
    tq = 128
    ts = _pick(lp, (1280, 640, 256, 128))
    bpt = 2
    group = ATT_HEADS // ATT_KV_HEADS
    kern = functools.partial(_attn_kernel, tq=tq, ts=ts, bpt=bpt)
    resident = pl.Buffered(1)
    return pl.pallas_call(
        kern,
        grid=(lp // tq,),
        in_specs=[pl.BlockSpec((tq, ATT_Q_W), lambda i: (i, 0)),
                  pl.BlockSpec((lp, ATT_KV_W), lambda i: (0, 0), pipeline_mode=resident),
                  pl.BlockSpec((ATT_KV_HEADS, V_AUG, lp), lambda i: (0, 0, 0), pipeline_mode=resident),
                  pl.BlockSpec((lp, tq), lambda i: (0, i))],
        out_specs=pl.BlockSpec((tq, ATT_Q_W), lambda i: (i, 0)),
        out_shape=jax.ShapeDtypeStruct((lp, ATT_Q_W), bf16),
        scratch_shapes=[pltpu.VMEM((ATT_KV_HEADS, group, tq), f32),
                        pltpu.VMEM((ATT_KV_HEADS, V_AUG, group * tq), f32)],
        compiler_params=_cparams(("parallel",)),
        name="dsa_attention",
    )(q, k, vt_aug, bias_t)


def _ffn_up_kernel(a_ref, ah_ref, wg_ref, wv_ref, cg_ref, cv_ref, bg_ref, bv_ref, o_ref,
                   eg_ref, ev_ref, *, tm):
    a = _halo_rows(a_ref, ah_ref, pl.program_id(0))
    eg_ref[...] = _dot(a, wg_ref[...])
    ev_ref[...] = _dot(a, wv_ref[...])

    def conv(ext_ref, w_ref, b_ref):
        w = w_ref[...]
        y = ext_ref[HALO:HALO + tm, :] * w[FFN_CONV - 1:FFN_CONV, :]
        for j in range(FFN_CONV - 1):
            s0 = HALO - (FFN_CONV - 1) + j
            y = y + ext_ref[s0:s0 + tm, :] * w[j:j + 1, :]
        return y + b_ref[...]

    gate = conv(eg_ref, cg_ref, bg_ref)
    val = conv(ev_ref, cv_ref, bv_ref)
    o_ref[...] = (gate * _sigmoid(gate) * val).astype(o_ref.dtype)


def ffn_up(u, w_up, conv_w, conv_b):
    lp, kdim = u.shape
    tm = _pick(lp, (1280, 640, 256, 128))
    tn = 512
    nc = D_FF // tn
    kern = functools.partial(_ffn_up_kernel, tm=tm)

    def wspec(off):
        return pl.BlockSpec((kdim, tn), lambda i, c: (0, c + off))

    def cspec(off):
        return pl.BlockSpec((FFN_CONV, tn), lambda i, c: (0, c + off))

    def bspec(off):
        return pl.BlockSpec((1, tn), lambda i, c: (0, c + off))

    b2 = conv_b.reshape(1, 2 * D_FF)
    return pl.pallas_call(
        kern,
        grid=(lp // tm, nc),
        in_specs=[pl.BlockSpec((tm, kdim), lambda i, c: (i, 0)),
                  pl.BlockSpec((HALO, kdim), lambda i, c: (jnp.maximum(i * (tm // HALO) - 1, 0), 0)),
                  wspec(0), wspec(nc), cspec(0), cspec(nc), bspec(0), bspec(nc)],
        out_specs=pl.BlockSpec((tm, tn), lambda i, c: (i, c)),
        out_shape=jax.ShapeDtypeStruct((lp, D_FF), bf16),
        scratch_shapes=[pltpu.VMEM((HALO + tm, tn), f32)] * 2,
        compiler_params=_cparams(("parallel", "arbitrary")),
        name="ffn_up",
    )(u, u, w_up, w_up, conv_w, conv_w, b2, b2)


def _split_w_in(w):
    o = 0
    parts = {}
    for name, width in (("gq", GDN_QK_W), ("gk", GDN_QK_W), ("gv", GDN_V_W), ("gz", GDN_V_W),
                        ("gb", GDN_V_HEADS), ("ga", GDN_V_HEADS), ("aq", ATT_Q_W), ("ak", ATT_KV_W),
                        ("av", ATT_KV_W), ("iq", IDX_Q_W), ("ik", HEAD_DIM), ("iw", IDX_HEADS),
                        ("gate_gdn", D_MODEL), ("gate_att", D_MODEL)):
        parts[name] = w[:, o:o + width]
        o += width
    cat = lambda names: jnp.concatenate([parts[n] for n in names], axis=1)
    small = cat(("gb", "ga", "iw"))
    small = jnp.pad(small, ((0, 0), (0, LANES - small.shape[1])))
    groups = dict(qkv=cat(("gq", "gk", "gv")), aq=parts["aq"], iq=parts["iq"], kv=cat(("ak", "ik")),
                  plain=cat(("gz", "gate_gdn", "gate_att", "av")), small=small)
    return {k: v.astype(bf16) for k, v in groups.items()}


def _layer(h0, p, lp, topk):
    tm = _pick(lp, (640, 256, 128))
    tm_big = _pick(lp, (1280, 640, 256, 128))
    wp = _split_w_in(p["w_in"])

    def proj(w, **kw):
        return matmul(u1, w, tm=tm_big, tk=D_MODEL, **kw)

    u1 = rms_rows(h0, p["mix_pre_g"])
    tabs = rope_tables(lp)
    tab_specs = (((tm_big, HEAD_DIM), lambda i, j: (i, 0)),) * 3
    qkv = gdn_proj(u1, wp["qkv"], p["gdn_conv_w"])
    (aq,) = proj(wp["aq"], tn=1024, epilogue=_make_ep_rope(HEAD_DIM ** -0.5 * LOG2E, False),
                 extra=tabs, extra_specs=tab_specs,
                 out_shapes=[jax.ShapeDtypeStruct((lp, ATT_Q_W), bf16)], name="proj_aq")
    iq_heads = 1024 // HEAD_DIM
    (iq,) = proj(wp["iq"], tn=1024, epilogue=_make_ep_rope(1.0, True), extra=tabs, extra_specs=tab_specs,
                 out_shapes=[jax.ShapeDtypeStruct((IDX_HEADS, lp, HEAD_DIM), bf16)],
                 out_block_specs=[((iq_heads, tm_big, HEAD_DIM), lambda i, j: (j, i, 0))], name="proj_iq")
    ak, ik = proj(wp["kv"], tn=ATT_KV_W + HEAD_DIM, epilogue=_ep_rope_kv, extra=tabs, extra_specs=tab_specs,
                  out_shapes=[jax.ShapeDtypeStruct((lp, ATT_KV_W), bf16),
                              jax.ShapeDtypeStruct((lp, HEAD_DIM), bf16)],
                  out_block_specs=[((tm_big, ATT_KV_W), lambda i, j: (i, 0)),
                                   ((tm_big, HEAD_DIM), lambda i, j: (i, 0))], name="proj_kv")
    (plain,) = proj(wp["plain"], tn=768, epilogue=_ep_cast,
                    out_shapes=[jax.ShapeDtypeStruct((lp, PLAIN_W), bf16)], name="proj_plain")
    (small,) = proj(wp["small"], tn=LANES, epilogue=_ep_cast,
                    out_shapes=[jax.ShapeDtypeStruct((lp, LANES), f32)], name="proj_small")
    gb = small[:, 0:GDN_V_HEADS]
    ga = small[:, GDN_V_HEADS:2 * GDN_V_HEADS]
    iw = small[:, 2 * GDN_V_HEADS:2 * GDN_V_HEADS + IDX_HEADS]

    beta, gcum = gdn_gates(gb, ga, p["gdn_a_log"], p["gdn_dt_bias"])
    o_gdn = gdn_core(qkv, plain, p["gdn_norm_g"], gcum.T[:, None, :], beta.T[:, None, :])

    avt = plain[:, COL_AV:COL_AV + ATT_KV_W].T.reshape(ATT_KV_HEADS, HEAD_DIM, lp)
    avt = jnp.concatenate([avt, jnp.ones((ATT_KV_HEADS, V_AUG - HEAD_DIM, lp), bf16)], axis=1)
    bias_t = dsa_select(iq, ik, iw.T, topk)
    o_att = dsa_attention(aq, ak, avt, bias_t)

    tn = 1024
    gate_spec = lambda col0: ((tm_big, tn), (lambda i, j, c=col0 // tn: (i, c + j)))
    (m1,) = matmul(o_gdn, p["w_branch_gdn"].astype(bf16), tm=tm_big, tn=tn, tk=2048,
                   epilogue=_ep_gate, extra=(plain,), extra_specs=(gate_spec(COL_GATE_GDN),),
                   out_shapes=[jax.ShapeDtypeStruct((lp, D_MODEL), f32)], name="branch_gdn")
    (merged,) = matmul(o_att, p["w_branch_att"].astype(bf16), tm=tm_big, tn=tn, tk=2048,
                       epilogue=_ep_gate_add, extra=(plain, m1),
                       extra_specs=(gate_spec(COL_GATE_ATT), ((tm_big, tn), lambda i, j: (i, j))),
                       out_shapes=[jax.ShapeDtypeStruct((lp, D_MODEL), bf16)], name="branch_att")
    row_spec = ((tm, D_MODEL), lambda i, j: (i, 0))
    vec_spec = ((1, D_MODEL), lambda i, j: (0, 0))
    h1, u2 = matmul(merged, p["w_out"].astype(bf16), tm=tm, tn=D_MODEL, tk=D_MODEL,
                    epilogue=_ep_res_norm2,
                    extra=(h0, p["mix_post_g"].reshape(1, D_MODEL), p["ffn_pre_g"].reshape(1, D_MODEL)),
                    extra_specs=(row_spec, vec_spec, vec_spec),
                    out_shapes=[jax.ShapeDtypeStruct((lp, D_MODEL), f32),
                                jax.ShapeDtypeStruct((lp, D_MODEL), bf16)], name="w_out")

    act = ffn_up(u2, p["w_up"].astype(bf16), p["ffn_conv_w"], p["ffn_conv_b"])
    (h2,) = matmul(act, p["w_down"].astype(bf16), tm=tm, tn=D_MODEL, tk=2048, epilogue=_ep_res_norm,
                   extra=(h1, p["ffn_post_g"].reshape(1, D_MODEL)),
                   extra_specs=(row_spec, vec_spec),
                   out_shapes=[jax.ShapeDtypeStruct((lp, D_MODEL), f32)], name="w_down")
    return h2


def kernel(x, meta_tokens, mix_pre_g, w_in, gdn_conv_w, gdn_a_log, gdn_dt_bias, gdn_norm_g,
           w_branch_gdn, w_branch_att, w_out, mix_post_g, ffn_pre_g, w_up, ffn_conv_w,
           ffn_conv_b, w_down, ffn_post_g):
    batch, seq, d = x.shape
    assert batch == 1 and d == D_MODEL
    lp = FRAME_X0 + seq
    topk = min(TOPK_MAX, (N_META + seq) // 4)
    h = jnp.concatenate([jnp.zeros((FRAME_OFF, d), x.dtype), meta_tokens.astype(x.dtype), x[0]], axis=0)
    for i in range(w_in.shape[0]):
        p = dict(mix_pre_g=mix_pre_g[i], w_in=w_in[i], gdn_conv_w=gdn_conv_w[i], gdn_a_log=gdn_a_log[i],
                 gdn_dt_bias=gdn_dt_bias[i], gdn_norm_g=gdn_norm_g[i], w_branch_gdn=w_branch_gdn[i],
                 w_branch_att=w_branch_att[i], w_out=w_out[i], mix_post_g=mix_post_g[i],
                 ffn_pre_g=ffn_pre_g[i], w_up=w_up[i], ffn_conv_w=ffn_conv_w[i],
                 ffn_conv_b=ffn_conv_b[i], w_down=w_down[i], ffn_post_g=ffn_post_g[i])
        h = _layer(h, p, lp, topk)
    return h[FRAME_X0:][None]
```

```python
import functools
import struct

import jax
import jax.numpy as jnp
from jax import lax
from jax.experimental import pallas as pl
from jax.experimental.pallas import tpu as pltpu

f32 = jnp.float32
bf16 = jnp.bfloat16
i32 = jnp.int32
i16 = jnp.int16

D_MODEL = 2048
N_META = 16
EPS = 1e-6
GDN_QK_HEADS = 16
GDN_V_HEADS = 32
HEAD_DIM = 128
GDN_CONV = 4
ATT_HEADS = 16
ATT_KV_HEADS = 2
IDX_HEADS = 16
TOPK_MAX = 256
NEG = -1e30
LOG2E = 1.4426950408889634
ROPE_THETA = 500000.0
ROPE_DIMS = HEAD_DIM // 4
D_FF = 3 * D_MODEL
FFN_CONV = 3
GDN_QK_W = GDN_QK_HEADS * HEAD_DIM
GDN_V_W = GDN_V_HEADS * HEAD_DIM
ATT_Q_W = ATT_HEADS * HEAD_DIM
ATT_KV_W = ATT_KV_HEADS * HEAD_DIM
IDX_Q_W = IDX_HEADS * HEAD_DIM

FRAME_X0 = 256
FRAME_OFF = FRAME_X0 - N_META
CHUNK = 128

LANES = 128
VMEM_LIMIT = 56 * 1024 * 1024

COL_GZ = 0
COL_GATE_GDN = COL_GZ + GDN_V_W
COL_GATE_ATT = COL_GATE_GDN + D_MODEL
COL_AV = COL_GATE_ATT + D_MODEL
PLAIN_W = COL_AV + ATT_KV_W


def _cparams(sem):
    return pltpu.CompilerParams(dimension_semantics=sem, vmem_limit_bytes=VMEM_LIMIT)


def _pick(n, cands):
    for c in cands:
        if n % c == 0:
            return c
    raise ValueError(f"no tile for {n} in {cands}")


def _sigmoid(x):
    return 1.0 / (1.0 + jnp.exp(-x))


def _dot(a, b):
    return jnp.dot(a, b, preferred_element_type=f32)


def _dot_nt(a, b):
    return lax.dot_general(a, b, (((1,), (1,)), ((), ())), preferred_element_type=f32)


def _rms_rows_kernel(h_ref, g_ref, o_ref):
    h = h_ref[...]
    y = h * lax.rsqrt(jnp.mean(h * h, axis=-1, keepdims=True) + EPS)
    o_ref[...] = (y * g_ref[...]).astype(o_ref.dtype)


def rms_rows(h, g):
    lp, d = h.shape
    tm = _pick(lp, (640, 256, 128))
    return pl.pallas_call(
        _rms_rows_kernel,
        grid=(lp // tm,),
        in_specs=[pl.BlockSpec((tm, d), lambda i: (i, 0)),
                  pl.BlockSpec((1, d), lambda i: (0, 0))],
        out_specs=pl.BlockSpec((tm, d), lambda i: (i, 0)),
        out_shape=jax.ShapeDtypeStruct((lp, d), bf16),
        compiler_params=_cparams(("parallel",)),
        name="rms_rows",
    )(h, g.reshape(1, d))


def _mm_kernel(*refs, nk, n_extra, n_out, epilogue, w_nt):
    a_ref, w_ref = refs[0], refs[1]
    dot = _dot_nt if w_nt else _dot
    extra = refs[2:2 + n_extra]
    outs = refs[2 + n_extra:2 + n_extra + n_out]
    if nk == 1:
        epilogue(dot(a_ref[...], w_ref[...]), extra, outs)
        return
    acc_ref = refs[-1]
    k = pl.program_id(2)

    @pl.when(k == 0)
    def _():
        acc_ref[...] = jnp.zeros_like(acc_ref)

    acc_ref[...] += dot(a_ref[...], w_ref[...])

    @pl.when(k == nk - 1)
    def _():
        epilogue(acc_ref[...], extra, outs)


def matmul(a, w, *, tm, tn, tk, epilogue, extra=(), extra_specs=(), out_shapes, out_block_specs=None,
           a_col0=0, w_nt=False, name):
    m = a.shape[0]
    kdim, n = w.shape[::-1] if w_nt else w.shape
    nk = kdim // tk
    assert m % tm == 0 and n % tn == 0 and kdim % tk == 0
    w_mode = dict(pipeline_mode=pl.Buffered(1)) if (nk == 1 and n == tn) else {}
    in_specs = [pl.BlockSpec((tm, tk), lambda i, j, k: (i, a_col0 + k)),
                (pl.BlockSpec((tn, tk), lambda i, j, k: (j, k), **w_mode) if w_nt else
                 pl.BlockSpec((tk, tn), lambda i, j, k: (k, j), **w_mode))]
    in_specs += [pl.BlockSpec(bs, (lambda i, j, k, f=f: f(i, j))) for bs, f in extra_specs]
    if out_block_specs is None:
        out_block_specs = [((tm, tn), lambda i, j: (i, j))] * len(out_shapes)
    out_specs = [pl.BlockSpec(bs, (lambda i, j, k, f=f: f(i, j))) for bs, f in out_block_specs]
    scratch = [] if nk == 1 else [pltpu.VMEM((tm, tn), f32)]
    kern = functools.partial(_mm_kernel, nk=nk, n_extra=len(extra), n_out=len(out_shapes),
                             epilogue=epilogue, w_nt=w_nt)
    res = pl.pallas_call(
        kern,
        grid=(m // tm, n // tn, nk),
        in_specs=in_specs,
        out_specs=out_specs,
        out_shape=out_shapes,
        scratch_shapes=scratch,
        compiler_params=_cparams(("parallel", "parallel", "arbitrary")),
        name=name,
    )(a, w, *extra)
    return res


def _ep_cast(acc, extra, outs):
    outs[0][...] = acc.astype(outs[0].dtype)


def _ep_gate(acc, extra, outs):
    g = extra[0][...].astype(f32)
    outs[0][...] = (_sigmoid(g) * acc).astype(outs[0].dtype)


def _ep_gate_add(acc, extra, outs):
    g = extra[0][...].astype(f32)
    outs[0][...] = (extra[1][...].astype(f32) + _sigmoid(g) * acc).astype(outs[0].dtype)


def _rms(t, g):
    return t * lax.rsqrt(jnp.mean(t * t, axis=-1, keepdims=True) + EPS) * g


def _ep_res_norm2(acc, extra, outs):
    h_ref, g_ref, g2_ref = extra
    h1 = h_ref[...] + _rms(acc, g_ref[...])
    outs[0][...] = h1
    outs[1][...] = _rms(h1, g2_ref[...]).astype(outs[1].dtype)


def _ep_res_norm(acc, extra, outs):
    h_ref, g_ref = extra
    outs[0][...] = h_ref[...] + _rms(acc, g_ref[...])


HALO = 16


def _halo_rows(a_ref, ah_ref, i):
    halo = ah_ref[...]
    halo = jnp.where(i > 0, halo, jnp.zeros_like(halo))
    return jnp.concatenate([halo, a_ref[...]], axis=0)


def _gdn_proj_kernel(a_ref, ah_ref, w_ref, cw_ref, o_ref, ext_ref, *, tm, tn):
    c = pl.program_id(1)
    ext_ref[...] = _dot_nt(_halo_rows(a_ref, ah_ref, pl.program_id(0)), w_ref[...])
    w = cw_ref[...]
    y = ext_ref[HALO:HALO + tm, :] * w[GDN_CONV - 1:GDN_CONV, :]
    for j in range(GDN_CONV - 1):
        s0 = HALO - (GDN_CONV - 1) + j
        y = y + ext_ref[s0:s0 + tm, :] * w[j:j + 1, :]
    s = y * _sigmoid(y)
    is_q = c < (GDN_QK_W // tn)
    is_qk = c < (2 * GDN_QK_W // tn)
    qscale = jnp.where(is_q, HEAD_DIM ** -0.5, 1.0).astype(f32)
    for hh in range(tn // HEAD_DIM):
        seg = s[:, hh * HEAD_DIM:(hh + 1) * HEAD_DIM]
        r = lax.rsqrt(jnp.sum(seg * seg, axis=-1, keepdims=True) + EPS) * qscale
        fac = jnp.where(is_qk, r, 1.0)
        o_ref[:, hh * HEAD_DIM:(hh + 1) * HEAD_DIM] = (seg * fac).astype(o_ref.dtype)


def gdn_proj(u, w_qkv, conv_w):
    lp, kdim = u.shape
    width = w_qkv.shape[0]
    tm = _pick(lp, (1280, 640, 256, 128))
    tn = 1024
    kern = functools.partial(_gdn_proj_kernel, tm=tm, tn=tn)
    return pl.pallas_call(
        kern,
        grid=(lp // tm, width // tn),
        in_specs=[pl.BlockSpec((tm, kdim), lambda i, c: (i, 0)),
                  pl.BlockSpec((HALO, kdim), lambda i, c: (jnp.maximum(i * (tm // HALO) - 1, 0), 0)),
                  pl.BlockSpec((tn, kdim), lambda i, c: (c, 0)),
                  pl.BlockSpec((GDN_CONV, tn), lambda i, c: (0, c))],
        out_specs=pl.BlockSpec((tm, tn), lambda i, c: (i, c)),
        out_shape=jax.ShapeDtypeStruct((lp, width), bf16),
        scratch_shapes=[pltpu.VMEM((HALO + tm, tn), f32)],
        compiler_params=_cparams(("parallel", "arbitrary")),
        name="gdn_proj",
    )(u, u, w_qkv, conv_w)


def _gdn_gates_kernel(b_ref, a_ref, alog_ref, dt_ref, beta_ref, gcum_ref, *, tm):
    i = pl.program_id(0)
    rows = i * tm + lax.broadcasted_iota(i32, (tm, 1), 0)
    valid = rows >= FRAME_OFF
    beta_ref[...] = jnp.where(valid, _sigmoid(b_ref[...]), 0.0)
    a = a_ref[...] + dt_ref[...]
    sp = jnp.maximum(a, 0.0) + jnp.log1p(jnp.exp(-jnp.abs(a)))
    g = jnp.where(valid, -jnp.exp(alog_ref[...]) * sp, 0.0)
    tri = (lax.broadcasted_iota(i32, (CHUNK, CHUNK), 0)
           >= lax.broadcasted_iota(i32, (CHUNK, CHUNK), 1)).astype(f32)
    for c in range(tm // CHUNK):
        rs = slice(c * CHUNK, (c + 1) * CHUNK)
        gcum_ref[rs, :] = jnp.dot(tri, g[rs, :], preferred_element_type=f32,
                                  precision=lax.Precision.HIGHEST)


def gdn_gates(gb, ga, a_log, dt_bias):
    lp, nh = gb.shape
    tm = _pick(lp, (1280, 640, 256, 128))
    spec = pl.BlockSpec((tm, nh), lambda i: (i, 0))
    vec = pl.BlockSpec((1, nh), lambda i: (0, 0))
    return pl.pallas_call(
        functools.partial(_gdn_gates_kernel, tm=tm),
        grid=(lp // tm,),
        in_specs=[spec, spec, vec, vec],
        out_specs=[spec, spec],
        out_shape=[jax.ShapeDtypeStruct((lp, nh), f32)] * 2,
        compiler_params=_cparams(("parallel",)),
        name="gdn_gates",
    )(gb, ga, a_log.reshape(1, nh), dt_bias.reshape(1, nh))


def _block_mask(size):
    r = lax.broadcasted_iota(i32, (CHUNK, CHUNK), 0) // size
    c = lax.broadcasted_iota(i32, (CHUNK, CHUNK), 1) // size
    return r == c


def _unit_lower_inverse_many(mats, tick):
    row = lax.broadcasted_iota(i32, (CHUNK, CHUNK), 0)
    col = lax.broadcasted_iota(i32, (CHUNK, CHUNK), 1)
    eye = (row == col).astype(f32)
    base = 8
    m_prev = _block_mask(base)
    ads = [jnp.where(m_prev, a, 0.0) for a in mats]
    adbs = [ad.astype(bf16) for ad in ads]
    a2s = [_dot(x, x) for x in adbs]
    tick()
    a2bs = [x.astype(bf16) for x in a2s]
    a4s = [_dot(x, x) for x in a2bs]
    tick()
    xs = [_dot((eye - ad).astype(bf16), (eye + a2).astype(bf16)) for ad, a2 in zip(ads, a2s)]
    tick()
    xs = [_dot(x.astype(bf16), (eye + a4).astype(bf16)) for x, a4 in zip(xs, a4s)]
    tick()
    size = base * 2
    while size <= CHUNK:
        m_cur = _block_mask(size)
        off_diag = jnp.logical_and(m_cur, jnp.logical_not(m_prev))
        bs = [jnp.where(off_diag, a, 0.0).astype(bf16) for a in mats]
        xbs = [x.astype(bf16) for x in xs]
        ys = [_dot(xb, b).astype(bf16) for xb, b in zip(xbs, bs)]
        tick()
        xs = [x - _dot(y, xb) for x, y, xb in zip(xs, ys, xbs)]
        tick()
        m_prev = m_cur
        size *= 2
    return xs


def _gdn_kernel(q_ref, k_ref, v_ref, z_ref, ng_ref, gr0_ref, gr1_ref, br0_ref, br1_ref,
                o_ref, s_ref, u_ref, wq_ref, ak_ref, eg_ref, egl_ref, *, rb):
    r = pl.program_id(1)

    @pl.when(r == 0)
    def _():
        for ref in (s_ref, u_ref, wq_ref, ak_ref, eg_ref, egl_ref):
            ref[...] = jnp.zeros_like(ref)

    grs, brs = (gr0_ref, gr1_ref), (br0_ref, br1_ref)
    row = lax.broadcasted_iota(i32, (CHUNK, CHUNK), 0)
    col = lax.broadcasted_iota(i32, (CHUNK, CHUNK), 1)
    incl = row >= col
    strict = row > col
    ng = ng_ref[...]
    nchunk = rb // CHUNK
    heads = range(2)

    def rows(c):
        return slice(c * CHUNK, (c + 1) * CHUNK)

    def lanes(e):
        return slice(e * HEAD_DIM, (e + 1) * HEAD_DIM)

    chains = [(c, e) for c in range(nchunk) for e in heads]

    def recurrence():
        state = [s_ref[e] for e in heads]
        for c in range(nchunk):
            n = [chains.index((c, e)) for e in heads]
            sb = [state[e].astype(bf16) for e in heads]
            ws = [_dot(wq_ref[n[e]], sb[e]) for e in heads]
            yield
            vb = [(u_ref[n[e]] - ws[e][:CHUNK]).astype(bf16) for e in heads]
            av = [_dot(ak_ref[n[e]], vb[e]) for e in heads]
            yield
            for e in heads:
                o = eg_ref[n[e]] * ws[e][CHUNK:] + av[e][:CHUNK]
                state[e] = state[e] * egl_ref[n[e]] + av[e][CHUNK:]
                z = z_ref[rows(c), lanes(e)].astype(f32)
                o_ref[rows(c), lanes(e)] = (_rms(o, ng) * (z * _sigmoid(z))).astype(o_ref.dtype)
        for e in heads:
            s_ref[e] = state[e]

    rec = recurrence()

    def tick():
        next(rec, None)

    qs = [q_ref[rows(c), :] for c in range(nchunk)]
    ks = [k_ref[rows(c), :] for c in range(nchunk)]
    kts = [k.astype(f32).T for k in ks]
    kks = [_dot_nt(k, k) for k in ks]
    qks = [_dot_nt(q, k) for q, k in zip(qs, ks)]
    tick()
    def per_row(v):
        return jnp.broadcast_to(v, (CHUNK, CHUNK)).T

    gr = {ce: grs[ce[1]][:, rows(ce[0])] for ce in chains}
    br = {ce: brs[ce[1]][:, rows(ce[0])] for ce in chains}
    gc = {ce: per_row(gr[ce]) for ce in chains}
    bc = {ce: per_row(br[ce]) for ce in chains}
    dec = {ce: jnp.where(incl, jnp.exp(jnp.where(incl, gc[ce] - gr[ce], 0.0)), 0.0) for ce in chains}
    amat = [jnp.where(strict, bc[ce] * kks[ce[0]] * dec[ce], 0.0) for ce in chains]
    tinv = dict(zip(chains, _unit_lower_inverse_many(amat, tick)))
    u = {ce: _dot((tinv[ce] * br[ce]).astype(bf16), v_ref[rows(ce[0]), lanes(ce[1])]) for ce in chains}
    tick()
    w = {ce: _dot((tinv[ce] * (br[ce] * jnp.exp(gr[ce]))).astype(bf16), ks[ce[0]]) for ce in chains}
    tick()
    g_last = {ce: gr[ce][:, CHUNK - 1:CHUNK] for ce in chains}
    wq = {ce: jnp.concatenate([w[ce].astype(bf16), qs[ce[0]]], axis=0) for ce in chains}
    ak = {ce: jnp.concatenate(
        [jnp.where(incl, qks[ce[0]] * dec[ce], 0.0).astype(bf16),
         (kts[ce[0]] * jnp.exp(g_last[ce] - gr[ce])).astype(bf16)], axis=0) for ce in chains}
    for _ in rec:
        pass
    for n, ce in enumerate(chains):
        u_ref[n] = u[ce]
        wq_ref[n] = wq[ce]
        ak_ref[n] = ak[ce]
        eg_ref[n] = jnp.exp(gc[ce])
        egl_ref[n] = jnp.broadcast_to(jnp.exp(g_last[ce]), (1, HEAD_DIM))


def gdn_core(qkv, plain, norm_g, grow, brow):
    assert CHUNK == HEAD_DIM
    lp = qkv.shape[0]
    rb = _pick(lp, (640, 256, 128))
    nqk = GDN_QK_HEADS
    kcol0 = GDN_QK_W // HEAD_DIM
    vcol0 = 2 * GDN_QK_W // (2 * HEAD_DIM)
    zcol0 = COL_GZ // (2 * HEAD_DIM)

    nr = lp // rb
    n_chain = 2 * (rb // CHUNK)

    def cur(r):
        return jnp.minimum(r, nr - 1)

    def prev(r):
        return jnp.maximum(r - 1, 0)

    def rowspec(e):
        return pl.BlockSpec((None, 1, rb), lambda j, r: (2 * j + e, 0, cur(r)))

    kern = functools.partial(_gdn_kernel, rb=rb)
    return pl.pallas_call(
        kern,
        grid=(nqk, nr + 1),
        in_specs=[pl.BlockSpec((rb, HEAD_DIM), lambda j, r: (cur(r), j)),
                  pl.BlockSpec((rb, HEAD_DIM), lambda j, r: (cur(r), kcol0 + j)),
                  pl.BlockSpec((rb, 2 * HEAD_DIM), lambda j, r: (cur(r), vcol0 + j)),
                  pl.BlockSpec((rb, 2 * HEAD_DIM), lambda j, r: (prev(r), zcol0 + j)),
                  pl.BlockSpec((1, HEAD_DIM), lambda j, r: (0, 0)),
                  rowspec(0), rowspec(1), rowspec(0), rowspec(1)],
        out_specs=pl.BlockSpec((rb, 2 * HEAD_DIM), lambda j, r: (prev(r), j)),
        out_shape=jax.ShapeDtypeStruct((lp, GDN_V_W), bf16),
        scratch_shapes=[pltpu.VMEM((2, HEAD_DIM, HEAD_DIM), f32),
                        pltpu.VMEM((n_chain, CHUNK, HEAD_DIM), f32),
                        pltpu.VMEM((n_chain, 2 * CHUNK, HEAD_DIM), bf16),
                        pltpu.VMEM((n_chain, 2 * CHUNK, CHUNK), bf16),
                        pltpu.VMEM((n_chain, CHUNK, HEAD_DIM), f32),
                        pltpu.VMEM((n_chain, 1, HEAD_DIM), f32)],
        compiler_params=_cparams(("parallel", "arbitrary")),
        name="gdn_core",
    )(qkv, qkv, qkv, plain, norm_g.reshape(1, HEAD_DIM), grow, grow, brow, brow)


def _rope_head(seg, cc, s1, s2):
    half = ROPE_DIMS // 2
    return seg * cc + pltpu.roll(seg, half, 1) * s1 + pltpu.roll(seg, HEAD_DIM - half, 1) * s2


def _make_ep_rope(scale, head_major):
    def ep(acc, extra, outs):
        cc, s1, s2 = (r[...] for r in extra)
        for h in range(acc.shape[1] // HEAD_DIM):
            y = _rope_head(acc[:, h * HEAD_DIM:(h + 1) * HEAD_DIM], cc, s1, s2)
            if scale != 1.0:
                y = y * scale
            if head_major:
                outs[0][h] = y.astype(outs[0].dtype)
            else:
                outs[0][:, h * HEAD_DIM:(h + 1) * HEAD_DIM] = y.astype(outs[0].dtype)
    return ep


def _ep_rope_kv(acc, extra, outs):
    cc, s1, s2 = (r[...] for r in extra)
    for h in range(ATT_KV_HEADS):
        y = _rope_head(acc[:, h * HEAD_DIM:(h + 1) * HEAD_DIM], cc, s1, s2)
        outs[0][:, h * HEAD_DIM:(h + 1) * HEAD_DIM] = y.astype(outs[0].dtype)
    y = _rope_head(acc[:, ATT_KV_W:ATT_KV_W + HEAD_DIM], cc, s1, s2)
    outs[1][...] = y.astype(outs[1].dtype)


def rope_tables(lp):
    half = ROPE_DIMS // 2
    pos = (jnp.arange(lp, dtype=jnp.int32) - FRAME_OFF).astype(f32)
    inv = ROPE_THETA ** (-jnp.arange(half, dtype=f32) / half)
    ang = pos[:, None] * inv[None, :]
    cos, sin = jnp.cos(ang), jnp.sin(ang)
    zeros = jnp.zeros((lp, HEAD_DIM - ROPE_DIMS), f32)
    z16 = jnp.zeros((lp, half), f32)
    cc = jnp.concatenate([cos, cos, jnp.ones_like(zeros)], axis=1)
    s1 = jnp.concatenate([z16, sin, zeros], axis=1)
    s2 = jnp.concatenate([-sin, z16, zeros], axis=1)
    return cc, s1, s2


INT_MAX = 2 ** 31 - 1
HALF_MIN, HALF_MAX, HALF_SPAN = -2 ** 15, 2 ** 15 - 1, 2 ** 16


def _sortable(x):
    b = lax.bitcast_convert_type(x, i32)
    return b ^ (lax.shift_right_arithmetic(b, 31) & INT_MAX)


_NEG_BITS = struct.unpack("<i", struct.pack("<f", NEG))[0]
_KEY_NEG = _NEG_BITS ^ ((_NEG_BITS >> 31) & INT_MAX)
_KEY_NEG_HI = _KEY_NEG >> 16
_KEY_NEG_LO = (_KEY_NEG & (HALF_SPAN - 1)) + HALF_MIN


def _select_kernel(iq_ref, ik_ref, iwt_ref, bias_ref, key_ref, half_ref, *, tq, ts, lp, topk):
    i = pl.program_id(0)
    nkb = ((i + 1) * tq + ts - 1) // ts
    rem = lp - nkb * ts
    iwt = iwt_ref[...] * ((IDX_HEADS ** -0.5) * (HEAD_DIM ** -0.5))
    qpos = i * tq + lax.broadcasted_iota(i32, (1, tq), 1)
    sub = lax.broadcasted_iota(i32, (ts, 1), 0)

    def score_blk(kb, carry):
        k0 = pl.multiple_of(kb * ts, ts)
        ikb = ik_ref[pl.ds(k0, ts), :]
        acc = jnp.zeros((ts, tq), f32)
        for h in range(IDX_HEADS):
            lg = _dot_nt(ikb, iq_ref[h])
            acc = acc + jnp.maximum(lg, 0.0) * iwt[h:h + 1, :]
        kpos = k0 + sub
        sc = jnp.where(kpos <= qpos, acc, NEG)
        sc = jnp.where(kpos < FRAME_OFF, -jnp.inf, sc)
        key = _sortable(sc)
        key_ref[pl.ds(k0, ts), :] = key
        half_ref[pl.ds(k0, ts), :] = lax.shift_right_arithmetic(key, 16).astype(i16)
        return carry

    lax.fori_loop(0, nkb, score_blk, 0)

    n_acc = 4

    def count(pred_fn):
        sub8 = lax.broadcasted_iota(i32, (8, 1), 0)

        def blk(kb, cnts):
            k0 = pl.multiple_of(kb * ts, ts)
            cnts = list(cnts)
            blk_ref = key_ref.at[pl.ds(k0, ts)]
            for j in range(ts // 8):
                kv = blk_ref[j * 8:(j + 1) * 8, :]
                cnts[j % n_acc] = cnts[j % n_acc] + jnp.where(pred_fn(kv, k0 + j * 8 + sub8), 1, 0)
            return tuple(cnts)
        cnts = lax.fori_loop(0, nkb, blk, tuple(jnp.zeros((8, tq), i32) for _ in range(n_acc)))
        return jnp.sum(sum(cnts[1:], cnts[0]), axis=0, keepdims=True)

    def count_half_ge(cand):
        one, zero = jnp.int16(1), jnp.int16(0)

        def blk(kb, cnts):
            k0 = pl.multiple_of(kb * ts, ts)
            cnts = list(cnts)
            blk_ref = half_ref.at[pl.ds(k0, ts)]
            for j in range(ts // 16):
                hv = blk_ref[j * 16:(j + 1) * 16, :]
                cnts[j % n_acc] = cnts[j % n_acc] + jnp.where(hv >= cand, one, zero)
            return tuple(cnts)
        cnts = lax.fori_loop(0, nkb, blk, tuple(jnp.zeros((16, tq), i16) for _ in range(n_acc)))
        return jnp.sum(sum(cnts[1:], cnts[0]).astype(i32), axis=0, keepdims=True)

    def search_half(neg_half, n_start):
        def body(it, c):
            tau, n_tau = c
            t2 = tau | lax.shift_left(jnp.int32(1), 15 - it)
            cand = t2 + HALF_MIN
            n2 = count_half_ge(cand.astype(i16)) + jnp.where(cand <= neg_half, rem, 0)
            ok = n2 >= topk
            return jnp.where(ok, t2, tau), jnp.where(ok, n2, n_tau)

        return lax.fori_loop(0, 16, body, (jnp.zeros((1, tq), i32), n_start))

    tau_hi, n_hi = search_half(jnp.full((1, tq), _KEY_NEG_HI, i32), jnp.full((1, tq), lp, i32))
    t_hi = tau_hi + HALF_MIN

    def low_blk(kb, carry):
        k0 = pl.multiple_of(kb * ts, ts)
        key = key_ref[pl.ds(k0, ts), :]
        hi = lax.shift_right_arithmetic(key, 16)
        lo = (key & (HALF_SPAN - 1)) + HALF_MIN
        lo = jnp.where(hi > t_hi, HALF_MAX, jnp.where(hi == t_hi, lo, HALF_MIN))
        half_ref[pl.ds(k0, ts), :] = lo.astype(i16)
        return carry

    lax.fori_loop(0, nkb, low_blk, 0)
    neg_lo = jnp.where(_KEY_NEG_HI > t_hi, HALF_MAX, jnp.where(_KEY_NEG_HI == t_hi, _KEY_NEG_LO, HALF_MIN))
    tau_lo, n_ge = search_half(neg_lo, n_hi)
    thr = lax.shift_left(t_hi, 16) | tau_lo
    has_tie = jnp.max(jnp.where(n_ge > topk, 1, 0)) > 0

    def tie_cut():
        n_gt = (count(lambda kv, kpos: kv > thr) + jnp.where(thr < _KEY_NEG, rem, 0))
        need = topk - n_gt

        def bis2(it, cut):
            bit = lax.shift_left(jnp.int32(1), 14 - it)
            c2 = cut | bit
            n = count(lambda kv, kpos: jnp.logical_and(kv == thr, kpos < c2))
            return jnp.where(n < need, c2, cut)
        return lax.fori_loop(0, 15, bis2, jnp.zeros((1, tq), i32))

    cut = lax.cond(has_tie, tie_cut, lambda: jnp.full((1, tq), INT_MAX, i32))

    def bias_blk(kb, carry):
        k0 = pl.multiple_of(kb * ts, ts)
        kv = key_ref[pl.ds(k0, ts), :]
        interior = jnp.logical_and(k0 >= FRAME_OFF, k0 + ts - 1 <= i * tq)

        @pl.when(jnp.logical_and(interior, jnp.logical_not(has_tie)))
        def _():
            bias_ref[pl.ds(k0, ts), :] = jnp.where(kv >= thr, 0.0, NEG).astype(bias_ref.dtype)

        @pl.when(jnp.logical_not(jnp.logical_and(interior, jnp.logical_not(has_tie))))
        def _():
            kpos = k0 + sub
            sel = jnp.logical_or(kv > thr, jnp.logical_and(kv == thr, kpos <= cut))
            vis = jnp.logical_and(kpos <= qpos, kpos >= FRAME_OFF)
            bias_ref[pl.ds(k0, ts), :] = jnp.where(jnp.logical_and(sel, vis), 0.0, NEG).astype(bias_ref.dtype)
        return carry

    lax.fori_loop(0, nkb, bias_blk, 0)

    def fill_blk(kb, carry):
        k0 = pl.multiple_of(kb * ts, ts)
        bias_ref[pl.ds(k0, ts), :] = jnp.full((ts, tq), NEG, bias_ref.dtype)
        return carry

    lax.fori_loop(nkb, lp // ts, fill_blk, 0)


def dsa_select(iq, ik, iwt, topk):
    nh, lp, _ = iq.shape
    assert lp < 2 ** 15
    tq = 256
    ts = _pick(lp, (640, 256, 128))
    kern = functools.partial(_select_kernel, tq=tq, ts=ts, lp=lp, topk=topk)
    return pl.pallas_call(
        kern,
        grid=(lp // tq,),
        in_specs=[pl.BlockSpec((nh, tq, HEAD_DIM), lambda i: (0, i, 0)),
                  pl.BlockSpec((lp, HEAD_DIM), lambda i: (0, 0), pipeline_mode=pl.Buffered(1)),
                  pl.BlockSpec((nh, tq), lambda i: (0, i))],
        out_specs=pl.BlockSpec((lp, tq), lambda i: (0, i)),
        out_shape=jax.ShapeDtypeStruct((lp, lp), bf16),
        scratch_shapes=[pltpu.VMEM((lp, tq), i32), pltpu.VMEM((lp, tq), i16)],
        compiler_params=_cparams(("parallel",)),
        name="dsa_select",
    )(iq, ik, iwt)


V_AUG = HEAD_DIM + 16


def _attn_kernel(q_ref, k_ref, vt_ref, b_ref, o_ref, m_ref, acc_ref, *, tq, ts, bpt):
    i = pl.program_id(0)
    nkb = ((i + 1) * tq + ts - 1) // ts
    group = ATT_HEADS // ATT_KV_HEADS
    kv_groups = range(ATT_KV_HEADS)
    qpos = i * tq + lax.broadcasted_iota(i32, (tq, 1), 0)
    eye = (lax.broadcasted_iota(i32, (tq, tq), 0)
           == lax.broadcasted_iota(i32, (tq, tq), 1)).astype(bf16)
    qa = [jnp.concatenate(
        [jnp.concatenate([q_ref[:, (g * group + r) * HEAD_DIM:(g * group + r + 1) * HEAD_DIM], eye],
                         axis=1) for r in range(group)], axis=0) for g in kv_groups]
    m_ref[...] = jnp.full(m_ref.shape, -jnp.inf, f32)
    acc_ref[...] = jnp.zeros_like(acc_ref)

    def scores(g, k0):
        ka = jnp.concatenate([k_ref[pl.ds(k0, ts), g * HEAD_DIM:(g + 1) * HEAD_DIM],
                              b_ref[pl.ds(k0, ts), :]], axis=1)
        return _dot_nt(ka, qa[g]).astype(bf16)

    def accumulate(g, k0, st):
        vt_ = vt_ref[g, :, pl.ds(k0, ts)]
        ps, alphas = [], []
        for r in range(group):
            sr = st[:, r * tq:(r + 1) * tq]
            m_old = m_ref[g, r:r + 1, :]
            m_new = jnp.maximum(m_old, jnp.max(sr, axis=0, keepdims=True).astype(f32))
            m_ref[g, r:r + 1, :] = m_new
            ps.append(jnp.exp2(sr - m_new.astype(bf16)))
            alphas.append(jnp.exp2(m_old - m_new))
        pt = jnp.concatenate(ps, axis=1)
        acc_ref[g] = jnp.concatenate(alphas, axis=1) * acc_ref[g] + _dot(vt_, pt)

    def make_trip(nblk, first_block):
        def trip(kp, carry):
            k0s = [pl.multiple_of((first_block + kp * nblk + b) * ts, ts) for b in range(nblk)]
            work = [(g, kk) for kk in k0s for g in kv_groups]
            st = scores(*work[0])
            for n in range(len(work)):
                st_next = scores(*work[n + 1]) if n + 1 < len(work) else None
                accumulate(*work[n], st)
                st = st_next
            return carry
        return trip

    n_full = nkb // bpt
    lax.fori_loop(0, n_full, make_trip(bpt, 0), 0)
    if bpt > 1:
        lax.fori_loop(0, nkb - n_full * bpt, make_trip(1, n_full * bpt), 0)
    for g in kv_groups:
        for r in range(group):
            ot = (acc_ref[g, 0:HEAD_DIM, r * tq:(r + 1) * tq]
                  / acc_ref[g, HEAD_DIM:HEAD_DIM + 1, r * tq:(r + 1) * tq])
            orr = jnp.where(qpos >= FRAME_OFF, ot.T, 0.0)
            o_ref[:, (g * group + r) * HEAD_DIM:(g * group + r + 1) * HEAD_DIM] = orr.astype(o_ref.dtype)


def dsa_attention(q, k, vt_aug, bias_t):
    lp = q.shape[0]
    tq = 128
    ts = _pick(lp, (1280, 640, 256, 128))
    bpt = 2
    group = ATT_HEADS // ATT_KV_HEADS
    kern = functools.partial(_attn_kernel, tq=tq, ts=ts, bpt=bpt)
    resident = pl.Buffered(1)
    return pl.pallas_call(
        kern,
        grid=(lp // tq,),
        in_specs=[pl.BlockSpec((tq, ATT_Q_W), lambda i: (i, 0)),
                  pl.BlockSpec((lp, ATT_KV_W), lambda i: (0, 0), pipeline_mode=resident),
                  pl.BlockSpec((ATT_KV_HEADS, V_AUG, lp), lambda i: (0, 0, 0), pipeline_mode=resident),
                  pl.BlockSpec((lp, tq), lambda i: (0, i))],
        out_specs=pl.BlockSpec((tq, ATT_Q_W), lambda i: (i, 0)),
        out_shape=jax.ShapeDtypeStruct((lp, ATT_Q_W), bf16),
        scratch_shapes=[pltpu.VMEM((ATT_KV_HEADS, group, tq), f32),
                        pltpu.VMEM((ATT_KV_HEADS, V_AUG, group * tq), f32)],
        compiler_params=_cparams(("parallel",)),
        name="dsa_attention",
    )(q, k, vt_aug, bias_t)


def _ffn_up_kernel(a_ref, ah_ref, wg_ref, wv_ref, cg_ref, cv_ref, bg_ref, bv_ref, o_ref,
                   eg_ref, ev_ref, *, tm):
    a = _halo_rows(a_ref, ah_ref, pl.program_id(0))
    eg_ref[...] = _dot(a, wg_ref[...])
    ev_ref[...] = _dot(a, wv_ref[...])

    def conv(ext_ref, w_ref, b_ref):
        w = w_ref[...]
        y = ext_ref[HALO:HALO + tm, :] * w[FFN_CONV - 1:FFN_CONV, :]
        for j in range(FFN_CONV - 1):
            s0 = HALO - (FFN_CONV - 1) + j
            y = y + ext_ref[s0:s0 + tm, :] * w[j:j + 1, :]
        return y + b_ref[...]

    gate = conv(eg_ref, cg_ref, bg_ref)
    val = conv(ev_ref, cv_ref, bv_ref)
    o_ref[...] = (gate * _sigmoid(gate) * val).astype(o_ref.dtype)


def ffn_up(u, w_up, conv_w, conv_b):
    lp, kdim = u.shape
    tm = _pick(lp, (1280, 640, 256, 128))
    tn = 512
    nc = D_FF // tn
    kern = functools.partial(_ffn_up_kernel, tm=tm)

    def wspec(off):
        return pl.BlockSpec((kdim, tn), lambda i, c: (0, c + off))

    def cspec(off):
        return pl.BlockSpec((FFN_CONV, tn), lambda i, c: (0, c + off))

    def bspec(off):
        return pl.BlockSpec((1, tn), lambda i, c: (0, c + off))

    b2 = conv_b.reshape(1, 2 * D_FF)
    return pl.pallas_call(
        kern,
        grid=(lp // tm, nc),
        in_specs=[pl.BlockSpec((tm, kdim), lambda i, c: (i, 0)),
                  pl.BlockSpec((HALO, kdim), lambda i, c: (jnp.maximum(i * (tm // HALO) - 1, 0), 0)),
                  wspec(0), wspec(nc), cspec(0), cspec(nc), bspec(0), bspec(nc)],
        out_specs=pl.BlockSpec((tm, tn), lambda i, c: (i, c)),
        out_shape=jax.ShapeDtypeStruct((lp, D_FF), bf16),
        scratch_shapes=[pltpu.VMEM((HALO + tm, tn), f32)] * 2,
        compiler_params=_cparams(("parallel", "arbitrary")),
        name="ffn_up",
    )(u, u, w_up, w_up, conv_w, conv_w, b2, b2)


def _split_w_in(w):
    wt = w.T
    o = 0
    parts = {}
    for name, width in (("gq", GDN_QK_W), ("gk", GDN_QK_W), ("gv", GDN_V_W), ("gz", GDN_V_W),
                        ("gb", GDN_V_HEADS), ("ga", GDN_V_HEADS), ("aq", ATT_Q_W), ("ak", ATT_KV_W),
                        ("av", ATT_KV_W), ("iq", IDX_Q_W), ("ik", HEAD_DIM), ("iw", IDX_HEADS),
                        ("gate_gdn", D_MODEL), ("gate_att", D_MODEL)):
        parts[name] = wt[o:o + width]
        o += width
    cat = lambda names: jnp.concatenate([parts[n] for n in names], axis=0)
    small = cat(("gb", "ga", "iw"))
    small = jnp.pad(small, ((0, LANES - small.shape[0]), (0, 0)))
    groups = dict(qkv=cat(("gq", "gk", "gv")), aq=parts["aq"], iq=parts["iq"], kv=cat(("ak", "ik")),
                  plain=cat(("gz", "gate_gdn", "gate_att", "av")), small=small)
    return {k: v.astype(bf16) for k, v in groups.items()}


def _layer(h0, p, lp, topk):
    tm = _pick(lp, (640, 256, 128))
    tm_big = _pick(lp, (1280, 640, 256, 128))
    wp = _split_w_in(p["w_in"])

    def proj(w, **kw):
        return matmul(u1, w, tm=tm_big, tk=D_MODEL, w_nt=True, **kw)

    u1 = rms_rows(h0, p["mix_pre_g"])
    tabs = rope_tables(lp)
    tab_specs = (((tm_big, HEAD_DIM), lambda i, j: (i, 0)),) * 3
    qkv = gdn_proj(u1, wp["qkv"], p["gdn_conv_w"])
    (aq,) = proj(wp["aq"], tn=1024, epilogue=_make_ep_rope(HEAD_DIM ** -0.5 * LOG2E, False),
                 extra=tabs, extra_specs=tab_specs,
                 out_shapes=[jax.ShapeDtypeStruct((lp, ATT_Q_W), bf16)], name="proj_aq")
    iq_heads = 1024 // HEAD_DIM
    (iq,) = proj(wp["iq"], tn=1024, epilogue=_make_ep_rope(1.0, True), extra=tabs, extra_specs=tab_specs,
                 out_shapes=[jax.ShapeDtypeStruct((IDX_HEADS, lp, HEAD_DIM), bf16)],
                 out_block_specs=[((iq_heads, tm_big, HEAD_DIM), lambda i, j: (j, i, 0))], name="proj_iq")
    ak, ik = proj(wp["kv"], tn=ATT_KV_W + HEAD_DIM, epilogue=_ep_rope_kv, extra=tabs, extra_specs=tab_specs,
                  out_shapes=[jax.ShapeDtypeStruct((lp, ATT_KV_W), bf16),
                              jax.ShapeDtypeStruct((lp, HEAD_DIM), bf16)],
                  out_block_specs=[((tm_big, ATT_KV_W), lambda i, j: (i, 0)),
                                   ((tm_big, HEAD_DIM), lambda i, j: (i, 0))], name="proj_kv")
    (plain,) = proj(wp["plain"], tn=768, epilogue=_ep_cast,
                    out_shapes=[jax.ShapeDtypeStruct((lp, PLAIN_W), bf16)], name="proj_plain")
    (small,) = proj(wp["small"], tn=LANES, epilogue=_ep_cast,
                    out_shapes=[jax.ShapeDtypeStruct((lp, LANES), f32)], name="proj_small")
    gb = small[:, 0:GDN_V_HEADS]
    ga = small[:, GDN_V_HEADS:2 * GDN_V_HEADS]
    iw = small[:, 2 * GDN_V_HEADS:2 * GDN_V_HEADS + IDX_HEADS]

    beta, gcum = gdn_gates(gb, ga, p["gdn_a_log"], p["gdn_dt_bias"])
    o_gdn = gdn_core(qkv, plain, p["gdn_norm_g"], gcum.T[:, None, :], beta.T[:, None, :])

    avt = plain[:, COL_AV:COL_AV + ATT_KV_W].T.reshape(ATT_KV_HEADS, HEAD_DIM, lp)
    avt = jnp.concatenate([avt, jnp.ones((ATT_KV_HEADS, V_AUG - HEAD_DIM, lp), bf16)], axis=1)
    bias_t = dsa_select(iq, ik, iw.T, topk)
    o_att = dsa_attention(aq, ak, avt, bias_t)

    tn = 1024
    gate_spec = lambda col0: ((tm_big, tn), (lambda i, j, c=col0 // tn: (i, c + j)))
    (m1,) = matmul(o_gdn, p["w_branch_gdn"].astype(bf16), tm=tm_big, tn=tn, tk=2048,
                   epilogue=_ep_gate, extra=(plain,), extra_specs=(gate_spec(COL_GATE_GDN),),
                   out_shapes=[jax.ShapeDtypeStruct((lp, D_MODEL), f32)], name="branch_gdn")
    (merged,) = matmul(o_att, p["w_branch_att"].astype(bf16), tm=tm_big, tn=tn, tk=2048,
                       epilogue=_ep_gate_add, extra=(plain, m1),
                       extra_specs=(gate_spec(COL_GATE_ATT), ((tm_big, tn), lambda i, j: (i, j))),
                       out_shapes=[jax.ShapeDtypeStruct((lp, D_MODEL), bf16)], name="branch_att")
    row_spec = ((tm, D_MODEL), lambda i, j: (i, 0))
    vec_spec = ((1, D_MODEL), lambda i, j: (0, 0))
    h1, u2 = matmul(merged, p["w_out"].astype(bf16), tm=tm, tn=D_MODEL, tk=D_MODEL,
                    epilogue=_ep_res_norm2,
                    extra=(h0, p["mix_post_g"].reshape(1, D_MODEL), p["ffn_pre_g"].reshape(1, D_MODEL)),
                    extra_specs=(row_spec, vec_spec, vec_spec),
                    out_shapes=[jax.ShapeDtypeStruct((lp, D_MODEL), f32),
                                jax.ShapeDtypeStruct((lp, D_MODEL), bf16)], name="w_out")

    act = ffn_up(u2, p["w_up"].astype(bf16), p["ffn_conv_w"], p["ffn_conv_b"])
    (h2,) = matmul(act, p["w_down"].astype(bf16), tm=tm, tn=D_MODEL, tk=2048, epilogue=_ep_res_norm,
                   extra=(h1, p["ffn_post_g"].reshape(1, D_MODEL)),
                   extra_specs=(row_spec, vec_spec),
                   out_shapes=[jax.ShapeDtypeStruct((lp, D_MODEL), f32)], name="w_down")
    return h2


def kernel(x, meta_tokens, mix_pre_g, w_in, gdn_conv_w, gdn_a_log, gdn_dt_bias, gdn_norm_g,
           w_branch_gdn, w_branch_att, w_out, mix_post_g, ffn_pre_g, w_up, ffn_conv_w,
           ffn_conv_b, w_down, ffn_post_g):
    batch, seq, d = x.shape
    assert batch == 1 and d == D_MODEL
    lp = FRAME_X0 + seq
    topk = min(TOPK_MAX, (N_META + seq) // 4)
    h = jnp.concatenate([jnp.zeros((FRAME_OFF, d), x.dtype), meta_tokens.astype(x.dtype), x[0]], axis=0)
    for i in range(w_in.shape[0]):
        p = dict(mix_pre_g=mix_pre_g[i], w_in=w_in[i], gdn_conv_w=gdn_conv_w[i], gdn_a_log=gdn_a_log[i],
                 gdn_dt_bias=gdn_dt_bias[i], gdn_norm_g=gdn_norm_g[i], w_branch_gdn=w_branch_gdn[i],
                 w_branch_att=w_branch_att[i], w_out=w_out[i], mix_post_g=mix_post_g[i],
                 ffn_pre_g=ffn_pre_g[i], w_up=w_up[i], ffn_conv_w=ffn_conv_w[i],
                 ffn_conv_b=ffn_conv_b[i], w_down=w_down[i], ffn_post_g=ffn_post_g[i])
        h = _layer(h, p, lp, topk)
    return h[FRAME_X0:][None]
```

```python
import functools
import struct

import jax
import jax.numpy as jnp
from jax import lax
from jax.experimental import pallas as pl
from jax.experimental.pallas import tpu as pltpu

f32 = jnp.float32
bf16 = jnp.bfloat16
i32 = jnp.int32
i16 = jnp.int16

D_MODEL = 2048
N_META = 16
EPS = 1e-6
GDN_QK_HEADS = 16
GDN_V_HEADS = 32
HEAD_DIM = 128
GDN_CONV = 4
ATT_HEADS = 16
ATT_KV_HEADS = 2
IDX_HEADS = 16
TOPK_MAX = 256
NEG = -1e30
LOG2E = 1.4426950408889634
ROPE_THETA = 500000.0
ROPE_DIMS = HEAD_DIM // 4
D_FF = 3 * D_MODEL
FFN_CONV = 3
GDN_QK_W = GDN_QK_HEADS * HEAD_DIM
GDN_V_W = GDN_V_HEADS * HEAD_DIM
ATT_Q_W = ATT_HEADS * HEAD_DIM
ATT_KV_W = ATT_KV_HEADS * HEAD_DIM
IDX_Q_W = IDX_HEADS * HEAD_DIM

FRAME_X0 = 256
FRAME_OFF = FRAME_X0 - N_META
CHUNK = 128
GDN_QK_PER_STEP = 2

LANES = 128
VMEM_LIMIT = 56 * 1024 * 1024

COL_GZ = 0
COL_GATE_GDN = COL_GZ + GDN_V_W
COL_GATE_ATT = COL_GATE_GDN + D_MODEL
COL_AV = COL_GATE_ATT + D_MODEL
PLAIN_W = COL_AV + ATT_KV_W


def _cparams(sem):
    return pltpu.CompilerParams(dimension_semantics=sem, vmem_limit_bytes=VMEM_LIMIT)


def _pick(n, cands):
    for c in cands:
        if n % c == 0:
            return c
    raise ValueError(f"no tile for {n} in {cands}")


def _sigmoid(x):
    return 1.0 / (1.0 + jnp.exp(-x))


def _dot(a, b):
    return jnp.dot(a, b, preferred_element_type=f32)


def _dot_nt(a, b):
    return lax.dot_general(a, b, (((1,), (1,)), ((), ())), preferred_element_type=f32)


def _rms_rows_kernel(h_ref, g_ref, o_ref):
    h = h_ref[...]
    y = h * lax.rsqrt(jnp.mean(h * h, axis=-1, keepdims=True) + EPS)
    o_ref[...] = (y * g_ref[...]).astype(o_ref.dtype)


def rms_rows(h, g):
    lp, d = h.shape
    tm = _pick(lp, (640, 256, 128))
    return pl.pallas_call(
        _rms_rows_kernel,
        grid=(lp // tm,),
        in_specs=[pl.BlockSpec((tm, d), lambda i: (i, 0)),
                  pl.BlockSpec((1, d), lambda i: (0, 0))],
        out_specs=pl.BlockSpec((tm, d), lambda i: (i, 0)),
        out_shape=jax.ShapeDtypeStruct((lp, d), bf16),
        compiler_params=_cparams(("parallel",)),
        name="rms_rows",
    )(h, g.reshape(1, d))


def _mm_kernel(*refs, nk, n_extra, n_out, epilogue, w_nt):
    a_ref, w_ref = refs[0], refs[1]
    dot = _dot_nt if w_nt else _dot
    extra = refs[2:2 + n_extra]
    outs = refs[2 + n_extra:2 + n_extra + n_out]
    if nk == 1:
        epilogue(dot(a_ref[...], w_ref[...]), extra, outs)
        return
    acc_ref = refs[-1]
    k = pl.program_id(2)

    @pl.when(k == 0)
    def _():
        acc_ref[...] = jnp.zeros_like(acc_ref)

    acc_ref[...] += dot(a_ref[...], w_ref[...])

    @pl.when(k == nk - 1)
    def _():
        epilogue(acc_ref[...], extra, outs)


def matmul(a, w, *, tm, tn, tk, epilogue, extra=(), extra_specs=(), out_shapes, out_block_specs=None,
           a_col0=0, w_nt=False, name):
    m = a.shape[0]
    kdim, n = w.shape[::-1] if w_nt else w.shape
    nk = kdim // tk
    assert m % tm == 0 and n % tn == 0 and kdim % tk == 0
    w_mode = dict(pipeline_mode=pl.Buffered(1)) if (nk == 1 and n == tn) else {}
    in_specs = [pl.BlockSpec((tm, tk), lambda i, j, k: (i, a_col0 + k)),
                (pl.BlockSpec((tn, tk), lambda i, j, k: (j, k), **w_mode) if w_nt else
                 pl.BlockSpec((tk, tn), lambda i, j, k: (k, j), **w_mode))]
    in_specs += [pl.BlockSpec(bs, (lambda i, j, k, f=f: f(i, j))) for bs, f in extra_specs]
    if out_block_specs is None:
        out_block_specs = [((tm, tn), lambda i, j: (i, j))] * len(out_shapes)
    out_specs = [pl.BlockSpec(bs, (lambda i, j, k, f=f: f(i, j))) for bs, f in out_block_specs]
    scratch = [] if nk == 1 else [pltpu.VMEM((tm, tn), f32)]
    kern = functools.partial(_mm_kernel, nk=nk, n_extra=len(extra), n_out=len(out_shapes),
                             epilogue=epilogue, w_nt=w_nt)
    res = pl.pallas_call(
        kern,
        grid=(m // tm, n // tn, nk),
        in_specs=in_specs,
        out_specs=out_specs,
        out_shape=out_shapes,
        scratch_shapes=scratch,
        compiler_params=_cparams(("parallel", "parallel", "arbitrary")),
        name=name,
    )(a, w, *extra)
    return res


def _ep_cast(acc, extra, outs):
    outs[0][...] = acc.astype(outs[0].dtype)


def _ep_gate(acc, extra, outs):
    g = extra[0][...].astype(f32)
    outs[0][...] = (_sigmoid(g) * acc).astype(outs[0].dtype)


def _ep_gate_add(acc, extra, outs):
    g = extra[0][...].astype(f32)
    outs[0][...] = (extra[1][...].astype(f32) + _sigmoid(g) * acc).astype(outs[0].dtype)


def _rms(t, g):
    return t * lax.rsqrt(jnp.mean(t * t, axis=-1, keepdims=True) + EPS) * g


def _ep_res_norm2(acc, extra, outs):
    h_ref, g_ref, g2_ref = extra
    h1 = h_ref[...] + _rms(acc, g_ref[...])
    outs[0][...] = h1
    outs[1][...] = _rms(h1, g2_ref[...]).astype(outs[1].dtype)


def _ep_res_norm(acc, extra, outs):
    h_ref, g_ref = extra
    outs[0][...] = h_ref[...] + _rms(acc, g_ref[...])


HALO = 16


def _halo_rows(a_ref, ah_ref, i):
    halo = ah_ref[...]
    halo = jnp.where(i > 0, halo, jnp.zeros_like(halo))
    return jnp.concatenate([halo, a_ref[...]], axis=0)


def _gdn_proj_kernel(a_ref, ah_ref, w_ref, cw_ref, o_ref, ext_ref, *, tm, tn):
    c = pl.program_id(1)
    ext_ref[...] = _dot_nt(_halo_rows(a_ref, ah_ref, pl.program_id(0)), w_ref[...])
    w = cw_ref[...]
    y = ext_ref[HALO:HALO + tm, :] * w[GDN_CONV - 1:GDN_CONV, :]
    for j in range(GDN_CONV - 1):
        s0 = HALO - (GDN_CONV - 1) + j
        y = y + ext_ref[s0:s0 + tm, :] * w[j:j + 1, :]
    s = y * _sigmoid(y)
    is_q = c < (GDN_QK_W // tn)
    is_qk = c < (2 * GDN_QK_W // tn)
    qscale = jnp.where(is_q, HEAD_DIM ** -0.5, 1.0).astype(f32)
    for hh in range(tn // HEAD_DIM):
        seg = s[:, hh * HEAD_DIM:(hh + 1) * HEAD_DIM]
        r = lax.rsqrt(jnp.sum(seg * seg, axis=-1, keepdims=True) + EPS) * qscale
        fac = jnp.where(is_qk, r, 1.0)
        o_ref[:, hh * HEAD_DIM:(hh + 1) * HEAD_DIM] = (seg * fac).astype(o_ref.dtype)


def gdn_proj(u, w_qkv, conv_w):
    lp, kdim = u.shape
    width = w_qkv.shape[0]
    tm = _pick(lp, (1280, 640, 256, 128))
    tn = 1024
    kern = functools.partial(_gdn_proj_kernel, tm=tm, tn=tn)
    return pl.pallas_call(
        kern,
        grid=(lp // tm, width // tn),
        in_specs=[pl.BlockSpec((tm, kdim), lambda i, c: (i, 0)),
                  pl.BlockSpec((HALO, kdim), lambda i, c: (jnp.maximum(i * (tm // HALO) - 1, 0), 0)),
                  pl.BlockSpec((tn, kdim), lambda i, c: (c, 0)),
                  pl.BlockSpec((GDN_CONV, tn), lambda i, c: (0, c))],
        out_specs=pl.BlockSpec((tm, tn), lambda i, c: (i, c)),
        out_shape=jax.ShapeDtypeStruct((lp, width), bf16),
        scratch_shapes=[pltpu.VMEM((HALO + tm, tn), f32)],
        compiler_params=_cparams(("parallel", "arbitrary")),
        name="gdn_proj",
    )(u, u, w_qkv, conv_w)


def _gdn_gates_kernel(b_ref, a_ref, alog_ref, dt_ref, beta_ref, gcum_ref, *, tm):
    i = pl.program_id(0)
    rows = i * tm + lax.broadcasted_iota(i32, (tm, 1), 0)
    valid = rows >= FRAME_OFF
    beta_ref[...] = jnp.where(valid, _sigmoid(b_ref[...]), 0.0)
    a = a_ref[...] + dt_ref[...]
    sp = jnp.maximum(a, 0.0) + jnp.log1p(jnp.exp(-jnp.abs(a)))
    g = jnp.where(valid, -jnp.exp(alog_ref[...]) * sp, 0.0)
    tri = (lax.broadcasted_iota(i32, (CHUNK, CHUNK), 0)
           >= lax.broadcasted_iota(i32, (CHUNK, CHUNK), 1)).astype(f32)
    for c in range(tm // CHUNK):
        rs = slice(c * CHUNK, (c + 1) * CHUNK)
        gcum_ref[rs, :] = jnp.dot(tri, g[rs, :], preferred_element_type=f32,
                                  precision=lax.Precision.HIGHEST)


def gdn_gates(gb, ga, a_log, dt_bias):
    lp, nh = gb.shape
    tm = _pick(lp, (1280, 640, 256, 128))
    spec = pl.BlockSpec((tm, nh), lambda i: (i, 0))
    vec = pl.BlockSpec((1, nh), lambda i: (0, 0))
    return pl.pallas_call(
        functools.partial(_gdn_gates_kernel, tm=tm),
        grid=(lp // tm,),
        in_specs=[spec, spec, vec, vec],
        out_specs=[spec, spec],
        out_shape=[jax.ShapeDtypeStruct((lp, nh), f32)] * 2,
        compiler_params=_cparams(("parallel",)),
        name="gdn_gates",
    )(gb, ga, a_log.reshape(1, nh), dt_bias.reshape(1, nh))


def _block_mask(size):
    r = lax.broadcasted_iota(i32, (CHUNK, CHUNK), 0) // size
    c = lax.broadcasted_iota(i32, (CHUNK, CHUNK), 1) // size
    return r == c


def _unit_lower_inverse_many(mats, tick):
    row = lax.broadcasted_iota(i32, (CHUNK, CHUNK), 0)
    col = lax.broadcasted_iota(i32, (CHUNK, CHUNK), 1)
    eye = (row == col).astype(f32)
    base = 8
    m_prev = _block_mask(base)
    ads = [jnp.where(m_prev, a, 0.0) for a in mats]
    adbs = [ad.astype(bf16) for ad in ads]
    a2s = [_dot(x, x) for x in adbs]
    tick()
    a2bs = [x.astype(bf16) for x in a2s]
    a4s = [_dot(x, x) for x in a2bs]
    tick()
    xs = [_dot((eye - ad).astype(bf16), (eye + a2).astype(bf16)) for ad, a2 in zip(ads, a2s)]
    tick()
    xs = [_dot(x.astype(bf16), (eye + a4).astype(bf16)) for x, a4 in zip(xs, a4s)]
    tick()
    size = base * 2
    while size <= CHUNK:
        m_cur = _block_mask(size)
        off_diag = jnp.logical_and(m_cur, jnp.logical_not(m_prev))
        bs = [jnp.where(off_diag, a, 0.0).astype(bf16) for a in mats]
        xbs = [x.astype(bf16) for x in xs]
        ys = [_dot(xb, b).astype(bf16) for xb, b in zip(xbs, bs)]
        tick()
        xs = [x - _dot(y, xb) for x, y, xb in zip(xs, ys, xbs)]
        tick()
        m_prev = m_cur
        size *= 2
    return xs


def _gdn_kernel(q_ref, k_ref, v_ref, z_ref, ng_ref, *rest, rb, nq):
    r = pl.program_id(1)
    nv = 2 * nq
    grs, brs = rest[:nv], rest[nv:2 * nv]
    o_ref, s_ref, u_ref, wq_ref, ak_ref, eg_ref, egl_ref = rest[2 * nv:]

    @pl.when(r == 0)
    def _():
        for ref in (s_ref, u_ref, wq_ref, ak_ref, eg_ref, egl_ref):
            ref[...] = jnp.zeros_like(ref)

    row = lax.broadcasted_iota(i32, (CHUNK, CHUNK), 0)
    col = lax.broadcasted_iota(i32, (CHUNK, CHUNK), 1)
    incl = row >= col
    strict = row > col
    ng = ng_ref[...]
    nchunk = rb // CHUNK
    heads = range(nv)

    def rows(c):
        return slice(c * CHUNK, (c + 1) * CHUNK)

    def lanes(e):
        return slice(e * HEAD_DIM, (e + 1) * HEAD_DIM)

    chains = [(c, e) for c in range(nchunk) for e in heads]

    def recurrence():
        state = [s_ref[e] for e in heads]
        for c in range(nchunk):
            n = [chains.index((c, e)) for e in heads]
            sb = [state[e].astype(bf16) for e in heads]
            ws = [_dot(wq_ref[n[e]], sb[e]) for e in heads]
            yield
            vb = [(u_ref[n[e]] - ws[e][:CHUNK]).astype(bf16) for e in heads]
            av = [_dot(ak_ref[n[e]], vb[e]) for e in heads]
            yield
            for e in heads:
                o = eg_ref[n[e]] * ws[e][CHUNK:] + av[e][:CHUNK]
                state[e] = state[e] * egl_ref[n[e]] + av[e][CHUNK:]
                z = z_ref[rows(c), lanes(e)].astype(f32)
                o_ref[rows(c), lanes(e)] = (_rms(o, ng) * (z * _sigmoid(z))).astype(o_ref.dtype)
        for e in heads:
            s_ref[e] = state[e]

    rec = recurrence()

    def tick():
        next(rec, None)

    qkh = [(c, a) for c in range(nchunk) for a in range(nq)]
    qs = {ca: q_ref[rows(ca[0]), lanes(ca[1])] for ca in qkh}
    ks = {ca: k_ref[rows(ca[0]), lanes(ca[1])] for ca in qkh}
    kts = {ca: ks[ca].astype(f32).T for ca in qkh}
    kks = {ca: _dot_nt(ks[ca], ks[ca]) for ca in qkh}
    qks = {ca: _dot_nt(qs[ca], ks[ca]) for ca in qkh}
    qk_of = {ce: (ce[0], ce[1] // 2) for ce in chains}
    tick()
    def per_row(v):
        return jnp.broadcast_to(v, (CHUNK, CHUNK)).T

    gr = {ce: grs[ce[1]][:, rows(ce[0])] for ce in chains}
    br = {ce: brs[ce[1]][:, rows(ce[0])] for ce in chains}
    gc = {ce: per_row(gr[ce]) for ce in chains}
    bc = {ce: per_row(br[ce]) for ce in chains}
    dec = {ce: jnp.where(incl, jnp.exp(jnp.where(incl, gc[ce] - gr[ce], 0.0)), 0.0) for ce in chains}
    amat = [jnp.where(strict, bc[ce] * kks[qk_of[ce]] * dec[ce], 0.0) for ce in chains]
    tinv = dict(zip(chains, _unit_lower_inverse_many(amat, tick)))
    u = {ce: _dot((tinv[ce] * br[ce]).astype(bf16), v_ref[rows(ce[0]), lanes(ce[1])]) for ce in chains}
    tick()
    w = {ce: _dot((tinv[ce] * (br[ce] * jnp.exp(gr[ce]))).astype(bf16), ks[qk_of[ce]]) for ce in chains}
    tick()
    g_last = {ce: gr[ce][:, CHUNK - 1:CHUNK] for ce in chains}
    wq = {ce: jnp.concatenate([w[ce].astype(bf16), qs[qk_of[ce]]], axis=0) for ce in chains}
    ak = {ce: jnp.concatenate(
        [jnp.where(incl, qks[qk_of[ce]] * dec[ce], 0.0).astype(bf16),
         (kts[qk_of[ce]] * jnp.exp(g_last[ce] - gr[ce])).astype(bf16)], axis=0) for ce in chains}
    for _ in rec:
        pass
    for n, ce in enumerate(chains):
        u_ref[n] = u[ce]
        wq_ref[n] = wq[ce]
        ak_ref[n] = ak[ce]
        eg_ref[n] = jnp.exp(gc[ce])
        egl_ref[n] = jnp.broadcast_to(jnp.exp(g_last[ce]), (1, HEAD_DIM))


def gdn_core(qkv, plain, norm_g, grow, brow):
    assert CHUNK == HEAD_DIM
    lp = qkv.shape[0]
    rb = _pick(lp, (640, 256, 128))
    nq = GDN_QK_PER_STEP
    nv = 2 * nq
    qw, vw = nq * HEAD_DIM, nv * HEAD_DIM
    kcol0 = GDN_QK_W // qw
    vcol0 = 2 * GDN_QK_W // vw
    zcol0 = COL_GZ // vw

    nr = lp // rb
    n_chain = nv * (rb // CHUNK)

    def cur(r):
        return jnp.minimum(r, nr - 1)

    def prev(r):
        return jnp.maximum(r - 1, 0)

    def rowspec(e):
        return pl.BlockSpec((None, 1, rb), lambda j, r: (nv * j + e, 0, cur(r)))

    rowspecs = [rowspec(e) for e in range(nv)]
    kern = functools.partial(_gdn_kernel, rb=rb, nq=nq)
    return pl.pallas_call(
        kern,
        grid=(GDN_QK_HEADS // nq, nr + 1),
        in_specs=[pl.BlockSpec((rb, qw), lambda j, r: (cur(r), j)),
                  pl.BlockSpec((rb, qw), lambda j, r: (cur(r), kcol0 + j)),
                  pl.BlockSpec((rb, vw), lambda j, r: (cur(r), vcol0 + j)),
                  pl.BlockSpec((rb, vw), lambda j, r: (prev(r), zcol0 + j)),
                  pl.BlockSpec((1, HEAD_DIM), lambda j, r: (0, 0))] + rowspecs + rowspecs,
        out_specs=pl.BlockSpec((rb, vw), lambda j, r: (prev(r), j)),
        out_shape=jax.ShapeDtypeStruct((lp, GDN_V_W), bf16),
        scratch_shapes=[pltpu.VMEM((nv, HEAD_DIM, HEAD_DIM), f32),
                        pltpu.VMEM((n_chain, CHUNK, HEAD_DIM), f32),
                        pltpu.VMEM((n_chain, 2 * CHUNK, HEAD_DIM), bf16),
                        pltpu.VMEM((n_chain, 2 * CHUNK, CHUNK), bf16),
                        pltpu.VMEM((n_chain, CHUNK, HEAD_DIM), f32),
                        pltpu.VMEM((n_chain, 1, HEAD_DIM), f32)],
        compiler_params=_cparams(("parallel", "arbitrary")),
        name="gdn_core",
    )(qkv, qkv, qkv, plain, norm_g.reshape(1, HEAD_DIM), *([grow] * nv), *([brow] * nv))


def _rope_head(seg, cc, s1, s2):
    half = ROPE_DIMS // 2
    return seg * cc + pltpu.roll(seg, half, 1) * s1 + pltpu.roll(seg, HEAD_DIM - half, 1) * s2


def _make_ep_rope(scale, head_major):
    def ep(acc, extra, outs):
        cc, s1, s2 = (r[...] for r in extra)
        for h in range(acc.shape[1] // HEAD_DIM):
            y = _rope_head(acc[:, h * HEAD_DIM:(h + 1) * HEAD_DIM], cc, s1, s2)
            if scale != 1.0:
                y = y * scale
            if head_major:
                outs[0][h] = y.astype(outs[0].dtype)
            else:
                outs[0][:, h * HEAD_DIM:(h + 1) * HEAD_DIM] = y.astype(outs[0].dtype)
    return ep


def _ep_rope_kv(acc, extra, outs):
    cc, s1, s2 = (r[...] for r in extra)
    for h in range(ATT_KV_HEADS):
        y = _rope_head(acc[:, h * HEAD_DIM:(h + 1) * HEAD_DIM], cc, s1, s2)
        outs[0][:, h * HEAD_DIM:(h + 1) * HEAD_DIM] = y.astype(outs[0].dtype)
    y = _rope_head(acc[:, ATT_KV_W:ATT_KV_W + HEAD_DIM], cc, s1, s2)
    outs[1][...] = y.astype(outs[1].dtype)


def rope_tables(lp):
    half = ROPE_DIMS // 2
    pos = (jnp.arange(lp, dtype=jnp.int32) - FRAME_OFF).astype(f32)
    inv = ROPE_THETA ** (-jnp.arange(half, dtype=f32) / half)
    ang = pos[:, None] * inv[None, :]
    cos, sin = jnp.cos(ang), jnp.sin(ang)
    zeros = jnp.zeros((lp, HEAD_DIM - ROPE_DIMS), f32)
    z16 = jnp.zeros((lp, half), f32)
    cc = jnp.concatenate([cos, cos, jnp.ones_like(zeros)], axis=1)
    s1 = jnp.concatenate([z16, sin, zeros], axis=1)
    s2 = jnp.concatenate([-sin, z16, zeros], axis=1)
    return cc, s1, s2


INT_MAX = 2 ** 31 - 1
HALF_MIN, HALF_MAX, HALF_SPAN = -2 ** 15, 2 ** 15 - 1, 2 ** 16


def _sortable(x):
    b = lax.bitcast_convert_type(x, i32)
    return b ^ (lax.shift_right_arithmetic(b, 31) & INT_MAX)


_NEG_BITS = struct.unpack("<i", struct.pack("<f", NEG))[0]
_KEY_NEG = _NEG_BITS ^ ((_NEG_BITS >> 31) & INT_MAX)
_KEY_NEG_HI = _KEY_NEG >> 16
_KEY_NEG_LO = (_KEY_NEG & (HALF_SPAN - 1)) + HALF_MIN


def _select_kernel(iq_ref, ik_ref, iwt_ref, bias_ref, key_ref, half_ref, *, tq, ts, lp, topk):
    i = pl.program_id(0)
    nkb = ((i + 1) * tq + ts - 1) // ts
    rem = lp - nkb * ts
    iwt = iwt_ref[...] * ((IDX_HEADS ** -0.5) * (HEAD_DIM ** -0.5))
    qpos = i * tq + lax.broadcasted_iota(i32, (1, tq), 1)
    sub = lax.broadcasted_iota(i32, (ts, 1), 0)

    def score_blk(kb, carry):
        k0 = pl.multiple_of(kb * ts, ts)
        ikb = ik_ref[pl.ds(k0, ts), :]
        acc = jnp.zeros((ts, tq), f32)
        for h in range(IDX_HEADS):
            lg = _dot_nt(ikb, iq_ref[h])
            acc = acc + jnp.maximum(lg, 0.0) * iwt[h:h + 1, :]
        kpos = k0 + sub
        sc = jnp.where(kpos <= qpos, acc, NEG)
        sc = jnp.where(kpos < FRAME_OFF, -jnp.inf, sc)
        key = _sortable(sc)
        key_ref[pl.ds(k0, ts), :] = key
        half_ref[pl.ds(k0, ts), :] = lax.shift_right_arithmetic(key, 16).astype(i16)
        return carry

    lax.fori_loop(0, nkb, score_blk, 0)

    n_acc = 4

    def count(pred_fn):
        sub8 = lax.broadcasted_iota(i32, (8, 1), 0)

        def blk(kb, cnts):
            k0 = pl.multiple_of(kb * ts, ts)
            cnts = list(cnts)
            blk_ref = key_ref.at[pl.ds(k0, ts)]
            for j in range(ts // 8):
                kv = blk_ref[j * 8:(j + 1) * 8, :]
                cnts[j % n_acc] = cnts[j % n_acc] + jnp.where(pred_fn(kv, k0 + j * 8 + sub8), 1, 0)
            return tuple(cnts)
        cnts = lax.fori_loop(0, nkb, blk, tuple(jnp.zeros((8, tq), i32) for _ in range(n_acc)))
        return jnp.sum(sum(cnts[1:], cnts[0]), axis=0, keepdims=True)

    def count_half_ge(cand):
        one, zero = jnp.int16(1), jnp.int16(0)

        def blk(kb, cnts):
            k0 = pl.multiple_of(kb * ts, ts)
            cnts = list(cnts)
            blk_ref = half_ref.at[pl.ds(k0, ts)]
            for j in range(ts // 16):
                hv = blk_ref[j * 16:(j + 1) * 16, :]
                cnts[j % n_acc] = cnts[j % n_acc] + jnp.where(hv >= cand, one, zero)
            return tuple(cnts)
        cnts = lax.fori_loop(0, nkb, blk, tuple(jnp.zeros((16, tq), i16) for _ in range(n_acc)))
        return jnp.sum(sum(cnts[1:], cnts[0]).astype(i32), axis=0, keepdims=True)

    def search_half(neg_half, n_start):
        def body(it, c):
            tau, n_tau = c
            t2 = tau | lax.shift_left(jnp.int32(1), 15 - it)
            cand = t2 + HALF_MIN
            n2 = count_half_ge(cand.astype(i16)) + jnp.where(cand <= neg_half, rem, 0)
            ok = n2 >= topk
            return jnp.where(ok, t2, tau), jnp.where(ok, n2, n_tau)

        return lax.fori_loop(0, 16, body, (jnp.zeros((1, tq), i32), n_start))

    tau_hi, n_hi = search_half(jnp.full((1, tq), _KEY_NEG_HI, i32), jnp.full((1, tq), lp, i32))
    t_hi = tau_hi + HALF_MIN

    base = lax.shift_left(t_hi, 16)

    def low_blk(kb, carry):
        k0 = pl.multiple_of(kb * ts, ts)
        key = key_ref[pl.ds(k0, ts), :]
        lo = jnp.minimum(jnp.maximum(key, base), base + (HALF_SPAN - 1)) - (base - HALF_MIN)
        half_ref[pl.ds(k0, ts), :] = lo.astype(i16)
        return carry

    lax.fori_loop(0, nkb, low_blk, 0)
    neg_lo = jnp.where(_KEY_NEG_HI > t_hi, HALF_MAX, jnp.where(_KEY_NEG_HI == t_hi, _KEY_NEG_LO, HALF_MIN))
    tau_lo, n_ge = search_half(neg_lo, n_hi)
    thr = lax.shift_left(t_hi, 16) | tau_lo
    has_tie = jnp.max(jnp.where(n_ge > topk, 1, 0)) > 0

    def tie_cut():
        n_gt = (count(lambda kv, kpos: kv > thr) + jnp.where(thr < _KEY_NEG, rem, 0))
        need = topk - n_gt

        def bis2(it, cut):
            bit = lax.shift_left(jnp.int32(1), 14 - it)
            c2 = cut | bit
            n = count(lambda kv, kpos: jnp.logical_and(kv == thr, kpos < c2))
            return jnp.where(n < need, c2, cut)
        return lax.fori_loop(0, 15, bis2, jnp.zeros((1, tq), i32))

    cut = lax.cond(has_tie, tie_cut, lambda: jnp.full((1, tq), INT_MAX, i32))

    def bias_blk(kb, carry):
        k0 = pl.multiple_of(kb * ts, ts)
        kv = key_ref[pl.ds(k0, ts), :]
        interior = jnp.logical_and(k0 >= FRAME_OFF, k0 + ts - 1 <= i * tq)

        @pl.when(jnp.logical_and(interior, jnp.logical_not(has_tie)))
        def _():
            bias_ref[pl.ds(k0, ts), :] = jnp.where(kv >= thr, 0.0, NEG).astype(bias_ref.dtype)

        @pl.when(jnp.logical_not(jnp.logical_and(interior, jnp.logical_not(has_tie))))
        def _():
            kpos = k0 + sub
            sel = jnp.logical_or(kv > thr, jnp.logical_and(kv == thr, kpos <= cut))
            vis = jnp.logical_and(kpos <= qpos, kpos >= FRAME_OFF)
            bias_ref[pl.ds(k0, ts), :] = jnp.where(jnp.logical_and(sel, vis), 0.0, NEG).astype(bias_ref.dtype)
        return carry

    lax.fori_loop(0, nkb, bias_blk, 0)

    def fill_blk(kb, carry):
        k0 = pl.multiple_of(kb * ts, ts)
        bias_ref[pl.ds(k0, ts), :] = jnp.full((ts, tq), NEG, bias_ref.dtype)
        return carry

    lax.fori_loop(nkb, lp // ts, fill_blk, 0)


def dsa_select(iq, ik, iwt, topk):
    nh, lp, _ = iq.shape
    assert lp < 2 ** 15
    tq = 256
    ts = _pick(lp, (640, 256, 128))
    kern = functools.partial(_select_kernel, tq=tq, ts=ts, lp=lp, topk=topk)
    return pl.pallas_call(
        kern,
        grid=(lp // tq,),
        in_specs=[pl.BlockSpec((nh, tq, HEAD_DIM), lambda i: (0, i, 0)),
                  pl.BlockSpec((lp, HEAD_DIM), lambda i: (0, 0), pipeline_mode=pl.Buffered(1)),
                  pl.BlockSpec((nh, tq), lambda i: (0, i))],
        out_specs=pl.BlockSpec((lp, tq), lambda i: (0, i)),
        out_shape=jax.ShapeDtypeStruct((lp, lp), bf16),
        scratch_shapes=[pltpu.VMEM((lp, tq), i32), pltpu.VMEM((lp, tq), i16)],
        compiler_params=_cparams(("parallel",)),
        name="dsa_select",
    )(iq, ik, iwt)


V_AUG = HEAD_DIM + 16


def _attn_kernel(q_ref, k_ref, vt_ref, b_ref, o_ref, m_ref, acc_ref, *, tq, ts, bpt):
    i = pl.program_id(0)
    nkb = ((i + 1) * tq + ts - 1) // ts
    group = ATT_HEADS // ATT_KV_HEADS
    kv_groups = range(ATT_KV_HEADS)
    qpos = i * tq + lax.broadcasted_iota(i32, (tq, 1), 0)
    eye = (lax.broadcasted_iota(i32, (tq, tq), 0)
           == lax.broadcasted_iota(i32, (tq, tq), 1)).astype(bf16)
    qa = [jnp.concatenate(
        [jnp.concatenate([q_ref[:, (g * group + r) * HEAD_DIM:(g * group + r + 1) * HEAD_DIM], eye],
                         axis=1) for r in range(group)], axis=0) for g in kv_groups]
    m_ref[...] = jnp.full(m_ref.shape, -jnp.inf, f32)
    acc_ref[...] = jnp.zeros_like(acc_ref)

    def scores(g, k0):
        ka = jnp.concatenate([k_ref[pl.ds(k0, ts), g * HEAD_DIM:(g + 1) * HEAD_DIM],
                              b_ref[pl.ds(k0, ts), :]], axis=1)
        return _dot_nt(ka, qa[g]).astype(bf16)

    def accumulate(g, k0, st):
        vt_ = vt_ref[g, :, pl.ds(k0, ts)]
        ps, alphas = [], []
        for r in range(group):
            sr = st[:, r * tq:(r + 1) * tq]
            m_old = m_ref[g, r:r + 1, :]
            m_new = jnp.maximum(m_old, jnp.max(sr, axis=0, keepdims=True).astype(f32))
            m_ref[g, r:r + 1, :] = m_new
            ps.append(jnp.exp2(sr - m_new.astype(bf16)))
            alphas.append(jnp.exp2(m_old - m_new))
        pt = jnp.concatenate(ps, axis=1)
        acc_ref[g] = jnp.concatenate(alphas, axis=1) * acc_ref[g] + _dot(vt_, pt)

    def make_trip(nblk, first_block):
        def trip(kp, carry):
            k0s = [pl.multiple_of((first_block + kp * nblk + b) * ts, ts) for b in range(nblk)]
            work = [(g, kk) for kk in k0s for g in kv_groups]
            st = scores(*work[0])
            for n in range(len(work)):
                st_next = scores(*work[n + 1]) if n + 1 < len(work) else None
                accumulate(*work[n], st)
                st = st_next
            return carry
        return trip

    n_full = nkb // bpt
    lax.fori_loop(0, n_full, make_trip(bpt, 0), 0)
    if bpt > 1:
        lax.fori_loop(0, nkb - n_full * bpt, make_trip(1, n_full * bpt), 0)
    for g in kv_groups:
        for r in range(group):
            ot = (acc_ref[g, 0:HEAD_DIM, r * tq:(r + 1) * tq]
                  / acc_ref[g, HEAD_DIM:HEAD_DIM + 1, r * tq:(r + 1) * tq])
            orr = jnp.where(qpos >= FRAME_OFF, ot.T, 0.0)
            o_ref[:, (g * group + r) * HEAD_DIM:(g * group + r + 1) * HEAD_DIM] = orr.astype(o_ref.dtype)


def dsa_attention(q, k, vt_aug, bias_t):
    lp = q.shape[0]
    tq = 128
    ts = _pick(lp, (1280, 640, 256, 128))
    bpt = 2
    group = ATT_HEADS // ATT_KV_HEADS
    kern = functools.partial(_attn_kernel, tq=tq, ts=ts, bpt=bpt)
    resident = pl.Buffered(1)
    return pl.pallas_call(
        kern,
        grid=(lp // tq,),
        in_specs=[pl.BlockSpec((tq, ATT_Q_W), lambda i: (i, 0)),
                  pl.BlockSpec((lp, ATT_KV_W), lambda i: (0, 0), pipeline_mode=resident),
                  pl.BlockSpec((ATT_KV_HEADS, V_AUG, lp), lambda i: (0, 0, 0), pipeline_mode=resident),
                  pl.BlockSpec((lp, tq), lambda i: (0, i))],
        out_specs=pl.BlockSpec((tq, ATT_Q_W), lambda i: (i, 0)),
        out_shape=jax.ShapeDtypeStruct((lp, ATT_Q_W), bf16),
        scratch_shapes=[pltpu.VMEM((ATT_KV_HEADS, group, tq), f32),
                        pltpu.VMEM((ATT_KV_HEADS, V_AUG, group * tq), f32)],
        compiler_params=_cparams(("parallel",)),
        name="dsa_attention",
    )(q, k, vt_aug, bias_t)


def _ffn_up_kernel(a_ref, ah_ref, wg_ref, wv_ref, cg_ref, cv_ref, bg_ref, bv_ref, o_ref,
                   eg_ref, ev_ref, *, tm):
    a = _halo_rows(a_ref, ah_ref, pl.program_id(0))
    eg_ref[...] = _dot(a, wg_ref[...])
    ev_ref[...] = _dot(a, wv_ref[...])

    def conv(ext_ref, w_ref, b_ref):
        w = w_ref[...]
        y = ext_ref[HALO:HALO + tm, :] * w[FFN_CONV - 1:FFN_CONV, :]
        for j in range(FFN_CONV - 1):
            s0 = HALO - (FFN_CONV - 1) + j
            y = y + ext_ref[s0:s0 + tm, :] * w[j:j + 1, :]
        return y + b_ref[...]

    gate = conv(eg_ref, cg_ref, bg_ref)
    val = conv(ev_ref, cv_ref, bv_ref)
    o_ref[...] = (gate * _sigmoid(gate) * val).astype(o_ref.dtype)


def ffn_up(u, w_up, conv_w, conv_b):
    lp, kdim = u.shape
    tm = _pick(lp, (1280, 640, 256, 128))
    tn = 512
    nc = D_FF // tn
    kern = functools.partial(_ffn_up_kernel, tm=tm)

    def wspec(off):
        return pl.BlockSpec((kdim, tn), lambda i, c: (0, c + off))

    def cspec(off):
        return pl.BlockSpec((FFN_CONV, tn), lambda i, c: (0, c + off))

    def bspec(off):
        return pl.BlockSpec((1, tn), lambda i, c: (0, c + off))

    b2 = conv_b.reshape(1, 2 * D_FF)
    return pl.pallas_call(
        kern,
        grid=(lp // tm, nc),
        in_specs=[pl.BlockSpec((tm, kdim), lambda i, c: (i, 0)),
                  pl.BlockSpec((HALO, kdim), lambda i, c: (jnp.maximum(i * (tm // HALO) - 1, 0), 0)),
                  wspec(0), wspec(nc), cspec(0), cspec(nc), bspec(0), bspec(nc)],
        out_specs=pl.BlockSpec((tm, tn), lambda i, c: (i, c)),
        out_shape=jax.ShapeDtypeStruct((lp, D_FF), bf16),
        scratch_shapes=[pltpu.VMEM((HALO + tm, tn), f32)] * 2,
        compiler_params=_cparams(("parallel", "arbitrary")),
        name="ffn_up",
    )(u, u, w_up, w_up, conv_w, conv_w, b2, b2)


def _split_w_in(w):
    wt = w.T
    o = 0
    parts = {}
    for name, width in (("gq", GDN_QK_W), ("gk", GDN_QK_W), ("gv", GDN_V_W), ("gz", GDN_V_W),
                        ("gb", GDN_V_HEADS), ("ga", GDN_V_HEADS), ("aq", ATT_Q_W), ("ak", ATT_KV_W),
                        ("av", ATT_KV_W), ("iq", IDX_Q_W), ("ik", HEAD_DIM), ("iw", IDX_HEADS),
                        ("gate_gdn", D_MODEL), ("gate_att", D_MODEL)):
        parts[name] = wt[o:o + width]
        o += width
    cat = lambda names: jnp.concatenate([parts[n] for n in names], axis=0)
    small = cat(("gb", "ga", "iw"))
    small = jnp.pad(small, ((0, LANES - small.shape[0]), (0, 0)))
    groups = dict(qkv=cat(("gq", "gk", "gv")), aq=parts["aq"], iq=parts["iq"], kv=cat(("ak", "ik")),
                  plain=cat(("gz", "gate_gdn", "gate_att", "av")), small=small)
    return {k: v.astype(bf16) for k, v in groups.items()}


def _layer(h0, p, lp, topk):
    tm = _pick(lp, (640, 256, 128))
    tm_big = _pick(lp, (1280, 640, 256, 128))
    wp = _split_w_in(p["w_in"])

    def proj(w, **kw):
        return matmul(u1, w, tm=tm_big, tk=D_MODEL, w_nt=True, **kw)

    u1 = rms_rows(h0, p["mix_pre_g"])
    tabs = rope_tables(lp)
    tab_specs = (((tm_big, HEAD_DIM), lambda i, j: (i, 0)),) * 3
    qkv = gdn_proj(u1, wp["qkv"], p["gdn_conv_w"])
    (aq,) = proj(wp["aq"], tn=1024, epilogue=_make_ep_rope(HEAD_DIM ** -0.5 * LOG2E, False),
                 extra=tabs, extra_specs=tab_specs,
                 out_shapes=[jax.ShapeDtypeStruct((lp, ATT_Q_W), bf16)], name="proj_aq")
    iq_heads = 1024 // HEAD_DIM
    (iq,) = proj(wp["iq"], tn=1024, epilogue=_make_ep_rope(1.0, True), extra=tabs, extra_specs=tab_specs,
                 out_shapes=[jax.ShapeDtypeStruct((IDX_HEADS, lp, HEAD_DIM), bf16)],
                 out_block_specs=[((iq_heads, tm_big, HEAD_DIM), lambda i, j: (j, i, 0))], name="proj_iq")
    ak, ik = proj(wp["kv"], tn=ATT_KV_W + HEAD_DIM, epilogue=_ep_rope_kv, extra=tabs, extra_specs=tab_specs,
                  out_shapes=[jax.ShapeDtypeStruct((lp, ATT_KV_W), bf16),
                              jax.ShapeDtypeStruct((lp, HEAD_DIM), bf16)],
                  out_block_specs=[((tm_big, ATT_KV_W), lambda i, j: (i, 0)),
                                   ((tm_big, HEAD_DIM), lambda i, j: (i, 0))], name="proj_kv")
    (plain,) = proj(wp["plain"], tn=768, epilogue=_ep_cast,
                    out_shapes=[jax.ShapeDtypeStruct((lp, PLAIN_W), bf16)], name="proj_plain")
    (small,) = proj(wp["small"], tn=LANES, epilogue=_ep_cast,
                    out_shapes=[jax.ShapeDtypeStruct((lp, LANES), f32)], name="proj_small")
    gb = small[:, 0:GDN_V_HEADS]
    ga = small[:, GDN_V_HEADS:2 * GDN_V_HEADS]
    iw = small[:, 2 * GDN_V_HEADS:2 * GDN_V_HEADS + IDX_HEADS]

    beta, gcum = gdn_gates(gb, ga, p["gdn_a_log"], p["gdn_dt_bias"])
    o_gdn = gdn_core(qkv, plain, p["gdn_norm_g"], gcum.T[:, None, :], beta.T[:, None, :])

    avt = plain[:, COL_AV:COL_AV + ATT_KV_W].T.reshape(ATT_KV_HEADS, HEAD_DIM, lp)
    avt = jnp.concatenate([avt, jnp.ones((ATT_KV_HEADS, V_AUG - HEAD_DIM, lp), bf16)], axis=1)
    bias_t = dsa_select(iq, ik, iw.T, topk)
    o_att = dsa_attention(aq, ak, avt, bias_t)

    tn = 1024
    gate_spec = lambda col0: ((tm_big, tn), (lambda i, j, c=col0 // tn: (i, c + j)))
    (m1,) = matmul(o_gdn, p["w_branch_gdn"].astype(bf16), tm=tm_big, tn=tn, tk=2048,
                   epilogue=_ep_gate, extra=(plain,), extra_specs=(gate_spec(COL_GATE_GDN),),
                   out_shapes=[jax.ShapeDtypeStruct((lp, D_MODEL), f32)], name="branch_gdn")
    (merged,) = matmul(o_att, p["w_branch_att"].astype(bf16), tm=tm_big, tn=tn, tk=2048,
                       epilogue=_ep_gate_add, extra=(plain, m1),
                       extra_specs=(gate_spec(COL_GATE_ATT), ((tm_big, tn), lambda i, j: (i, j))),
                       out_shapes=[jax.ShapeDtypeStruct((lp, D_MODEL), bf16)], name="branch_att")
    row_spec = ((tm, D_MODEL), lambda i, j: (i, 0))
    vec_spec = ((1, D_MODEL), lambda i, j: (0, 0))
    h1, u2 = matmul(merged, p["w_out"].astype(bf16), tm=tm, tn=D_MODEL, tk=D_MODEL,
                    epilogue=_ep_res_norm2,
                    extra=(h0, p["mix_post_g"].reshape(1, D_MODEL), p["ffn_pre_g"].reshape(1, D_MODEL)),
                    extra_specs=(row_spec, vec_spec, vec_spec),
                    out_shapes=[jax.ShapeDtypeStruct((lp, D_MODEL), f32),
                                jax.ShapeDtypeStruct((lp, D_MODEL), bf16)], name="w_out")

    act = ffn_up(u2, p["w_up"].astype(bf16), p["ffn_conv_w"], p["ffn_conv_b"])
    (h2,) = matmul(act, p["w_down"].astype(bf16), tm=tm, tn=D_MODEL, tk=2048, epilogue=_ep_res_norm,
                   extra=(h1, p["ffn_post_g"].reshape(1, D_MODEL)),
                   extra_specs=(row_spec, vec_spec),
                   out_shapes=[jax.ShapeDtypeStruct((lp, D_MODEL), f32)], name="w_down")
    return h2


def kernel(x, meta_tokens, mix_pre_g, w_in, gdn_conv_w, gdn_a_log, gdn_dt_bias, gdn_norm_g,
           w_branch_gdn, w_branch_att, w_out, mix_post_g, ffn_pre_g, w_up, ffn_conv_w,
           ffn_conv_b, w_down, ffn_post_g):
    batch, seq, d = x.shape
    assert batch == 1 and d == D_MODEL
    lp = FRAME_X0 + seq
    topk = min(TOPK_MAX, (N_META + seq) // 4)
    h = jnp.concatenate([jnp.zeros((FRAME_OFF, d), x.dtype), meta_tokens.astype(x.dtype), x[0]], axis=0)
    for i in range(w_in.shape[0]):
        p = dict(mix_pre_g=mix_pre_g[i], w_in=w_in[i], gdn_conv_w=gdn_conv_w[i], gdn_a_log=gdn_a_log[i],
                 gdn_dt_bias=gdn_dt_bias[i], gdn_norm_g=gdn_norm_g[i], w_branch_gdn=w_branch_gdn[i],
                 w_branch_att=w_branch_att[i], w_out=w_out[i], mix_post_g=mix_post_g[i],
                 ffn_pre_g=ffn_pre_g[i], w_up=w_up[i], ffn_conv_w=ffn_conv_w[i],
                 ffn_conv_b=ffn_conv_b[i], w_down=w_down[i], ffn_post_g=ffn_post_g[i])
        h = _layer(h, p, lp, topk)
    return h[FRAME_X0:][None]
```

```python
import functools
import struct

import jax
import jax.numpy as jnp
from jax import lax
from jax.experimental import pallas as pl
from jax.experimental.pallas import tpu as pltpu

f32 = jnp.float32
bf16 = jnp.bfloat16
i32 = jnp.int32
i16 = jnp.int16

D_MODEL = 2048
N_META = 16
EPS = 1e-6
GDN_QK_HEADS = 16
GDN_V_HEADS = 32
HEAD_DIM = 128
GDN_CONV = 4
ATT_HEADS = 16
ATT_KV_HEADS = 2
IDX_HEADS = 16
TOPK_MAX = 256
NEG = -1e30
LOG2E = 1.4426950408889634
ROPE_THETA = 500000.0
ROPE_DIMS = HEAD_DIM // 4
D_FF = 3 * D_MODEL
FFN_CONV = 3
GDN_QK_W = GDN_QK_HEADS * HEAD_DIM
GDN_V_W = GDN_V_HEADS * HEAD_DIM
ATT_Q_W = ATT_HEADS * HEAD_DIM
ATT_KV_W = ATT_KV_HEADS * HEAD_DIM
IDX_Q_W = IDX_HEADS * HEAD_DIM

FRAME_X0 = 256
FRAME_OFF = FRAME_X0 - N_META
CHUNK = 128
GDN_QK_PER_STEP = 2

LANES = 128
VMEM_LIMIT = 56 * 1024 * 1024

COL_GZ = 0
COL_GATE_GDN = COL_GZ + GDN_V_W
COL_GATE_ATT = COL_GATE_GDN + D_MODEL
COL_AV = COL_GATE_ATT + D_MODEL
PLAIN_W = COL_AV + ATT_KV_W


def _cparams(sem):
    return pltpu.CompilerParams(dimension_semantics=sem, vmem_limit_bytes=VMEM_LIMIT)


def _pick(n, cands):
    for c in cands:
        if n % c == 0:
            return c
    raise ValueError(f"no tile for {n} in {cands}")


def _sigmoid(x):
    return 1.0 / (1.0 + jnp.exp(-x))


def _dot(a, b):
    return jnp.dot(a, b, preferred_element_type=f32)


def _dot_nt(a, b):
    return lax.dot_general(a, b, (((1,), (1,)), ((), ())), preferred_element_type=f32)


def _rms_rows_kernel(h_ref, g_ref, o_ref):
    h = h_ref[...]
    y = h * lax.rsqrt(jnp.mean(h * h, axis=-1, keepdims=True) + EPS)
    o_ref[...] = (y * g_ref[...]).astype(o_ref.dtype)


def rms_rows(h, g):
    lp, d = h.shape
    tm = _pick(lp, (640, 256, 128))
    return pl.pallas_call(
        _rms_rows_kernel,
        grid=(lp // tm,),
        in_specs=[pl.BlockSpec((tm, d), lambda i: (i, 0)),
                  pl.BlockSpec((1, d), lambda i: (0, 0))],
        out_specs=pl.BlockSpec((tm, d), lambda i: (i, 0)),
        out_shape=jax.ShapeDtypeStruct((lp, d), bf16),
        compiler_params=_cparams(("parallel",)),
        name="rms_rows",
    )(h, g.reshape(1, d))


def _mm_kernel(*refs, nk, n_extra, n_out, epilogue, w_nt):
    a_ref, w_ref = refs[0], refs[1]
    dot = _dot_nt if w_nt else _dot
    extra = refs[2:2 + n_extra]
    outs = refs[2 + n_extra:2 + n_extra + n_out]
    if nk == 1:
        epilogue(dot(a_ref[...], w_ref[...]), extra, outs)
        return
    acc_ref = refs[-1]
    k = pl.program_id(2)

    @pl.when(k == 0)
    def _():
        acc_ref[...] = jnp.zeros_like(acc_ref)

    acc_ref[...] += dot(a_ref[...], w_ref[...])

    @pl.when(k == nk - 1)
    def _():
        epilogue(acc_ref[...], extra, outs)


def matmul(a, w, *, tm, tn, tk, epilogue, extra=(), extra_specs=(), out_shapes, out_block_specs=None,
           a_col0=0, w_nt=False, rows_sem="parallel", name):
    m = a.shape[0]
    kdim, n = w.shape[::-1] if w_nt else w.shape
    nk = kdim // tk
    assert m % tm == 0 and n % tn == 0 and kdim % tk == 0
    w_mode = dict(pipeline_mode=pl.Buffered(1)) if (nk == 1 and n == tn) else {}
    in_specs = [pl.BlockSpec((tm, tk), lambda i, j, k: (i, a_col0 + k)),
                (pl.BlockSpec((tn, tk), lambda i, j, k: (j, k), **w_mode) if w_nt else
                 pl.BlockSpec((tk, tn), lambda i, j, k: (k, j), **w_mode))]
    in_specs += [pl.BlockSpec(bs, (lambda i, j, k, f=f: f(i, j))) for bs, f in extra_specs]
    if out_block_specs is None:
        out_block_specs = [((tm, tn), lambda i, j: (i, j))] * len(out_shapes)
    out_specs = [pl.BlockSpec(bs, (lambda i, j, k, f=f: f(i, j))) for bs, f in out_block_specs]
    scratch = [] if nk == 1 else [pltpu.VMEM((tm, tn), f32)]
    kern = functools.partial(_mm_kernel, nk=nk, n_extra=len(extra), n_out=len(out_shapes),
                             epilogue=epilogue, w_nt=w_nt)
    res = pl.pallas_call(
        kern,
        grid=(m // tm, n // tn, nk),
        in_specs=in_specs,
        out_specs=out_specs,
        out_shape=out_shapes,
        scratch_shapes=scratch,
        compiler_params=_cparams((rows_sem, rows_sem, "arbitrary")),
        name=name,
    )(a, w, *extra)
    return res


def _ep_cast(acc, extra, outs):
    outs[0][...] = acc.astype(outs[0].dtype)


def _ep_gate(acc, extra, outs):
    g = extra[0][...].astype(f32)
    outs[0][...] = (_sigmoid(g) * acc).astype(outs[0].dtype)


def _ep_gate_add(acc, extra, outs):
    g = extra[0][...].astype(f32)
    outs[0][...] = (extra[1][...].astype(f32) + _sigmoid(g) * acc).astype(outs[0].dtype)


def _rms(t, g):
    return t * lax.rsqrt(jnp.mean(t * t, axis=-1, keepdims=True) + EPS) * g


def _ep_res_norm2(acc, extra, outs):
    h_ref, g_ref, g2_ref = extra
    h1 = h_ref[...] + _rms(acc, g_ref[...])
    outs[0][...] = h1
    outs[1][...] = _rms(h1, g2_ref[...]).astype(outs[1].dtype)


def _ep_res_norm(acc, extra, outs):
    h_ref, g_ref = extra
    outs[0][...] = h_ref[...] + _rms(acc, g_ref[...])


HALO = 16


def _halo_rows(a_ref, ah_ref, i):
    halo = ah_ref[...]
    halo = jnp.where(i > 0, halo, jnp.zeros_like(halo))
    return jnp.concatenate([halo, a_ref[...]], axis=0)


def _gdn_proj_kernel(a_ref, ah_ref, w_ref, cw_ref, o_ref, ext_ref, *, tm, tn):
    c = pl.program_id(1)
    ext_ref[...] = _dot_nt(_halo_rows(a_ref, ah_ref, pl.program_id(0)), w_ref[...])
    w = cw_ref[...]
    y = ext_ref[HALO:HALO + tm, :] * w[GDN_CONV - 1:GDN_CONV, :]
    for j in range(GDN_CONV - 1):
        s0 = HALO - (GDN_CONV - 1) + j
        y = y + ext_ref[s0:s0 + tm, :] * w[j:j + 1, :]
    s = y * _sigmoid(y)
    is_q = c < (GDN_QK_W // tn)
    is_qk = c < (2 * GDN_QK_W // tn)
    qscale = jnp.where(is_q, HEAD_DIM ** -0.5, 1.0).astype(f32)
    for hh in range(tn // HEAD_DIM):
        seg = s[:, hh * HEAD_DIM:(hh + 1) * HEAD_DIM]
        r = lax.rsqrt(jnp.sum(seg * seg, axis=-1, keepdims=True) + EPS) * qscale
        fac = jnp.where(is_qk, r, 1.0)
        o_ref[:, hh * HEAD_DIM:(hh + 1) * HEAD_DIM] = (seg * fac).astype(o_ref.dtype)


def gdn_proj(u, w_qkv, conv_w):
    lp, kdim = u.shape
    width = w_qkv.shape[0]
    tm = _pick(lp, (1280, 640, 256, 128))
    tn = 1024
    kern = functools.partial(_gdn_proj_kernel, tm=tm, tn=tn)
    return pl.pallas_call(
        kern,
        grid=(lp // tm, width // tn),
        in_specs=[pl.BlockSpec((tm, kdim), lambda i, c: (i, 0)),
                  pl.BlockSpec((HALO, kdim), lambda i, c: (jnp.maximum(i * (tm // HALO) - 1, 0), 0)),
                  pl.BlockSpec((tn, kdim), lambda i, c: (c, 0)),
                  pl.BlockSpec((GDN_CONV, tn), lambda i, c: (0, c))],
        out_specs=pl.BlockSpec((tm, tn), lambda i, c: (i, c)),
        out_shape=jax.ShapeDtypeStruct((lp, width), bf16),
        scratch_shapes=[pltpu.VMEM((HALO + tm, tn), f32)],
        compiler_params=_cparams(("parallel", "arbitrary")),
        name="gdn_proj",
    )(u, u, w_qkv, conv_w)


def _gdn_gates_kernel(b_ref, a_ref, alog_ref, dt_ref, beta_ref, gcum_ref, *, tm):
    i = pl.program_id(0)
    rows = i * tm + lax.broadcasted_iota(i32, (tm, 1), 0)
    valid = rows >= FRAME_OFF
    beta_ref[...] = jnp.where(valid, _sigmoid(b_ref[...]), 0.0)
    a = a_ref[...] + dt_ref[...]
    sp = jnp.maximum(a, 0.0) + jnp.log1p(jnp.exp(-jnp.abs(a)))
    g = jnp.where(valid, -jnp.exp(alog_ref[...]) * sp, 0.0)
    tri = (lax.broadcasted_iota(i32, (CHUNK, CHUNK), 0)
           >= lax.broadcasted_iota(i32, (CHUNK, CHUNK), 1)).astype(f32)
    for c in range(tm // CHUNK):
        rs = slice(c * CHUNK, (c + 1) * CHUNK)
        gcum_ref[rs, :] = jnp.dot(tri, g[rs, :], preferred_element_type=f32,
                                  precision=lax.Precision.HIGHEST)


def gdn_gates(gb, ga, a_log, dt_bias):
    lp, nh = gb.shape
    tm = _pick(lp, (1280, 640, 256, 128))
    spec = pl.BlockSpec((tm, nh), lambda i: (i, 0))
    vec = pl.BlockSpec((1, nh), lambda i: (0, 0))
    return pl.pallas_call(
        functools.partial(_gdn_gates_kernel, tm=tm),
        grid=(lp // tm,),
        in_specs=[spec, spec, vec, vec],
        out_specs=[spec, spec],
        out_shape=[jax.ShapeDtypeStruct((lp, nh), f32)] * 2,
        compiler_params=_cparams(("parallel",)),
        name="gdn_gates",
    )(gb, ga, a_log.reshape(1, nh), dt_bias.reshape(1, nh))


def _block_mask(size):
    r = lax.broadcasted_iota(i32, (CHUNK, CHUNK), 0) // size
    c = lax.broadcasted_iota(i32, (CHUNK, CHUNK), 1) // size
    return r == c


def _unit_lower_inverse_many(mats, tick):
    row = lax.broadcasted_iota(i32, (CHUNK, CHUNK), 0)
    col = lax.broadcasted_iota(i32, (CHUNK, CHUNK), 1)
    eye = (row == col).astype(f32)
    base = 8
    m_prev = _block_mask(base)
    ads = [jnp.where(m_prev, a, 0.0) for a in mats]
    adbs = [ad.astype(bf16) for ad in ads]
    a2s = [_dot(x, x) for x in adbs]
    tick()
    a2bs = [x.astype(bf16) for x in a2s]
    a4s = [_dot(x, x) for x in a2bs]
    tick()
    xs = [_dot((eye - ad).astype(bf16), (eye + a2).astype(bf16)) for ad, a2 in zip(ads, a2s)]
    tick()
    xs = [_dot(x.astype(bf16), (eye + a4).astype(bf16)) for x, a4 in zip(xs, a4s)]
    tick()
    size = base * 2
    while size <= CHUNK:
        m_cur = _block_mask(size)
        off_diag = jnp.logical_and(m_cur, jnp.logical_not(m_prev))
        bs = [jnp.where(off_diag, a, 0.0).astype(bf16) for a in mats]
        xbs = [x.astype(bf16) for x in xs]
        ys = [_dot(xb, b).astype(bf16) for xb, b in zip(xbs, bs)]
        tick()
        xs = [x - _dot(y, xb) for x, y, xb in zip(xs, ys, xbs)]
        tick()
        m_prev = m_cur
        size *= 2
    return xs


def _gdn_kernel(q_ref, k_ref, v_ref, z_ref, ng_ref, *rest, rb, nq):
    r = pl.program_id(1)
    nv = 2 * nq
    grs, brs = rest[:nv], rest[nv:2 * nv]
    o_ref, s_ref, u_ref, wq_ref, ak_ref, eg_ref, egl_ref = rest[2 * nv:]

    @pl.when(r == 0)
    def _():
        for ref in (s_ref, u_ref, wq_ref, ak_ref, eg_ref, egl_ref):
            ref[...] = jnp.zeros_like(ref)

    row = lax.broadcasted_iota(i32, (CHUNK, CHUNK), 0)
    col = lax.broadcasted_iota(i32, (CHUNK, CHUNK), 1)
    incl = row >= col
    strict = row > col
    ng = ng_ref[...]
    nchunk = rb // CHUNK
    heads = range(nv)

    def rows(c):
        return slice(c * CHUNK, (c + 1) * CHUNK)

    def lanes(e):
        return slice(e * HEAD_DIM, (e + 1) * HEAD_DIM)

    chains = [(c, e) for c in range(nchunk) for e in heads]

    def recurrence():
        state = [s_ref[e] for e in heads]
        for c in range(nchunk):
            n = [chains.index((c, e)) for e in heads]
            sb = [state[e].astype(bf16) for e in heads]
            ws = [_dot(wq_ref[n[e]], sb[e]) for e in heads]
            yield
            vb = [(u_ref[n[e]] - ws[e][:CHUNK]).astype(bf16) for e in heads]
            av = [_dot(ak_ref[n[e]], vb[e]) for e in heads]
            yield
            for e in heads:
                o = eg_ref[n[e]] * ws[e][CHUNK:] + av[e][:CHUNK]
                state[e] = state[e] * egl_ref[n[e]] + av[e][CHUNK:]
                z = z_ref[rows(c), lanes(e)].astype(f32)
                o_ref[rows(c), lanes(e)] = (_rms(o, ng) * (z * _sigmoid(z))).astype(o_ref.dtype)
        for e in heads:
            s_ref[e] = state[e]

    rec = recurrence()

    def tick():
        next(rec, None)

    qkh = [(c, a) for c in range(nchunk) for a in range(nq)]
    qs = {ca: q_ref[rows(ca[0]), lanes(ca[1])] for ca in qkh}
    ks = {ca: k_ref[rows(ca[0]), lanes(ca[1])] for ca in qkh}
    kts = {ca: ks[ca].astype(f32).T for ca in qkh}
    kks = {ca: _dot_nt(ks[ca], ks[ca]) for ca in qkh}
    qks = {ca: _dot_nt(qs[ca], ks[ca]) for ca in qkh}
    qk_of = {ce: (ce[0], ce[1] // 2) for ce in chains}
    tick()
    def per_row(v):
        return jnp.broadcast_to(v, (CHUNK, CHUNK)).T

    gr = {ce: grs[ce[1]][:, rows(ce[0])] for ce in chains}
    br = {ce: brs[ce[1]][:, rows(ce[0])] for ce in chains}
    gc = {ce: per_row(gr[ce]) for ce in chains}
    bc = {ce: per_row(br[ce]) for ce in chains}
    dec = {ce: jnp.where(incl, jnp.exp(jnp.where(incl, gc[ce] - gr[ce], 0.0)), 0.0) for ce in chains}
    amat = [jnp.where(strict, bc[ce] * kks[qk_of[ce]] * dec[ce], 0.0) for ce in chains]
    tinv = dict(zip(chains, _unit_lower_inverse_many(amat, tick)))
    u = {ce: _dot((tinv[ce] * br[ce]).astype(bf16), v_ref[rows(ce[0]), lanes(ce[1])]) for ce in chains}
    tick()
    w = {ce: _dot((tinv[ce] * (br[ce] * jnp.exp(gr[ce]))).astype(bf16), ks[qk_of[ce]]) for ce in chains}
    tick()
    g_last = {ce: gr[ce][:, CHUNK - 1:CHUNK] for ce in chains}
    wq = {ce: jnp.concatenate([w[ce].astype(bf16), qs[qk_of[ce]]], axis=0) for ce in chains}
    ak = {ce: jnp.concatenate(
        [jnp.where(incl, qks[qk_of[ce]] * dec[ce], 0.0).astype(bf16),
         (kts[qk_of[ce]] * jnp.exp(g_last[ce] - gr[ce])).astype(bf16)], axis=0) for ce in chains}
    for _ in rec:
        pass
    for n, ce in enumerate(chains):
        u_ref[n] = u[ce]
        wq_ref[n] = wq[ce]
        ak_ref[n] = ak[ce]
        eg_ref[n] = jnp.exp(gc[ce])
        egl_ref[n] = jnp.broadcast_to(jnp.exp(g_last[ce]), (1, HEAD_DIM))


def gdn_core(qkv, plain, norm_g, grow, brow):
    assert CHUNK == HEAD_DIM
    lp = qkv.shape[0]
    rb = _pick(lp, (640, 256, 128))
    nq = GDN_QK_PER_STEP
    nv = 2 * nq
    qw, vw = nq * HEAD_DIM, nv * HEAD_DIM
    kcol0 = GDN_QK_W // qw
    vcol0 = 2 * GDN_QK_W // vw
    zcol0 = COL_GZ // vw

    nr = lp // rb
    n_chain = nv * (rb // CHUNK)

    def cur(r):
        return jnp.minimum(r, nr - 1)

    def prev(r):
        return jnp.maximum(r - 1, 0)

    def rowspec(e):
        return pl.BlockSpec((None, 1, rb), lambda j, r: (nv * j + e, 0, cur(r)))

    rowspecs = [rowspec(e) for e in range(nv)]
    kern = functools.partial(_gdn_kernel, rb=rb, nq=nq)
    return pl.pallas_call(
        kern,
        grid=(GDN_QK_HEADS // nq, nr + 1),
        in_specs=[pl.BlockSpec((rb, qw), lambda j, r: (cur(r), j)),
                  pl.BlockSpec((rb, qw), lambda j, r: (cur(r), kcol0 + j)),
                  pl.BlockSpec((rb, vw), lambda j, r: (cur(r), vcol0 + j)),
                  pl.BlockSpec((rb, vw), lambda j, r: (prev(r), zcol0 + j)),
                  pl.BlockSpec((1, HEAD_DIM), lambda j, r: (0, 0))] + rowspecs + rowspecs,
        out_specs=pl.BlockSpec((rb, vw), lambda j, r: (prev(r), j)),
        out_shape=jax.ShapeDtypeStruct((lp, GDN_V_W), bf16),
        scratch_shapes=[pltpu.VMEM((nv, HEAD_DIM, HEAD_DIM), f32),
                        pltpu.VMEM((n_chain, CHUNK, HEAD_DIM), f32),
                        pltpu.VMEM((n_chain, 2 * CHUNK, HEAD_DIM), bf16),
                        pltpu.VMEM((n_chain, 2 * CHUNK, CHUNK), bf16),
                        pltpu.VMEM((n_chain, CHUNK, HEAD_DIM), f32),
                        pltpu.VMEM((n_chain, 1, HEAD_DIM), f32)],
        compiler_params=_cparams(("parallel", "arbitrary")),
        name="gdn_core",
    )(qkv, qkv, qkv, plain, norm_g.reshape(1, HEAD_DIM), *([grow] * nv), *([brow] * nv))


def _rope_head(seg, cc, s1, s2):
    half = ROPE_DIMS // 2
    return seg * cc + pltpu.roll(seg, half, 1) * s1 + pltpu.roll(seg, HEAD_DIM - half, 1) * s2


def _make_ep_rope(scale, head_major):
    def ep(acc, extra, outs):
        cc, s1, s2 = (r[...] for r in extra)
        for h in range(acc.shape[1] // HEAD_DIM):
            y = _rope_head(acc[:, h * HEAD_DIM:(h + 1) * HEAD_DIM], cc, s1, s2)
            if scale != 1.0:
                y = y * scale
            if head_major:
                outs[0][h] = y.astype(outs[0].dtype)
            else:
                outs[0][:, h * HEAD_DIM:(h + 1) * HEAD_DIM] = y.astype(outs[0].dtype)
    return ep


def _ep_rope_kv(acc, extra, outs):
    cc, s1, s2 = (r[...] for r in extra)
    for h in range(ATT_KV_HEADS):
        y = _rope_head(acc[:, h * HEAD_DIM:(h + 1) * HEAD_DIM], cc, s1, s2)
        outs[0][:, h * HEAD_DIM:(h + 1) * HEAD_DIM] = y.astype(outs[0].dtype)
    y = _rope_head(acc[:, ATT_KV_W:ATT_KV_W + HEAD_DIM], cc, s1, s2)
    outs[1][...] = y.astype(outs[1].dtype)


def rope_tables(lp):
    half = ROPE_DIMS // 2
    pos = (jnp.arange(lp, dtype=jnp.int32) - FRAME_OFF).astype(f32)
    inv = ROPE_THETA ** (-jnp.arange(half, dtype=f32) / half)
    ang = pos[:, None] * inv[None, :]
    cos, sin = jnp.cos(ang), jnp.sin(ang)
    zeros = jnp.zeros((lp, HEAD_DIM - ROPE_DIMS), f32)
    z16 = jnp.zeros((lp, half), f32)
    cc = jnp.concatenate([cos, cos, jnp.ones_like(zeros)], axis=1)
    s1 = jnp.concatenate([z16, sin, zeros], axis=1)
    s2 = jnp.concatenate([-sin, z16, zeros], axis=1)
    return cc, s1, s2


INT_MAX = 2 ** 31 - 1
HALF_MIN, HALF_MAX, HALF_SPAN = -2 ** 15, 2 ** 15 - 1, 2 ** 16


def _sortable(x):
    b = lax.bitcast_convert_type(x, i32)
    return b ^ (lax.shift_right_arithmetic(b, 31) & INT_MAX)


_NEG_BITS = struct.unpack("<i", struct.pack("<f", NEG))[0]
_KEY_NEG = _NEG_BITS ^ ((_NEG_BITS >> 31) & INT_MAX)
_KEY_NEG_HI = _KEY_NEG >> 16
_KEY_NEG_LO = (_KEY_NEG & (HALF_SPAN - 1)) + HALF_MIN


def _select_kernel(iq_ref, ik_ref, iwt_ref, bias_ref, key_ref, half_ref, *, tq, ts, lp, topk):
    i = pl.program_id(0)
    nkb = ((i + 1) * tq + ts - 1) // ts
    rem = lp - nkb * ts
    iwt = iwt_ref[...] * ((IDX_HEADS ** -0.5) * (HEAD_DIM ** -0.5))
    qpos = i * tq + lax.broadcasted_iota(i32, (1, tq), 1)
    sub = lax.broadcasted_iota(i32, (ts, 1), 0)

    def score_blk(kb, carry):
        k0 = pl.multiple_of(kb * ts, ts)
        ikb = ik_ref[pl.ds(k0, ts), :]
        acc = jnp.zeros((ts, tq), f32)
        for h in range(IDX_HEADS):
            lg = _dot_nt(ikb, iq_ref[h])
            acc = acc + jnp.maximum(lg, 0.0) * iwt[h:h + 1, :]
        kpos = k0 + sub
        sc = jnp.where(kpos <= qpos, acc, NEG)
        sc = jnp.where(kpos < FRAME_OFF, -jnp.inf, sc)
        key = _sortable(sc)
        key_ref[pl.ds(k0, ts), :] = key
        half_ref[pl.ds(k0, ts), :] = lax.shift_right_arithmetic(key, 16).astype(i16)
        return carry

    lax.fori_loop(0, nkb, score_blk, 0)

    n_acc = 4

    def count(pred_fn):
        sub8 = lax.broadcasted_iota(i32, (8, 1), 0)

        def blk(kb, cnts):
            k0 = pl.multiple_of(kb * ts, ts)
            cnts = list(cnts)
            blk_ref = key_ref.at[pl.ds(k0, ts)]
            for j in range(ts // 8):
                kv = blk_ref[j * 8:(j + 1) * 8, :]
                cnts[j % n_acc] = cnts[j % n_acc] + jnp.where(pred_fn(kv, k0 + j * 8 + sub8), 1, 0)
            return tuple(cnts)
        cnts = lax.fori_loop(0, nkb, blk, tuple(jnp.zeros((8, tq), i32) for _ in range(n_acc)))
        return jnp.sum(sum(cnts[1:], cnts[0]), axis=0, keepdims=True)

    def count_half_ge(cand):
        one, zero = jnp.int16(1), jnp.int16(0)

        def blk(kb, cnts):
            k0 = pl.multiple_of(kb * ts, ts)
            cnts = list(cnts)
            blk_ref = half_ref.at[pl.ds(k0, ts)]
            for j in range(ts // 16):
                hv = blk_ref[j * 16:(j + 1) * 16, :]
                cnts[j % n_acc] = cnts[j % n_acc] + jnp.where(hv >= cand, one, zero)
            return tuple(cnts)
        cnts = lax.fori_loop(0, nkb, blk, tuple(jnp.zeros((16, tq), i16) for _ in range(n_acc)))
        return jnp.sum(sum(cnts[1:], cnts[0]).astype(i32), axis=0, keepdims=True)

    def search_half(neg_half, n_start):
        def body(it, c):
            tau, n_tau = c
            t2 = tau | lax.shift_left(jnp.int32(1), 15 - it)
            cand = t2 + HALF_MIN
            n2 = count_half_ge(cand.astype(i16)) + jnp.where(cand <= neg_half, rem, 0)
            ok = n2 >= topk
            return jnp.where(ok, t2, tau), jnp.where(ok, n2, n_tau)

        return lax.fori_loop(0, 16, body, (jnp.zeros((1, tq), i32), n_start))

    tau_hi, n_hi = search_half(jnp.full((1, tq), _KEY_NEG_HI, i32), jnp.full((1, tq), lp, i32))
    t_hi = tau_hi + HALF_MIN

    base = lax.shift_left(t_hi, 16)

    def low_blk(kb, carry):
        k0 = pl.multiple_of(kb * ts, ts)
        key = key_ref[pl.ds(k0, ts), :]
        lo = jnp.minimum(jnp.maximum(key, base), base + (HALF_SPAN - 1)) - (base - HALF_MIN)
        half_ref[pl.ds(k0, ts), :] = lo.astype(i16)
        return carry

    lax.fori_loop(0, nkb, low_blk, 0)
    neg_lo = jnp.where(_KEY_NEG_HI > t_hi, HALF_MAX, jnp.where(_KEY_NEG_HI == t_hi, _KEY_NEG_LO, HALF_MIN))
    tau_lo, n_ge = search_half(neg_lo, n_hi)
    thr = lax.shift_left(t_hi, 16) | tau_lo
    has_tie = jnp.max(jnp.where(n_ge > topk, 1, 0)) > 0

    def tie_cut():
        n_gt = (count(lambda kv, kpos: kv > thr) + jnp.where(thr < _KEY_NEG, rem, 0))
        need = topk - n_gt

        def bis2(it, cut):
            bit = lax.shift_left(jnp.int32(1), 14 - it)
            c2 = cut | bit
            n = count(lambda kv, kpos: jnp.logical_and(kv == thr, kpos < c2))
            return jnp.where(n < need, c2, cut)
        return lax.fori_loop(0, 15, bis2, jnp.zeros((1, tq), i32))

    cut = lax.cond(has_tie, tie_cut, lambda: jnp.full((1, tq), INT_MAX, i32))

    def bias_blk(kb, carry):
        k0 = pl.multiple_of(kb * ts, ts)
        kv = key_ref[pl.ds(k0, ts), :]
        interior = jnp.logical_and(k0 >= FRAME_OFF, k0 + ts - 1 <= i * tq)

        @pl.when(jnp.logical_and(interior, jnp.logical_not(has_tie)))
        def _():
            bias_ref[pl.ds(k0, ts), :] = jnp.where(kv >= thr, 0.0, NEG).astype(bias_ref.dtype)

        @pl.when(jnp.logical_not(jnp.logical_and(interior, jnp.logical_not(has_tie))))
        def _():
            kpos = k0 + sub
            sel = jnp.logical_or(kv > thr, jnp.logical_and(kv == thr, kpos <= cut))
            vis = jnp.logical_and(kpos <= qpos, kpos >= FRAME_OFF)
            bias_ref[pl.ds(k0, ts), :] = jnp.where(jnp.logical_and(sel, vis), 0.0, NEG).astype(bias_ref.dtype)
        return carry

    lax.fori_loop(0, nkb, bias_blk, 0)

    def fill_blk(kb, carry):
        k0 = pl.multiple_of(kb * ts, ts)
        bias_ref[pl.ds(k0, ts), :] = jnp.full((ts, tq), NEG, bias_ref.dtype)
        return carry

    lax.fori_loop(nkb, lp // ts, fill_blk, 0)


def dsa_select(iq, ik, iwt, topk):
    nh, lp, _ = iq.shape
    assert lp < 2 ** 15
    tq = 256
    ts = _pick(lp, (640, 256, 128))
    kern = functools.partial(_select_kernel, tq=tq, ts=ts, lp=lp, topk=topk)
    return pl.pallas_call(
        kern,
        grid=(lp // tq,),
        in_specs=[pl.BlockSpec((nh, tq, HEAD_DIM), lambda i: (0, i, 0)),
                  pl.BlockSpec((lp, HEAD_DIM), lambda i: (0, 0), pipeline_mode=pl.Buffered(1)),
                  pl.BlockSpec((nh, tq), lambda i: (0, i))],
        out_specs=pl.BlockSpec((lp, tq), lambda i: (0, i)),
        out_shape=jax.ShapeDtypeStruct((lp, lp), bf16),
        scratch_shapes=[pltpu.VMEM((lp, tq), i32), pltpu.VMEM((lp, tq), i16)],
        compiler_params=_cparams(("parallel",)),
        name="dsa_select",
    )(iq, ik, iwt)


V_AUG = HEAD_DIM + 16


def _attn_kernel(q_ref, k_ref, vt_ref, b_ref, o_ref, m_ref, acc_ref, *, tq, ts, bpt):
    i = pl.program_id(0)
    nkb = ((i + 1) * tq + ts - 1) // ts
    group = ATT_HEADS // ATT_KV_HEADS
    kv_groups = range(ATT_KV_HEADS)
    qpos = i * tq + lax.broadcasted_iota(i32, (tq, 1), 0)
    eye = (lax.broadcasted_iota(i32, (tq, tq), 0)
           == lax.broadcasted_iota(i32, (tq, tq), 1)).astype(bf16)
    qa = [jnp.concatenate(
        [jnp.concatenate([q_ref[:, (g * group + r) * HEAD_DIM:(g * group + r + 1) * HEAD_DIM], eye],
                         axis=1) for r in range(group)], axis=0) for g in kv_groups]
    m_ref[...] = jnp.full(m_ref.shape, -jnp.inf, f32)
    acc_ref[...] = jnp.zeros_like(acc_ref)

    def scores(g, k0):
        ka = jnp.concatenate([k_ref[pl.ds(k0, ts), g * HEAD_DIM:(g + 1) * HEAD_DIM],
                              b_ref[pl.ds(k0, ts), :]], axis=1)
        return _dot_nt(ka, qa[g]).astype(bf16)

    def accumulate(g, k0, st):
        vt_ = vt_ref[g, :, pl.ds(k0, ts)]
        ps, alphas = [], []
        for r in range(group):
            sr = st[:, r * tq:(r + 1) * tq]
            m_old = m_ref[g, r:r + 1, :]
            m_new = jnp.maximum(m_old, jnp.max(sr, axis=0, keepdims=True).astype(f32))
            m_ref[g, r:r + 1, :] = m_new
            ps.append(jnp.exp2(sr - m_new.astype(bf16)))
            alphas.append(jnp.exp2(m_old - m_new))
        pt = jnp.concatenate(ps, axis=1)
        acc_ref[g] = jnp.concatenate(alphas, axis=1) * acc_ref[g] + _dot(vt_, pt)

    def make_trip(nblk, first_block):
        def trip(kp, carry):
            k0s = [pl.multiple_of((first_block + kp * nblk + b) * ts, ts) for b in range(nblk)]
            work = [(g, kk) for kk in k0s for g in kv_groups]
            st = scores(*work[0])
            for n in range(len(work)):
                st_next = scores(*work[n + 1]) if n + 1 < len(work) else None
                accumulate(*work[n], st)
                st = st_next
            return carry
        return trip

    n_full = nkb // bpt
    lax.fori_loop(0, n_full, make_trip(bpt, 0), 0)
    if bpt > 1:
        lax.fori_loop(0, nkb - n_full * bpt, make_trip(1, n_full * bpt), 0)
    for g in kv_groups:
        for r in range(group):
            ot = (acc_ref[g, 0:HEAD_DIM, r * tq:(r + 1) * tq]
                  / acc_ref[g, HEAD_DIM:HEAD_DIM + 1, r * tq:(r + 1) * tq])
            orr = jnp.where(qpos >= FRAME_OFF, ot.T, 0.0)
            o_ref[:, (g * group + r) * HEAD_DIM:(g * group + r + 1) * HEAD_DIM] = orr.astype(o_ref.dtype)


def dsa_attention(q, k, vt_aug, bias_t):
    lp = q.shape[0]
    tq = 128
    ts = _pick(lp, (1280, 640, 256, 128))
    bpt = 2
    group = ATT_HEADS // ATT_KV_HEADS
    kern = functools.partial(_attn_kernel, tq=tq, ts=ts, bpt=bpt)
    resident = pl.Buffered(1)
    return pl.pallas_call(
        kern,
        grid=(lp // tq,),
        in_specs=[pl.BlockSpec((tq, ATT_Q_W), lambda i: (i, 0)),
                  pl.BlockSpec((lp, ATT_KV_W), lambda i: (0, 0), pipeline_mode=resident),
                  pl.BlockSpec((ATT_KV_HEADS, V_AUG, lp), lambda i: (0, 0, 0), pipeline_mode=resident),
                  pl.BlockSpec((lp, tq), lambda i: (0, i))],
        out_specs=pl.BlockSpec((tq, ATT_Q_W), lambda i: (i, 0)),
        out_shape=jax.ShapeDtypeStruct((lp, ATT_Q_W), bf16),
        scratch_shapes=[pltpu.VMEM((ATT_KV_HEADS, group, tq), f32),
                        pltpu.VMEM((ATT_KV_HEADS, V_AUG, group * tq), f32)],
        compiler_params=_cparams(("parallel",)),
        name="dsa_attention",
    )(q, k, vt_aug, bias_t)


def _ffn_up_kernel(a_ref, ah_ref, wg_ref, wv_ref, cg_ref, cv_ref, bg_ref, bv_ref, o_ref,
                   eg_ref, ev_ref, *, tm):
    a = _halo_rows(a_ref, ah_ref, pl.program_id(0))
    eg_ref[...] = _dot(a, wg_ref[...])
    ev_ref[...] = _dot(a, wv_ref[...])

    def conv(ext_ref, w_ref, b_ref):
        w = w_ref[...]
        y = ext_ref[HALO:HALO + tm, :] * w[FFN_CONV - 1:FFN_CONV, :]
        for j in range(FFN_CONV - 1):
            s0 = HALO - (FFN_CONV - 1) + j
            y = y + ext_ref[s0:s0 + tm, :] * w[j:j + 1, :]
        return y + b_ref[...]

    gate = conv(eg_ref, cg_ref, bg_ref)
    val = conv(ev_ref, cv_ref, bv_ref)
    o_ref[...] = (gate * _sigmoid(gate) * val).astype(o_ref.dtype)


def ffn_up(u, w_up, conv_w, conv_b):
    lp, kdim = u.shape
    tm = _pick(lp, (1280, 640, 256, 128))
    tn = 512
    nc = D_FF // tn
    kern = functools.partial(_ffn_up_kernel, tm=tm)

    def wspec(off):
        return pl.BlockSpec((kdim, tn), lambda i, c: (0, c + off))

    def cspec(off):
        return pl.BlockSpec((FFN_CONV, tn), lambda i, c: (0, c + off))

    def bspec(off):
        return pl.BlockSpec((1, tn), lambda i, c: (0, c + off))

    b2 = conv_b.reshape(1, 2 * D_FF)
    return pl.pallas_call(
        kern,
        grid=(lp // tm, nc),
        in_specs=[pl.BlockSpec((tm, kdim), lambda i, c: (i, 0)),
                  pl.BlockSpec((HALO, kdim), lambda i, c: (jnp.maximum(i * (tm // HALO) - 1, 0), 0)),
                  wspec(0), wspec(nc), cspec(0), cspec(nc), bspec(0), bspec(nc)],
        out_specs=pl.BlockSpec((tm, tn), lambda i, c: (i, c)),
        out_shape=jax.ShapeDtypeStruct((lp, D_FF), bf16),
        scratch_shapes=[pltpu.VMEM((HALO + tm, tn), f32)] * 2,
        compiler_params=_cparams(("parallel", "arbitrary")),
        name="ffn_up",
    )(u, u, w_up, w_up, conv_w, conv_w, b2, b2)


def _split_w_in(w):
    wt = w.T
    o = 0
    parts = {}
    for name, width in (("gq", GDN_QK_W), ("gk", GDN_QK_W), ("gv", GDN_V_W), ("gz", GDN_V_W),
                        ("gb", GDN_V_HEADS), ("ga", GDN_V_HEADS), ("aq", ATT_Q_W), ("ak", ATT_KV_W),
                        ("av", ATT_KV_W), ("iq", IDX_Q_W), ("ik", HEAD_DIM), ("iw", IDX_HEADS),
                        ("gate_gdn", D_MODEL), ("gate_att", D_MODEL)):
        parts[name] = wt[o:o + width]
        o += width
    cat = lambda names: jnp.concatenate([parts[n] for n in names], axis=0)
    small = cat(("gb", "ga", "iw"))
    small = jnp.pad(small, ((0, LANES - small.shape[0]), (0, 0)))
    groups = dict(qkv=cat(("gq", "gk", "gv")), aq=parts["aq"], iq=parts["iq"], kv=cat(("ak", "ik")),
                  plain=cat(("gz", "gate_gdn", "gate_att", "av")), small=small)
    return {k: v.astype(bf16) for k, v in groups.items()}


def _layer(h0, p, lp, topk, last):
    tm = _pick(lp, (640, 256, 128))
    tm_big = _pick(lp, (1280, 640, 256, 128))
    wp = _split_w_in(p["w_in"])

    def proj(w, **kw):
        return matmul(u1, w, tm=tm_big, tk=D_MODEL, w_nt=True, **kw)

    u1 = rms_rows(h0, p["mix_pre_g"])
    tabs = rope_tables(lp)
    tab_specs = (((tm_big, HEAD_DIM), lambda i, j: (i, 0)),) * 3
    qkv = gdn_proj(u1, wp["qkv"], p["gdn_conv_w"])
    (aq,) = proj(wp["aq"], tn=1024, epilogue=_make_ep_rope(HEAD_DIM ** -0.5 * LOG2E, False),
                 extra=tabs, extra_specs=tab_specs,
                 out_shapes=[jax.ShapeDtypeStruct((lp, ATT_Q_W), bf16)], name="proj_aq")
    iq_heads = 1024 // HEAD_DIM
    (iq,) = proj(wp["iq"], tn=1024, epilogue=_make_ep_rope(1.0, True), extra=tabs, extra_specs=tab_specs,
                 out_shapes=[jax.ShapeDtypeStruct((IDX_HEADS, lp, HEAD_DIM), bf16)],
                 out_block_specs=[((iq_heads, tm_big, HEAD_DIM), lambda i, j: (j, i, 0))], name="proj_iq")
    ak, ik = proj(wp["kv"], tn=ATT_KV_W + HEAD_DIM, epilogue=_ep_rope_kv, extra=tabs, extra_specs=tab_specs,
                  out_shapes=[jax.ShapeDtypeStruct((lp, ATT_KV_W), bf16),
                              jax.ShapeDtypeStruct((lp, HEAD_DIM), bf16)],
                  out_block_specs=[((tm_big, ATT_KV_W), lambda i, j: (i, 0)),
                                   ((tm_big, HEAD_DIM), lambda i, j: (i, 0))], name="proj_kv")
    (plain,) = proj(wp["plain"], tn=768, epilogue=_ep_cast,
                    out_shapes=[jax.ShapeDtypeStruct((lp, PLAIN_W), bf16)], name="proj_plain")
    (small,) = proj(wp["small"], tn=LANES, epilogue=_ep_cast,
                    out_shapes=[jax.ShapeDtypeStruct((lp, LANES), f32)], name="proj_small")
    gb = small[:, 0:GDN_V_HEADS]
    ga = small[:, GDN_V_HEADS:2 * GDN_V_HEADS]
    iw = small[:, 2 * GDN_V_HEADS:2 * GDN_V_HEADS + IDX_HEADS]

    beta, gcum = gdn_gates(gb, ga, p["gdn_a_log"], p["gdn_dt_bias"])
    o_gdn = gdn_core(qkv, plain, p["gdn_norm_g"], gcum.T[:, None, :], beta.T[:, None, :])

    avt = plain[:, COL_AV:COL_AV + ATT_KV_W].T.reshape(ATT_KV_HEADS, HEAD_DIM, lp)
    avt = jnp.concatenate([avt, jnp.ones((ATT_KV_HEADS, V_AUG - HEAD_DIM, lp), bf16)], axis=1)
    bias_t = dsa_select(iq, ik, iw.T, topk)
    o_att = dsa_attention(aq, ak, avt, bias_t)

    tn = 1024
    gate_spec = lambda col0: ((tm_big, tn), (lambda i, j, c=col0 // tn: (i, c + j)))
    (m1,) = matmul(o_gdn, p["w_branch_gdn"].astype(bf16), tm=tm_big, tn=tn, tk=2048,
                   epilogue=_ep_gate, extra=(plain,), extra_specs=(gate_spec(COL_GATE_GDN),),
                   out_shapes=[jax.ShapeDtypeStruct((lp, D_MODEL), f32)], name="branch_gdn")
    (merged,) = matmul(o_att, p["w_branch_att"].astype(bf16), tm=tm_big, tn=tn, tk=2048,
                       epilogue=_ep_gate_add, extra=(plain, m1),
                       extra_specs=(gate_spec(COL_GATE_ATT), ((tm_big, tn), lambda i, j: (i, j))),
                       out_shapes=[jax.ShapeDtypeStruct((lp, D_MODEL), bf16)], name="branch_att")
    row_spec = ((tm, D_MODEL), lambda i, j: (i, 0))
    vec_spec = ((1, D_MODEL), lambda i, j: (0, 0))
    h1, u2 = matmul(merged, p["w_out"].astype(bf16), tm=tm, tn=D_MODEL, tk=D_MODEL,
                    epilogue=_ep_res_norm2,
                    extra=(h0, p["mix_post_g"].reshape(1, D_MODEL), p["ffn_pre_g"].reshape(1, D_MODEL)),
                    extra_specs=(row_spec, vec_spec, vec_spec),
                    out_shapes=[jax.ShapeDtypeStruct((lp, D_MODEL), f32),
                                jax.ShapeDtypeStruct((lp, D_MODEL), bf16)], name="w_out")

    act = ffn_up(u2, p["w_up"].astype(bf16), p["ffn_conv_w"], p["ffn_conv_b"])
    if not last:
        (h2,) = matmul(act, p["w_down"].astype(bf16), tm=tm, tn=D_MODEL, tk=2048, epilogue=_ep_res_norm,
                       extra=(h1, p["ffn_post_g"].reshape(1, D_MODEL)),
                       extra_specs=(row_spec, vec_spec),
                       out_shapes=[jax.ShapeDtypeStruct((lp, D_MODEL), f32)], name="w_down")
        return h2
    tmx = FRAME_X0
    (out,) = matmul(act, p["w_down"].astype(bf16), tm=tmx, tn=D_MODEL, tk=D_FF, epilogue=_ep_res_norm,
                    extra=(h1, p["ffn_post_g"].reshape(1, D_MODEL)),
                    extra_specs=(((tmx, D_MODEL), lambda i, j: (i, 0)), vec_spec),
                    out_shapes=[jax.ShapeDtypeStruct((lp - FRAME_X0, D_MODEL), f32)],
                    out_block_specs=[((tmx, D_MODEL), lambda i, j: (jnp.maximum(i - 1, 0), 0))],
                    rows_sem="arbitrary", name="w_down")
    return out


def kernel(x, meta_tokens, mix_pre_g, w_in, gdn_conv_w, gdn_a_log, gdn_dt_bias, gdn_norm_g,
           w_branch_gdn, w_branch_att, w_out, mix_post_g, ffn_pre_g, w_up, ffn_conv_w,
           ffn_conv_b, w_down, ffn_post_g):
    batch, seq, d = x.shape
    assert batch == 1 and d == D_MODEL
    lp = FRAME_X0 + seq
    topk = min(TOPK_MAX, (N_META + seq) // 4)
    h = jnp.concatenate([jnp.zeros((FRAME_OFF, d), x.dtype), meta_tokens.astype(x.dtype), x[0]], axis=0)
    depth = w_in.shape[0]
    for i in range(depth):
        p = dict(mix_pre_g=mix_pre_g[i], w_in=w_in[i], gdn_conv_w=gdn_conv_w[i], gdn_a_log=gdn_a_log[i],
                 gdn_dt_bias=gdn_dt_bias[i], gdn_norm_g=gdn_norm_g[i], w_branch_gdn=w_branch_gdn[i],
                 w_branch_att=w_branch_att[i], w_out=w_out[i], mix_post_g=mix_post_g[i],
                 ffn_pre_g=ffn_pre_g[i], w_up=w_up[i], ffn_conv_w=ffn_conv_w[i],
                 ffn_conv_b=ffn_conv_b[i], w_down=w_down[i], ffn_post_g=ffn_post_g[i])
        h = _layer(h, p, lp, topk, last=(i == depth - 1))
    return h[None]
```

```python
import functools
import struct

import jax
import jax.numpy as jnp
from jax import lax
from jax.experimental import pallas as pl
from jax.experimental.pallas import tpu as pltpu

f32 = jnp.float32
bf16 = jnp.bfloat16
i32 = jnp.int32
i16 = jnp.int16

D_MODEL = 2048
N_META = 16
EPS = 1e-6
GDN_QK_HEADS = 16
GDN_V_HEADS = 32
HEAD_DIM = 128
GDN_CONV = 4
ATT_HEADS = 16
ATT_KV_HEADS = 2
IDX_HEADS = 16
TOPK_MAX = 256
NEG = -1e30
LOG2E = 1.4426950408889634
ROPE_THETA = 500000.0
ROPE_DIMS = HEAD_DIM // 4
D_FF = 3 * D_MODEL
FFN_CONV = 3
GDN_QK_W = GDN_QK_HEADS * HEAD_DIM
GDN_V_W = GDN_V_HEADS * HEAD_DIM
ATT_Q_W = ATT_HEADS * HEAD_DIM
ATT_KV_W = ATT_KV_HEADS * HEAD_DIM
IDX_Q_W = IDX_HEADS * HEAD_DIM

FRAME_X0 = 256
FRAME_OFF = FRAME_X0 - N_META
CHUNK = 128
GDN_QK_PER_STEP = 2

LANES = 128
BF16_SUBLANES = 16
VMEM_LIMIT = 56 * 1024 * 1024

COL_GZ = 0
COL_GATE_GDN = COL_GZ + GDN_V_W
COL_GATE_ATT = COL_GATE_GDN + D_MODEL
COL_AV = COL_GATE_ATT + D_MODEL
PLAIN_W = COL_AV + ATT_KV_W


def _cparams(sem):
    return pltpu.CompilerParams(dimension_semantics=sem, vmem_limit_bytes=VMEM_LIMIT)


def _pick(n, cands):
    for c in cands:
        if n % c == 0:
            return c
    raise ValueError(f"no tile for {n} in {cands}")


def _sigmoid(x):
    return 1.0 / (1.0 + jnp.exp(-x))


def _dot(a, b):
    return jnp.dot(a, b, preferred_element_type=f32)


def _dot_nt(a, b):
    return lax.dot_general(a, b, (((1,), (1,)), ((), ())), preferred_element_type=f32)


def _rms_rows_kernel(h_ref, g_ref, o_ref):
    h = h_ref[...]
    y = h * lax.rsqrt(jnp.mean(h * h, axis=-1, keepdims=True) + EPS)
    o_ref[...] = (y * g_ref[...]).astype(o_ref.dtype)


def rms_rows(h, g):
    lp, d = h.shape
    tm = _pick(lp, (640, 256, 128))
    return pl.pallas_call(
        _rms_rows_kernel,
        grid=(lp // tm,),
        in_specs=[pl.BlockSpec((tm, d), lambda i: (i, 0)),
                  pl.BlockSpec((1, d), lambda i: (0, 0))],
        out_specs=pl.BlockSpec((tm, d), lambda i: (i, 0)),
        out_shape=jax.ShapeDtypeStruct((lp, d), bf16),
        compiler_params=_cparams(("parallel",)),
        name="rms_rows",
    )(h, g.reshape(1, d))


def _mm_kernel(*refs, nk, n_extra, n_out, epilogue, w_nt):
    a_ref, w_ref = refs[0], refs[1]
    dot = _dot_nt if w_nt else _dot
    extra = refs[2:2 + n_extra]
    outs = refs[2 + n_extra:2 + n_extra + n_out]
    if nk == 1:
        epilogue(dot(a_ref[...], w_ref[...]), extra, outs)
        return
    acc_ref = refs[-1]
    k = pl.program_id(2)

    @pl.when(k == 0)
    def _():
        acc_ref[...] = jnp.zeros_like(acc_ref)

    acc_ref[...] += dot(a_ref[...], w_ref[...])

    @pl.when(k == nk - 1)
    def _():
        epilogue(acc_ref[...], extra, outs)


def matmul(a, w, *, tm, tn, tk, epilogue, extra=(), extra_specs=(), out_shapes, out_block_specs=None,
           w_nt=False, rows_sem="parallel", name):
    m = a.shape[0]
    kdim, n = w.shape[::-1] if w_nt else w.shape
    nk = kdim // tk
    assert m % tm == 0 and n % tn == 0 and kdim % tk == 0
    w_mode = dict(pipeline_mode=pl.Buffered(1)) if (nk == 1 and n == tn) else {}
    in_specs = [pl.BlockSpec((tm, tk), lambda i, j, k: (i, k)),
                (pl.BlockSpec((tn, tk), lambda i, j, k: (j, k), **w_mode) if w_nt else
                 pl.BlockSpec((tk, tn), lambda i, j, k: (k, j), **w_mode))]
    in_specs += [pl.BlockSpec(bs, (lambda i, j, k, f=f: f(i, j))) for bs, f in extra_specs]
    if out_block_specs is None:
        out_block_specs = [((tm, tn), lambda i, j: (i, j))] * len(out_shapes)
    out_specs = [pl.BlockSpec(bs, (lambda i, j, k, f=f: f(i, j))) for bs, f in out_block_specs]
    scratch = [] if nk == 1 else [pltpu.VMEM((tm, tn), f32)]
    kern = functools.partial(_mm_kernel, nk=nk, n_extra=len(extra), n_out=len(out_shapes),
                             epilogue=epilogue, w_nt=w_nt)
    res = pl.pallas_call(
        kern,
        grid=(m // tm, n // tn, nk),
        in_specs=in_specs,
        out_specs=out_specs,
        out_shape=out_shapes,
        scratch_shapes=scratch,
        compiler_params=_cparams((rows_sem, rows_sem, "arbitrary")),
        name=name,
    )(a, w, *extra)
    return res


def _ep_cast(acc, extra, outs):
    outs[0][...] = acc.astype(outs[0].dtype)


def _ep_gate(acc, extra, outs):
    g = extra[0][...].astype(f32)
    outs[0][...] = (_sigmoid(g) * acc).astype(outs[0].dtype)


def _ep_gate_add(acc, extra, outs):
    g = extra[0][...].astype(f32)
    outs[0][...] = (extra[1][...].astype(f32) + _sigmoid(g) * acc).astype(outs[0].dtype)


def _rms(t, g):
    return t * lax.rsqrt(jnp.mean(t * t, axis=-1, keepdims=True) + EPS) * g


def _ep_res_norm2(acc, extra, outs):
    h_ref, g_ref, g2_ref = extra
    h1 = h_ref[...] + _rms(acc, g_ref[...])
    outs[0][...] = h1
    outs[1][...] = _rms(h1, g2_ref[...]).astype(outs[1].dtype)


def _ep_res_norm(acc, extra, outs):
    h_ref, g_ref = extra
    outs[0][...] = h_ref[...] + _rms(acc, g_ref[...])


HALO = BF16_SUBLANES


def _halo_rows(a_ref, ah_ref, i):
    halo = ah_ref[...]
    halo = jnp.where(i > 0, halo, jnp.zeros_like(halo))
    return jnp.concatenate([halo, a_ref[...]], axis=0)


def _gdn_proj_kernel(a_ref, ah_ref, w_ref, cw_ref, o_ref, ext_ref, *, tm, tn):
    c = pl.program_id(1)
    ext_ref[...] = _dot_nt(_halo_rows(a_ref, ah_ref, pl.program_id(0)), w_ref[...])
    w = cw_ref[...]
    y = ext_ref[HALO:HALO + tm, :] * w[GDN_CONV - 1:GDN_CONV, :]
    for j in range(GDN_CONV - 1):
        s0 = HALO - (GDN_CONV - 1) + j
        y = y + ext_ref[s0:s0 + tm, :] * w[j:j + 1, :]
    s = y * _sigmoid(y)
    is_q = c < (GDN_QK_W // tn)
    is_qk = c < (2 * GDN_QK_W // tn)
    qscale = jnp.where(is_q, HEAD_DIM ** -0.5, 1.0).astype(f32)
    for hh in range(tn // HEAD_DIM):
        seg = s[:, hh * HEAD_DIM:(hh + 1) * HEAD_DIM]
        r = lax.rsqrt(jnp.sum(seg * seg, axis=-1, keepdims=True) + EPS) * qscale
        fac = jnp.where(is_qk, r, 1.0)
        o_ref[:, hh * HEAD_DIM:(hh + 1) * HEAD_DIM] = (seg * fac).astype(o_ref.dtype)


def gdn_proj(u, w_qkv, conv_w):
    lp, kdim = u.shape
    width = w_qkv.shape[0]
    tm = _pick(lp, (1280, 640, 256, 128))
    tn = 1024
    kern = functools.partial(_gdn_proj_kernel, tm=tm, tn=tn)
    return pl.pallas_call(
        kern,
        grid=(lp // tm, width // tn),
        in_specs=[pl.BlockSpec((tm, kdim), lambda i, c: (i, 0)),
                  pl.BlockSpec((HALO, kdim), lambda i, c: (jnp.maximum(i * (tm // HALO) - 1, 0), 0)),
                  pl.BlockSpec((tn, kdim), lambda i, c: (c, 0)),
                  pl.BlockSpec((GDN_CONV, tn), lambda i, c: (0, c))],
        out_specs=pl.BlockSpec((tm, tn), lambda i, c: (i, c)),
        out_shape=jax.ShapeDtypeStruct((lp, width), bf16),
        scratch_shapes=[pltpu.VMEM((HALO + tm, tn), f32)],
        compiler_params=_cparams(("parallel", "arbitrary")),
        name="gdn_proj",
    )(u, u, w_qkv, conv_w)


def _gdn_gates_kernel(b_ref, a_ref, alog_ref, dt_ref, beta_ref, gcum_ref, *, tm):
    i = pl.program_id(0)
    rows = i * tm + lax.broadcasted_iota(i32, (tm, 1), 0)
    valid = rows >= FRAME_OFF
    beta_ref[...] = jnp.where(valid, _sigmoid(b_ref[...]), 0.0)
    a = a_ref[...] + dt_ref[...]
    sp = jnp.maximum(a, 0.0) + jnp.log1p(jnp.exp(-jnp.abs(a)))
    g = jnp.where(valid, -jnp.exp(alog_ref[...]) * sp, 0.0)
    tri = (lax.broadcasted_iota(i32, (CHUNK, CHUNK), 0)
           >= lax.broadcasted_iota(i32, (CHUNK, CHUNK), 1)).astype(f32)
    for c in range(tm // CHUNK):
        rs = slice(c * CHUNK, (c + 1) * CHUNK)
        gcum_ref[rs, :] = jnp.dot(tri, g[rs, :], preferred_element_type=f32,
                                  precision=lax.Precision.HIGHEST)


def gdn_gates(gb, ga, a_log, dt_bias):
    lp, nh = gb.shape
    tm = _pick(lp, (1280, 640, 256, 128))
    spec = pl.BlockSpec((tm, nh), lambda i: (i, 0))
    vec = pl.BlockSpec((1, nh), lambda i: (0, 0))
    return pl.pallas_call(
        functools.partial(_gdn_gates_kernel, tm=tm),
        grid=(lp // tm,),
        in_specs=[spec, spec, vec, vec],
        out_specs=[spec, spec],
        out_shape=[jax.ShapeDtypeStruct((lp, nh), f32)] * 2,
        compiler_params=_cparams(("parallel",)),
        name="gdn_gates",
    )(gb, ga, a_log.reshape(1, nh), dt_bias.reshape(1, nh))


def _block_mask(size):
    r = lax.broadcasted_iota(i32, (CHUNK, CHUNK), 0) // size
    c = lax.broadcasted_iota(i32, (CHUNK, CHUNK), 1) // size
    return r == c


def _unit_lower_inverse_many(mats, tick):
    row = lax.broadcasted_iota(i32, (CHUNK, CHUNK), 0)
    col = lax.broadcasted_iota(i32, (CHUNK, CHUNK), 1)
    eye = (row == col).astype(f32)
    base = 8
    m_prev = _block_mask(base)
    ads = [jnp.where(m_prev, a, 0.0) for a in mats]
    adbs = [ad.astype(bf16) for ad in ads]
    a2s = [_dot(x, x) for x in adbs]
    tick()
    a2bs = [x.astype(bf16) for x in a2s]
    a4s = [_dot(x, x) for x in a2bs]
    tick()
    xs = [_dot((eye - ad).astype(bf16), (eye + a2).astype(bf16)) for ad, a2 in zip(ads, a2s)]
    tick()
    xs = [_dot(x.astype(bf16), (eye + a4).astype(bf16)) for x, a4 in zip(xs, a4s)]
    tick()
    size = base * 2
    while size <= CHUNK:
        m_cur = _block_mask(size)
        off_diag = jnp.logical_and(m_cur, jnp.logical_not(m_prev))
        bs = [jnp.where(off_diag, a, 0.0).astype(bf16) for a in mats]
        xbs = [x.astype(bf16) for x in xs]
        ys = [_dot(xb, b).astype(bf16) for xb, b in zip(xbs, bs)]
        tick()
        xs = [x - _dot(y, xb) for x, y, xb in zip(xs, ys, xbs)]
        tick()
        m_prev = m_cur
        size *= 2
    return xs


def _gdn_kernel(q_ref, k_ref, v_ref, z_ref, ng_ref, *rest, rb, nq):
    r = pl.program_id(1)
    nv = 2 * nq
    grs, brs = rest[:nv], rest[nv:2 * nv]
    o_ref, s_ref, u_ref, wq_ref, ak_ref, eg_ref, egl_ref = rest[2 * nv:]

    @pl.when(r == 0)
    def _():
        for ref in (s_ref, u_ref, wq_ref, ak_ref, eg_ref, egl_ref):
            ref[...] = jnp.zeros_like(ref)

    row = lax.broadcasted_iota(i32, (CHUNK, CHUNK), 0)
    col = lax.broadcasted_iota(i32, (CHUNK, CHUNK), 1)
    incl = row >= col
    strict = row > col
    ng = ng_ref[...]
    nchunk = rb // CHUNK
    heads = range(nv)

    def rows(c):
        return slice(c * CHUNK, (c + 1) * CHUNK)

    def lanes(e):
        return slice(e * HEAD_DIM, (e + 1) * HEAD_DIM)

    chains = [(c, e) for c in range(nchunk) for e in heads]

    def recurrence():
        state = [s_ref[e] for e in heads]
        for c in range(nchunk):
            n = [chains.index((c, e)) for e in heads]
            sb = [state[e].astype(bf16) for e in heads]
            ws = [_dot(wq_ref[n[e]], sb[e]) for e in heads]
            yield
            vb = [(u_ref[n[e]] - ws[e][:CHUNK]).astype(bf16) for e in heads]
            av = [_dot(ak_ref[n[e]], vb[e]) for e in heads]
            yield
            for e in heads:
                o = eg_ref[n[e]] * ws[e][CHUNK:] + av[e][:CHUNK]
                state[e] = state[e] * egl_ref[n[e]] + av[e][CHUNK:]
                z = z_ref[rows(c), lanes(e)].astype(f32)
                o_ref[rows(c), lanes(e)] = (_rms(o, ng) * (z * _sigmoid(z))).astype(o_ref.dtype)
        for e in heads:
            s_ref[e] = state[e]

    rec = recurrence()

    def tick():
        next(rec, None)

    qkh = [(c, a) for c in range(nchunk) for a in range(nq)]
    qs = {ca: q_ref[rows(ca[0]), lanes(ca[1])] for ca in qkh}
    ks = {ca: k_ref[rows(ca[0]), lanes(ca[1])] for ca in qkh}
    kts = {ca: ks[ca].astype(f32).T for ca in qkh}
    kks = {ca: _dot_nt(ks[ca], ks[ca]) for ca in qkh}
    qks = {ca: _dot_nt(qs[ca], ks[ca]) for ca in qkh}
    qk_of = {ce: (ce[0], ce[1] // 2) for ce in chains}
    tick()
    def per_row(v):
        return jnp.broadcast_to(v, (CHUNK, CHUNK)).T

    gr = {ce: grs[ce[1]][:, rows(ce[0])] for ce in chains}
    br = {ce: brs[ce[1]][:, rows(ce[0])] for ce in chains}
    gc = {ce: per_row(gr[ce]) for ce in chains}
    bc = {ce: per_row(br[ce]) for ce in chains}
    dec = {ce: jnp.where(incl, jnp.exp(jnp.where(incl, gc[ce] - gr[ce], 0.0)), 0.0) for ce in chains}
    amat = [jnp.where(strict, bc[ce] * kks[qk_of[ce]] * dec[ce], 0.0) for ce in chains]
    tinv = dict(zip(chains, _unit_lower_inverse_many(amat, tick)))
    u = {ce: _dot((tinv[ce] * br[ce]).astype(bf16), v_ref[rows(ce[0]), lanes(ce[1])]) for ce in chains}
    tick()
    w = {ce: _dot((tinv[ce] * (br[ce] * jnp.exp(gr[ce]))).astype(bf16), ks[qk_of[ce]]) for ce in chains}
    tick()
    g_last = {ce: gr[ce][:, CHUNK - 1:CHUNK] for ce in chains}
    wq = {ce: jnp.concatenate([w[ce].astype(bf16), qs[qk_of[ce]]], axis=0) for ce in chains}
    ak = {ce: jnp.concatenate(
        [jnp.where(incl, qks[qk_of[ce]] * dec[ce], 0.0).astype(bf16),
         (kts[qk_of[ce]] * jnp.exp(g_last[ce] - gr[ce])).astype(bf16)], axis=0) for ce in chains}
    for _ in rec:
        pass
    for n, ce in enumerate(chains):
        u_ref[n] = u[ce]
        wq_ref[n] = wq[ce]
        ak_ref[n] = ak[ce]
        eg_ref[n] = jnp.exp(gc[ce])
        egl_ref[n] = jnp.broadcast_to(jnp.exp(g_last[ce]), (1, HEAD_DIM))


def gdn_core(qkv, plain, norm_g, grow, brow):
    assert CHUNK == HEAD_DIM
    lp = qkv.shape[0]
    rb = _pick(lp, (640, 256, 128))
    nq = GDN_QK_PER_STEP
    nv = 2 * nq
    qw, vw = nq * HEAD_DIM, nv * HEAD_DIM
    kcol0 = GDN_QK_W // qw
    vcol0 = 2 * GDN_QK_W // vw
    zcol0 = COL_GZ // vw

    nr = lp // rb
    n_chain = nv * (rb // CHUNK)

    def cur(r):
        return jnp.minimum(r, nr - 1)

    def prev(r):
        return jnp.maximum(r - 1, 0)

    def rowspec(e):
        return pl.BlockSpec((None, 1, rb), lambda j, r: (nv * j + e, 0, cur(r)))

    rowspecs = [rowspec(e) for e in range(nv)]
    kern = functools.partial(_gdn_kernel, rb=rb, nq=nq)
    return pl.pallas_call(
        kern,
        grid=(GDN_QK_HEADS // nq, nr + 1),
        in_specs=[pl.BlockSpec((rb, qw), lambda j, r: (cur(r), j)),
                  pl.BlockSpec((rb, qw), lambda j, r: (cur(r), kcol0 + j)),
                  pl.BlockSpec((rb, vw), lambda j, r: (cur(r), vcol0 + j)),
                  pl.BlockSpec((rb, vw), lambda j, r: (prev(r), zcol0 + j)),
                  pl.BlockSpec((1, HEAD_DIM), lambda j, r: (0, 0))] + rowspecs + rowspecs,
        out_specs=pl.BlockSpec((rb, vw), lambda j, r: (prev(r), j)),
        out_shape=jax.ShapeDtypeStruct((lp, GDN_V_W), bf16),
        scratch_shapes=[pltpu.VMEM((nv, HEAD_DIM, HEAD_DIM), f32),
                        pltpu.VMEM((n_chain, CHUNK, HEAD_DIM), f32),
                        pltpu.VMEM((n_chain, 2 * CHUNK, HEAD_DIM), bf16),
                        pltpu.VMEM((n_chain, 2 * CHUNK, CHUNK), bf16),
                        pltpu.VMEM((n_chain, CHUNK, HEAD_DIM), f32),
                        pltpu.VMEM((n_chain, 1, HEAD_DIM), f32)],
        compiler_params=_cparams(("parallel", "arbitrary")),
        name="gdn_core",
    )(qkv, qkv, qkv, plain, norm_g.reshape(1, HEAD_DIM), *([grow] * nv), *([brow] * nv))


def _rope_head(seg, cc, s1, s2):
    half = ROPE_DIMS // 2
    return seg * cc + pltpu.roll(seg, half, 1) * s1 + pltpu.roll(seg, HEAD_DIM - half, 1) * s2


def _make_ep_rope(scale, head_major):
    def ep(acc, extra, outs):
        cc, s1, s2 = (r[...] for r in extra)
        for h in range(acc.shape[1] // HEAD_DIM):
            y = _rope_head(acc[:, h * HEAD_DIM:(h + 1) * HEAD_DIM], cc, s1, s2)
            if scale != 1.0:
                y = y * scale
            if head_major:
                outs[0][h] = y.astype(outs[0].dtype)
            else:
                outs[0][:, h * HEAD_DIM:(h + 1) * HEAD_DIM] = y.astype(outs[0].dtype)
    return ep


def _ep_rope_kv(acc, extra, outs):
    cc, s1, s2 = (r[...] for r in extra)
    for h in range(ATT_KV_HEADS):
        y = _rope_head(acc[:, h * HEAD_DIM:(h + 1) * HEAD_DIM], cc, s1, s2)
        outs[0][:, h * HEAD_DIM:(h + 1) * HEAD_DIM] = y.astype(outs[0].dtype)
    y = _rope_head(acc[:, ATT_KV_W:ATT_KV_W + HEAD_DIM], cc, s1, s2)
    outs[1][...] = y.astype(outs[1].dtype)


def rope_tables(lp):
    half = ROPE_DIMS // 2
    pos = (jnp.arange(lp, dtype=jnp.int32) - FRAME_OFF).astype(f32)
    inv = ROPE_THETA ** (-jnp.arange(half, dtype=f32) / half)
    ang = pos[:, None] * inv[None, :]
    cos, sin = jnp.cos(ang), jnp.sin(ang)
    zeros = jnp.zeros((lp, HEAD_DIM - ROPE_DIMS), f32)
    z16 = jnp.zeros((lp, half), f32)
    cc = jnp.concatenate([cos, cos, jnp.ones_like(zeros)], axis=1)
    s1 = jnp.concatenate([z16, sin, zeros], axis=1)
    s2 = jnp.concatenate([-sin, z16, zeros], axis=1)
    return cc, s1, s2


INT_MAX = 2 ** 31 - 1
HALF_MIN, HALF_MAX, HALF_SPAN = -2 ** 15, 2 ** 15 - 1, 2 ** 16


def _sortable(x):
    b = lax.bitcast_convert_type(x, i32)
    return b ^ (lax.shift_right_arithmetic(b, 31) & INT_MAX)


_NEG_BITS = struct.unpack("<i", struct.pack("<f", NEG))[0]
_KEY_NEG = _NEG_BITS ^ ((_NEG_BITS >> 31) & INT_MAX)
_KEY_NEG_HI = _KEY_NEG >> 16
_KEY_NEG_LO = (_KEY_NEG & (HALF_SPAN - 1)) + HALF_MIN


def _select_kernel(iq_ref, ik_ref, iwt_ref, bias_ref, key_ref, half_ref, *, tq, ts, lp, topk):
    i = pl.program_id(0)
    nkb = ((i + 1) * tq + ts - 1) // ts
    rem = lp - nkb * ts
    iwt = iwt_ref[...] * ((IDX_HEADS ** -0.5) * (HEAD_DIM ** -0.5))
    qpos = i * tq + lax.broadcasted_iota(i32, (1, tq), 1)
    sub = lax.broadcasted_iota(i32, (ts, 1), 0)

    def score_blk(kb, carry):
        k0 = pl.multiple_of(kb * ts, ts)
        ikb = ik_ref[pl.ds(k0, ts), :]
        acc = jnp.zeros((ts, tq), f32)
        for h in range(IDX_HEADS):
            lg = _dot_nt(ikb, iq_ref[h])
            acc = acc + jnp.maximum(lg, 0.0) * iwt[h:h + 1, :]
        kpos = k0 + sub
        sc = jnp.where(kpos <= qpos, acc, NEG)
        sc = jnp.where(kpos < FRAME_OFF, -jnp.inf, sc)
        key = _sortable(sc)
        key_ref[pl.ds(k0, ts), :] = key
        half_ref[pl.ds(k0, ts), :] = lax.shift_right_arithmetic(key, 16).astype(i16)
        return carry

    lax.fori_loop(0, nkb, score_blk, 0)

    n_acc = 4

    def count(pred_fn):
        sub8 = lax.broadcasted_iota(i32, (8, 1), 0)

        def blk(kb, cnts):
            k0 = pl.multiple_of(kb * ts, ts)
            cnts = list(cnts)
            blk_ref = key_ref.at[pl.ds(k0, ts)]
            for j in range(ts // 8):
                kv = blk_ref[j * 8:(j + 1) * 8, :]
                cnts[j % n_acc] = cnts[j % n_acc] + jnp.where(pred_fn(kv, k0 + j * 8 + sub8), 1, 0)
            return tuple(cnts)
        cnts = lax.fori_loop(0, nkb, blk, tuple(jnp.zeros((8, tq), i32) for _ in range(n_acc)))
        return jnp.sum(sum(cnts[1:], cnts[0]), axis=0, keepdims=True)

    def count_half_ge(cand):
        one, zero = jnp.int16(1), jnp.int16(0)

        def blk(kb, cnts):
            k0 = pl.multiple_of(kb * ts, ts)
            cnts = list(cnts)
            blk_ref = half_ref.at[pl.ds(k0, ts)]
            for j in range(ts // 16):
                hv = blk_ref[j * 16:(j + 1) * 16, :]
                cnts[j % n_acc] = cnts[j % n_acc] + jnp.where(hv >= cand, one, zero)
            return tuple(cnts)
        cnts = lax.fori_loop(0, nkb, blk, tuple(jnp.zeros((16, tq), i16) for _ in range(n_acc)))
        return jnp.sum(sum(cnts[1:], cnts[0]).astype(i32), axis=0, keepdims=True)

    def search_half(neg_half, n_start):
        def body(it, c):
            tau, n_tau = c
            t2 = tau | lax.shift_left(jnp.int32(1), 15 - it)
            cand = t2 + HALF_MIN
            n2 = count_half_ge(cand.astype(i16)) + jnp.where(cand <= neg_half, rem, 0)
            ok = n2 >= topk
            return jnp.where(ok, t2, tau), jnp.where(ok, n2, n_tau)

        return lax.fori_loop(0, 16, body, (jnp.zeros((1, tq), i32), n_start))

    tau_hi, n_hi = search_half(jnp.full((1, tq), _KEY_NEG_HI, i32), jnp.full((1, tq), lp, i32))
    t_hi = tau_hi + HALF_MIN

    base = lax.shift_left(t_hi, 16)

    def low_blk(kb, carry):
        k0 = pl.multiple_of(kb * ts, ts)
        key = key_ref[pl.ds(k0, ts), :]
        lo = jnp.minimum(jnp.maximum(key, base), base + (HALF_SPAN - 1)) - (base - HALF_MIN)
        half_ref[pl.ds(k0, ts), :] = lo.astype(i16)
        return carry

    lax.fori_loop(0, nkb, low_blk, 0)
    neg_lo = jnp.where(_KEY_NEG_HI > t_hi, HALF_MAX, jnp.where(_KEY_NEG_HI == t_hi, _KEY_NEG_LO, HALF_MIN))
    tau_lo, n_ge = search_half(neg_lo, n_hi)
    thr = lax.shift_left(t_hi, 16) | tau_lo
    has_tie = jnp.max(jnp.where(n_ge > topk, 1, 0)) > 0

    def tie_cut():
        n_gt = (count(lambda kv, kpos: kv > thr) + jnp.where(thr < _KEY_NEG, rem, 0))
        need = topk - n_gt

        def bis2(it, cut):
            bit = lax.shift_left(jnp.int32(1), 14 - it)
            c2 = cut | bit
            n = count(lambda kv, kpos: jnp.logical_and(kv == thr, kpos < c2))
            return jnp.where(n < need, c2, cut)
        return lax.fori_loop(0, 15, bis2, jnp.zeros((1, tq), i32))

    cut = lax.cond(has_tie, tie_cut, lambda: jnp.full((1, tq), INT_MAX, i32))

    def bias_blk(kb, carry):
        k0 = pl.multiple_of(kb * ts, ts)
        kv = key_ref[pl.ds(k0, ts), :]
        interior = jnp.logical_and(k0 >= FRAME_OFF, k0 + ts - 1 <= i * tq)

        @pl.when(jnp.logical_and(interior, jnp.logical_not(has_tie)))
        def _():
            bias_ref[pl.ds(k0, ts), :] = jnp.where(kv >= thr, 0.0, NEG).astype(bias_ref.dtype)

        @pl.when(jnp.logical_not(jnp.logical_and(interior, jnp.logical_not(has_tie))))
        def _():
            kpos = k0 + sub
            sel = jnp.logical_or(kv > thr, jnp.logical_and(kv == thr, kpos <= cut))
            vis = jnp.logical_and(kpos <= qpos, kpos >= FRAME_OFF)
            bias_ref[pl.ds(k0, ts), :] = jnp.where(jnp.logical_and(sel, vis), 0.0, NEG).astype(bias_ref.dtype)
        return carry

    lax.fori_loop(0, nkb, bias_blk, 0)

    def fill_blk(kb, carry):
        k0 = pl.multiple_of(kb * ts, ts)
        bias_ref[pl.ds(k0, ts), :] = jnp.full((ts, tq), NEG, bias_ref.dtype)
        return carry

    lax.fori_loop(nkb, lp // ts, fill_blk, 0)


def dsa_select(iq, ik, iwt, topk):
    nh, lp, _ = iq.shape
    assert lp < 2 ** 15
    tq = 256
    ts = _pick(lp, (640, 256, 128))
    kern = functools.partial(_select_kernel, tq=tq, ts=ts, lp=lp, topk=topk)
    return pl.pallas_call(
        kern,
        grid=(lp // tq,),
        in_specs=[pl.BlockSpec((nh, tq, HEAD_DIM), lambda i: (0, i, 0)),
                  pl.BlockSpec((lp, HEAD_DIM), lambda i: (0, 0), pipeline_mode=pl.Buffered(1)),
                  pl.BlockSpec((nh, tq), lambda i: (0, i))],
        out_specs=pl.BlockSpec((lp, tq), lambda i: (0, i)),
        out_shape=jax.ShapeDtypeStruct((lp, lp), bf16),
        scratch_shapes=[pltpu.VMEM((lp, tq), i32), pltpu.VMEM((lp, tq), i16)],
        compiler_params=_cparams(("parallel",)),
        name="dsa_select",
    )(iq, ik, iwt)


V_AUG = HEAD_DIM + BF16_SUBLANES


def _attn_kernel(q_ref, k_ref, vt_ref, b_ref, o_ref, m_ref, acc_ref, *, tq, ts, bpt):
    i = pl.program_id(0)
    nkb = ((i + 1) * tq + ts - 1) // ts
    group = ATT_HEADS // ATT_KV_HEADS
    kv_groups = range(ATT_KV_HEADS)
    qpos = i * tq + lax.broadcasted_iota(i32, (tq, 1), 0)
    eye = (lax.broadcasted_iota(i32, (tq, tq), 0)
           == lax.broadcasted_iota(i32, (tq, tq), 1)).astype(bf16)
    qa = [jnp.concatenate(
        [jnp.concatenate([q_ref[:, (g * group + r) * HEAD_DIM:(g * group + r + 1) * HEAD_DIM], eye],
                         axis=1) for r in range(group)], axis=0) for g in kv_groups]
    m_ref[...] = jnp.full(m_ref.shape, -jnp.inf, f32)
    acc_ref[...] = jnp.zeros_like(acc_ref)

    def scores(g, k0):
        ka = jnp.concatenate([k_ref[pl.ds(k0, ts), g * HEAD_DIM:(g + 1) * HEAD_DIM],
                              b_ref[pl.ds(k0, ts), :]], axis=1)
        return _dot_nt(ka, qa[g]).astype(bf16)

    def accumulate(g, k0, st):
        vt_ = vt_ref[g, :, pl.ds(k0, ts)]
        ps, alphas = [], []
        for r in range(group):
            sr = st[:, r * tq:(r + 1) * tq]
            m_old = m_ref[g, r:r + 1, :]
            m_new = jnp.maximum(m_old, jnp.max(sr, axis=0, keepdims=True).astype(f32))
            m_ref[g, r:r + 1, :] = m_new
            ps.append(jnp.exp2(sr - m_new.astype(bf16)))
            alphas.append(jnp.exp2(m_old - m_new))
        pt = jnp.concatenate(ps, axis=1)
        acc_ref[g] = jnp.concatenate(alphas, axis=1) * acc_ref[g] + _dot(vt_, pt)

    def make_trip(nblk, first_block):
        def trip(kp, carry):
            k0s = [pl.multiple_of((first_block + kp * nblk + b) * ts, ts) for b in range(nblk)]
            work = [(g, kk) for kk in k0s for g in kv_groups]
            st = scores(*work[0])
            for n in range(len(work)):
                st_next = scores(*work[n + 1]) if n + 1 < len(work) else None
                accumulate(*work[n], st)
                st = st_next
            return carry
        return trip

    n_full = nkb // bpt
    lax.fori_loop(0, n_full, make_trip(bpt, 0), 0)
    if bpt > 1:
        lax.fori_loop(0, nkb - n_full * bpt, make_trip(1, n_full * bpt), 0)
    for g in kv_groups:
        for r in range(group):
            ot = (acc_ref[g, 0:HEAD_DIM, r * tq:(r + 1) * tq]
                  / acc_ref[g, HEAD_DIM:HEAD_DIM + 1, r * tq:(r + 1) * tq])
            orr = jnp.where(qpos >= FRAME_OFF, ot.T, 0.0)
            o_ref[:, (g * group + r) * HEAD_DIM:(g * group + r + 1) * HEAD_DIM] = orr.astype(o_ref.dtype)


def dsa_attention(q, k, vt_aug, bias_t):
    lp = q.shape[0]
    tq = 128
    ts = _pick(lp, (1280, 640, 256, 128))
    bpt = 2
    group = ATT_HEADS // ATT_KV_HEADS
    kern = functools.partial(_attn_kernel, tq=tq, ts=ts, bpt=bpt)
    resident = pl.Buffered(1)
    return pl.pallas_call(
        kern,
        grid=(lp // tq,),
        in_specs=[pl.BlockSpec((tq, ATT_Q_W), lambda i: (i, 0)),
                  pl.BlockSpec((lp, ATT_KV_W), lambda i: (0, 0), pipeline_mode=resident),
                  pl.BlockSpec((ATT_KV_HEADS, V_AUG, lp), lambda i: (0, 0, 0), pipeline_mode=resident),
                  pl.BlockSpec((lp, tq), lambda i: (0, i))],
        out_specs=pl.BlockSpec((tq, ATT_Q_W), lambda i: (i, 0)),
        out_shape=jax.ShapeDtypeStruct((lp, ATT_Q_W), bf16),
        scratch_shapes=[pltpu.VMEM((ATT_KV_HEADS, group, tq), f32),
                        pltpu.VMEM((ATT_KV_HEADS, V_AUG, group * tq), f32)],
        compiler_params=_cparams(("parallel",)),
        name="dsa_attention",
    )(q, k, vt_aug, bias_t)


def _ffn_up_kernel(a_ref, ah_ref, wg_ref, wv_ref, cg_ref, cv_ref, bg_ref, bv_ref, o_ref,
                   eg_ref, ev_ref, *, tm):
    a = _halo_rows(a_ref, ah_ref, pl.program_id(0))
    eg_ref[...] = _dot(a, wg_ref[...])
    ev_ref[...] = _dot(a, wv_ref[...])

    def conv(ext_ref, w_ref, b_ref):
        w = w_ref[...]
        y = ext_ref[HALO:HALO + tm, :] * w[FFN_CONV - 1:FFN_CONV, :]
        for j in range(FFN_CONV - 1):
            s0 = HALO - (FFN_CONV - 1) + j
            y = y + ext_ref[s0:s0 + tm, :] * w[j:j + 1, :]
        return y + b_ref[...]

    gate = conv(eg_ref, cg_ref, bg_ref)
    val = conv(ev_ref, cv_ref, bv_ref)
    o_ref[...] = (gate * _sigmoid(gate) * val).astype(o_ref.dtype)


def ffn_up(u, w_up, conv_w, conv_b):
    lp, kdim = u.shape
    tm = _pick(lp, (1280, 640, 256, 128))
    tn = 512
    nc = D_FF // tn
    kern = functools.partial(_ffn_up_kernel, tm=tm)

    def wspec(off):
        return pl.BlockSpec((kdim, tn), lambda i, c: (0, c + off))

    def cspec(off):
        return pl.BlockSpec((FFN_CONV, tn), lambda i, c: (0, c + off))

    def bspec(off):
        return pl.BlockSpec((1, tn), lambda i, c: (0, c + off))

    b2 = conv_b.reshape(1, 2 * D_FF)
    return pl.pallas_call(
        kern,
        grid=(lp // tm, nc),
        in_specs=[pl.BlockSpec((tm, kdim), lambda i, c: (i, 0)),
                  pl.BlockSpec((HALO, kdim), lambda i, c: (jnp.maximum(i * (tm // HALO) - 1, 0), 0)),
                  wspec(0), wspec(nc), cspec(0), cspec(nc), bspec(0), bspec(nc)],
        out_specs=pl.BlockSpec((tm, tn), lambda i, c: (i, c)),
        out_shape=jax.ShapeDtypeStruct((lp, D_FF), bf16),
        scratch_shapes=[pltpu.VMEM((HALO + tm, tn), f32)] * 2,
        compiler_params=_cparams(("parallel", "arbitrary")),
        name="ffn_up",
    )(u, u, w_up, w_up, conv_w, conv_w, b2, b2)


def _split_w_in(w):
    wt = w.T
    o = 0
    parts = {}
    for name, width in (("gq", GDN_QK_W), ("gk", GDN_QK_W), ("gv", GDN_V_W), ("gz", GDN_V_W),
                        ("gb", GDN_V_HEADS), ("ga", GDN_V_HEADS), ("aq", ATT_Q_W), ("ak", ATT_KV_W),
                        ("av", ATT_KV_W), ("iq", IDX_Q_W), ("ik", HEAD_DIM), ("iw", IDX_HEADS),
                        ("gate_gdn", D_MODEL), ("gate_att", D_MODEL)):
        parts[name] = wt[o:o + width]
        o += width
    cat = lambda names: jnp.concatenate([parts[n] for n in names], axis=0)
    small = cat(("gb", "ga", "iw"))
    small = jnp.pad(small, ((0, LANES - small.shape[0]), (0, 0)))
    groups = dict(qkv=cat(("gq", "gk", "gv")), aq=parts["aq"], iq=parts["iq"], kv=cat(("ak", "ik")),
                  plain=cat(("gz", "gate_gdn", "gate_att", "av")), small=small)
    return {k: v.astype(bf16) for k, v in groups.items()}


def _layer(h0, p, lp, topk, last):
    tm = _pick(lp, (640, 256, 128))
    tm_big = _pick(lp, (1280, 640, 256, 128))
    wp = _split_w_in(p["w_in"])

    def proj(w, **kw):
        return matmul(u1, w, tm=tm_big, tk=D_MODEL, w_nt=True, **kw)

    u1 = rms_rows(h0, p["mix_pre_g"])
    tabs = rope_tables(lp)
    tab_specs = (((tm_big, HEAD_DIM), lambda i, j: (i, 0)),) * 3
    qkv = gdn_proj(u1, wp["qkv"], p["gdn_conv_w"])
    (aq,) = proj(wp["aq"], tn=1024, epilogue=_make_ep_rope(HEAD_DIM ** -0.5 * LOG2E, False),
                 extra=tabs, extra_specs=tab_specs,
                 out_shapes=[jax.ShapeDtypeStruct((lp, ATT_Q_W), bf16)], name="proj_aq")
    iq_heads = 1024 // HEAD_DIM
    (iq,) = proj(wp["iq"], tn=1024, epilogue=_make_ep_rope(1.0, True), extra=tabs, extra_specs=tab_specs,
                 out_shapes=[jax.ShapeDtypeStruct((IDX_HEADS, lp, HEAD_DIM), bf16)],
                 out_block_specs=[((iq_heads, tm_big, HEAD_DIM), lambda i, j: (j, i, 0))], name="proj_iq")
    ak, ik = proj(wp["kv"], tn=ATT_KV_W + HEAD_DIM, epilogue=_ep_rope_kv, extra=tabs, extra_specs=tab_specs,
                  out_shapes=[jax.ShapeDtypeStruct((lp, ATT_KV_W), bf16),
                              jax.ShapeDtypeStruct((lp, HEAD_DIM), bf16)],
                  out_block_specs=[((tm_big, ATT_KV_W), lambda i, j: (i, 0)),
                                   ((tm_big, HEAD_DIM), lambda i, j: (i, 0))], name="proj_kv")
    (plain,) = proj(wp["plain"], tn=768, epilogue=_ep_cast,
                    out_shapes=[jax.ShapeDtypeStruct((lp, PLAIN_W), bf16)], name="proj_plain")
    (small,) = proj(wp["small"], tn=LANES, epilogue=_ep_cast,
                    out_shapes=[jax.ShapeDtypeStruct((lp, LANES), f32)], name="proj_small")
    gb = small[:, 0:GDN_V_HEADS]
    ga = small[:, GDN_V_HEADS:2 * GDN_V_HEADS]
    iw = small[:, 2 * GDN_V_HEADS:2 * GDN_V_HEADS + IDX_HEADS]

    beta, gcum = gdn_gates(gb, ga, p["gdn_a_log"], p["gdn_dt_bias"])
    o_gdn = gdn_core(qkv, plain, p["gdn_norm_g"], gcum.T[:, None, :], beta.T[:, None, :])

    avt = plain[:, COL_AV:COL_AV + ATT_KV_W].T.reshape(ATT_KV_HEADS, HEAD_DIM, lp)
    avt = jnp.concatenate([avt, jnp.ones((ATT_KV_HEADS, V_AUG - HEAD_DIM, lp), bf16)], axis=1)
    bias_t = dsa_select(iq, ik, iw.T, topk)
    o_att = dsa_attention(aq, ak, avt, bias_t)

    tn = 1024
    gate_spec = lambda col0: ((tm_big, tn), (lambda i, j, c=col0 // tn: (i, c + j)))
    (m1,) = matmul(o_gdn, p["w_branch_gdn"].astype(bf16), tm=tm_big, tn=tn, tk=2048,
                   epilogue=_ep_gate, extra=(plain,), extra_specs=(gate_spec(COL_GATE_GDN),),
                   out_shapes=[jax.ShapeDtypeStruct((lp, D_MODEL), f32)], name="branch_gdn")
    (merged,) = matmul(o_att, p["w_branch_att"].astype(bf16), tm=tm_big, tn=tn, tk=2048,
                       epilogue=_ep_gate_add, extra=(plain, m1),
                       extra_specs=(gate_spec(COL_GATE_ATT), ((tm_big, tn), lambda i, j: (i, j))),
                       out_shapes=[jax.ShapeDtypeStruct((lp, D_MODEL), bf16)], name="branch_att")
    row_spec = ((tm, D_MODEL), lambda i, j: (i, 0))
    vec_spec = ((1, D_MODEL), lambda i, j: (0, 0))
    h1, u2 = matmul(merged, p["w_out"].astype(bf16), tm=tm, tn=D_MODEL, tk=D_MODEL,
                    epilogue=_ep_res_norm2,
                    extra=(h0, p["mix_post_g"].reshape(1, D_MODEL), p["ffn_pre_g"].reshape(1, D_MODEL)),
                    extra_specs=(row_spec, vec_spec, vec_spec),
                    out_shapes=[jax.ShapeDtypeStruct((lp, D_MODEL), f32),
                                jax.ShapeDtypeStruct((lp, D_MODEL), bf16)], name="w_out")

    act = ffn_up(u2, p["w_up"].astype(bf16), p["ffn_conv_w"], p["ffn_conv_b"])
    if not last:
        (h2,) = matmul(act, p["w_down"].astype(bf16), tm=tm, tn=D_MODEL, tk=2048, epilogue=_ep_res_norm,
                       extra=(h1, p["ffn_post_g"].reshape(1, D_MODEL)),
                       extra_specs=(row_spec, vec_spec),
                       out_shapes=[jax.ShapeDtypeStruct((lp, D_MODEL), f32)], name="w_down")
        return h2
    tmx = FRAME_X0
    (out,) = matmul(act, p["w_down"].astype(bf16), tm=tmx, tn=D_MODEL, tk=D_FF, epilogue=_ep_res_norm,
                    extra=(h1, p["ffn_post_g"].reshape(1, D_MODEL)),
                    extra_specs=(((tmx, D_MODEL), lambda i, j: (i, 0)), vec_spec),
                    out_shapes=[jax.ShapeDtypeStruct((lp - FRAME_X0, D_MODEL), f32)],
                    out_block_specs=[((tmx, D_MODEL), lambda i, j: (jnp.maximum(i - 1, 0), 0))],
                    rows_sem="arbitrary", name="w_down")
    return out


def kernel(x, meta_tokens, mix_pre_g, w_in, gdn_conv_w, gdn_a_log, gdn_dt_bias, gdn_norm_g,
           w_branch_gdn, w_branch_att, w_out, mix_post_g, ffn_pre_g, w_up, ffn_conv_w,
           ffn_conv_b, w_down, ffn_post_g):
    batch, seq, d = x.shape
    assert batch == 1 and d == D_MODEL
    lp = FRAME_X0 + seq
    topk = min(TOPK_MAX, (N_META + seq) // 4)
    h = jnp.concatenate([jnp.zeros((FRAME_OFF, d), x.dtype), meta_tokens.astype(x.dtype), x[0]], axis=0)
    depth = w_in.shape[0]
    for i in range(depth):
        p = dict(mix_pre_g=mix_pre_g[i], w_in=w_in[i], gdn_conv_w=gdn_conv_w[i], gdn_a_log=gdn_a_log[i],
                 gdn_dt_bias=gdn_dt_bias[i], gdn_norm_g=gdn_norm_g[i], w_branch_gdn=w_branch_gdn[i],
                 w_branch_att=w_branch_att[i], w_out=w_out[i], mix_post_g=mix_post_g[i],
                 ffn_pre_g=ffn_pre_g[i], w_up=w_up[i], ffn_conv_w=ffn_conv_w[i],
                 ffn_conv_b=ffn_conv_b[i], w_down=w_down[i], ffn_post_g=ffn_post_g[i])
        h = _layer(h, p, lp, topk, last=(i == depth - 1))
    return h[None]
```

```python
import functools
import struct

import jax
import jax.numpy as jnp
from jax import lax
from jax.experimental import pallas as pl
from jax.experimental.pallas import tpu as pltpu

f32 = jnp.float32
bf16 = jnp.bfloat16
i32 = jnp.int32
i16 = jnp.int16

D_MODEL = 2048
N_META = 16
EPS = 1e-6
GDN_QK_HEADS = 16
GDN_V_HEADS = 32
HEAD_DIM = 128
GDN_CONV = 4
ATT_HEADS = 16
ATT_KV_HEADS = 2
IDX_HEADS = 16
TOPK_MAX = 256
NEG = -1e30
LOG2E = 1.4426950408889634
ROPE_THETA = 500000.0
ROPE_DIMS = HEAD_DIM // 4
D_FF = 3 * D_MODEL
FFN_CONV = 3
GDN_QK_W = GDN_QK_HEADS * HEAD_DIM
GDN_V_W = GDN_V_HEADS * HEAD_DIM
ATT_Q_W = ATT_HEADS * HEAD_DIM
ATT_KV_W = ATT_KV_HEADS * HEAD_DIM
IDX_Q_W = IDX_HEADS * HEAD_DIM

FRAME_X0 = 256
FRAME_OFF = FRAME_X0 - N_META
CHUNK = 128
GDN_QK_PER_STEP = 2

LANES = 128
BF16_SUBLANES = 16
VMEM_LIMIT = 56 * 1024 * 1024

COL_GZ = 0
COL_GATE_GDN = COL_GZ + GDN_V_W
COL_GATE_ATT = COL_GATE_GDN + D_MODEL
COL_AV = COL_GATE_ATT + D_MODEL
PLAIN_W = COL_AV + ATT_KV_W


def _cparams(sem):
    return pltpu.CompilerParams(dimension_semantics=sem, vmem_limit_bytes=VMEM_LIMIT)


def _pick(n, cands):
    for c in cands:
        if n % c == 0:
            return c
    raise ValueError(f"no tile for {n} in {cands}")


def _sigmoid(x):
    return 1.0 / (1.0 + jnp.exp(-x))


def _dot(a, b):
    return jnp.dot(a, b, preferred_element_type=f32)


def _dot_nt(a, b):
    return lax.dot_general(a, b, (((1,), (1,)), ((), ())), preferred_element_type=f32)


def _rms_rows_kernel(h_ref, g_ref, o_ref):
    h = h_ref[...]
    y = h * lax.rsqrt(jnp.mean(h * h, axis=-1, keepdims=True) + EPS)
    o_ref[...] = (y * g_ref[...]).astype(o_ref.dtype)


def rms_rows(h, g):
    lp, d = h.shape
    tm = _pick(lp, (640, 256, 128))
    return pl.pallas_call(
        _rms_rows_kernel,
        grid=(lp // tm,),
        in_specs=[pl.BlockSpec((tm, d), lambda i: (i, 0)),
                  pl.BlockSpec((1, d), lambda i: (0, 0))],
        out_specs=pl.BlockSpec((tm, d), lambda i: (i, 0)),
        out_shape=jax.ShapeDtypeStruct((lp, d), bf16),
        compiler_params=_cparams(("parallel",)),
        name="rms_rows",
    )(h, g.reshape(1, d))


def _mm_kernel(*refs, nk, n_extra, n_out, epilogue, w_nt):
    a_ref, w_ref = refs[0], refs[1]
    dot = _dot_nt if w_nt else _dot
    extra = refs[2:2 + n_extra]
    outs = refs[2 + n_extra:2 + n_extra + n_out]
    if nk == 1:
        epilogue(dot(a_ref[...], w_ref[...]), extra, outs)
        return
    acc_ref = refs[-1]
    k = pl.program_id(2)

    @pl.when(k == 0)
    def _():
        acc_ref[...] = jnp.zeros_like(acc_ref)

    acc_ref[...] += dot(a_ref[...], w_ref[...])

    @pl.when(k == nk - 1)
    def _():
        epilogue(acc_ref[...], extra, outs)


def matmul(a, w, *, tm, tn, tk, epilogue, extra=(), extra_specs=(), out_shapes, out_block_specs=None,
           w_nt=False, rows_sem="parallel", name):
    m = a.shape[0]
    kdim, n = w.shape[::-1] if w_nt else w.shape
    nk = kdim // tk
    assert m % tm == 0 and n % tn == 0 and kdim % tk == 0
    w_mode = dict(pipeline_mode=pl.Buffered(1)) if (nk == 1 and n == tn) else {}
    in_specs = [pl.BlockSpec((tm, tk), lambda i, j, k: (i, k)),
                (pl.BlockSpec((tn, tk), lambda i, j, k: (j, k), **w_mode) if w_nt else
                 pl.BlockSpec((tk, tn), lambda i, j, k: (k, j), **w_mode))]
    in_specs += [pl.BlockSpec(bs, (lambda i, j, k, f=f: f(i, j))) for bs, f in extra_specs]
    if out_block_specs is None:
        out_block_specs = [((tm, tn), lambda i, j: (i, j))] * len(out_shapes)
    out_specs = [pl.BlockSpec(bs, (lambda i, j, k, f=f: f(i, j))) for bs, f in out_block_specs]
    scratch = [] if nk == 1 else [pltpu.VMEM((tm, tn), f32)]
    kern = functools.partial(_mm_kernel, nk=nk, n_extra=len(extra), n_out=len(out_shapes),
                             epilogue=epilogue, w_nt=w_nt)
    res = pl.pallas_call(
        kern,
        grid=(m // tm, n // tn, nk),
        in_specs=in_specs,
        out_specs=out_specs,
        out_shape=out_shapes,
        scratch_shapes=scratch,
        compiler_params=_cparams((rows_sem, rows_sem, "arbitrary")),
        name=name,
    )(a, w, *extra)
    return res


def _ep_cast(acc, extra, outs):
    outs[0][...] = acc.astype(outs[0].dtype)


def _ep_gate(acc, extra, outs):
    g = extra[0][...].astype(f32)
    outs[0][...] = (_sigmoid(g) * acc).astype(outs[0].dtype)


def _ep_gate_add(acc, extra, outs):
    g = extra[0][...].astype(f32)
    outs[0][...] = (extra[1][...].astype(f32) + _sigmoid(g) * acc).astype(outs[0].dtype)


def _rms(t, g):
    return t * lax.rsqrt(jnp.mean(t * t, axis=-1, keepdims=True) + EPS) * g


def _ep_res_norm2(acc, extra, outs):
    h_ref, g_ref, g2_ref = extra
    h1 = h_ref[...] + _rms(acc, g_ref[...])
    outs[0][...] = h1
    outs[1][...] = _rms(h1, g2_ref[...]).astype(outs[1].dtype)


def _ep_res_norm(acc, extra, outs):
    h_ref, g_ref = extra
    outs[0][...] = h_ref[...] + _rms(acc, g_ref[...])


HALO = BF16_SUBLANES


def _halo_rows(a_ref, ah_ref, i):
    halo = ah_ref[...]
    halo = jnp.where(i > 0, halo, jnp.zeros_like(halo))
    return jnp.concatenate([halo, a_ref[...]], axis=0)


def _gdn_proj_kernel(a_ref, ah_ref, w_ref, cw_ref, o_ref, ext_ref, *, tm, tn):
    c = pl.program_id(1)
    ext_ref[...] = _dot_nt(_halo_rows(a_ref, ah_ref, pl.program_id(0)), w_ref[...])
    w = cw_ref[...]
    y = ext_ref[HALO:HALO + tm, :] * w[GDN_CONV - 1:GDN_CONV, :]
    for j in range(GDN_CONV - 1):
        s0 = HALO - (GDN_CONV - 1) + j
        y = y + ext_ref[s0:s0 + tm, :] * w[j:j + 1, :]
    s = y * _sigmoid(y)
    is_q = c < (GDN_QK_W // tn)
    is_qk = c < (2 * GDN_QK_W // tn)
    qscale = jnp.where(is_q, HEAD_DIM ** -0.5, 1.0).astype(f32)
    for hh in range(tn // HEAD_DIM):
        seg = s[:, hh * HEAD_DIM:(hh + 1) * HEAD_DIM]
        r = lax.rsqrt(jnp.sum(seg * seg, axis=-1, keepdims=True) + EPS) * qscale
        fac = jnp.where(is_qk, r, 1.0)
        o_ref[:, hh * HEAD_DIM:(hh + 1) * HEAD_DIM] = (seg * fac).astype(o_ref.dtype)


def gdn_proj(u, w_qkv, conv_w):
    lp, kdim = u.shape
    width = w_qkv.shape[0]
    tm = _pick(lp, (1280, 640, 256, 128))
    tn = 1024
    kern = functools.partial(_gdn_proj_kernel, tm=tm, tn=tn)
    return pl.pallas_call(
        kern,
        grid=(lp // tm, width // tn),
        in_specs=[pl.BlockSpec((tm, kdim), lambda i, c: (i, 0)),
                  pl.BlockSpec((HALO, kdim), lambda i, c: (jnp.maximum(i * (tm // HALO) - 1, 0), 0)),
                  pl.BlockSpec((tn, kdim), lambda i, c: (c, 0)),
                  pl.BlockSpec((GDN_CONV, tn), lambda i, c: (0, c))],
        out_specs=pl.BlockSpec((tm, tn), lambda i, c: (i, c)),
        out_shape=jax.ShapeDtypeStruct((lp, width), bf16),
        scratch_shapes=[pltpu.VMEM((HALO + tm, tn), f32)],
        compiler_params=_cparams(("parallel", "arbitrary")),
        name="gdn_proj",
    )(u, u, w_qkv, conv_w)


def _gdn_gates_kernel(b_ref, a_ref, alog_ref, dt_ref, beta_ref, gcum_ref, *, tm):
    i = pl.program_id(0)
    rows = i * tm + lax.broadcasted_iota(i32, (tm, 1), 0)
    valid = rows >= FRAME_OFF
    beta_ref[...] = jnp.where(valid, _sigmoid(b_ref[...]), 0.0)
    a = a_ref[...] + dt_ref[...]
    sp = jnp.maximum(a, 0.0) + jnp.log1p(jnp.exp(-jnp.abs(a)))
    g = jnp.where(valid, -jnp.exp(alog_ref[...]) * sp, 0.0)
    tri = (lax.broadcasted_iota(i32, (CHUNK, CHUNK), 0)
           >= lax.broadcasted_iota(i32, (CHUNK, CHUNK), 1)).astype(f32)
    for c in range(tm // CHUNK):
        rs = slice(c * CHUNK, (c + 1) * CHUNK)
        gcum_ref[rs, :] = jnp.dot(tri, g[rs, :], preferred_element_type=f32,
                                  precision=lax.Precision.HIGHEST)


def gdn_gates(gb, ga, a_log, dt_bias):
    lp, nh = gb.shape
    tm = _pick(lp, (1280, 640, 256, 128))
    spec = pl.BlockSpec((tm, nh), lambda i: (i, 0))
    vec = pl.BlockSpec((1, nh), lambda i: (0, 0))
    return pl.pallas_call(
        functools.partial(_gdn_gates_kernel, tm=tm),
        grid=(lp // tm,),
        in_specs=[spec, spec, vec, vec],
        out_specs=[spec, spec],
        out_shape=[jax.ShapeDtypeStruct((lp, nh), f32)] * 2,
        compiler_params=_cparams(("parallel",)),
        name="gdn_gates",
    )(gb, ga, a_log.reshape(1, nh), dt_bias.reshape(1, nh))


def _block_mask(size):
    r = lax.broadcasted_iota(i32, (CHUNK, CHUNK), 0) // size
    c = lax.broadcasted_iota(i32, (CHUNK, CHUNK), 1) // size
    return r == c


def _unit_lower_inverse_many(mats, tick):
    row = lax.broadcasted_iota(i32, (CHUNK, CHUNK), 0)
    col = lax.broadcasted_iota(i32, (CHUNK, CHUNK), 1)
    eye = (row == col).astype(f32)
    base = 8
    m_prev = _block_mask(base)
    ads = [jnp.where(m_prev, a, 0.0) for a in mats]
    adbs = [ad.astype(bf16) for ad in ads]
    a2s = [_dot(x, x) for x in adbs]
    tick()
    a2bs = [x.astype(bf16) for x in a2s]
    a4s = [_dot(x, x) for x in a2bs]
    tick()
    xs = [_dot((eye - ad).astype(bf16), (eye + a2).astype(bf16)) for ad, a2 in zip(ads, a2s)]
    tick()
    xs = [_dot(x.astype(bf16), (eye + a4).astype(bf16)) for x, a4 in zip(xs, a4s)]
    tick()
    size = base * 2
    while size <= CHUNK:
        m_cur = _block_mask(size)
        off_diag = jnp.logical_and(m_cur, jnp.logical_not(m_prev))
        bs = [jnp.where(off_diag, a, 0.0).astype(bf16) for a in mats]
        xbs = [x.astype(bf16) for x in xs]
        ys = [_dot(xb, b).astype(bf16) for xb, b in zip(xbs, bs)]
        tick()
        xs = [x - _dot(y, xb) for x, y, xb in zip(xs, ys, xbs)]
        tick()
        m_prev = m_cur
        size *= 2
    return xs


def _gdn_kernel(q_ref, k_ref, v_ref, z_ref, ng_ref, *rest, rb, nq):
    r = pl.program_id(1)
    nv = 2 * nq
    grs, brs = rest[:nv], rest[nv:2 * nv]
    o_ref, s_ref, u_ref, wq_ref, ak_ref, eg_ref, egl_ref = rest[2 * nv:]

    @pl.when(r == 0)
    def _():
        for ref in (s_ref, u_ref, wq_ref, ak_ref, eg_ref, egl_ref):
            ref[...] = jnp.zeros_like(ref)

    row = lax.broadcasted_iota(i32, (CHUNK, CHUNK), 0)
    col = lax.broadcasted_iota(i32, (CHUNK, CHUNK), 1)
    incl = row >= col
    strict = row > col
    ng = ng_ref[...]
    nchunk = rb // CHUNK
    heads = range(nv)

    def rows(c):
        return slice(c * CHUNK, (c + 1) * CHUNK)

    def lanes(e):
        return slice(e * HEAD_DIM, (e + 1) * HEAD_DIM)

    chains = [(c, e) for c in range(nchunk) for e in heads]

    def recurrence():
        state = [s_ref[e] for e in heads]
        for c in range(nchunk):
            n = [chains.index((c, e)) for e in heads]
            sb = [state[e].astype(bf16) for e in heads]
            ws = [_dot(wq_ref[n[e]], sb[e]) for e in heads]
            yield
            vb = [(u_ref[n[e]] - ws[e][:CHUNK]).astype(bf16) for e in heads]
            av = [_dot(ak_ref[n[e]], vb[e]) for e in heads]
            yield
            for e in heads:
                o = eg_ref[n[e]] * ws[e][CHUNK:] + av[e][:CHUNK]
                state[e] = state[e] * egl_ref[n[e]] + av[e][CHUNK:]
                z = z_ref[rows(c), lanes(e)].astype(f32)
                o_ref[rows(c), lanes(e)] = (_rms(o, ng) * (z * _sigmoid(z))).astype(o_ref.dtype)
        for e in heads:
            s_ref[e] = state[e]

    rec = recurrence()

    def tick():
        next(rec, None)

    qkh = [(c, a) for c in range(nchunk) for a in range(nq)]
    qs = {ca: q_ref[rows(ca[0]), lanes(ca[1])] for ca in qkh}
    ks = {ca: k_ref[rows(ca[0]), lanes(ca[1])] for ca in qkh}
    kts = {ca: ks[ca].astype(f32).T for ca in qkh}
    kks = {ca: _dot_nt(ks[ca], ks[ca]) for ca in qkh}
    qks = {ca: _dot_nt(qs[ca], ks[ca]) for ca in qkh}
    qk_of = {ce: (ce[0], ce[1] // 2) for ce in chains}
    tick()
    def per_row(v):
        return jnp.broadcast_to(v, (CHUNK, CHUNK)).T

    gr = {ce: grs[ce[1]][:, rows(ce[0])] for ce in chains}
    br = {ce: brs[ce[1]][:, rows(ce[0])] for ce in chains}
    gc = {ce: per_row(gr[ce]) for ce in chains}
    bc = {ce: per_row(br[ce]) for ce in chains}
    dec = {ce: jnp.where(incl, jnp.exp(jnp.where(incl, gc[ce] - gr[ce], 0.0)), 0.0) for ce in chains}
    amat = [jnp.where(strict, bc[ce] * kks[qk_of[ce]] * dec[ce], 0.0) for ce in chains]
    tinv = dict(zip(chains, _unit_lower_inverse_many(amat, tick)))
    u = {ce: _dot((tinv[ce] * br[ce]).astype(bf16), v_ref[rows(ce[0]), lanes(ce[1])]) for ce in chains}
    tick()
    w = {ce: _dot((tinv[ce] * (br[ce] * jnp.exp(gr[ce]))).astype(bf16), ks[qk_of[ce]]) for ce in chains}
    tick()
    g_last = {ce: gr[ce][:, CHUNK - 1:CHUNK] for ce in chains}
    wq = {ce: jnp.concatenate([w[ce].astype(bf16), qs[qk_of[ce]]], axis=0) for ce in chains}
    ak = {ce: jnp.concatenate(
        [jnp.where(incl, qks[qk_of[ce]] * dec[ce], 0.0).astype(bf16),
         (kts[qk_of[ce]] * jnp.exp(g_last[ce] - gr[ce])).astype(bf16)], axis=0) for ce in chains}
    for _ in rec:
        pass
    for n, ce in enumerate(chains):
        u_ref[n] = u[ce]
        wq_ref[n] = wq[ce]
        ak_ref[n] = ak[ce]
        eg_ref[n] = jnp.exp(gc[ce])
        egl_ref[n] = jnp.broadcast_to(jnp.exp(g_last[ce]), (1, HEAD_DIM))


def gdn_core(qkv, plain, norm_g, grow, brow):
    assert CHUNK == HEAD_DIM
    lp = qkv.shape[0]
    rb = _pick(lp, (640, 256, 128))
    nq = GDN_QK_PER_STEP
    nv = 2 * nq
    qw, vw = nq * HEAD_DIM, nv * HEAD_DIM
    kcol0 = GDN_QK_W // qw
    vcol0 = 2 * GDN_QK_W // vw
    zcol0 = COL_GZ // vw

    nr = lp // rb
    n_chain = nv * (rb // CHUNK)

    def cur(r):
        return jnp.minimum(r, nr - 1)

    def prev(r):
        return jnp.maximum(r - 1, 0)

    def rowspec(e):
        return pl.BlockSpec((None, 1, rb), lambda j, r: (nv * j + e, 0, cur(r)))

    rowspecs = [rowspec(e) for e in range(nv)]
    kern = functools.partial(_gdn_kernel, rb=rb, nq=nq)
    return pl.pallas_call(
        kern,
        grid=(GDN_QK_HEADS // nq, nr + 1),
        in_specs=[pl.BlockSpec((rb, qw), lambda j, r: (cur(r), j)),
                  pl.BlockSpec((rb, qw), lambda j, r: (cur(r), kcol0 + j)),
                  pl.BlockSpec((rb, vw), lambda j, r: (cur(r), vcol0 + j)),
                  pl.BlockSpec((rb, vw), lambda j, r: (prev(r), zcol0 + j)),
                  pl.BlockSpec((1, HEAD_DIM), lambda j, r: (0, 0))] + rowspecs + rowspecs,
        out_specs=pl.BlockSpec((rb, vw), lambda j, r: (prev(r), j)),
        out_shape=jax.ShapeDtypeStruct((lp, GDN_V_W), bf16),
        scratch_shapes=[pltpu.VMEM((nv, HEAD_DIM, HEAD_DIM), f32),
                        pltpu.VMEM((n_chain, CHUNK, HEAD_DIM), f32),
                        pltpu.VMEM((n_chain, 2 * CHUNK, HEAD_DIM), bf16),
                        pltpu.VMEM((n_chain, 2 * CHUNK, CHUNK), bf16),
                        pltpu.VMEM((n_chain, CHUNK, HEAD_DIM), f32),
                        pltpu.VMEM((n_chain, 1, HEAD_DIM), f32)],
        compiler_params=_cparams(("parallel", "arbitrary")),
        name="gdn_core",
    )(qkv, qkv, qkv, plain, norm_g.reshape(1, HEAD_DIM), *([grow] * nv), *([brow] * nv))


def _rope_head(seg, cc, s1, s2):
    half = ROPE_DIMS // 2
    return seg * cc + pltpu.roll(seg, half, 1) * s1 + pltpu.roll(seg, HEAD_DIM - half, 1) * s2


def _make_ep_rope(scale, head_major):
    def ep(acc, extra, outs):
        cc, s1, s2 = (r[...] for r in extra)
        for h in range(acc.shape[1] // HEAD_DIM):
            y = _rope_head(acc[:, h * HEAD_DIM:(h + 1) * HEAD_DIM], cc, s1, s2)
            if scale != 1.0:
                y = y * scale
            if head_major:
                outs[0][h] = y.astype(outs[0].dtype)
            else:
                outs[0][:, h * HEAD_DIM:(h + 1) * HEAD_DIM] = y.astype(outs[0].dtype)
    return ep


def _ep_rope_kv(acc, extra, outs):
    cc, s1, s2 = (r[...] for r in extra)
    for h in range(ATT_KV_HEADS):
        y = _rope_head(acc[:, h * HEAD_DIM:(h + 1) * HEAD_DIM], cc, s1, s2)
        outs[0][:, h * HEAD_DIM:(h + 1) * HEAD_DIM] = y.astype(outs[0].dtype)
    y = _rope_head(acc[:, ATT_KV_W:ATT_KV_W + HEAD_DIM], cc, s1, s2)
    outs[1][...] = y.astype(outs[1].dtype)


def rope_tables(lp):
    half = ROPE_DIMS // 2
    pos = (jnp.arange(lp, dtype=jnp.int32) - FRAME_OFF).astype(f32)
    inv = ROPE_THETA ** (-jnp.arange(half, dtype=f32) / half)
    ang = pos[:, None] * inv[None, :]
    cos, sin = jnp.cos(ang), jnp.sin(ang)
    zeros = jnp.zeros((lp, HEAD_DIM - ROPE_DIMS), f32)
    z16 = jnp.zeros((lp, half), f32)
    cc = jnp.concatenate([cos, cos, jnp.ones_like(zeros)], axis=1)
    s1 = jnp.concatenate([z16, sin, zeros], axis=1)
    s2 = jnp.concatenate([-sin, z16, zeros], axis=1)
    return cc, s1, s2


INT_MAX = 2 ** 31 - 1
HALF_MIN, HALF_MAX, HALF_SPAN = -2 ** 15, 2 ** 15 - 1, 2 ** 16


def _sortable(x):
    b = lax.bitcast_convert_type(x, i32)
    return b ^ (lax.shift_right_arithmetic(b, 31) & INT_MAX)


_NEG_BITS = struct.unpack("<i", struct.pack("<f", NEG))[0]
_KEY_NEG = _NEG_BITS ^ ((_NEG_BITS >> 31) & INT_MAX)
_KEY_NEG_HI = _KEY_NEG >> 16
_KEY_NEG_LO = (_KEY_NEG & (HALF_SPAN - 1)) + HALF_MIN


def _select_kernel(iq_ref, ik_ref, iwt_ref, bias_ref, key_ref, half_ref, *, tq, ts, lp, topk):
    i = pl.program_id(0)
    nkb = ((i + 1) * tq + ts - 1) // ts
    rem = lp - nkb * ts
    iwt = iwt_ref[...] * ((IDX_HEADS ** -0.5) * (HEAD_DIM ** -0.5))
    qpos = i * tq + lax.broadcasted_iota(i32, (1, tq), 1)
    sub = lax.broadcasted_iota(i32, (ts, 1), 0)

    def score_blk(kb):
        k0 = pl.multiple_of(kb * ts, ts)
        ikb = ik_ref[pl.ds(k0, ts), :]
        acc = jnp.zeros((ts, tq), f32)
        for h in range(IDX_HEADS):
            lg = _dot_nt(ikb, iq_ref[h])
            acc = acc + jnp.maximum(lg, 0.0) * iwt[h:h + 1, :]
        kpos = k0 + sub
        sc = jnp.where(kpos <= qpos, acc, NEG)
        sc = jnp.where(kpos < FRAME_OFF, -jnp.inf, sc)
        key = _sortable(sc)
        key_ref[pl.ds(k0, ts), :] = key
        half_ref[pl.ds(k0, ts), :] = lax.shift_right_arithmetic(key, 16).astype(i16)

    def score_pair(kp, carry):
        score_blk(2 * kp)
        score_blk(2 * kp + 1)
        return carry

    def score_last(_, carry):
        score_blk(nkb - 1)
        return carry

    lax.fori_loop(0, nkb // 2, score_pair, 0)
    lax.fori_loop(0, nkb % 2, score_last, 0)

    n_acc = 4

    def count(pred_fn):
        sub8 = lax.broadcasted_iota(i32, (8, 1), 0)

        def blk(kb, cnts):
            k0 = pl.multiple_of(kb * ts, ts)
            cnts = list(cnts)
            blk_ref = key_ref.at[pl.ds(k0, ts)]
            for j in range(ts // 8):
                kv = blk_ref[j * 8:(j + 1) * 8, :]
                cnts[j % n_acc] = cnts[j % n_acc] + jnp.where(pred_fn(kv, k0 + j * 8 + sub8), 1, 0)
            return tuple(cnts)
        cnts = lax.fori_loop(0, nkb, blk, tuple(jnp.zeros((8, tq), i32) for _ in range(n_acc)))
        return jnp.sum(sum(cnts[1:], cnts[0]), axis=0, keepdims=True)

    def count_half_ge(cand):
        one, zero = jnp.int16(1), jnp.int16(0)

        def blk(kb, cnts):
            k0 = pl.multiple_of(kb * ts, ts)
            cnts = list(cnts)
            blk_ref = half_ref.at[pl.ds(k0, ts)]
            for j in range(ts // 16):
                hv = blk_ref[j * 16:(j + 1) * 16, :]
                cnts[j % n_acc] = cnts[j % n_acc] + jnp.where(hv >= cand, one, zero)
            return tuple(cnts)
        cnts = lax.fori_loop(0, nkb, blk, tuple(jnp.zeros((16, tq), i16) for _ in range(n_acc)))
        return jnp.sum(sum(cnts[1:], cnts[0]).astype(i32), axis=0, keepdims=True)

    def search_half(neg_half, n_start):
        def body(it, c):
            tau, n_tau = c
            t2 = tau | lax.shift_left(jnp.int32(1), 15 - it)
            cand = t2 + HALF_MIN
            n2 = count_half_ge(cand.astype(i16)) + jnp.where(cand <= neg_half, rem, 0)
            ok = n2 >= topk
            return jnp.where(ok, t2, tau), jnp.where(ok, n2, n_tau)

        return lax.fori_loop(0, 16, body, (jnp.zeros((1, tq), i32), n_start))

    tau_hi, n_hi = search_half(jnp.full((1, tq), _KEY_NEG_HI, i32), jnp.full((1, tq), lp, i32))
    t_hi = tau_hi + HALF_MIN

    base = lax.shift_left(t_hi, 16)

    def low_blk(kb, carry):
        k0 = pl.multiple_of(kb * ts, ts)
        key = key_ref[pl.ds(k0, ts), :]
        lo = jnp.minimum(jnp.maximum(key, base), base + (HALF_SPAN - 1)) - (base - HALF_MIN)
        half_ref[pl.ds(k0, ts), :] = lo.astype(i16)
        return carry

    lax.fori_loop(0, nkb, low_blk, 0)
    neg_lo = jnp.where(_KEY_NEG_HI > t_hi, HALF_MAX, jnp.where(_KEY_NEG_HI == t_hi, _KEY_NEG_LO, HALF_MIN))
    tau_lo, n_ge = search_half(neg_lo, n_hi)
    thr = lax.shift_left(t_hi, 16) | tau_lo
    has_tie = jnp.max(jnp.where(n_ge > topk, 1, 0)) > 0

    def tie_cut():
        n_gt = (count(lambda kv, kpos: kv > thr) + jnp.where(thr < _KEY_NEG, rem, 0))
        need = topk - n_gt

        def bis2(it, cut):
            bit = lax.shift_left(jnp.int32(1), 14 - it)
            c2 = cut | bit
            n = count(lambda kv, kpos: jnp.logical_and(kv == thr, kpos < c2))
            return jnp.where(n < need, c2, cut)
        return lax.fori_loop(0, 15, bis2, jnp.zeros((1, tq), i32))

    cut = lax.cond(has_tie, tie_cut, lambda: jnp.full((1, tq), INT_MAX, i32))

    def bias_blk(kb, carry):
        k0 = pl.multiple_of(kb * ts, ts)
        kv = key_ref[pl.ds(k0, ts), :]
        interior = jnp.logical_and(k0 >= FRAME_OFF, k0 + ts - 1 <= i * tq)

        @pl.when(jnp.logical_and(interior, jnp.logical_not(has_tie)))
        def _():
            bias_ref[pl.ds(k0, ts), :] = jnp.where(kv >= thr, 0.0, NEG).astype(bias_ref.dtype)

        @pl.when(jnp.logical_not(jnp.logical_and(interior, jnp.logical_not(has_tie))))
        def _():
            kpos = k0 + sub
            sel = jnp.logical_or(kv > thr, jnp.logical_and(kv == thr, kpos <= cut))
            vis = jnp.logical_and(kpos <= qpos, kpos >= FRAME_OFF)
            bias_ref[pl.ds(k0, ts), :] = jnp.where(jnp.logical_and(sel, vis), 0.0, NEG).astype(bias_ref.dtype)
        return carry

    lax.fori_loop(0, nkb, bias_blk, 0)

    def fill_blk(kb, carry):
        k0 = pl.multiple_of(kb * ts, ts)
        bias_ref[pl.ds(k0, ts), :] = jnp.full((ts, tq), NEG, bias_ref.dtype)
        return carry

    lax.fori_loop(nkb, lp // ts, fill_blk, 0)


def dsa_select(iq, ik, iwt, topk):
    nh, lp, _ = iq.shape
    assert lp < 2 ** 15
    tq = 256
    ts = _pick(lp, (640, 256, 128))
    kern = functools.partial(_select_kernel, tq=tq, ts=ts, lp=lp, topk=topk)
    return pl.pallas_call(
        kern,
        grid=(lp // tq,),
        in_specs=[pl.BlockSpec((nh, tq, HEAD_DIM), lambda i: (0, i, 0)),
                  pl.BlockSpec((lp, HEAD_DIM), lambda i: (0, 0), pipeline_mode=pl.Buffered(1)),
                  pl.BlockSpec((nh, tq), lambda i: (0, i))],
        out_specs=pl.BlockSpec((lp, tq), lambda i: (0, i)),
        out_shape=jax.ShapeDtypeStruct((lp, lp), bf16),
        scratch_shapes=[pltpu.VMEM((lp, tq), i32), pltpu.VMEM((lp, tq), i16)],
        compiler_params=_cparams(("parallel",)),
        name="dsa_select",
    )(iq, ik, iwt)


V_AUG = HEAD_DIM + BF16_SUBLANES


def _attn_kernel(q_ref, k_ref, vt_ref, b_ref, o_ref, m_ref, acc_ref, *, tq, ts, bpt):
    i = pl.program_id(0)
    nkb = ((i + 1) * tq + ts - 1) // ts
    group = ATT_HEADS // ATT_KV_HEADS
    kv_groups = range(ATT_KV_HEADS)
    qpos = i * tq + lax.broadcasted_iota(i32, (tq, 1), 0)
    eye = (lax.broadcasted_iota(i32, (tq, tq), 0)
           == lax.broadcasted_iota(i32, (tq, tq), 1)).astype(bf16)
    qa = [jnp.concatenate(
        [jnp.concatenate([q_ref[:, (g * group + r) * HEAD_DIM:(g * group + r + 1) * HEAD_DIM], eye],
                         axis=1) for r in range(group)], axis=0) for g in kv_groups]
    m_ref[...] = jnp.full(m_ref.shape, -jnp.inf, f32)
    acc_ref[...] = jnp.zeros_like(acc_ref)

    def scores(g, k0):
        ka = jnp.concatenate([k_ref[pl.ds(k0, ts), g * HEAD_DIM:(g + 1) * HEAD_DIM],
                              b_ref[pl.ds(k0, ts), :]], axis=1)
        return _dot_nt(ka, qa[g]).astype(bf16)

    def accumulate(g, k0, st):
        vt_ = vt_ref[g, :, pl.ds(k0, ts)]
        ps, alphas = [], []
        for r in range(group):
            sr = st[:, r * tq:(r + 1) * tq]
            m_old = m_ref[g, r:r + 1, :]
            m_new = jnp.maximum(m_old, jnp.max(sr, axis=0, keepdims=True).astype(f32))
            m_ref[g, r:r + 1, :] = m_new
            ps.append(jnp.exp2(sr - m_new.astype(bf16)))
            alphas.append(jnp.exp2(m_old - m_new))
        pt = jnp.concatenate(ps, axis=1)
        acc_ref[g] = jnp.concatenate(alphas, axis=1) * acc_ref[g] + _dot(vt_, pt)

    def make_trip(nblk, first_block):
        def trip(kp, carry):
            k0s = [pl.multiple_of((first_block + kp * nblk + b) * ts, ts) for b in range(nblk)]
            work = [(g, kk) for kk in k0s for g in kv_groups]
            st = scores(*work[0])
            for n in range(len(work)):
                st_next = scores(*work[n + 1]) if n + 1 < len(work) else None
                accumulate(*work[n], st)
                st = st_next
            return carry
        return trip

    n_full = nkb // bpt
    lax.fori_loop(0, n_full, make_trip(bpt, 0), 0)
    if bpt > 1:
        lax.fori_loop(0, nkb - n_full * bpt, make_trip(1, n_full * bpt), 0)
    for g in kv_groups:
        for r in range(group):
            ot = (acc_ref[g, 0:HEAD_DIM, r * tq:(r + 1) * tq]
                  / acc_ref[g, HEAD_DIM:HEAD_DIM + 1, r * tq:(r + 1) * tq])
            orr = jnp.where(qpos >= FRAME_OFF, ot.T, 0.0)
            o_ref[:, (g * group + r) * HEAD_DIM:(g * group + r + 1) * HEAD_DIM] = orr.astype(o_ref.dtype)


def dsa_attention(q, k, vt_aug, bias_t):
    lp = q.shape[0]
    tq = 128
    ts = _pick(lp, (1280, 640, 256, 128))
    bpt = 2
    group = ATT_HEADS // ATT_KV_HEADS
    kern = functools.partial(_attn_kernel, tq=tq, ts=ts, bpt=bpt)
    resident = pl.Buffered(1)
    return pl.pallas_call(
        kern,
        grid=(lp // tq,),
        in_specs=[pl.BlockSpec((tq, ATT_Q_W), lambda i: (i, 0)),
                  pl.BlockSpec((lp, ATT_KV_W), lambda i: (0, 0), pipeline_mode=resident),
                  pl.BlockSpec((ATT_KV_HEADS, V_AUG, lp), lambda i: (0, 0, 0), pipeline_mode=resident),
                  pl.BlockSpec((lp, tq), lambda i: (0, i))],
        out_specs=pl.BlockSpec((tq, ATT_Q_W), lambda i: (i, 0)),
        out_shape=jax.ShapeDtypeStruct((lp, ATT_Q_W), bf16),
        scratch_shapes=[pltpu.VMEM((ATT_KV_HEADS, group, tq), f32),
                        pltpu.VMEM((ATT_KV_HEADS, V_AUG, group * tq), f32)],
        compiler_params=_cparams(("parallel",)),
        name="dsa_attention",
    )(q, k, vt_aug, bias_t)


def _ffn_up_kernel(a_ref, ah_ref, wg_ref, wv_ref, cg_ref, cv_ref, bg_ref, bv_ref, o_ref,
                   eg_ref, ev_ref, *, tm):
    a = _halo_rows(a_ref, ah_ref, pl.program_id(0))
    eg_ref[...] = _dot(a, wg_ref[...])
    ev_ref[...] = _dot(a, wv_ref[...])

    def conv(ext_ref, w_ref, b_ref):
        w = w_ref[...]
        y = ext_ref[HALO:HALO + tm, :] * w[FFN_CONV - 1:FFN_CONV, :]
        for j in range(FFN_CONV - 1):
            s0 = HALO - (FFN_CONV - 1) + j
            y = y + ext_ref[s0:s0 + tm, :] * w[j:j + 1, :]
        return y + b_ref[...]

    gate = conv(eg_ref, cg_ref, bg_ref)
    val = conv(ev_ref, cv_ref, bv_ref)
    o_ref[...] = (gate * _sigmoid(gate) * val).astype(o_ref.dtype)


def ffn_up(u, w_up, conv_w, conv_b):
    lp, kdim = u.shape
    tm = _pick(lp, (1280, 640, 256, 128))
    tn = 512
    nc = D_FF // tn
    kern = functools.partial(_ffn_up_kernel, tm=tm)

    def wspec(off):
        return pl.BlockSpec((kdim, tn), lambda i, c: (0, c + off))

    def cspec(off):
        return pl.BlockSpec((FFN_CONV, tn), lambda i, c: (0, c + off))

    def bspec(off):
        return pl.BlockSpec((1, tn), lambda i, c: (0, c + off))

    b2 = conv_b.reshape(1, 2 * D_FF)
    return pl.pallas_call(
        kern,
        grid=(lp // tm, nc),
        in_specs=[pl.BlockSpec((tm, kdim), lambda i, c: (i, 0)),
                  pl.BlockSpec((HALO, kdim), lambda i, c: (jnp.maximum(i * (tm // HALO) - 1, 0), 0)),
                  wspec(0), wspec(nc), cspec(0), cspec(nc), bspec(0), bspec(nc)],
        out_specs=pl.BlockSpec((tm, tn), lambda i, c: (i, c)),
        out_shape=jax.ShapeDtypeStruct((lp, D_FF), bf16),
        scratch_shapes=[pltpu.VMEM((HALO + tm, tn), f32)] * 2,
        compiler_params=_cparams(("parallel", "arbitrary")),
        name="ffn_up",
    )(u, u, w_up, w_up, conv_w, conv_w, b2, b2)


def _split_w_in(w):
    wt = w.T
    o = 0
    parts = {}
    for name, width in (("gq", GDN_QK_W), ("gk", GDN_QK_W), ("gv", GDN_V_W), ("gz", GDN_V_W),
                        ("gb", GDN_V_HEADS), ("ga", GDN_V_HEADS), ("aq", ATT_Q_W), ("ak", ATT_KV_W),
                        ("av", ATT_KV_W), ("iq", IDX_Q_W), ("ik", HEAD_DIM), ("iw", IDX_HEADS),
                        ("gate_gdn", D_MODEL), ("gate_att", D_MODEL)):
        parts[name] = wt[o:o + width]
        o += width
    cat = lambda names: jnp.concatenate([parts[n] for n in names], axis=0)
    small = cat(("gb", "ga", "iw"))
    small = jnp.pad(small, ((0, LANES - small.shape[0]), (0, 0)))
    groups = dict(qkv=cat(("gq", "gk", "gv")), aq=parts["aq"], iq=parts["iq"], kv=cat(("ak", "ik")),
                  plain=cat(("gz", "gate_gdn", "gate_att", "av")), small=small)
    return {k: v.astype(bf16) for k, v in groups.items()}


def _layer(h0, p, lp, topk, last):
    tm = _pick(lp, (640, 256, 128))
    tm_big = _pick(lp, (1280, 640, 256, 128))
    wp = _split_w_in(p["w_in"])

    def proj(w, **kw):
        return matmul(u1, w, tm=tm_big, tk=D_MODEL, w_nt=True, **kw)

    u1 = rms_rows(h0, p["mix_pre_g"])
    tabs = rope_tables(lp)
    tab_specs = (((tm_big, HEAD_DIM), lambda i, j: (i, 0)),) * 3
    qkv = gdn_proj(u1, wp["qkv"], p["gdn_conv_w"])
    (aq,) = proj(wp["aq"], tn=1024, epilogue=_make_ep_rope(HEAD_DIM ** -0.5 * LOG2E, False),
                 extra=tabs, extra_specs=tab_specs,
                 out_shapes=[jax.ShapeDtypeStruct((lp, ATT_Q_W), bf16)], name="proj_aq")
    iq_heads = 1024 // HEAD_DIM
    (iq,) = proj(wp["iq"], tn=1024, epilogue=_make_ep_rope(1.0, True), extra=tabs, extra_specs=tab_specs,
                 out_shapes=[jax.ShapeDtypeStruct((IDX_HEADS, lp, HEAD_DIM), bf16)],
                 out_block_specs=[((iq_heads, tm_big, HEAD_DIM), lambda i, j: (j, i, 0))], name="proj_iq")
    ak, ik = proj(wp["kv"], tn=ATT_KV_W + HEAD_DIM, epilogue=_ep_rope_kv, extra=tabs, extra_specs=tab_specs,
                  out_shapes=[jax.ShapeDtypeStruct((lp, ATT_KV_W), bf16),
                              jax.ShapeDtypeStruct((lp, HEAD_DIM), bf16)],
                  out_block_specs=[((tm_big, ATT_KV_W), lambda i, j: (i, 0)),
                                   ((tm_big, HEAD_DIM), lambda i, j: (i, 0))], name="proj_kv")
    (plain,) = proj(wp["plain"], tn=768, epilogue=_ep_cast,
                    out_shapes=[jax.ShapeDtypeStruct((lp, PLAIN_W), bf16)], name="proj_plain")
    (small,) = proj(wp["small"], tn=LANES, epilogue=_ep_cast,
                    out_shapes=[jax.ShapeDtypeStruct((lp, LANES), f32)], name="proj_small")
    gb = small[:, 0:GDN_V_HEADS]
    ga = small[:, GDN_V_HEADS:2 * GDN_V_HEADS]
    iw = small[:, 2 * GDN_V_HEADS:2 * GDN_V_HEADS + IDX_HEADS]

    beta, gcum = gdn_gates(gb, ga, p["gdn_a_log"], p["gdn_dt_bias"])
    o_gdn = gdn_core(qkv, plain, p["gdn_norm_g"], gcum.T[:, None, :], beta.T[:, None, :])

    avt = plain[:, COL_AV:COL_AV + ATT_KV_W].T.reshape(ATT_KV_HEADS, HEAD_DIM, lp)
    avt = jnp.concatenate([avt, jnp.ones((ATT_KV_HEADS, V_AUG - HEAD_DIM, lp), bf16)], axis=1)
    bias_t = dsa_select(iq, ik, iw.T, topk)
    o_att = dsa_attention(aq, ak, avt, bias_t)

    tn = 1024
    gate_spec = lambda col0: ((tm_big, tn), (lambda i, j, c=col0 // tn: (i, c + j)))
    (m1,) = matmul(o_gdn, p["w_branch_gdn"].astype(bf16), tm=tm_big, tn=tn, tk=2048,
                   epilogue=_ep_gate, extra=(plain,), extra_specs=(gate_spec(COL_GATE_GDN),),
                   out_shapes=[jax.ShapeDtypeStruct((lp, D_MODEL), f32)], name="branch_gdn")
    (merged,) = matmul(o_att, p["w_branch_att"].astype(bf16), tm=tm_big, tn=tn, tk=2048,
                       epilogue=_ep_gate_add, extra=(plain, m1),
                       extra_specs=(gate_spec(COL_GATE_ATT), ((tm_big, tn), lambda i, j: (i, j))),
                       out_shapes=[jax.ShapeDtypeStruct((lp, D_MODEL), bf16)], name="branch_att")
    row_spec = ((tm, D_MODEL), lambda i, j: (i, 0))
    vec_spec = ((1, D_MODEL), lambda i, j: (0, 0))
    h1, u2 = matmul(merged, p["w_out"].astype(bf16), tm=tm, tn=D_MODEL, tk=D_MODEL,
                    epilogue=_ep_res_norm2,
                    extra=(h0, p["mix_post_g"].reshape(1, D_MODEL), p["ffn_pre_g"].reshape(1, D_MODEL)),
                    extra_specs=(row_spec, vec_spec, vec_spec),
                    out_shapes=[jax.ShapeDtypeStruct((lp, D_MODEL), f32),
                                jax.ShapeDtypeStruct((lp, D_MODEL), bf16)], name="w_out")

    act = ffn_up(u2, p["w_up"].astype(bf16), p["ffn_conv_w"], p["ffn_conv_b"])
    if not last:
        (h2,) = matmul(act, p["w_down"].astype(bf16), tm=tm, tn=D_MODEL, tk=2048, epilogue=_ep_res_norm,
                       extra=(h1, p["ffn_post_g"].reshape(1, D_MODEL)),
                       extra_specs=(row_spec, vec_spec),
                       out_shapes=[jax.ShapeDtypeStruct((lp, D_MODEL), f32)], name="w_down")
        return h2
    tmx = FRAME_X0
    (out,) = matmul(act, p["w_down"].astype(bf16), tm=tmx, tn=D_MODEL, tk=D_FF, epilogue=_ep_res_norm,
                    extra=(h1, p["ffn_post_g"].reshape(1, D_MODEL)),
                    extra_specs=(((tmx, D_MODEL), lambda i, j: (i, 0)), vec_spec),
                    out_shapes=[jax.ShapeDtypeStruct((lp - FRAME_X0, D_MODEL), f32)],
                    out_block_specs=[((tmx, D_MODEL), lambda i, j: (jnp.maximum(i - 1, 0), 0))],
                    rows_sem="arbitrary", name="w_down")
    return out


def kernel(x, meta_tokens, mix_pre_g, w_in, gdn_conv_w, gdn_a_log, gdn_dt_bias, gdn_norm_g,
           w_branch_gdn, w_branch_att, w_out, mix_post_g, ffn_pre_g, w_up, ffn_conv_w,
           ffn_conv_b, w_down, ffn_post_g):
    batch, seq, d = x.shape
    assert batch == 1 and d == D_MODEL
    lp = FRAME_X0 + seq
    topk = min(TOPK_MAX, (N_META + seq) // 4)
    h = jnp.concatenate([jnp.zeros((FRAME_OFF, d), x.dtype), meta_tokens.astype(x.dtype), x[0]], axis=0)
    depth = w_in.shape[0]
    for i in range(depth):
        p = dict(mix_pre_g=mix_pre_g[i], w_in=w_in[i], gdn_conv_w=gdn_conv_w[i], gdn_a_log=gdn_a_log[i],
                 gdn_dt_bias=gdn_dt_bias[i], gdn_norm_g=gdn_norm_g[i], w_branch_gdn=w_branch_gdn[i],
                 w_branch_att=w_branch_att[i], w_out=w_out[i], mix_post_g=mix_post_g[i],
                 ffn_pre_g=ffn_pre_g[i], w_up=w_up[i], ffn_conv_w=ffn_conv_w[i],
                 ffn_conv_b=ffn_conv_b[i], w_down=w_down[i], ffn_post_g=ffn_post_g[i])
        h = _layer(h, p, lp, topk, last=(i == depth - 1))
    return h[None]
```

```python
import functools
import struct

import jax
import jax.numpy as jnp
from jax import lax
from jax.experimental import pallas as pl
from jax.experimental.pallas import tpu as pltpu

f32 = jnp.float32
bf16 = jnp.bfloat16
i32 = jnp.int32
i16 = jnp.int16

D_MODEL = 2048
N_META = 16
EPS = 1e-6
GDN_QK_HEADS = 16
GDN_V_HEADS = 32
HEAD_DIM = 128
GDN_CONV = 4
ATT_HEADS = 16
ATT_KV_HEADS = 2
IDX_HEADS = 16
TOPK_MAX = 256
NEG = -1e30
LOG2E = 1.4426950408889634
ROPE_THETA = 500000.0
ROPE_DIMS = HEAD_DIM // 4
D_FF = 3 * D_MODEL
FFN_CONV = 3
GDN_QK_W = GDN_QK_HEADS * HEAD_DIM
GDN_V_W = GDN_V_HEADS * HEAD_DIM
ATT_Q_W = ATT_HEADS * HEAD_DIM
ATT_KV_W = ATT_KV_HEADS * HEAD_DIM
IDX_Q_W = IDX_HEADS * HEAD_DIM

FRAME_X0 = 256
FRAME_OFF = FRAME_X0 - N_META
CHUNK = 128
GDN_QK_PER_STEP = 2

LANES = 128
BF16_SUBLANES = 16
VMEM_LIMIT = 56 * 1024 * 1024

COL_GZ = 0
COL_GATE_GDN = COL_GZ + GDN_V_W
COL_GATE_ATT = COL_GATE_GDN + D_MODEL
COL_AV = COL_GATE_ATT + D_MODEL
PLAIN_W = COL_AV + ATT_KV_W


def _cparams(sem):
    return pltpu.CompilerParams(dimension_semantics=sem, vmem_limit_bytes=VMEM_LIMIT)


def _pick(n, cands):
    for c in cands:
        if n % c == 0:
            return c
    raise ValueError(f"no tile for {n} in {cands}")


def _sigmoid(x):
    return 1.0 / (1.0 + jnp.exp(-x))


def _dot(a, b):
    return jnp.dot(a, b, preferred_element_type=f32)


def _dot_nt(a, b):
    return lax.dot_general(a, b, (((1,), (1,)), ((), ())), preferred_element_type=f32)


def _rms_rows_kernel(h_ref, g_ref, o_ref):
    h = h_ref[...]
    y = h * lax.rsqrt(jnp.mean(h * h, axis=-1, keepdims=True) + EPS)
    o_ref[...] = (y * g_ref[...]).astype(o_ref.dtype)


def rms_rows(h, g):
    lp, d = h.shape
    tm = _pick(lp, (640, 256, 128))
    return pl.pallas_call(
        _rms_rows_kernel,
        grid=(lp // tm,),
        in_specs=[pl.BlockSpec((tm, d), lambda i: (i, 0)),
                  pl.BlockSpec((1, d), lambda i: (0, 0))],
        out_specs=pl.BlockSpec((tm, d), lambda i: (i, 0)),
        out_shape=jax.ShapeDtypeStruct((lp, d), bf16),
        compiler_params=_cparams(("parallel",)),
        name="rms_rows",
    )(h, g.reshape(1, d))


def _mm_kernel(*refs, nk, n_extra, n_out, epilogue, w_nt):
    a_ref, w_ref = refs[0], refs[1]
    dot = _dot_nt if w_nt else _dot
    extra = refs[2:2 + n_extra]
    outs = refs[2 + n_extra:2 + n_extra + n_out]
    if nk == 1:
        epilogue(dot(a_ref[...], w_ref[...]), extra, outs)
        return
    acc_ref = refs[-1]
    k = pl.program_id(2)

    @pl.when(k == 0)
    def _():
        acc_ref[...] = jnp.zeros_like(acc_ref)

    acc_ref[...] += dot(a_ref[...], w_ref[...])

    @pl.when(k == nk - 1)
    def _():
        epilogue(acc_ref[...], extra, outs)


def matmul(a, w, *, tm, tn, tk, epilogue, extra=(), extra_specs=(), out_shapes, out_block_specs=None,
           w_nt=False, rows_sem="parallel", name):
    m = a.shape[0]
    kdim, n = w.shape[::-1] if w_nt else w.shape
    nk = kdim // tk
    assert m % tm == 0 and n % tn == 0 and kdim % tk == 0
    w_mode = dict(pipeline_mode=pl.Buffered(1)) if (nk == 1 and n == tn) else {}
    in_specs = [pl.BlockSpec((tm, tk), lambda i, j, k: (i, k)),
                (pl.BlockSpec((tn, tk), lambda i, j, k: (j, k), **w_mode) if w_nt else
                 pl.BlockSpec((tk, tn), lambda i, j, k: (k, j), **w_mode))]
    in_specs += [pl.BlockSpec(bs, (lambda i, j, k, f=f: f(i, j))) for bs, f in extra_specs]
    if out_block_specs is None:
        out_block_specs = [((tm, tn), lambda i, j: (i, j))] * len(out_shapes)
    out_specs = [pl.BlockSpec(bs, (lambda i, j, k, f=f: f(i, j))) for bs, f in out_block_specs]
    scratch = [] if nk == 1 else [pltpu.VMEM((tm, tn), f32)]
    kern = functools.partial(_mm_kernel, nk=nk, n_extra=len(extra), n_out=len(out_shapes),
                             epilogue=epilogue, w_nt=w_nt)
    res = pl.pallas_call(
        kern,
        grid=(m // tm, n // tn, nk),
        in_specs=in_specs,
        out_specs=out_specs,
        out_shape=out_shapes,
        scratch_shapes=scratch,
        compiler_params=_cparams((rows_sem, rows_sem, "arbitrary")),
        name=name,
    )(a, w, *extra)
    return res


def _ep_cast(acc, extra, outs):
    outs[0][...] = acc.astype(outs[0].dtype)


def _ep_gate(acc, extra, outs):
    g = extra[0][...].astype(f32)
    outs[0][...] = (_sigmoid(g) * acc).astype(outs[0].dtype)


def _ep_gate_add(acc, extra, outs):
    g = extra[0][...].astype(f32)
    outs[0][...] = (extra[1][...].astype(f32) + _sigmoid(g) * acc).astype(outs[0].dtype)


def _rms(t, g):
    return t * lax.rsqrt(jnp.mean(t * t, axis=-1, keepdims=True) + EPS) * g


def _ep_res_norm2(acc, extra, outs):
    h_ref, g_ref, g2_ref = extra
    h1 = h_ref[...] + _rms(acc, g_ref[...])
    outs[0][...] = h1
    outs[1][...] = _rms(h1, g2_ref[...]).astype(outs[1].dtype)


def _ep_res_norm(acc, extra, outs):
    h_ref, g_ref = extra
    outs[0][...] = h_ref[...] + _rms(acc, g_ref[...])


HALO = BF16_SUBLANES


def _halo_rows(a_ref, ah_ref, i):
    halo = ah_ref[...]
    halo = jnp.where(i > 0, halo, jnp.zeros_like(halo))
    return jnp.concatenate([halo, a_ref[...]], axis=0)


def _gdn_proj_kernel(a_ref, ah_ref, w_ref, cw_ref, o_ref, ext_ref, *, tm, tn):
    c = pl.program_id(1)
    ext_ref[...] = _dot_nt(_halo_rows(a_ref, ah_ref, pl.program_id(0)), w_ref[...])
    w = cw_ref[...]
    y = ext_ref[HALO:HALO + tm, :] * w[GDN_CONV - 1:GDN_CONV, :]
    for j in range(GDN_CONV - 1):
        s0 = HALO - (GDN_CONV - 1) + j
        y = y + ext_ref[s0:s0 + tm, :] * w[j:j + 1, :]
    s = y * _sigmoid(y)
    is_q = c < (GDN_QK_W // tn)
    is_qk = c < (2 * GDN_QK_W // tn)
    qscale = jnp.where(is_q, HEAD_DIM ** -0.5, 1.0).astype(f32)
    for hh in range(tn // HEAD_DIM):
        seg = s[:, hh * HEAD_DIM:(hh + 1) * HEAD_DIM]
        r = lax.rsqrt(jnp.sum(seg * seg, axis=-1, keepdims=True) + EPS) * qscale
        fac = jnp.where(is_qk, r, 1.0)
        o_ref[:, hh * HEAD_DIM:(hh + 1) * HEAD_DIM] = (seg * fac).astype(o_ref.dtype)


def gdn_proj(u, w_qkv, conv_w):
    lp, kdim = u.shape
    width = w_qkv.shape[0]
    tm = _pick(lp, (1280, 640, 256, 128))
    tn = 1024
    kern = functools.partial(_gdn_proj_kernel, tm=tm, tn=tn)
    return pl.pallas_call(
        kern,
        grid=(lp // tm, width // tn),
        in_specs=[pl.BlockSpec((tm, kdim), lambda i, c: (i, 0)),
                  pl.BlockSpec((HALO, kdim), lambda i, c: (jnp.maximum(i * (tm // HALO) - 1, 0), 0)),
                  pl.BlockSpec((tn, kdim), lambda i, c: (c, 0)),
                  pl.BlockSpec((GDN_CONV, tn), lambda i, c: (0, c))],
        out_specs=pl.BlockSpec((tm, tn), lambda i, c: (i, c)),
        out_shape=jax.ShapeDtypeStruct((lp, width), bf16),
        scratch_shapes=[pltpu.VMEM((HALO + tm, tn), f32)],
        compiler_params=_cparams(("parallel", "arbitrary")),
        name="gdn_proj",
    )(u, u, w_qkv, conv_w)


def _gdn_gates_kernel(b_ref, a_ref, alog_ref, dt_ref, beta_ref, gcum_ref, *, tm):
    i = pl.program_id(0)
    rows = i * tm + lax.broadcasted_iota(i32, (tm, 1), 0)
    valid = rows >= FRAME_OFF
    beta_ref[...] = jnp.where(valid, _sigmoid(b_ref[...]), 0.0)
    a = a_ref[...] + dt_ref[...]
    sp = jnp.maximum(a, 0.0) + jnp.log1p(jnp.exp(-jnp.abs(a)))
    g = jnp.where(valid, -jnp.exp(alog_ref[...]) * sp, 0.0)
    tri = (lax.broadcasted_iota(i32, (CHUNK, CHUNK), 0)
           >= lax.broadcasted_iota(i32, (CHUNK, CHUNK), 1)).astype(f32)
    for c in range(tm // CHUNK):
        rs = slice(c * CHUNK, (c + 1) * CHUNK)
        gcum_ref[rs, :] = jnp.dot(tri, g[rs, :], preferred_element_type=f32,
                                  precision=lax.Precision.HIGHEST)


def gdn_gates(gb, ga, a_log, dt_bias):
    lp, nh = gb.shape
    tm = _pick(lp, (1280, 640, 256, 128))
    spec = pl.BlockSpec((tm, nh), lambda i: (i, 0))
    vec = pl.BlockSpec((1, nh), lambda i: (0, 0))
    return pl.pallas_call(
        functools.partial(_gdn_gates_kernel, tm=tm),
        grid=(lp // tm,),
        in_specs=[spec, spec, vec, vec],
        out_specs=[spec, spec],
        out_shape=[jax.ShapeDtypeStruct((lp, nh), f32)] * 2,
        compiler_params=_cparams(("parallel",)),
        name="gdn_gates",
    )(gb, ga, a_log.reshape(1, nh), dt_bias.reshape(1, nh))


def _block_mask(size):
    r = lax.broadcasted_iota(i32, (CHUNK, CHUNK), 0) // size
    c = lax.broadcasted_iota(i32, (CHUNK, CHUNK), 1) // size
    return r == c


def _unit_lower_inverse_many(mats, tick):
    row = lax.broadcasted_iota(i32, (CHUNK, CHUNK), 0)
    col = lax.broadcasted_iota(i32, (CHUNK, CHUNK), 1)
    eye = (row == col).astype(f32)
    base = 8
    m_prev = _block_mask(base)
    ads = [jnp.where(m_prev, a, 0.0) for a in mats]
    adbs = [ad.astype(bf16) for ad in ads]
    a2s = [_dot(x, x) for x in adbs]
    tick()
    a2bs = [x.astype(bf16) for x in a2s]
    a4s = [_dot(x, x) for x in a2bs]
    tick()
    xs = [_dot((eye - ad).astype(bf16), (eye + a2).astype(bf16)) for ad, a2 in zip(ads, a2s)]
    tick()
    xs = [_dot(x.astype(bf16), (eye + a4).astype(bf16)) for x, a4 in zip(xs, a4s)]
    tick()
    size = base * 2
    while size <= CHUNK:
        m_cur = _block_mask(size)
        off_diag = jnp.logical_and(m_cur, jnp.logical_not(m_prev))
        bs = [jnp.where(off_diag, a, 0.0).astype(bf16) for a in mats]
        xbs = [x.astype(bf16) for x in xs]
        ys = [_dot(xb, b).astype(bf16) for xb, b in zip(xbs, bs)]
        tick()
        xs = [x - _dot(y, xb) for x, y, xb in zip(xs, ys, xbs)]
        tick()
        m_prev = m_cur
        size *= 2
    return xs


def _gdn_kernel(q_ref, k_ref, v_ref, z_ref, ng_ref, *rest, rb, nq):
    r = pl.program_id(1)
    nv = 2 * nq
    grs, brs = rest[:nv], rest[nv:2 * nv]
    o_ref, s_ref, u_ref, wq_ref, ak_ref, eg_ref, egl_ref = rest[2 * nv:]

    @pl.when(r == 0)
    def _():
        for ref in (s_ref, u_ref, wq_ref, ak_ref, eg_ref, egl_ref):
            ref[...] = jnp.zeros_like(ref)

    row = lax.broadcasted_iota(i32, (CHUNK, CHUNK), 0)
    col = lax.broadcasted_iota(i32, (CHUNK, CHUNK), 1)
    incl = row >= col
    strict = row > col
    ng = ng_ref[...]
    nchunk = rb // CHUNK
    heads = range(nv)

    def rows(c):
        return slice(c * CHUNK, (c + 1) * CHUNK)

    def lanes(e):
        return slice(e * HEAD_DIM, (e + 1) * HEAD_DIM)

    chains = [(c, e) for c in range(nchunk) for e in heads]

    def recurrence():
        state = [s_ref[e] for e in heads]
        for c in range(nchunk):
            n = [chains.index((c, e)) for e in heads]
            sb = [state[e].astype(bf16) for e in heads]
            ws = [_dot(wq_ref[n[e]], sb[e]) for e in heads]
            yield
            vb = [(u_ref[n[e]] - ws[e][:CHUNK]).astype(bf16) for e in heads]
            av = [_dot(ak_ref[n[e]], vb[e]) for e in heads]
            yield
            for e in heads:
                o = eg_ref[n[e]] * ws[e][CHUNK:] + av[e][:CHUNK]
                state[e] = state[e] * egl_ref[n[e]] + av[e][CHUNK:]
                z = z_ref[rows(c), lanes(e)].astype(f32)
                o_ref[rows(c), lanes(e)] = (_rms(o, ng) * (z * _sigmoid(z))).astype(o_ref.dtype)
        for e in heads:
            s_ref[e] = state[e]

    rec = recurrence()

    def tick():
        next(rec, None)

    qkh = [(c, a) for c in range(nchunk) for a in range(nq)]
    qs = {ca: q_ref[rows(ca[0]), lanes(ca[1])] for ca in qkh}
    ks = {ca: k_ref[rows(ca[0]), lanes(ca[1])] for ca in qkh}
    kts = {ca: ks[ca].astype(f32).T for ca in qkh}
    kks = {ca: _dot_nt(ks[ca], ks[ca]) for ca in qkh}
    qks = {ca: _dot_nt(qs[ca], ks[ca]) for ca in qkh}
    qk_of = {ce: (ce[0], ce[1] // 2) for ce in chains}
    tick()
    def per_row(v):
        return jnp.broadcast_to(v, (CHUNK, CHUNK)).T

    gr = {ce: grs[ce[1]][:, rows(ce[0])] for ce in chains}
    br = {ce: brs[ce[1]][:, rows(ce[0])] for ce in chains}
    gc = {ce: per_row(gr[ce]) for ce in chains}
    bc = {ce: per_row(br[ce]) for ce in chains}
    dec = {ce: jnp.where(incl, jnp.exp(jnp.where(incl, gc[ce] - gr[ce], 0.0)), 0.0) for ce in chains}
    amat = [jnp.where(strict, bc[ce] * kks[qk_of[ce]] * dec[ce], 0.0) for ce in chains]
    tinv = dict(zip(chains, _unit_lower_inverse_many(amat, tick)))
    u = {ce: _dot((tinv[ce] * br[ce]).astype(bf16), v_ref[rows(ce[0]), lanes(ce[1])]) for ce in chains}
    tick()
    w = {ce: _dot((tinv[ce] * (br[ce] * jnp.exp(gr[ce]))).astype(bf16), ks[qk_of[ce]]) for ce in chains}
    tick()
    g_last = {ce: gr[ce][:, CHUNK - 1:CHUNK] for ce in chains}
    wq = {ce: jnp.concatenate([w[ce].astype(bf16), qs[qk_of[ce]]], axis=0) for ce in chains}
    ak = {ce: jnp.concatenate(
        [jnp.where(incl, qks[qk_of[ce]] * dec[ce], 0.0).astype(bf16),
         (kts[qk_of[ce]] * jnp.exp(g_last[ce] - gr[ce])).astype(bf16)], axis=0) for ce in chains}
    for _ in rec:
        pass
    for n, ce in enumerate(chains):
        u_ref[n] = u[ce]
        wq_ref[n] = wq[ce]
        ak_ref[n] = ak[ce]
        eg_ref[n] = jnp.exp(gc[ce])
        egl_ref[n] = jnp.broadcast_to(jnp.exp(g_last[ce]), (1, HEAD_DIM))


def gdn_core(qkv, plain, norm_g, grow, brow):
    assert CHUNK == HEAD_DIM
    lp = qkv.shape[0]
    rb = _pick(lp, (640, 256, 128))
    nq = GDN_QK_PER_STEP
    nv = 2 * nq
    qw, vw = nq * HEAD_DIM, nv * HEAD_DIM
    kcol0 = GDN_QK_W // qw
    vcol0 = 2 * GDN_QK_W // vw
    zcol0 = COL_GZ // vw

    nr = lp // rb
    n_chain = nv * (rb // CHUNK)

    def cur(r):
        return jnp.minimum(r, nr - 1)

    def prev(r):
        return jnp.maximum(r - 1, 0)

    def rowspec(e):
        return pl.BlockSpec((None, 1, rb), lambda j, r: (nv * j + e, 0, cur(r)))

    rowspecs = [rowspec(e) for e in range(nv)]
    kern = functools.partial(_gdn_kernel, rb=rb, nq=nq)
    return pl.pallas_call(
        kern,
        grid=(GDN_QK_HEADS // nq, nr + 1),
        in_specs=[pl.BlockSpec((rb, qw), lambda j, r: (cur(r), j)),
                  pl.BlockSpec((rb, qw), lambda j, r: (cur(r), kcol0 + j)),
                  pl.BlockSpec((rb, vw), lambda j, r: (cur(r), vcol0 + j)),
                  pl.BlockSpec((rb, vw), lambda j, r: (prev(r), zcol0 + j)),
                  pl.BlockSpec((1, HEAD_DIM), lambda j, r: (0, 0))] + rowspecs + rowspecs,
        out_specs=pl.BlockSpec((rb, vw), lambda j, r: (prev(r), j)),
        out_shape=jax.ShapeDtypeStruct((lp, GDN_V_W), bf16),
        scratch_shapes=[pltpu.VMEM((nv, HEAD_DIM, HEAD_DIM), f32),
                        pltpu.VMEM((n_chain, CHUNK, HEAD_DIM), f32),
                        pltpu.VMEM((n_chain, 2 * CHUNK, HEAD_DIM), bf16),
                        pltpu.VMEM((n_chain, 2 * CHUNK, CHUNK), bf16),
                        pltpu.VMEM((n_chain, CHUNK, HEAD_DIM), f32),
                        pltpu.VMEM((n_chain, 1, HEAD_DIM), f32)],
        compiler_params=_cparams(("parallel", "arbitrary")),
        name="gdn_core",
    )(qkv, qkv, qkv, plain, norm_g.reshape(1, HEAD_DIM), *([grow] * nv), *([brow] * nv))


def _rope_head(seg, cc, s1, s2):
    half = ROPE_DIMS // 2
    return seg * cc + pltpu.roll(seg, half, 1) * s1 + pltpu.roll(seg, HEAD_DIM - half, 1) * s2


def _make_ep_rope(scale, head_major):
    def ep(acc, extra, outs):
        cc, s1, s2 = (r[...] for r in extra)
        for h in range(acc.shape[1] // HEAD_DIM):
            y = _rope_head(acc[:, h * HEAD_DIM:(h + 1) * HEAD_DIM], cc, s1, s2)
            if scale != 1.0:
                y = y * scale
            if head_major:
                outs[0][h] = y.astype(outs[0].dtype)
            else:
                outs[0][:, h * HEAD_DIM:(h + 1) * HEAD_DIM] = y.astype(outs[0].dtype)
    return ep


def _ep_rope_kv(acc, extra, outs):
    cc, s1, s2 = (r[...] for r in extra)
    for h in range(ATT_KV_HEADS):
        y = _rope_head(acc[:, h * HEAD_DIM:(h + 1) * HEAD_DIM], cc, s1, s2)
        outs[0][:, h * HEAD_DIM:(h + 1) * HEAD_DIM] = y.astype(outs[0].dtype)
    y = _rope_head(acc[:, ATT_KV_W:ATT_KV_W + HEAD_DIM], cc, s1, s2)
    outs[1][...] = y.astype(outs[1].dtype)


def rope_tables(lp):
    half = ROPE_DIMS // 2
    pos = (jnp.arange(lp, dtype=jnp.int32) - FRAME_OFF).astype(f32)
    inv = ROPE_THETA ** (-jnp.arange(half, dtype=f32) / half)
    ang = pos[:, None] * inv[None, :]
    cos, sin = jnp.cos(ang), jnp.sin(ang)
    zeros = jnp.zeros((lp, HEAD_DIM - ROPE_DIMS), f32)
    z16 = jnp.zeros((lp, half), f32)
    cc = jnp.concatenate([cos, cos, jnp.ones_like(zeros)], axis=1)
    s1 = jnp.concatenate([z16, sin, zeros], axis=1)
    s2 = jnp.concatenate([-sin, z16, zeros], axis=1)
    return cc, s1, s2


INT_MAX = 2 ** 31 - 1
HALF_MIN, HALF_MAX, HALF_SPAN = -2 ** 15, 2 ** 15 - 1, 2 ** 16


def _sortable(x):
    b = lax.bitcast_convert_type(x, i32)
    return b ^ (lax.shift_right_arithmetic(b, 31) & INT_MAX)


_NEG_BITS = struct.unpack("<i", struct.pack("<f", NEG))[0]
_KEY_NEG = _NEG_BITS ^ ((_NEG_BITS >> 31) & INT_MAX)
_KEY_NEG_HI = _KEY_NEG >> 16
_KEY_NEG_LO = (_KEY_NEG & (HALF_SPAN - 1)) + HALF_MIN


def _select_kernel(iq_ref, ik_ref, iwt_ref, bias_ref, key_ref, half_ref, *, tq, ts, lp, topk):
    i = pl.program_id(0)
    nkb = ((i + 1) * tq + ts - 1) // ts
    rem = lp - nkb * ts
    iwt = iwt_ref[...] * ((IDX_HEADS ** -0.5) * (HEAD_DIM ** -0.5))
    qpos = i * tq + lax.broadcasted_iota(i32, (1, tq), 1)
    sub = lax.broadcasted_iota(i32, (ts, 1), 0)

    def score_blk(kb):
        k0 = pl.multiple_of(kb * ts, ts)
        ikb = ik_ref[pl.ds(k0, ts), :]
        acc = jnp.zeros((ts, tq), f32)
        for h in range(IDX_HEADS):
            lg = _dot_nt(ikb, iq_ref[h])
            acc = acc + jnp.maximum(lg, 0.0) * iwt[h:h + 1, :]
        kpos = k0 + sub
        sc = jnp.where(kpos <= qpos, acc, NEG)
        sc = jnp.where(kpos < FRAME_OFF, -jnp.inf, sc)
        key = _sortable(sc)
        key_ref[pl.ds(k0, ts), :] = key
        half_ref[pl.ds(k0, ts), :] = lax.shift_right_arithmetic(key, 16).astype(i16)

    spt = 3

    def score_trip(kp, carry):
        for b in range(spt):
            score_blk(spt * kp + b)
        return carry

    def score_rest(kb, carry):
        score_blk(kb)
        return carry

    lax.fori_loop(0, nkb // spt, score_trip, 0)
    lax.fori_loop(nkb - nkb % spt, nkb, score_rest, 0)

    n_acc = 4

    def count(pred_fn):
        sub8 = lax.broadcasted_iota(i32, (8, 1), 0)

        def blk(kb, cnts):
            k0 = pl.multiple_of(kb * ts, ts)
            cnts = list(cnts)
            blk_ref = key_ref.at[pl.ds(k0, ts)]
            for j in range(ts // 8):
                kv = blk_ref[j * 8:(j + 1) * 8, :]
                cnts[j % n_acc] = cnts[j % n_acc] + jnp.where(pred_fn(kv, k0 + j * 8 + sub8), 1, 0)
            return tuple(cnts)
        cnts = lax.fori_loop(0, nkb, blk, tuple(jnp.zeros((8, tq), i32) for _ in range(n_acc)))
        return jnp.sum(sum(cnts[1:], cnts[0]), axis=0, keepdims=True)

    def count_half_ge(cand):
        one, zero = jnp.int16(1), jnp.int16(0)

        def blk(kb, cnts):
            k0 = pl.multiple_of(kb * ts, ts)
            cnts = list(cnts)
            blk_ref = half_ref.at[pl.ds(k0, ts)]
            for j in range(ts // BF16_SUBLANES):
                hv = blk_ref[j * BF16_SUBLANES:(j + 1) * BF16_SUBLANES, :]
                cnts[j % n_acc] = cnts[j % n_acc] + jnp.where(hv >= cand, one, zero)
            return tuple(cnts)

        def pair(kp, cnts):
            return blk(2 * kp + 1, blk(2 * kp, cnts))

        cnts = tuple(jnp.zeros((BF16_SUBLANES, tq), i16) for _ in range(n_acc))
        cnts = lax.fori_loop(0, nkb // 2, pair, cnts)
        cnts = lax.fori_loop(nkb - nkb % 2, nkb, blk, cnts)
        return jnp.sum(sum(cnts[1:], cnts[0]).astype(i32), axis=0, keepdims=True)

    def search_half(neg_half, n_start):
        def body(it, c):
            tau, n_tau = c
            t2 = tau | lax.shift_left(jnp.int32(1), 15 - it)
            cand = t2 + HALF_MIN
            n2 = count_half_ge(cand.astype(i16)) + jnp.where(cand <= neg_half, rem, 0)
            ok = n2 >= topk
            return jnp.where(ok, t2, tau), jnp.where(ok, n2, n_tau)

        return lax.fori_loop(0, 16, body, (jnp.zeros((1, tq), i32), n_start))

    tau_hi, n_hi = search_half(jnp.full((1, tq), _KEY_NEG_HI, i32), jnp.full((1, tq), lp, i32))
    t_hi = tau_hi + HALF_MIN

    base = lax.shift_left(t_hi, 16)

    def low_blk(kb, carry):
        k0 = pl.multiple_of(kb * ts, ts)
        key = key_ref[pl.ds(k0, ts), :]
        lo = jnp.minimum(jnp.maximum(key, base), base + (HALF_SPAN - 1)) - (base - HALF_MIN)
        half_ref[pl.ds(k0, ts), :] = lo.astype(i16)
        return carry

    lax.fori_loop(0, nkb, low_blk, 0)
    neg_lo = jnp.where(_KEY_NEG_HI > t_hi, HALF_MAX, jnp.where(_KEY_NEG_HI == t_hi, _KEY_NEG_LO, HALF_MIN))
    tau_lo, n_ge = search_half(neg_lo, n_hi)
    thr = lax.shift_left(t_hi, 16) | tau_lo
    has_tie = jnp.max(jnp.where(n_ge > topk, 1, 0)) > 0

    def tie_cut():
        n_gt = (count(lambda kv, kpos: kv > thr) + jnp.where(thr < _KEY_NEG, rem, 0))
        need = topk - n_gt

        def bis2(it, cut):
            bit = lax.shift_left(jnp.int32(1), 14 - it)
            c2 = cut | bit
            n = count(lambda kv, kpos: jnp.logical_and(kv == thr, kpos < c2))
            return jnp.where(n < need, c2, cut)
        return lax.fori_loop(0, 15, bis2, jnp.zeros((1, tq), i32))

    cut = lax.cond(has_tie, tie_cut, lambda: jnp.full((1, tq), INT_MAX, i32))

    def bias_blk(kb, carry):
        k0 = pl.multiple_of(kb * ts, ts)
        kv = key_ref[pl.ds(k0, ts), :]
        interior = jnp.logical_and(k0 >= FRAME_OFF, k0 + ts - 1 <= i * tq)

        @pl.when(jnp.logical_and(interior, jnp.logical_not(has_tie)))
        def _():
            bias_ref[pl.ds(k0, ts), :] = jnp.where(kv >= thr, 0.0, NEG).astype(bias_ref.dtype)

        @pl.when(jnp.logical_not(jnp.logical_and(interior, jnp.logical_not(has_tie))))
        def _():
            kpos = k0 + sub
            sel = jnp.logical_or(kv > thr, jnp.logical_and(kv == thr, kpos <= cut))
            vis = jnp.logical_and(kpos <= qpos, kpos >= FRAME_OFF)
            bias_ref[pl.ds(k0, ts), :] = jnp.where(jnp.logical_and(sel, vis), 0.0, NEG).astype(bias_ref.dtype)
        return carry

    lax.fori_loop(0, nkb, bias_blk, 0)

    def fill_blk(kb, carry):
        k0 = pl.multiple_of(kb * ts, ts)
        bias_ref[pl.ds(k0, ts), :] = jnp.full((ts, tq), NEG, bias_ref.dtype)
        return carry

    lax.fori_loop(nkb, lp // ts, fill_blk, 0)


def dsa_select(iq, ik, iwt, topk):
    nh, lp, _ = iq.shape
    assert lp < 2 ** 15
    tq = 256
    ts = _pick(lp, (640, 256, 128))
    kern = functools.partial(_select_kernel, tq=tq, ts=ts, lp=lp, topk=topk)
    return pl.pallas_call(
        kern,
        grid=(lp // tq,),
        in_specs=[pl.BlockSpec((nh, tq, HEAD_DIM), lambda i: (0, i, 0)),
                  pl.BlockSpec((lp, HEAD_DIM), lambda i: (0, 0), pipeline_mode=pl.Buffered(1)),
                  pl.BlockSpec((nh, tq), lambda i: (0, i))],
        out_specs=pl.BlockSpec((lp, tq), lambda i: (0, i)),
        out_shape=jax.ShapeDtypeStruct((lp, lp), bf16),
        scratch_shapes=[pltpu.VMEM((lp, tq), i32), pltpu.VMEM((lp, tq), i16)],
        compiler_params=_cparams(("parallel",)),
        name="dsa_select",
    )(iq, ik, iwt)


V_AUG = HEAD_DIM + BF16_SUBLANES


def _attn_kernel(q_ref, k_ref, vt_ref, b_ref, o_ref, m_ref, acc_ref, *, tq, ts, bpt):
    i = pl.program_id(0)
    nkb = ((i + 1) * tq + ts - 1) // ts
    group = ATT_HEADS // ATT_KV_HEADS
    kv_groups = range(ATT_KV_HEADS)
    qpos = i * tq + lax.broadcasted_iota(i32, (tq, 1), 0)
    eye = (lax.broadcasted_iota(i32, (tq, tq), 0)
           == lax.broadcasted_iota(i32, (tq, tq), 1)).astype(bf16)
    qa = [jnp.concatenate(
        [jnp.concatenate([q_ref[:, (g * group + r) * HEAD_DIM:(g * group + r + 1) * HEAD_DIM], eye],
                         axis=1) for r in range(group)], axis=0) for g in kv_groups]
    m_ref[...] = jnp.full(m_ref.shape, -jnp.inf, f32)
    acc_ref[...] = jnp.zeros_like(acc_ref)

    def scores(g, k0):
        ka = jnp.concatenate([k_ref[pl.ds(k0, ts), g * HEAD_DIM:(g + 1) * HEAD_DIM],
                              b_ref[pl.ds(k0, ts), :]], axis=1)
        return _dot_nt(ka, qa[g]).astype(bf16)

    def accumulate(g, k0, st):
        vt_ = vt_ref[g, :, pl.ds(k0, ts)]
        ps, alphas = [], []
        for r in range(group):
            sr = st[:, r * tq:(r + 1) * tq]
            m_old = m_ref[g, r:r + 1, :]
            m_new = jnp.maximum(m_old, jnp.max(sr, axis=0, keepdims=True).astype(f32))
            m_ref[g, r:r + 1, :] = m_new
            ps.append(jnp.exp2(sr - m_new.astype(bf16)))
            alphas.append(jnp.exp2(m_old - m_new))
        pt = jnp.concatenate(ps, axis=1)
        acc_ref[g] = jnp.concatenate(alphas, axis=1) * acc_ref[g] + _dot(vt_, pt)

    def make_trip(nblk, first_block):
        def trip(kp, carry):
            k0s = [pl.multiple_of((first_block + kp * nblk + b) * ts, ts) for b in range(nblk)]
            work = [(g, kk) for kk in k0s for g in kv_groups]
            st = scores(*work[0])
            for n in range(len(work)):
                st_next = scores(*work[n + 1]) if n + 1 < len(work) else None
                accumulate(*work[n], st)
                st = st_next
            return carry
        return trip

    n_full = nkb // bpt
    lax.fori_loop(0, n_full, make_trip(bpt, 0), 0)
    if bpt > 1:
        lax.fori_loop(0, nkb - n_full * bpt, make_trip(1, n_full * bpt), 0)
    for g in kv_groups:
        for r in range(group):
            ot = (acc_ref[g, 0:HEAD_DIM, r * tq:(r + 1) * tq]
                  / acc_ref[g, HEAD_DIM:HEAD_DIM + 1, r * tq:(r + 1) * tq])
            orr = jnp.where(qpos >= FRAME_OFF, ot.T, 0.0)
            o_ref[:, (g * group + r) * HEAD_DIM:(g * group + r + 1) * HEAD_DIM] = orr.astype(o_ref.dtype)


def dsa_attention(q, k, vt_aug, bias_t):
    lp = q.shape[0]
    tq = 128
    ts = _pick(lp, (1280, 640, 256, 128))
    bpt = 2
    group = ATT_HEADS // ATT_KV_HEADS
    kern = functools.partial(_attn_kernel, tq=tq, ts=ts, bpt=bpt)
    resident = pl.Buffered(1)
    return pl.pallas_call(
        kern,
        grid=(lp // tq,),
        in_specs=[pl.BlockSpec((tq, ATT_Q_W), lambda i: (i, 0)),
                  pl.BlockSpec((lp, ATT_KV_W), lambda i: (0, 0), pipeline_mode=resident),
                  pl.BlockSpec((ATT_KV_HEADS, V_AUG, lp), lambda i: (0, 0, 0), pipeline_mode=resident),
                  pl.BlockSpec((lp, tq), lambda i: (0, i))],
        out_specs=pl.BlockSpec((tq, ATT_Q_W), lambda i: (i, 0)),
        out_shape=jax.ShapeDtypeStruct((lp, ATT_Q_W), bf16),
        scratch_shapes=[pltpu.VMEM((ATT_KV_HEADS, group, tq), f32),
                        pltpu.VMEM((ATT_KV_HEADS, V_AUG, group * tq), f32)],
        compiler_params=_cparams(("parallel",)),
        name="dsa_attention",
    )(q, k, vt_aug, bias_t)


def _ffn_up_kernel(a_ref, ah_ref, wg_ref, wv_ref, cg_ref, cv_ref, bg_ref, bv_ref, o_ref,
                   eg_ref, ev_ref, *, tm):
    a = _halo_rows(a_ref, ah_ref, pl.program_id(0))
    eg_ref[...] = _dot(a, wg_ref[...])
    ev_ref[...] = _dot(a, wv_ref[...])

    def conv(ext_ref, w_ref, b_ref):
        w = w_ref[...]
        y = ext_ref[HALO:HALO + tm, :] * w[FFN_CONV - 1:FFN_CONV, :]
        for j in range(FFN_CONV - 1):
            s0 = HALO - (FFN_CONV - 1) + j
            y = y + ext_ref[s0:s0 + tm, :] * w[j:j + 1, :]
        return y + b_ref[...]

    gate = conv(eg_ref, cg_ref, bg_ref)
    val = conv(ev_ref, cv_ref, bv_ref)
    o_ref[...] = (gate * _sigmoid(gate) * val).astype(o_ref.dtype)


def ffn_up(u, w_up, conv_w, conv_b):
    lp, kdim = u.shape
    tm = _pick(lp, (1280, 640, 256, 128))
    tn = 512
    nc = D_FF // tn
    kern = functools.partial(_ffn_up_kernel, tm=tm)

    def wspec(off):
        return pl.BlockSpec((kdim, tn), lambda i, c: (0, c + off))

    def cspec(off):
        return pl.BlockSpec((FFN_CONV, tn), lambda i, c: (0, c + off))

    def bspec(off):
        return pl.BlockSpec((1, tn), lambda i, c: (0, c + off))

    b2 = conv_b.reshape(1, 2 * D_FF)
    return pl.pallas_call(
        kern,
        grid=(lp // tm, nc),
        in_specs=[pl.BlockSpec((tm, kdim), lambda i, c: (i, 0)),
                  pl.BlockSpec((HALO, kdim), lambda i, c: (jnp.maximum(i * (tm // HALO) - 1, 0), 0)),
                  wspec(0), wspec(nc), cspec(0), cspec(nc), bspec(0), bspec(nc)],
        out_specs=pl.BlockSpec((tm, tn), lambda i, c: (i, c)),
        out_shape=jax.ShapeDtypeStruct((lp, D_FF), bf16),
        scratch_shapes=[pltpu.VMEM((HALO + tm, tn), f32)] * 2,
        compiler_params=_cparams(("parallel", "arbitrary")),
        name="ffn_up",
    )(u, u, w_up, w_up, conv_w, conv_w, b2, b2)


def _split_w_in(w):
    wt = w.T
    o = 0
    parts = {}
    for name, width in (("gq", GDN_QK_W), ("gk", GDN_QK_W), ("gv", GDN_V_W), ("gz", GDN_V_W),
                        ("gb", GDN_V_HEADS), ("ga", GDN_V_HEADS), ("aq", ATT_Q_W), ("ak", ATT_KV_W),
                        ("av", ATT_KV_W), ("iq", IDX_Q_W), ("ik", HEAD_DIM), ("iw", IDX_HEADS),
                        ("gate_gdn", D_MODEL), ("gate_att", D_MODEL)):
        parts[name] = wt[o:o + width]
        o += width
    cat = lambda names: jnp.concatenate([parts[n] for n in names], axis=0)
    small = cat(("gb", "ga", "iw"))
    small = jnp.pad(small, ((0, LANES - small.shape[0]), (0, 0)))
    groups = dict(qkv=cat(("gq", "gk", "gv")), aq=parts["aq"], iq=parts["iq"], kv=cat(("ak", "ik")),
                  plain=cat(("gz", "gate_gdn", "gate_att", "av")), small=small)
    return {k: v.astype(bf16) for k, v in groups.items()}


def _layer(h0, p, lp, topk, last):
    tm = _pick(lp, (640, 256, 128))
    tm_big = _pick(lp, (1280, 640, 256, 128))
    wp = _split_w_in(p["w_in"])

    def proj(w, **kw):
        return matmul(u1, w, tm=tm_big, tk=D_MODEL, w_nt=True, **kw)

    u1 = rms_rows(h0, p["mix_pre_g"])
    tabs = rope_tables(lp)
    tab_specs = (((tm_big, HEAD_DIM), lambda i, j: (i, 0)),) * 3
    qkv = gdn_proj(u1, wp["qkv"], p["gdn_conv_w"])
    (aq,) = proj(wp["aq"], tn=1024, epilogue=_make_ep_rope(HEAD_DIM ** -0.5 * LOG2E, False),
                 extra=tabs, extra_specs=tab_specs,
                 out_shapes=[jax.ShapeDtypeStruct((lp, ATT_Q_W), bf16)], name="proj_aq")
    iq_heads = 1024 // HEAD_DIM
    (iq,) = proj(wp["iq"], tn=1024, epilogue=_make_ep_rope(1.0, True), extra=tabs, extra_specs=tab_specs,
                 out_shapes=[jax.ShapeDtypeStruct((IDX_HEADS, lp, HEAD_DIM), bf16)],
                 out_block_specs=[((iq_heads, tm_big, HEAD_DIM), lambda i, j: (j, i, 0))], name="proj_iq")
    ak, ik = proj(wp["kv"], tn=ATT_KV_W + HEAD_DIM, epilogue=_ep_rope_kv, extra=tabs, extra_specs=tab_specs,
                  out_shapes=[jax.ShapeDtypeStruct((lp, ATT_KV_W), bf16),
                              jax.ShapeDtypeStruct((lp, HEAD_DIM), bf16)],
                  out_block_specs=[((tm_big, ATT_KV_W), lambda i, j: (i, 0)),
                                   ((tm_big, HEAD_DIM), lambda i, j: (i, 0))], name="proj_kv")
    (plain,) = proj(wp["plain"], tn=768, epilogue=_ep_cast,
                    out_shapes=[jax.ShapeDtypeStruct((lp, PLAIN_W), bf16)], name="proj_plain")
    (small,) = proj(wp["small"], tn=LANES, epilogue=_ep_cast,
                    out_shapes=[jax.ShapeDtypeStruct((lp, LANES), f32)], name="proj_small")
    gb = small[:, 0:GDN_V_HEADS]
    ga = small[:, GDN_V_HEADS:2 * GDN_V_HEADS]
    iw = small[:, 2 * GDN_V_HEADS:2 * GDN_V_HEADS + IDX_HEADS]

    beta, gcum = gdn_gates(gb, ga, p["gdn_a_log"], p["gdn_dt_bias"])
    o_gdn = gdn_core(qkv, plain, p["gdn_norm_g"], gcum.T[:, None, :], beta.T[:, None, :])

    avt = plain[:, COL_AV:COL_AV + ATT_KV_W].T.reshape(ATT_KV_HEADS, HEAD_DIM, lp)
    avt = jnp.concatenate([avt, jnp.ones((ATT_KV_HEADS, V_AUG - HEAD_DIM, lp), bf16)], axis=1)
    bias_t = dsa_select(iq, ik, iw.T, topk)
    o_att = dsa_attention(aq, ak, avt, bias_t)

    tn = 1024
    gate_spec = lambda col0: ((tm_big, tn), (lambda i, j, c=col0 // tn: (i, c + j)))
    (m1,) = matmul(o_gdn, p["w_branch_gdn"].astype(bf16), tm=tm_big, tn=tn, tk=2048,
                   epilogue=_ep_gate, extra=(plain,), extra_specs=(gate_spec(COL_GATE_GDN),),
                   out_shapes=[jax.ShapeDtypeStruct((lp, D_MODEL), f32)], name="branch_gdn")
    (merged,) = matmul(o_att, p["w_branch_att"].astype(bf16), tm=tm_big, tn=tn, tk=2048,
                       epilogue=_ep_gate_add, extra=(plain, m1),
                       extra_specs=(gate_spec(COL_GATE_ATT), ((tm_big, tn), lambda i, j: (i, j))),
                       out_shapes=[jax.ShapeDtypeStruct((lp, D_MODEL), bf16)], name="branch_att")
    row_spec = ((tm, D_MODEL), lambda i, j: (i, 0))
    vec_spec = ((1, D_MODEL), lambda i, j: (0, 0))
    h1, u2 = matmul(merged, p["w_out"].astype(bf16), tm=tm, tn=D_MODEL, tk=D_MODEL,
                    epilogue=_ep_res_norm2,
                    extra=(h0, p["mix_post_g"].reshape(1, D_MODEL), p["ffn_pre_g"].reshape(1, D_MODEL)),
                    extra_specs=(row_spec, vec_spec, vec_spec),
                    out_shapes=[jax.ShapeDtypeStruct((lp, D_MODEL), f32),
                                jax.ShapeDtypeStruct((lp, D_MODEL), bf16)], name="w_out")

    act = ffn_up(u2, p["w_up"].astype(bf16), p["ffn_conv_w"], p["ffn_conv_b"])
    if not last:
        (h2,) = matmul(act, p["w_down"].astype(bf16), tm=tm, tn=D_MODEL, tk=2048, epilogue=_ep_res_norm,
                       extra=(h1, p["ffn_post_g"].reshape(1, D_MODEL)),
                       extra_specs=(row_spec, vec_spec),
                       out_shapes=[jax.ShapeDtypeStruct((lp, D_MODEL), f32)], name="w_down")
        return h2
    tmx = FRAME_X0
    (out,) = matmul(act, p["w_down"].astype(bf16), tm=tmx, tn=D_MODEL, tk=D_FF, epilogue=_ep_res_norm,
                    extra=(h1, p["ffn_post_g"].reshape(1, D_MODEL)),
                    extra_specs=(((tmx, D_MODEL), lambda i, j: (i, 0)), vec_spec),
                    out_shapes=[jax.ShapeDtypeStruct((lp - FRAME_X0, D_MODEL), f32)],
                    out_block_specs=[((tmx, D_MODEL), lambda i, j: (jnp.maximum(i - 1, 0), 0))],
                    rows_sem="arbitrary", name="w_down")
    return out


def kernel(x, meta_tokens, mix_pre_g, w_in, gdn_conv_w, gdn_a_log, gdn_dt_bias, gdn_norm_g,
           w_branch_gdn, w_branch_att, w_out, mix_post_g, ffn_pre_g, w_up, ffn_conv_w,
           ffn_conv_b, w_down, ffn_post_g):
    batch, seq, d = x.shape
    assert batch == 1 and d == D_MODEL
    lp = FRAME_X0 + seq
    topk = min(TOPK_MAX, (N_META + seq) // 4)
    h = jnp.concatenate([jnp.zeros((FRAME_OFF, d), x.dtype), meta_tokens.astype(x.dtype), x[0]], axis=0)
    depth = w_in.shape[0]
    for i in range(depth):
        p = dict(mix_pre_g=mix_pre_g[i], w_in=w_in[i], gdn_conv_w=gdn_conv_w[i], gdn_a_log=gdn_a_log[i],
                 gdn_dt_bias=gdn_dt_bias[i], gdn_norm_g=gdn_norm_g[i], w_branch_gdn=w_branch_gdn[i],
                 w_branch_att=w_branch_att[i], w_out=w_out[i], mix_post_g=mix_post_g[i],
                 ffn_pre_g=ffn_pre_g[i], w_up=w_up[i], ffn_conv_w=ffn_conv_w[i],
                 ffn_conv_b=ffn_conv_b[i], w_down=w_down[i], ffn_post_g=ffn_post_g[i])
        h = _layer(h, p, lp, topk, last=(i == depth - 1))
    return h[None]
```

```python
import functools
import struct

import jax
import jax.numpy as jnp
from jax import lax
from jax.experimental import pallas as pl
from jax.experimental.pallas import tpu as pltpu

f32 = jnp.float32
bf16 = jnp.bfloat16
i32 = jnp.int32
i16 = jnp.int16

D_MODEL = 2048
N_META = 16
EPS = 1e-6
GDN_QK_HEADS = 16
GDN_V_HEADS = 32
HEAD_DIM = 128
GDN_CONV = 4
ATT_HEADS = 16
ATT_KV_HEADS = 2
IDX_HEADS = 16
TOPK_MAX = 256
NEG = -1e30
LOG2E = 1.4426950408889634
ROPE_THETA = 500000.0
ROPE_DIMS = HEAD_DIM // 4
D_FF = 3 * D_MODEL
FFN_CONV = 3
GDN_QK_W = GDN_QK_HEADS * HEAD_DIM
GDN_V_W = GDN_V_HEADS * HEAD_DIM
ATT_Q_W = ATT_HEADS * HEAD_DIM
ATT_KV_W = ATT_KV_HEADS * HEAD_DIM
IDX_Q_W = IDX_HEADS * HEAD_DIM

FRAME_X0 = 256
FRAME_OFF = FRAME_X0 - N_META
CHUNK = 128
GDN_QK_PER_STEP = 2

LANES = 128
BF16_SUBLANES = 16
VMEM_LIMIT = 56 * 1024 * 1024

COL_GZ = 0
COL_GATE_GDN = COL_GZ + GDN_V_W
COL_GATE_ATT = COL_GATE_GDN + D_MODEL
COL_AV = COL_GATE_ATT + D_MODEL
PLAIN_W = COL_AV + ATT_KV_W


def _cparams(sem):
    return pltpu.CompilerParams(dimension_semantics=sem, vmem_limit_bytes=VMEM_LIMIT)


def _pick(n, cands):
    for c in cands:
        if n % c == 0:
            return c
    raise ValueError(f"no tile for {n} in {cands}")


def _sigmoid(x):
    return 1.0 / (1.0 + jnp.exp(-x))


def _dot(a, b):
    return jnp.dot(a, b, preferred_element_type=f32)


def _dot_nt(a, b):
    return lax.dot_general(a, b, (((1,), (1,)), ((), ())), preferred_element_type=f32)


def _rms_rows_kernel(h_ref, g_ref, o_ref):
    h = h_ref[...]
    y = h * lax.rsqrt(jnp.mean(h * h, axis=-1, keepdims=True) + EPS)
    o_ref[...] = (y * g_ref[...]).astype(o_ref.dtype)


def rms_rows(h, g):
    lp, d = h.shape
    tm = _pick(lp, (640, 256, 128))
    return pl.pallas_call(
        _rms_rows_kernel,
        grid=(lp // tm,),
        in_specs=[pl.BlockSpec((tm, d), lambda i: (i, 0)),
                  pl.BlockSpec((1, d), lambda i: (0, 0))],
        out_specs=pl.BlockSpec((tm, d), lambda i: (i, 0)),
        out_shape=jax.ShapeDtypeStruct((lp, d), bf16),
        compiler_params=_cparams(("parallel",)),
        name="rms_rows",
    )(h, g.reshape(1, d))


def _mm_kernel(*refs, nk, n_extra, n_out, epilogue, w_nt):
    a_ref, w_ref = refs[0], refs[1]
    dot = _dot_nt if w_nt else _dot
    extra = refs[2:2 + n_extra]
    outs = refs[2 + n_extra:2 + n_extra + n_out]
    if nk == 1:
        epilogue(dot(a_ref[...], w_ref[...]), extra, outs)
        return
    acc_ref = refs[-1]
    k = pl.program_id(2)

    @pl.when(k == 0)
    def _():
        acc_ref[...] = jnp.zeros_like(acc_ref)

    acc_ref[...] += dot(a_ref[...], w_ref[...])

    @pl.when(k == nk - 1)
    def _():
        epilogue(acc_ref[...], extra, outs)


def matmul(a, w, *, tm, tn, tk, epilogue, extra=(), extra_specs=(), out_shapes, out_block_specs=None,
           w_nt=False, rows_sem="parallel", name):
    m = a.shape[0]
    kdim, n = w.shape[::-1] if w_nt else w.shape
    nk = kdim // tk
    assert m % tm == 0 and n % tn == 0 and kdim % tk == 0
    w_mode = dict(pipeline_mode=pl.Buffered(1)) if (nk == 1 and n == tn) else {}
    in_specs = [pl.BlockSpec((tm, tk), lambda i, j, k: (i, k)),
                (pl.BlockSpec((tn, tk), lambda i, j, k: (j, k), **w_mode) if w_nt else
                 pl.BlockSpec((tk, tn), lambda i, j, k: (k, j), **w_mode))]
    in_specs += [pl.BlockSpec(bs, (lambda i, j, k, f=f: f(i, j))) for bs, f in extra_specs]
    if out_block_specs is None:
        out_block_specs = [((tm, tn), lambda i, j: (i, j))] * len(out_shapes)
    out_specs = [pl.BlockSpec(bs, (lambda i, j, k, f=f: f(i, j))) for bs, f in out_block_specs]
    scratch = [] if nk == 1 else [pltpu.VMEM((tm, tn), f32)]
    kern = functools.partial(_mm_kernel, nk=nk, n_extra=len(extra), n_out=len(out_shapes),
                             epilogue=epilogue, w_nt=w_nt)
    res = pl.pallas_call(
        kern,
        grid=(m // tm, n // tn, nk),
        in_specs=in_specs,
        out_specs=out_specs,
        out_shape=out_shapes,
        scratch_shapes=scratch,
        compiler_params=_cparams((rows_sem, rows_sem, "arbitrary")),
        name=name,
    )(a, w, *extra)
    return res


def _ep_cast(acc, extra, outs):
    outs[0][...] = acc.astype(outs[0].dtype)


def _ep_gate(acc, extra, outs):
    g = extra[0][...].astype(f32)
    outs[0][...] = (_sigmoid(g) * acc).astype(outs[0].dtype)


def _ep_gate_add(acc, extra, outs):
    g = extra[0][...].astype(f32)
    outs[0][...] = (extra[1][...].astype(f32) + _sigmoid(g) * acc).astype(outs[0].dtype)


def _rms(t, g):
    return t * lax.rsqrt(jnp.mean(t * t, axis=-1, keepdims=True) + EPS) * g


def _ep_res_norm2(acc, extra, outs):
    h_ref, g_ref, g2_ref = extra
    h1 = h_ref[...] + _rms(acc, g_ref[...])
    outs[0][...] = h1
    outs[1][...] = _rms(h1, g2_ref[...]).astype(outs[1].dtype)


def _ep_res_norm(acc, extra, outs):
    h_ref, g_ref = extra
    outs[0][...] = h_ref[...] + _rms(acc, g_ref[...])


HALO = BF16_SUBLANES


def _halo_rows(a_ref, ah_ref, i):
    halo = ah_ref[...]
    halo = jnp.where(i > 0, halo, jnp.zeros_like(halo))
    return jnp.concatenate([halo, a_ref[...]], axis=0)


def _gdn_proj_kernel(a_ref, ah_ref, w_ref, cw_ref, o_ref, ext_ref, *, tm, tn):
    c = pl.program_id(1)
    ext_ref[...] = _dot_nt(_halo_rows(a_ref, ah_ref, pl.program_id(0)), w_ref[...])
    w = cw_ref[...]
    y = ext_ref[HALO:HALO + tm, :] * w[GDN_CONV - 1:GDN_CONV, :]
    for j in range(GDN_CONV - 1):
        s0 = HALO - (GDN_CONV - 1) + j
        y = y + ext_ref[s0:s0 + tm, :] * w[j:j + 1, :]
    s = y * _sigmoid(y)
    is_q = c < (GDN_QK_W // tn)
    is_qk = c < (2 * GDN_QK_W // tn)
    qscale = jnp.where(is_q, HEAD_DIM ** -0.5, 1.0).astype(f32)
    for hh in range(tn // HEAD_DIM):
        seg = s[:, hh * HEAD_DIM:(hh + 1) * HEAD_DIM]
        r = lax.rsqrt(jnp.sum(seg * seg, axis=-1, keepdims=True) + EPS) * qscale
        fac = jnp.where(is_qk, r, 1.0)
        o_ref[:, hh * HEAD_DIM:(hh + 1) * HEAD_DIM] = (seg * fac).astype(o_ref.dtype)


def gdn_proj(u, w_qkv, conv_w):
    lp, kdim = u.shape
    width = w_qkv.shape[0]
    tm = _pick(lp, (1280, 640, 256, 128))
    tn = 1024
    kern = functools.partial(_gdn_proj_kernel, tm=tm, tn=tn)
    return pl.pallas_call(
        kern,
        grid=(lp // tm, width // tn),
        in_specs=[pl.BlockSpec((tm, kdim), lambda i, c: (i, 0)),
                  pl.BlockSpec((HALO, kdim), lambda i, c: (jnp.maximum(i * (tm // HALO) - 1, 0), 0)),
                  pl.BlockSpec((tn, kdim), lambda i, c: (c, 0)),
                  pl.BlockSpec((GDN_CONV, tn), lambda i, c: (0, c))],
        out_specs=pl.BlockSpec((tm, tn), lambda i, c: (i, c)),
        out_shape=jax.ShapeDtypeStruct((lp, width), bf16),
        scratch_shapes=[pltpu.VMEM((HALO + tm, tn), f32)],
        compiler_params=_cparams(("parallel", "arbitrary")),
        name="gdn_proj",
    )(u, u, w_qkv, conv_w)


def _gdn_gates_kernel(b_ref, a_ref, alog_ref, dt_ref, beta_ref, gcum_ref, *, tm):
    i = pl.program_id(0)
    rows = i * tm + lax.broadcasted_iota(i32, (tm, 1), 0)
    valid = rows >= FRAME_OFF
    beta_ref[...] = jnp.where(valid, _sigmoid(b_ref[...]), 0.0)
    a = a_ref[...] + dt_ref[...]
    sp = jnp.maximum(a, 0.0) + jnp.log1p(jnp.exp(-jnp.abs(a)))
    g = jnp.where(valid, -jnp.exp(alog_ref[...]) * sp, 0.0)
    tri = (lax.broadcasted_iota(i32, (CHUNK, CHUNK), 0)
           >= lax.broadcasted_iota(i32, (CHUNK, CHUNK), 1)).astype(f32)
    for c in range(tm // CHUNK):
        rs = slice(c * CHUNK, (c + 1) * CHUNK)
        gcum_ref[rs, :] = jnp.dot(tri, g[rs, :], preferred_element_type=f32,
                                  precision=lax.Precision.HIGHEST)


def gdn_gates(gb, ga, a_log, dt_bias):
    lp, nh = gb.shape
    tm = _pick(lp, (1280, 640, 256, 128))
    spec = pl.BlockSpec((tm, nh), lambda i: (i, 0))
    vec = pl.BlockSpec((1, nh), lambda i: (0, 0))
    return pl.pallas_call(
        functools.partial(_gdn_gates_kernel, tm=tm),
        grid=(lp // tm,),
        in_specs=[spec, spec, vec, vec],
        out_specs=[spec, spec],
        out_shape=[jax.ShapeDtypeStruct((lp, nh), f32)] * 2,
        compiler_params=_cparams(("parallel",)),
        name="gdn_gates",
    )(gb, ga, a_log.reshape(1, nh), dt_bias.reshape(1, nh))


def _block_mask(size):
    r = lax.broadcasted_iota(i32, (CHUNK, CHUNK), 0) // size
    c = lax.broadcasted_iota(i32, (CHUNK, CHUNK), 1) // size
    return r == c


def _unit_lower_inverse_many(mats, tick):
    row = lax.broadcasted_iota(i32, (CHUNK, CHUNK), 0)
    col = lax.broadcasted_iota(i32, (CHUNK, CHUNK), 1)
    eye = (row == col).astype(f32)
    base = 8
    m_prev = _block_mask(base)
    ads = [jnp.where(m_prev, a, 0.0) for a in mats]
    adbs = [ad.astype(bf16) for ad in ads]
    a2s = [_dot(x, x) for x in adbs]
    tick()
    a2bs = [x.astype(bf16) for x in a2s]
    a4s = [_dot(x, x) for x in a2bs]
    tick()
    xs = [_dot((eye - ad).astype(bf16), (eye + a2).astype(bf16)) for ad, a2 in zip(ads, a2s)]
    tick()
    xs = [_dot(x.astype(bf16), (eye + a4).astype(bf16)) for x, a4 in zip(xs, a4s)]
    tick()
    size = base * 2
    while size <= CHUNK:
        m_cur = _block_mask(size)
        off_diag = jnp.logical_and(m_cur, jnp.logical_not(m_prev))
        bs = [jnp.where(off_diag, a, 0.0).astype(bf16) for a in mats]
        xbs = [x.astype(bf16) for x in xs]
        ys = [_dot(xb, b).astype(bf16) for xb, b in zip(xbs, bs)]
        tick()
        xs = [x - _dot(y, xb) for x, y, xb in zip(xs, ys, xbs)]
        tick()
        m_prev = m_cur
        size *= 2
    return xs


def _gdn_kernel(q_ref, k_ref, v_ref, z_ref, ng_ref, *rest, rb, nq):
    r = pl.program_id(1)
    nv = 2 * nq
    grs, brs = rest[:nv], rest[nv:2 * nv]
    o_ref, s_ref, u_ref, wq_ref, ak_ref, eg_ref, egl_ref = rest[2 * nv:]

    @pl.when(r == 0)
    def _():
        for ref in (s_ref, u_ref, wq_ref, ak_ref, eg_ref, egl_ref):
            ref[...] = jnp.zeros_like(ref)

    row = lax.broadcasted_iota(i32, (CHUNK, CHUNK), 0)
    col = lax.broadcasted_iota(i32, (CHUNK, CHUNK), 1)
    incl = row >= col
    strict = row > col
    ng = ng_ref[...]
    nchunk = rb // CHUNK
    heads = range(nv)

    def rows(c):
        return slice(c * CHUNK, (c + 1) * CHUNK)

    def lanes(e):
        return slice(e * HEAD_DIM, (e + 1) * HEAD_DIM)

    chains = [(c, e) for c in range(nchunk) for e in heads]

    def recurrence():
        state = [s_ref[e] for e in heads]
        for c in range(nchunk):
            n = [chains.index((c, e)) for e in heads]
            sb = [state[e].astype(bf16) for e in heads]
            ws = [_dot(wq_ref[n[e]], sb[e]) for e in heads]
            yield
            vb = [(u_ref[n[e]] - ws[e][:CHUNK]).astype(bf16) for e in heads]
            av = [_dot(ak_ref[n[e]], vb[e]) for e in heads]
            yield
            for e in heads:
                o = eg_ref[n[e]] * ws[e][CHUNK:] + av[e][:CHUNK]
                state[e] = state[e] * egl_ref[n[e]] + av[e][CHUNK:]
                z = z_ref[rows(c), lanes(e)].astype(f32)
                o_ref[rows(c), lanes(e)] = (_rms(o, ng) * (z * _sigmoid(z))).astype(o_ref.dtype)
        for e in heads:
            s_ref[e] = state[e]

    rec = recurrence()

    def tick():
        next(rec, None)

    qkh = [(c, a) for c in range(nchunk) for a in range(nq)]
    qs = {ca: q_ref[rows(ca[0]), lanes(ca[1])] for ca in qkh}
    ks = {ca: k_ref[rows(ca[0]), lanes(ca[1])] for ca in qkh}
    kts = {ca: ks[ca].astype(f32).T for ca in qkh}
    kks = {ca: _dot_nt(ks[ca], ks[ca]) for ca in qkh}
    qks = {ca: _dot_nt(qs[ca], ks[ca]) for ca in qkh}
    qk_of = {ce: (ce[0], ce[1] // 2) for ce in chains}
    tick()
    def per_row(v):
        return jnp.broadcast_to(v, (CHUNK, CHUNK)).T

    gr = {ce: grs[ce[1]][:, rows(ce[0])] for ce in chains}
    br = {ce: brs[ce[1]][:, rows(ce[0])] for ce in chains}
    gc = {ce: per_row(gr[ce]) for ce in chains}
    bc = {ce: per_row(br[ce]) for ce in chains}
    dec = {ce: jnp.where(incl, jnp.exp(jnp.where(incl, gc[ce] - gr[ce], 0.0)), 0.0) for ce in chains}
    amat = [jnp.where(strict, bc[ce] * kks[qk_of[ce]] * dec[ce], 0.0) for ce in chains]
    tinv = dict(zip(chains, _unit_lower_inverse_many(amat, tick)))
    u = {ce: _dot((tinv[ce] * br[ce]).astype(bf16), v_ref[rows(ce[0]), lanes(ce[1])]) for ce in chains}
    tick()
    w = {ce: _dot((tinv[ce] * (br[ce] * jnp.exp(gr[ce]))).astype(bf16), ks[qk_of[ce]]) for ce in chains}
    tick()
    g_last = {ce: gr[ce][:, CHUNK - 1:CHUNK] for ce in chains}
    wq = {ce: jnp.concatenate([w[ce].astype(bf16), qs[qk_of[ce]]], axis=0) for ce in chains}
    ak = {ce: jnp.concatenate(
        [jnp.where(incl, qks[qk_of[ce]] * dec[ce], 0.0).astype(bf16),
         (kts[qk_of[ce]] * jnp.exp(g_last[ce] - gr[ce])).astype(bf16)], axis=0) for ce in chains}
    for _ in rec:
        pass
    for n, ce in enumerate(chains):
        u_ref[n] = u[ce]
        wq_ref[n] = wq[ce]
        ak_ref[n] = ak[ce]
        eg_ref[n] = jnp.exp(gc[ce])
        egl_ref[n] = jnp.broadcast_to(jnp.exp(g_last[ce]), (1, HEAD_DIM))


def gdn_core(qkv, plain, norm_g, grow, brow):
    assert CHUNK == HEAD_DIM
    lp = qkv.shape[0]
    rb = _pick(lp, (640, 256, 128))
    nq = GDN_QK_PER_STEP
    nv = 2 * nq
    qw, vw = nq * HEAD_DIM, nv * HEAD_DIM
    kcol0 = GDN_QK_W // qw
    vcol0 = 2 * GDN_QK_W // vw
    zcol0 = COL_GZ // vw

    nr = lp // rb
    n_chain = nv * (rb // CHUNK)

    def cur(r):
        return jnp.minimum(r, nr - 1)

    def prev(r):
        return jnp.maximum(r - 1, 0)

    def rowspec(e):
        return pl.BlockSpec((None, 1, rb), lambda j, r: (nv * j + e, 0, cur(r)))

    rowspecs = [rowspec(e) for e in range(nv)]
    kern = functools.partial(_gdn_kernel, rb=rb, nq=nq)
    return pl.pallas_call(
        kern,
        grid=(GDN_QK_HEADS // nq, nr + 1),
        in_specs=[pl.BlockSpec((rb, qw), lambda j, r: (cur(r), j)),
                  pl.BlockSpec((rb, qw), lambda j, r: (cur(r), kcol0 + j)),
                  pl.BlockSpec((rb, vw), lambda j, r: (cur(r), vcol0 + j)),
                  pl.BlockSpec((rb, vw), lambda j, r: (prev(r), zcol0 + j)),
                  pl.BlockSpec((1, HEAD_DIM), lambda j, r: (0, 0))] + rowspecs + rowspecs,
        out_specs=pl.BlockSpec((rb, vw), lambda j, r: (prev(r), j)),
        out_shape=jax.ShapeDtypeStruct((lp, GDN_V_W), bf16),
        scratch_shapes=[pltpu.VMEM((nv, HEAD_DIM, HEAD_DIM), f32),
                        pltpu.VMEM((n_chain, CHUNK, HEAD_DIM), f32),
                        pltpu.VMEM((n_chain, 2 * CHUNK, HEAD_DIM), bf16),
                        pltpu.VMEM((n_chain, 2 * CHUNK, CHUNK), bf16),
                        pltpu.VMEM((n_chain, CHUNK, HEAD_DIM), f32),
                        pltpu.VMEM((n_chain, 1, HEAD_DIM), f32)],
        compiler_params=_cparams(("parallel", "arbitrary")),
        name="gdn_core",
    )(qkv, qkv, qkv, plain, norm_g.reshape(1, HEAD_DIM), *([grow] * nv), *([brow] * nv))


def _rope_head(seg, cc, s1, s2):
    half = ROPE_DIMS // 2
    return seg * cc + pltpu.roll(seg, half, 1) * s1 + pltpu.roll(seg, HEAD_DIM - half, 1) * s2


def _make_ep_rope(scale, head_major):
    def ep(acc, extra, outs):
        cc, s1, s2 = (r[...] for r in extra)
        for h in range(acc.shape[1] // HEAD_DIM):
            y = _rope_head(acc[:, h * HEAD_DIM:(h + 1) * HEAD_DIM], cc, s1, s2)
            if scale != 1.0:
                y = y * scale
            if head_major:
                outs[0][h] = y.astype(outs[0].dtype)
            else:
                outs[0][:, h * HEAD_DIM:(h + 1) * HEAD_DIM] = y.astype(outs[0].dtype)
    return ep


def _ep_rope_kv(acc, extra, outs):
    cc, s1, s2 = (r[...] for r in extra)
    for h in range(ATT_KV_HEADS):
        y = _rope_head(acc[:, h * HEAD_DIM:(h + 1) * HEAD_DIM], cc, s1, s2)
        outs[0][:, h * HEAD_DIM:(h + 1) * HEAD_DIM] = y.astype(outs[0].dtype)
    y = _rope_head(acc[:, ATT_KV_W:ATT_KV_W + HEAD_DIM], cc, s1, s2)
    outs[1][...] = y.astype(outs[1].dtype)


def rope_tables(lp):
    half = ROPE_DIMS // 2
    pos = (jnp.arange(lp, dtype=jnp.int32) - FRAME_OFF).astype(f32)
    inv = ROPE_THETA ** (-jnp.arange(half, dtype=f32) / half)
    ang = pos[:, None] * inv[None, :]
    cos, sin = jnp.cos(ang), jnp.sin(ang)
    zeros = jnp.zeros((lp, HEAD_DIM - ROPE_DIMS), f32)
    z16 = jnp.zeros((lp, half), f32)
    cc = jnp.concatenate([cos, cos, jnp.ones_like(zeros)], axis=1)
    s1 = jnp.concatenate([z16, sin, zeros], axis=1)
    s2 = jnp.concatenate([-sin, z16, zeros], axis=1)
    return cc, s1, s2


INT_MAX = 2 ** 31 - 1
HALF_MIN, HALF_MAX, HALF_SPAN = -2 ** 15, 2 ** 15 - 1, 2 ** 16


def _sortable(x):
    b = lax.bitcast_convert_type(x, i32)
    return b ^ (lax.shift_right_arithmetic(b, 31) & INT_MAX)


_NEG_BITS = struct.unpack("<i", struct.pack("<f", NEG))[0]
_KEY_NEG = _NEG_BITS ^ ((_NEG_BITS >> 31) & INT_MAX)
_KEY_NEG_HI = _KEY_NEG >> 16
_KEY_NEG_LO = (_KEY_NEG & (HALF_SPAN - 1)) + HALF_MIN


def _select_kernel(iq_ref, ik_ref, iwt_ref, bias_ref, key_ref, half_ref, *, tq, ts, lp, topk):
    i = pl.program_id(0)
    nkb = ((i + 1) * tq + ts - 1) // ts
    rem = lp - nkb * ts
    iwt = iwt_ref[...] * ((IDX_HEADS ** -0.5) * (HEAD_DIM ** -0.5))
    qpos = i * tq + lax.broadcasted_iota(i32, (1, tq), 1)
    sub = lax.broadcasted_iota(i32, (ts, 1), 0)

    def score_blk(kb):
        k0 = pl.multiple_of(kb * ts, ts)
        ikb = ik_ref[pl.ds(k0, ts), :]
        acc = jnp.zeros((ts, tq), f32)
        for h in range(IDX_HEADS):
            lg = _dot_nt(ikb, iq_ref[h])
            acc = acc + jnp.maximum(lg, 0.0) * iwt[h:h + 1, :]
        kpos = k0 + sub
        sc = jnp.where(kpos <= qpos, acc, NEG)
        sc = jnp.where(kpos < FRAME_OFF, -jnp.inf, sc)
        key = _sortable(sc)
        key_ref[pl.ds(k0, ts), :] = key
        half_ref[pl.ds(k0, ts), :] = lax.shift_right_arithmetic(key, 16).astype(i16)

    spt = 3

    def score_trip(kp, carry):
        for b in range(spt):
            score_blk(spt * kp + b)
        return carry

    def score_rest(kb, carry):
        score_blk(kb)
        return carry

    lax.fori_loop(0, nkb // spt, score_trip, 0)
    lax.fori_loop(nkb - nkb % spt, nkb, score_rest, 0)

    n_acc = 4

    def count(pred_fn):
        sub8 = lax.broadcasted_iota(i32, (8, 1), 0)

        def blk(kb, cnts):
            k0 = pl.multiple_of(kb * ts, ts)
            cnts = list(cnts)
            blk_ref = key_ref.at[pl.ds(k0, ts)]
            for j in range(ts // 8):
                kv = blk_ref[j * 8:(j + 1) * 8, :]
                cnts[j % n_acc] = cnts[j % n_acc] + jnp.where(pred_fn(kv, k0 + j * 8 + sub8), 1, 0)
            return tuple(cnts)
        cnts = lax.fori_loop(0, nkb, blk, tuple(jnp.zeros((8, tq), i32) for _ in range(n_acc)))
        return jnp.sum(sum(cnts[1:], cnts[0]), axis=0, keepdims=True)

    def count_half_ge(cand):
        one, zero = jnp.int16(1), jnp.int16(0)

        def blk(kb, cnts):
            k0 = pl.multiple_of(kb * ts, ts)
            cnts = list(cnts)
            blk_ref = half_ref.at[pl.ds(k0, ts)]
            for j in range(ts // BF16_SUBLANES):
                hv = blk_ref[j * BF16_SUBLANES:(j + 1) * BF16_SUBLANES, :]
                cnts[j % n_acc] = cnts[j % n_acc] + jnp.where(hv >= cand, one, zero)
            return tuple(cnts)

        def pair(kp, cnts):
            return blk(2 * kp + 1, blk(2 * kp, cnts))

        cnts = tuple(jnp.zeros((BF16_SUBLANES, tq), i16) for _ in range(n_acc))
        cnts = lax.fori_loop(0, nkb // 2, pair, cnts)
        cnts = lax.fori_loop(nkb - nkb % 2, nkb, blk, cnts)
        return jnp.sum(sum(cnts[1:], cnts[0]).astype(i32), axis=0, keepdims=True)

    def search_half(neg_half, n_start):
        def body(it, c):
            tau, n_tau = c
            t2 = tau | lax.shift_left(jnp.int32(1), 15 - it)
            cand = t2 + HALF_MIN
            n2 = count_half_ge(cand.astype(i16)) + jnp.where(cand <= neg_half, rem, 0)
            ok = n2 >= topk
            return jnp.where(ok, t2, tau), jnp.where(ok, n2, n_tau)

        return lax.fori_loop(0, 16, body, (jnp.zeros((1, tq), i32), n_start))

    tau_hi, n_hi = search_half(jnp.full((1, tq), _KEY_NEG_HI, i32), jnp.full((1, tq), lp, i32))
    t_hi = tau_hi + HALF_MIN

    base = lax.shift_left(t_hi, 16)

    def low_blk(kb, carry):
        k0 = pl.multiple_of(kb * ts, ts)
        key = key_ref[pl.ds(k0, ts), :]
        lo = jnp.minimum(jnp.maximum(key, base), base + (HALF_SPAN - 1)) - (base - HALF_MIN)
        half_ref[pl.ds(k0, ts), :] = lo.astype(i16)
        return carry

    lax.fori_loop(0, nkb, low_blk, 0)
    neg_lo = jnp.where(_KEY_NEG_HI > t_hi, HALF_MAX, jnp.where(_KEY_NEG_HI == t_hi, _KEY_NEG_LO, HALF_MIN))
    tau_lo, n_ge = search_half(neg_lo, n_hi)
    thr = lax.shift_left(t_hi, 16) | tau_lo
    has_tie = jnp.max(jnp.where(n_ge > topk, 1, 0)) > 0

    def tie_cut():
        n_gt = (count(lambda kv, kpos: kv > thr) + jnp.where(thr < _KEY_NEG, rem, 0))
        need = topk - n_gt

        def bis2(it, cut):
            bit = lax.shift_left(jnp.int32(1), 14 - it)
            c2 = cut | bit
            n = count(lambda kv, kpos: jnp.logical_and(kv == thr, kpos < c2))
            return jnp.where(n < need, c2, cut)
        return lax.fori_loop(0, 15, bis2, jnp.zeros((1, tq), i32))

    cut = lax.cond(has_tie, tie_cut, lambda: jnp.full((1, tq), INT_MAX, i32))

    def bias_blk(kb, carry):
        k0 = pl.multiple_of(kb * ts, ts)
        kv = key_ref[pl.ds(k0, ts), :]
        interior = jnp.logical_and(k0 >= FRAME_OFF, k0 + ts - 1 <= i * tq)

        @pl.when(jnp.logical_and(interior, jnp.logical_not(has_tie)))
        def _():
            bias_ref[pl.ds(k0, ts), :] = jnp.where(kv >= thr, 0.0, NEG).astype(bias_ref.dtype)

        @pl.when(jnp.logical_not(jnp.logical_and(interior, jnp.logical_not(has_tie))))
        def _():
            kpos = k0 + sub
            sel = jnp.logical_or(kv > thr, jnp.logical_and(kv == thr, kpos <= cut))
            vis = jnp.logical_and(kpos <= qpos, kpos >= FRAME_OFF)
            bias_ref[pl.ds(k0, ts), :] = jnp.where(jnp.logical_and(sel, vis), 0.0, NEG).astype(bias_ref.dtype)
        return carry

    lax.fori_loop(0, nkb, bias_blk, 0)

    def fill_blk(kb, carry):
        k0 = pl.multiple_of(kb * ts, ts)
        bias_ref[pl.ds(k0, ts), :] = jnp.full((ts, tq), NEG, bias_ref.dtype)
        return carry

    lax.fori_loop(nkb, lp // ts, fill_blk, 0)


def dsa_select(iq, ik, iwt, topk):
    nh, lp, _ = iq.shape
    assert lp < 2 ** 15
    tq = 256
    ts = _pick(lp, (640, 256, 128))
    kern = functools.partial(_select_kernel, tq=tq, ts=ts, lp=lp, topk=topk)
    return pl.pallas_call(
        kern,
        grid=(lp // tq,),
        in_specs=[pl.BlockSpec((nh, tq, HEAD_DIM), lambda i: (0, i, 0)),
                  pl.BlockSpec((lp, HEAD_DIM), lambda i: (0, 0), pipeline_mode=pl.Buffered(1)),
                  pl.BlockSpec((nh, tq), lambda i: (0, i))],
        out_specs=pl.BlockSpec((lp, tq), lambda i: (0, i)),
        out_shape=jax.ShapeDtypeStruct((lp, lp), bf16),
        scratch_shapes=[pltpu.VMEM((lp, tq), i32), pltpu.VMEM((lp, tq), i16)],
        compiler_params=_cparams(("parallel",)),
        name="dsa_select",
    )(iq, ik, iwt)


V_AUG = HEAD_DIM + BF16_SUBLANES


def _attn_kernel(q_ref, k_ref, vt_ref, b_ref, o_ref, m_ref, acc_ref, *, tq, ts, bpt):
    i = pl.program_id(0)
    nkb = ((i + 1) * tq + ts - 1) // ts
    group = ATT_HEADS // ATT_KV_HEADS
    kv_groups = range(ATT_KV_HEADS)
    qpos = i * tq + lax.broadcasted_iota(i32, (tq, 1), 0)
    eye = (lax.broadcasted_iota(i32, (tq, tq), 0)
           == lax.broadcasted_iota(i32, (tq, tq), 1)).astype(bf16)
    qa = [jnp.concatenate(
        [jnp.concatenate([q_ref[:, (g * group + r) * HEAD_DIM:(g * group + r + 1) * HEAD_DIM], eye],
                         axis=1) for r in range(group)], axis=0) for g in kv_groups]
    m_ref[...] = jnp.full(m_ref.shape, -jnp.inf, f32)
    acc_ref[...] = jnp.zeros_like(acc_ref)

    def scores(g, k0):
        ka = jnp.concatenate([k_ref[pl.ds(k0, ts), g * HEAD_DIM:(g + 1) * HEAD_DIM],
                              b_ref[pl.ds(k0, ts), :]], axis=1)
        return _dot_nt(ka, qa[g]).astype(bf16)

    def accumulate(g, k0, st):
        vt_ = vt_ref[g, :, pl.ds(k0, ts)]
        ps, alphas = [], []
        for r in range(group):
            sr = st[:, r * tq:(r + 1) * tq]
            m_old = m_ref[g, r:r + 1, :]
            m_new = jnp.maximum(m_old, jnp.max(sr, axis=0, keepdims=True).astype(f32))
            m_ref[g, r:r + 1, :] = m_new
            ps.append(jnp.exp2(sr - m_new.astype(bf16)))
            alphas.append(jnp.exp2(m_old - m_new))
        pt = jnp.concatenate(ps, axis=1)
        acc_ref[g] = jnp.concatenate(alphas, axis=1) * acc_ref[g] + _dot(vt_, pt)

    def make_trip(nblk, first_block):
        def trip(kp, carry):
            k0s = [pl.multiple_of((first_block + kp * nblk + b) * ts, ts) for b in range(nblk)]
            work = [(g, kk) for kk in k0s for g in kv_groups]
            st = scores(*work[0])
            for n in range(len(work)):
                st_next = scores(*work[n + 1]) if n + 1 < len(work) else None
                accumulate(*work[n], st)
                st = st_next
            return carry
        return trip

    n_full = nkb // bpt
    lax.fori_loop(0, n_full, make_trip(bpt, 0), 0)
    if bpt > 1:
        lax.fori_loop(0, nkb - n_full * bpt, make_trip(1, n_full * bpt), 0)
    for g in kv_groups:
        for r in range(group):
            ot = (acc_ref[g, 0:HEAD_DIM, r * tq:(r + 1) * tq]
                  / acc_ref[g, HEAD_DIM:HEAD_DIM + 1, r * tq:(r + 1) * tq])
            orr = jnp.where(qpos >= FRAME_OFF, ot.T, 0.0)
            o_ref[:, (g * group + r) * HEAD_DIM:(g * group + r + 1) * HEAD_DIM] = orr.astype(o_ref.dtype)


def dsa_attention(q, k, vt_aug, bias_t):
    lp = q.shape[0]
    tq = 128
    ts = _pick(lp, (1280, 640, 256, 128))
    bpt = 2
    group = ATT_HEADS // ATT_KV_HEADS
    kern = functools.partial(_attn_kernel, tq=tq, ts=ts, bpt=bpt)
    resident = pl.Buffered(1)
    return pl.pallas_call(
        kern,
        grid=(lp // tq,),
        in_specs=[pl.BlockSpec((tq, ATT_Q_W), lambda i: (i, 0)),
                  pl.BlockSpec((lp, ATT_KV_W), lambda i: (0, 0), pipeline_mode=resident),
                  pl.BlockSpec((ATT_KV_HEADS, V_AUG, lp), lambda i: (0, 0, 0), pipeline_mode=resident),
                  pl.BlockSpec((lp, tq), lambda i: (0, i))],
        out_specs=pl.BlockSpec((tq, ATT_Q_W), lambda i: (i, 0)),
        out_shape=jax.ShapeDtypeStruct((lp, ATT_Q_W), bf16),
        scratch_shapes=[pltpu.VMEM((ATT_KV_HEADS, group, tq), f32),
                        pltpu.VMEM((ATT_KV_HEADS, V_AUG, group * tq), f32)],
        compiler_params=_cparams(("parallel",)),
        name="dsa_attention",
    )(q, k, vt_aug, bias_t)


def _ffn_up_kernel(a_ref, ah_ref, wg_ref, wv_ref, cg_ref, cv_ref, bg_ref, bv_ref, o_ref,
                   eg_ref, ev_ref, *, tm):
    a = _halo_rows(a_ref, ah_ref, pl.program_id(0))
    eg_ref[...] = _dot(a, wg_ref[...])
    ev_ref[...] = _dot(a, wv_ref[...])

    def conv(ext_ref, w_ref, b_ref):
        w = w_ref[...]
        y = ext_ref[HALO:HALO + tm, :] * w[FFN_CONV - 1:FFN_CONV, :]
        for j in range(FFN_CONV - 1):
            s0 = HALO - (FFN_CONV - 1) + j
            y = y + ext_ref[s0:s0 + tm, :] * w[j:j + 1, :]
        return y + b_ref[...]

    gate = conv(eg_ref, cg_ref, bg_ref)
    val = conv(ev_ref, cv_ref, bv_ref)
    o_ref[...] = (gate * _sigmoid(gate) * val).astype(o_ref.dtype)


def ffn_up(u, w_up, conv_w, conv_b):
    lp, kdim = u.shape
    tm = _pick(lp, (1280, 640, 256, 128))
    tn = 1024
    nc = D_FF // tn
    kern = functools.partial(_ffn_up_kernel, tm=tm)

    def wspec(off):
        return pl.BlockSpec((kdim, tn), lambda i, c: (0, c + off))

    def cspec(off):
        return pl.BlockSpec((FFN_CONV, tn), lambda i, c: (0, c + off))

    def bspec(off):
        return pl.BlockSpec((1, tn), lambda i, c: (0, c + off))

    b2 = conv_b.reshape(1, 2 * D_FF)
    return pl.pallas_call(
        kern,
        grid=(lp // tm, nc),
        in_specs=[pl.BlockSpec((tm, kdim), lambda i, c: (i, 0)),
                  pl.BlockSpec((HALO, kdim), lambda i, c: (jnp.maximum(i * (tm // HALO) - 1, 0), 0)),
                  wspec(0), wspec(nc), cspec(0), cspec(nc), bspec(0), bspec(nc)],
        out_specs=pl.BlockSpec((tm, tn), lambda i, c: (i, c)),
        out_shape=jax.ShapeDtypeStruct((lp, D_FF), bf16),
        scratch_shapes=[pltpu.VMEM((HALO + tm, tn), f32)] * 2,
        compiler_params=_cparams(("parallel", "arbitrary")),
        name="ffn_up",
    )(u, u, w_up, w_up, conv_w, conv_w, b2, b2)


def _split_w_in(w):
    wt = w.T
    o = 0
    parts = {}
    for name, width in (("gq", GDN_QK_W), ("gk", GDN_QK_W), ("gv", GDN_V_W), ("gz", GDN_V_W),
                        ("gb", GDN_V_HEADS), ("ga", GDN_V_HEADS), ("aq", ATT_Q_W), ("ak", ATT_KV_W),
                        ("av", ATT_KV_W), ("iq", IDX_Q_W), ("ik", HEAD_DIM), ("iw", IDX_HEADS),
                        ("gate_gdn", D_MODEL), ("gate_att", D_MODEL)):
        parts[name] = wt[o:o + width]
        o += width
    cat = lambda names: jnp.concatenate([parts[n] for n in names], axis=0)
    small = cat(("gb", "ga", "iw"))
    small = jnp.pad(small, ((0, LANES - small.shape[0]), (0, 0)))
    groups = dict(qkv=cat(("gq", "gk", "gv")), aq=parts["aq"], iq=parts["iq"], kv=cat(("ak", "ik")),
                  plain=cat(("gz", "gate_gdn", "gate_att", "av")), small=small)
    return {k: v.astype(bf16) for k, v in groups.items()}


def _layer(h0, p, lp, topk, last):
    tm = _pick(lp, (640, 256, 128))
    tm_big = _pick(lp, (1280, 640, 256, 128))
    wp = _split_w_in(p["w_in"])

    def proj(w, **kw):
        return matmul(u1, w, tm=tm_big, tk=D_MODEL, w_nt=True, **kw)

    u1 = rms_rows(h0, p["mix_pre_g"])
    tabs = rope_tables(lp)
    tab_specs = (((tm_big, HEAD_DIM), lambda i, j: (i, 0)),) * 3
    qkv = gdn_proj(u1, wp["qkv"], p["gdn_conv_w"])
    (aq,) = proj(wp["aq"], tn=1024, epilogue=_make_ep_rope(HEAD_DIM ** -0.5 * LOG2E, False),
                 extra=tabs, extra_specs=tab_specs,
                 out_shapes=[jax.ShapeDtypeStruct((lp, ATT_Q_W), bf16)], name="proj_aq")
    iq_heads = 1024 // HEAD_DIM
    (iq,) = proj(wp["iq"], tn=1024, epilogue=_make_ep_rope(1.0, True), extra=tabs, extra_specs=tab_specs,
                 out_shapes=[jax.ShapeDtypeStruct((IDX_HEADS, lp, HEAD_DIM), bf16)],
                 out_block_specs=[((iq_heads, tm_big, HEAD_DIM), lambda i, j: (j, i, 0))], name="proj_iq")
    ak, ik = proj(wp["kv"], tn=ATT_KV_W + HEAD_DIM, epilogue=_ep_rope_kv, extra=tabs, extra_specs=tab_specs,
                  out_shapes=[jax.ShapeDtypeStruct((lp, ATT_KV_W), bf16),
                              jax.ShapeDtypeStruct((lp, HEAD_DIM), bf16)],
                  out_block_specs=[((tm_big, ATT_KV_W), lambda i, j: (i, 0)),
                                   ((tm_big, HEAD_DIM), lambda i, j: (i, 0))], name="proj_kv")
    (plain,) = proj(wp["plain"], tn=768, epilogue=_ep_cast,
                    out_shapes=[jax.ShapeDtypeStruct((lp, PLAIN_W), bf16)], name="proj_plain")
    (small,) = proj(wp["small"], tn=LANES, epilogue=_ep_cast,
                    out_shapes=[jax.ShapeDtypeStruct((lp, LANES), f32)], name="proj_small")
    gb = small[:, 0:GDN_V_HEADS]
    ga = small[:, GDN_V_HEADS:2 * GDN_V_HEADS]
    iw = small[:, 2 * GDN_V_HEADS:2 * GDN_V_HEADS + IDX_HEADS]

    beta, gcum = gdn_gates(gb, ga, p["gdn_a_log"], p["gdn_dt_bias"])
    o_gdn = gdn_core(qkv, plain, p["gdn_norm_g"], gcum.T[:, None, :], beta.T[:, None, :])

    avt = plain[:, COL_AV:COL_AV + ATT_KV_W].T.reshape(ATT_KV_HEADS, HEAD_DIM, lp)
    avt = jnp.concatenate([avt, jnp.ones((ATT_KV_HEADS, V_AUG - HEAD_DIM, lp), bf16)], axis=1)
    bias_t = dsa_select(iq, ik, iw.T, topk)
    o_att = dsa_attention(aq, ak, avt, bias_t)

    tn = 1024
    gate_spec = lambda col0: ((tm_big, tn), (lambda i, j, c=col0 // tn: (i, c + j)))
    (m1,) = matmul(o_gdn, p["w_branch_gdn"].astype(bf16), tm=tm_big, tn=tn, tk=2048,
                   epilogue=_ep_gate, extra=(plain,), extra_specs=(gate_spec(COL_GATE_GDN),),
                   out_shapes=[jax.ShapeDtypeStruct((lp, D_MODEL), f32)], name="branch_gdn")
    (merged,) = matmul(o_att, p["w_branch_att"].astype(bf16), tm=tm_big, tn=tn, tk=2048,
                       epilogue=_ep_gate_add, extra=(plain, m1),
                       extra_specs=(gate_spec(COL_GATE_ATT), ((tm_big, tn), lambda i, j: (i, j))),
                       out_shapes=[jax.ShapeDtypeStruct((lp, D_MODEL), bf16)], name="branch_att")
    row_spec = ((tm, D_MODEL), lambda i, j: (i, 0))
    vec_spec = ((1, D_MODEL), lambda i, j: (0, 0))
    h1, u2 = matmul(merged, p["w_out"].astype(bf16), tm=tm, tn=D_MODEL, tk=D_MODEL,
                    epilogue=_ep_res_norm2,
                    extra=(h0, p["mix_post_g"].reshape(1, D_MODEL), p["ffn_pre_g"].reshape(1, D_MODEL)),
                    extra_specs=(row_spec, vec_spec, vec_spec),
                    out_shapes=[jax.ShapeDtypeStruct((lp, D_MODEL), f32),
                                jax.ShapeDtypeStruct((lp, D_MODEL), bf16)], name="w_out")

    act = ffn_up(u2, p["w_up"].astype(bf16), p["ffn_conv_w"], p["ffn_conv_b"])
    if not last:
        (h2,) = matmul(act, p["w_down"].astype(bf16), tm=tm, tn=D_MODEL, tk=2048, epilogue=_ep_res_norm,
                       extra=(h1, p["ffn_post_g"].reshape(1, D_MODEL)),
                       extra_specs=(row_spec, vec_spec),
                       out_shapes=[jax.ShapeDtypeStruct((lp, D_MODEL), f32)], name="w_down")
        return h2
    tmx = FRAME_X0
    (out,) = matmul(act, p["w_down"].astype(bf16), tm=tmx, tn=D_MODEL, tk=D_FF, epilogue=_ep_res_norm,
                    extra=(h1, p["ffn_post_g"].reshape(1, D_MODEL)),
                    extra_specs=(((tmx, D_MODEL), lambda i, j: (i, 0)), vec_spec),
                    out_shapes=[jax.ShapeDtypeStruct((lp - FRAME_X0, D_MODEL), f32)],
                    out_block_specs=[((tmx, D_MODEL), lambda i, j: (jnp.maximum(i - 1, 0), 0))],
                    rows_sem="arbitrary", name="w_down")
    return out


def kernel(x, meta_tokens, mix_pre_g, w_in, gdn_conv_w, gdn_a_log, gdn_dt_bias, gdn_norm_g,
           w_branch_gdn, w_branch_att, w_out, mix_post_g, ffn_pre_g, w_up, ffn_conv_w,
           ffn_conv_b, w_down, ffn_post_g):
    batch, seq, d = x.shape
    assert batch == 1 and d == D_MODEL
    lp = FRAME_X0 + seq
    topk = min(TOPK_MAX, (N_META + seq) // 4)
    h = jnp.concatenate([jnp.zeros((FRAME_OFF, d), x.dtype), meta_tokens.astype(x.dtype), x[0]], axis=0)
    depth = w_in.shape[0]
    for i in range(depth):
        p = dict(mix_pre_g=mix_pre_g[i], w_in=w_in[i], gdn_conv_w=gdn_conv_w[i], gdn_a_log=gdn_a_log[i],
                 gdn_dt_bias=gdn_dt_bias[i], gdn_norm_g=gdn_norm_g[i], w_branch_gdn=w_branch_gdn[i],
                 w_branch_att=w_branch_att[i], w_out=w_out[i], mix_post_g=mix_post_g[i],
                 ffn_pre_g=ffn_pre_g[i], w_up=w_up[i], ffn_conv_w=ffn_conv_w[i],
                 ffn_conv_b=ffn_conv_b[i], w_down=w_down[i], ffn_post_g=ffn_post_g[i])
        h = _layer(h, p, lp, topk, last=(i == depth - 1))
    return h[None]
```
